```python
import jax, jax.numpy as jnp
from jax import lax
import numpy as np

D_MODEL = 2048
BATCH = 8
SEQ = 8192
DEPTH = 1

GDN_HEADS = 8
GDN_DK = 128
GDN_DV = 128
GDN_CONV = 4
GDN_CHUNK = 64
FOX_HEADS = 8
FOX_DH = 128
FOX_BLOCK = 128
MEM_LEN = 256
MEM_HEADS = 4
MEM_DH = 256
D_FF = 4 * D_MODEL
N_BRANCH = 3
EPS = 1e-6

GDN_QK = GDN_HEADS * GDN_DK
GDN_V = GDN_HEADS * GDN_DV
GDN_QKV = 2 * GDN_QK + GDN_V
FOX_W = FOX_HEADS * FOX_DH
MEM_W = MEM_HEADS * MEM_DH
IN_SPLITS = (GDN_QKV, GDN_V, GDN_HEADS, GDN_HEADS, FOX_W, FOX_W, FOX_W, FOX_HEADS, MEM_W, N_BRANCH * D_MODEL)
D_IN = 2 * GDN_QK + 2 * GDN_V + 2 * GDN_HEADS + 3 * FOX_W + FOX_HEADS + MEM_W + N_BRANCH * D_MODEL

kernel_name = "hybrid_gdn_fox_memory_block"


def rms_norm(x, g):
    xf = x.astype(jnp.float32)
    y = xf * lax.rsqrt(jnp.mean(xf * xf, axis=-1, keepdims=True) + EPS)
    return (y * g.astype(jnp.float32)).astype(x.dtype)


def l2_norm(x):
    return x * lax.rsqrt(jnp.sum(x * x, axis=-1, keepdims=True) + EPS)


def to_heads(t, n_heads):
    b, s, _ = t.shape
    return t.reshape(b, s, n_heads, -1).transpose(0, 2, 1, 3)


def causal_conv_silu(x, w):
    k_w = w.shape[0]
    s = x.shape[1]
    xp = jnp.pad(x, ((0, 0), (k_w - 1, 0), (0, 0)))
    y = xp[:, 0:s] * w[0]
    for i in range(1, k_w):
        y = y + xp[:, i:i + s] * w[i]
    return jax.nn.silu(y)


def gated_delta_rule(q, k, v, g, beta):
    b, h, s, dk = q.shape
    dv = v.shape[-1]
    c = GDN_CHUNK
    n = s // c
    q = q.reshape(b, h, n, c, dk)
    k = k.reshape(b, h, n, c, dk)
    v = v.reshape(b, h, n, c, dv)
    beta = beta.reshape(b, h, n, c)
    gam = jnp.cumsum(g.reshape(b, h, n, c), axis=-1)
    idx = jnp.arange(c)
    strict = idx[:, None] > idx[None, :]
    incl = idx[:, None] >= idx[None, :]
    diff = gam[..., :, None] - gam[..., None, :]
    dec_strict = jnp.where(strict, jnp.exp(jnp.where(strict, diff, 0.0)), 0.0)
    dec_incl = jnp.where(incl, jnp.exp(jnp.where(incl, diff, 0.0)), 0.0)
    m_low = beta[..., :, None] * jnp.einsum('bhnid,bhnjd->bhnij', k, k) * dec_strict
    a_mat = jnp.eye(c, dtype=jnp.float32) + m_low
    rhs = jnp.concatenate([(beta * jnp.exp(gam))[..., None] * k, beta[..., None] * v], axis=-1)
    sol = lax.linalg.triangular_solve(a_mat, rhs, left_side=True, lower=True, unit_diagonal=True)
    w_c, u_c = sol[..., :dk], sol[..., dk:]
    qk = jnp.einsum('bhnid,bhnjd->bhnij', q, k) * dec_incl
    q_dec = q * jnp.exp(gam)[..., None]
    k_dec = k * jnp.exp(gam[..., -1:] - gam)[..., None]
    chunk_dec = jnp.exp(gam[..., -1])

    def step(state, xs):
        w_i, u_i, qk_i, qd_i, kd_i, cd_i = xs
        u = u_i - jnp.einsum('bhid,bhde->bhie', w_i, state)
        o = jnp.einsum('bhid,bhde->bhie', qd_i, state) + jnp.einsum('bhij,bhje->bhie', qk_i, u)
        state = cd_i[..., None, None] * state + jnp.einsum('bhid,bhie->bhde', kd_i, u)
        return state, o

    xs = tuple(jnp.moveaxis(t, 2, 0) for t in (w_c, u_c, qk, q_dec, k_dec, chunk_dec))
    _, o = lax.scan(step, jnp.zeros((b, h, dk, dv), jnp.float32), xs)
    return jnp.moveaxis(o, 0, 2).reshape(b, h, s, dv)


def forgetting_attention(q, k, v, log_f):
    _, _, s, d = q.shape
    cum = jnp.cumsum(log_f, axis=-1)
    scale = d ** -0.5
    outs = []
    for start in range(0, s, FOX_BLOCK):
        end = start + FOX_BLOCK
        logits = jnp.einsum('bhqd,bhkd->bhqk', q[:, :, start:end], k[:, :, :end]).astype(jnp.float32) * scale
        logits = logits + cum[:, :, start:end, None] - cum[:, :, None, :end]
        mask = (start + jnp.arange(FOX_BLOCK))[:, None] >= jnp.arange(end)[None, :]
        p = jax.nn.softmax(jnp.where(mask, logits, -jnp.inf), axis=-1)
        outs.append(jnp.einsum('bhqk,bhkd->bhqd', p.astype(v.dtype), v[:, :, :end]))
    return jnp.concatenate(outs, axis=2)


def _fwd_setup_inputs(seed: int = 0) -> dict:
    key = jax.random.key(seed)
    ks = jax.random.split(key, 24)
    L, D = DEPTH, D_MODEL
    nrm = lambda k, shape, fan_in: jax.random.normal(k, shape, jnp.float32) * (fan_in ** -0.5)
    gain = lambda k, shape: 1.0 + 0.02 * jax.random.normal(k, shape, jnp.float32)
    a_log = jnp.log(jax.random.uniform(ks[5], (L, GDN_HEADS), jnp.float32, 1.0, 16.0))
    dt = jnp.exp(jax.random.uniform(ks[6], (L, GDN_HEADS), jnp.float32, np.log(1e-3), np.log(1e-1)))
    dt_bias = dt + jnp.log(-jnp.expm1(-dt))
    return {
        "x": jax.random.normal(ks[0], (BATCH, SEQ, D), jnp.float32),
        "mem": jax.random.normal(ks[1], (BATCH, MEM_LEN, D), jnp.float32),
        "g_mix": gain(ks[2], (L, D)),
        "w_in": nrm(ks[3], (L, D, D_IN), D),
        "conv_w": nrm(ks[4], (L, GDN_CONV, GDN_QKV), GDN_CONV),
        "a_log": a_log,
        "dt_bias": dt_bias,
        "gdn_norm_g": gain(ks[7], (L, GDN_DV)),
        "fox_b_f": jax.random.uniform(ks[8], (L, FOX_HEADS), jnp.float32, 1.0, 4.0),
        "fox_q_norm": gain(ks[9], (L, FOX_DH)),
        "fox_k_norm": gain(ks[10], (L, FOX_DH)),
        "g_mem": gain(ks[11], (L, D)),
        "w_mem_kv": nrm(ks[12], (L, D, 2 * MEM_W), D),
        "mem_q_norm": gain(ks[13], (L, MEM_DH)),
        "mem_k_norm": gain(ks[14], (L, MEM_DH)),
        "w_up_gdn": nrm(ks[15], (L, GDN_V, D), GDN_V),
        "w_up_fox": nrm(ks[16], (L, FOX_W, D), FOX_W),
        "w_up_mem": nrm(ks[17], (L, MEM_W, D), MEM_W),
        "w_out": nrm(ks[18], (L, D, D), D),
        "g_mlp": gain(ks[19], (L, D)),
        "w_ff1": nrm(ks[20], (L, D, D_FF), D),
        "w_ff2": nrm(ks[21], (L, D_FF, D), D_FF),
    }


def _fwd_reference(x, mem, g_mix, w_in, conv_w, a_log, dt_bias, gdn_norm_g, fox_b_f, fox_q_norm, fox_k_norm,
              g_mem, w_mem_kv, mem_q_norm, mem_k_norm, w_up_gdn, w_up_fox, w_up_mem, w_out, g_mlp, w_ff1, w_ff2):
    b, s, _ = x.shape
    splits = np.cumsum(IN_SPLITS)[:-1].tolist()
    f32 = jnp.float32
    for l in range(DEPTH):
        h = rms_norm(x, g_mix[l])
        proj = h @ w_in[l]
        qkv_a, z_a, b_a, a_a, q_b, k_b, v_b, f_b, q_m, gates = jnp.split(proj, splits, axis=-1)

        qkv_a = causal_conv_silu(qkv_a, conv_w[l])
        q_a, k_a, v_a = jnp.split(qkv_a, [GDN_QK, 2 * GDN_QK], axis=-1)
        q_a = l2_norm(to_heads(q_a, GDN_HEADS).astype(f32)) * (GDN_DK ** -0.5)
        k_a = l2_norm(to_heads(k_a, GDN_HEADS).astype(f32))
        v_a = to_heads(v_a, GDN_HEADS).astype(f32)
        beta = jax.nn.sigmoid(b_a.astype(f32)).transpose(0, 2, 1)
        g_dec = (-jnp.exp(a_log[l].astype(f32)) * jax.nn.softplus(a_a.astype(f32) + dt_bias[l].astype(f32))).transpose(0, 2, 1)
        o_a = gated_delta_rule(q_a, k_a, v_a, g_dec, beta).transpose(0, 2, 1, 3)
        o_a = rms_norm(o_a, gdn_norm_g[l]) * jax.nn.silu(z_a.astype(f32).reshape(b, s, GDN_HEADS, GDN_DV))
        o_a = o_a.reshape(b, s, GDN_V).astype(x.dtype)

        q_bh = rms_norm(to_heads(q_b, FOX_HEADS), fox_q_norm[l])
        k_bh = rms_norm(to_heads(k_b, FOX_HEADS), fox_k_norm[l])
        v_bh = to_heads(v_b, FOX_HEADS)
        log_f = jax.nn.log_sigmoid(f_b.astype(f32) + fox_b_f[l].astype(f32)).transpose(0, 2, 1)
        o_b = forgetting_attention(q_bh, k_bh, v_bh, log_f)
        o_b = o_b.transpose(0, 2, 1, 3).reshape(b, s, FOX_W)

        kv_m = rms_norm(mem, g_mem[l]) @ w_mem_kv[l]
        k_m, v_m = jnp.split(kv_m, 2, axis=-1)
        q_mh = rms_norm(to_heads(q_m, MEM_HEADS), mem_q_norm[l])
        k_mh = rms_norm(to_heads(k_m, MEM_HEADS), mem_k_norm[l])
        v_mh = to_heads(v_m, MEM_HEADS)
        logits_m = jnp.einsum('bhqd,bhkd->bhqk', q_mh, k_mh).astype(f32) * (MEM_DH ** -0.5)
        p_m = jax.nn.softmax(logits_m, axis=-1).astype(v_mh.dtype)
        o_m = jnp.einsum('bhqk,bhkd->bhqd', p_m, v_mh).transpose(0, 2, 1, 3).reshape(b, s, MEM_W)

        gate_a, gate_b, gate_m = jnp.split(jax.nn.sigmoid(gates), N_BRANCH, axis=-1)
        y = gate_a * (o_a @ w_up_gdn[l]) + gate_b * (o_b @ w_up_fox[l]) + gate_m * (o_m @ w_up_mem[l])
        x = x + y @ w_out[l]

        h2 = rms_norm(x, g_mlp[l])
        x = x + jnp.square(jax.nn.relu(h2 @ w_ff1[l])) @ w_ff2[l]
    return x


import jax as _jax
import jax.numpy as _jnp

TWIN_FORMAT = 'train_step'
FWD_PARAMS = ['x', 'mem', 'g_mix', 'w_in', 'conv_w', 'a_log', 'dt_bias', 'gdn_norm_g', 'fox_b_f', 'fox_q_norm', 'fox_k_norm', 'g_mem', 'w_mem_kv', 'mem_q_norm', 'mem_k_norm', 'w_up_gdn', 'w_up_fox', 'w_up_mem', 'w_out', 'g_mlp', 'w_ff1', 'w_ff2']
TWIN_WEIGHTS = ['g_mix', 'w_in', 'conv_w', 'a_log', 'dt_bias', 'gdn_norm_g', 'fox_b_f', 'fox_q_norm', 'fox_k_norm', 'g_mem', 'w_mem_kv', 'mem_q_norm', 'mem_k_norm', 'w_up_gdn', 'w_up_fox', 'w_up_mem', 'w_out', 'g_mlp', 'w_ff1', 'w_ff2']
TWIN_DIFF_INPUT = 'x'
TWIN_INPUTS = ['x', 'mem', 'g_mix', 'w_in', 'conv_w', 'a_log', 'dt_bias', 'gdn_norm_g', 'fox_b_f', 'fox_q_norm', 'fox_k_norm', 'g_mem', 'w_mem_kv', 'mem_q_norm', 'mem_k_norm', 'w_up_gdn', 'w_up_fox', 'w_up_mem', 'w_out', 'g_mlp', 'w_ff1', 'w_ff2', 'loss_target', 'm_g_mix', 'm_w_in', 'm_conv_w', 'm_a_log', 'm_dt_bias', 'm_gdn_norm_g', 'm_fox_b_f', 'm_fox_q_norm', 'm_fox_k_norm', 'm_g_mem', 'm_w_mem_kv', 'm_mem_q_norm', 'm_mem_k_norm', 'm_w_up_gdn', 'm_w_up_fox', 'm_w_up_mem', 'm_w_out', 'm_g_mlp', 'm_w_ff1', 'm_w_ff2', 'v_g_mix', 'v_w_in', 'v_conv_w', 'v_a_log', 'v_dt_bias', 'v_gdn_norm_g', 'v_fox_b_f', 'v_fox_q_norm', 'v_fox_k_norm', 'v_g_mem', 'v_w_mem_kv', 'v_mem_q_norm', 'v_mem_k_norm', 'v_w_up_gdn', 'v_w_up_fox', 'v_w_up_mem', 'v_w_out', 'v_g_mlp', 'v_w_ff1', 'v_w_ff2']
TWIN_OUTPUTS = ['loss', 'grad_x', 'grad_g_mix', 'grad_w_in', 'grad_conv_w', 'grad_a_log', 'grad_dt_bias', 'grad_gdn_norm_g', 'grad_fox_b_f', 'grad_fox_q_norm', 'grad_fox_k_norm', 'grad_g_mem', 'grad_w_mem_kv', 'grad_mem_q_norm', 'grad_mem_k_norm', 'grad_w_up_gdn', 'grad_w_up_fox', 'grad_w_up_mem', 'grad_w_out', 'grad_g_mlp', 'grad_w_ff1', 'grad_w_ff2', 'delta_g_mix', 'delta_w_in', 'delta_conv_w', 'delta_a_log', 'delta_dt_bias', 'delta_gdn_norm_g', 'delta_fox_b_f', 'delta_fox_q_norm', 'delta_fox_k_norm', 'delta_g_mem', 'delta_w_mem_kv', 'delta_mem_q_norm', 'delta_mem_k_norm', 'delta_w_up_gdn', 'delta_w_up_fox', 'delta_w_up_mem', 'delta_w_out', 'delta_g_mlp', 'delta_w_ff1', 'delta_w_ff2', 'new_m_g_mix', 'new_m_w_in', 'new_m_conv_w', 'new_m_a_log', 'new_m_dt_bias', 'new_m_gdn_norm_g', 'new_m_fox_b_f', 'new_m_fox_q_norm', 'new_m_fox_k_norm', 'new_m_g_mem', 'new_m_w_mem_kv', 'new_m_mem_q_norm', 'new_m_mem_k_norm', 'new_m_w_up_gdn', 'new_m_w_up_fox', 'new_m_w_up_mem', 'new_m_w_out', 'new_m_g_mlp', 'new_m_w_ff1', 'new_m_w_ff2', 'new_v_g_mix', 'new_v_w_in', 'new_v_conv_w', 'new_v_a_log', 'new_v_dt_bias', 'new_v_gdn_norm_g', 'new_v_fox_b_f', 'new_v_fox_q_norm', 'new_v_fox_k_norm', 'new_v_g_mem', 'new_v_w_mem_kv', 'new_v_mem_q_norm', 'new_v_mem_k_norm', 'new_v_w_up_gdn', 'new_v_w_up_fox', 'new_v_w_up_mem', 'new_v_w_out', 'new_v_g_mlp', 'new_v_w_ff1', 'new_v_w_ff2']
TWIN_LEAF_KINDS = {'loss': 'loss', 'grad_x': 'grad_x', 'grad_g_mix': 'grad_w', 'grad_w_in': 'grad_w', 'grad_conv_w': 'grad_w', 'grad_a_log': 'grad_w', 'grad_dt_bias': 'grad_w', 'grad_gdn_norm_g': 'grad_w', 'grad_fox_b_f': 'grad_w', 'grad_fox_q_norm': 'grad_w', 'grad_fox_k_norm': 'grad_w', 'grad_g_mem': 'grad_w', 'grad_w_mem_kv': 'grad_w', 'grad_mem_q_norm': 'grad_w', 'grad_mem_k_norm': 'grad_w', 'grad_w_up_gdn': 'grad_w', 'grad_w_up_fox': 'grad_w', 'grad_w_up_mem': 'grad_w', 'grad_w_out': 'grad_w', 'grad_g_mlp': 'grad_w', 'grad_w_ff1': 'grad_w', 'grad_w_ff2': 'grad_w', 'delta_g_mix': 'delta_w', 'delta_w_in': 'delta_w', 'delta_conv_w': 'delta_w', 'delta_a_log': 'delta_w', 'delta_dt_bias': 'delta_w', 'delta_gdn_norm_g': 'delta_w', 'delta_fox_b_f': 'delta_w', 'delta_fox_q_norm': 'delta_w', 'delta_fox_k_norm': 'delta_w', 'delta_g_mem': 'delta_w', 'delta_w_mem_kv': 'delta_w', 'delta_mem_q_norm': 'delta_w', 'delta_mem_k_norm': 'delta_w', 'delta_w_up_gdn': 'delta_w', 'delta_w_up_fox': 'delta_w', 'delta_w_up_mem': 'delta_w', 'delta_w_out': 'delta_w', 'delta_g_mlp': 'delta_w', 'delta_w_ff1': 'delta_w', 'delta_w_ff2': 'delta_w', 'new_m_g_mix': 'new_m', 'new_m_w_in': 'new_m', 'new_m_conv_w': 'new_m', 'new_m_a_log': 'new_m', 'new_m_dt_bias': 'new_m', 'new_m_gdn_norm_g': 'new_m', 'new_m_fox_b_f': 'new_m', 'new_m_fox_q_norm': 'new_m', 'new_m_fox_k_norm': 'new_m', 'new_m_g_mem': 'new_m', 'new_m_w_mem_kv': 'new_m', 'new_m_mem_q_norm': 'new_m', 'new_m_mem_k_norm': 'new_m', 'new_m_w_up_gdn': 'new_m', 'new_m_w_up_fox': 'new_m', 'new_m_w_up_mem': 'new_m', 'new_m_w_out': 'new_m', 'new_m_g_mlp': 'new_m', 'new_m_w_ff1': 'new_m', 'new_m_w_ff2': 'new_m', 'new_v_g_mix': 'new_v', 'new_v_w_in': 'new_v', 'new_v_conv_w': 'new_v', 'new_v_a_log': 'new_v', 'new_v_dt_bias': 'new_v', 'new_v_gdn_norm_g': 'new_v', 'new_v_fox_b_f': 'new_v', 'new_v_fox_q_norm': 'new_v', 'new_v_fox_k_norm': 'new_v', 'new_v_g_mem': 'new_v', 'new_v_w_mem_kv': 'new_v', 'new_v_mem_q_norm': 'new_v', 'new_v_mem_k_norm': 'new_v', 'new_v_w_up_gdn': 'new_v', 'new_v_w_up_fox': 'new_v', 'new_v_w_up_mem': 'new_v', 'new_v_w_out': 'new_v', 'new_v_g_mlp': 'new_v', 'new_v_w_ff1': 'new_v', 'new_v_w_ff2': 'new_v'}


def _forward(args):
    return _fwd_reference(*[args[k] for k in FWD_PARAMS])


def _output_shape():
    def fwd():
        inp = _fwd_setup_inputs(0)
        return _fwd_reference(*[inp[k] for k in FWD_PARAMS])
    out = _jax.eval_shape(fwd)
    return out.shape, out.dtype

N_MICROBATCH = 1
ADAM_LR = 0.001
ADAM_B1 = 0.9
ADAM_B2 = 0.999
ADAM_EPS = 1e-08
ADAM_WD = 0.01
ADAM_STEP = 10
PER_EXAMPLE_BATCH_AXIS = {'x': 0, 'mem': 0, 'loss_target': 0}
SHARED_INPUTS = []
_WEIGHT_DTYPES = {'g_mix': _jnp.float32, 'w_in': _jnp.float32, 'conv_w': _jnp.float32, 'a_log': _jnp.float32, 'dt_bias': _jnp.float32, 'gdn_norm_g': _jnp.float32, 'fox_b_f': _jnp.float32, 'fox_q_norm': _jnp.float32, 'fox_k_norm': _jnp.float32, 'g_mem': _jnp.float32, 'w_mem_kv': _jnp.float32, 'mem_q_norm': _jnp.float32, 'mem_k_norm': _jnp.float32, 'w_up_gdn': _jnp.float32, 'w_up_fox': _jnp.float32, 'w_up_mem': _jnp.float32, 'w_out': _jnp.float32, 'g_mlp': _jnp.float32, 'w_ff1': _jnp.float32, 'w_ff2': _jnp.float32}
MOMENT_SCALE = {'g_mix': 6.038664e+00, 'w_in': 1.563160e-01, 'conv_w': 8.575218e-01, 'a_log': 2.771847e+01, 'dt_bias': 2.669140e+01, 'gdn_norm_g': 5.217310e+01, 'fox_b_f': 1.247768e+02, 'fox_q_norm': 8.413898e+00, 'fox_k_norm': 8.421099e+00, 'g_mem': 2.969507e-01, 'w_mem_kv': 3.064365e-01, 'mem_q_norm': 6.863758e-01, 'mem_k_norm': 6.873950e-01, 'w_up_gdn': 1.677763e+00, 'w_up_fox': 1.621562e-01, 'w_up_mem': 3.215203e-01, 'w_out': 1.604944e+00, 'g_mlp': 9.578339e+01, 'w_ff1': 7.626193e-01, 'w_ff2': 7.991685e+00}


def _to_microbatches(a, axis):
    t = _jnp.moveaxis(a, axis, 0)
    t = t.reshape((N_MICROBATCH, t.shape[0] // N_MICROBATCH) + t.shape[1:])
    return _jnp.moveaxis(t, 1, axis + 1)


def setup_inputs(seed: int = 0) -> dict:
    inp = _fwd_setup_inputs(seed)
    key = _jax.random.fold_in(_jax.random.key(seed), 7919)
    shape, _ = _output_shape()
    out = dict(inp)
    out["loss_target"] = _jax.random.normal(_jax.random.fold_in(key, 0), shape, _jnp.float32)
    for i, name in enumerate(TWIN_WEIGHTS):
        w = inp[name].astype(_jnp.float32)
        if MOMENT_SCALE is None:
            s = _jnp.sqrt(_jnp.mean(_jnp.square(w)) + 1e-30)
        else:
            s = MOMENT_SCALE[name]
        km, kv = _jax.random.split(_jax.random.fold_in(key, i + 1))
        out[name] = w
        out["m_" + name] = s * _jax.random.normal(km, w.shape, _jnp.float32)
        out["v_" + name] = (s * s) * _jax.random.uniform(kv, w.shape, _jnp.float32, 0.5, 1.5)
    if N_MICROBATCH > 1:
        for name, axis in PER_EXAMPLE_BATCH_AXIS.items():
            out[name] = _to_microbatches(out[name], axis)
    return {'x': out['x'], 'mem': out['mem'], 'g_mix': out['g_mix'], 'w_in': out['w_in'], 'conv_w': out['conv_w'], 'a_log': out['a_log'], 'dt_bias': out['dt_bias'], 'gdn_norm_g': out['gdn_norm_g'], 'fox_b_f': out['fox_b_f'], 'fox_q_norm': out['fox_q_norm'], 'fox_k_norm': out['fox_k_norm'], 'g_mem': out['g_mem'], 'w_mem_kv': out['w_mem_kv'], 'mem_q_norm': out['mem_q_norm'], 'mem_k_norm': out['mem_k_norm'], 'w_up_gdn': out['w_up_gdn'], 'w_up_fox': out['w_up_fox'], 'w_up_mem': out['w_up_mem'], 'w_out': out['w_out'], 'g_mlp': out['g_mlp'], 'w_ff1': out['w_ff1'], 'w_ff2': out['w_ff2'], 'loss_target': out['loss_target'], 'm_g_mix': out['m_g_mix'], 'm_w_in': out['m_w_in'], 'm_conv_w': out['m_conv_w'], 'm_a_log': out['m_a_log'], 'm_dt_bias': out['m_dt_bias'], 'm_gdn_norm_g': out['m_gdn_norm_g'], 'm_fox_b_f': out['m_fox_b_f'], 'm_fox_q_norm': out['m_fox_q_norm'], 'm_fox_k_norm': out['m_fox_k_norm'], 'm_g_mem': out['m_g_mem'], 'm_w_mem_kv': out['m_w_mem_kv'], 'm_mem_q_norm': out['m_mem_q_norm'], 'm_mem_k_norm': out['m_mem_k_norm'], 'm_w_up_gdn': out['m_w_up_gdn'], 'm_w_up_fox': out['m_w_up_fox'], 'm_w_up_mem': out['m_w_up_mem'], 'm_w_out': out['m_w_out'], 'm_g_mlp': out['m_g_mlp'], 'm_w_ff1': out['m_w_ff1'], 'm_w_ff2': out['m_w_ff2'], 'v_g_mix': out['v_g_mix'], 'v_w_in': out['v_w_in'], 'v_conv_w': out['v_conv_w'], 'v_a_log': out['v_a_log'], 'v_dt_bias': out['v_dt_bias'], 'v_gdn_norm_g': out['v_gdn_norm_g'], 'v_fox_b_f': out['v_fox_b_f'], 'v_fox_q_norm': out['v_fox_q_norm'], 'v_fox_k_norm': out['v_fox_k_norm'], 'v_g_mem': out['v_g_mem'], 'v_w_mem_kv': out['v_w_mem_kv'], 'v_mem_q_norm': out['v_mem_q_norm'], 'v_mem_k_norm': out['v_mem_k_norm'], 'v_w_up_gdn': out['v_w_up_gdn'], 'v_w_up_fox': out['v_w_up_fox'], 'v_w_up_mem': out['v_w_up_mem'], 'v_w_out': out['v_w_out'], 'v_g_mlp': out['v_g_mlp'], 'v_w_ff1': out['v_w_ff1'], 'v_w_ff2': out['v_w_ff2']}


def _loss(weights, diff, rest, loss_target):
    with _jax.named_scope("forward"):
        args = {**rest, TWIN_DIFF_INPUT: diff, **{k: w.astype(_WEIGHT_DTYPES[k]) for k, w in weights.items()}}
        y = _forward(args)
    with _jax.named_scope("loss_head"):
        err = _jnp.square(y.astype(_jnp.float32) - loss_target)
        return 0.5 * _jnp.sum(_jnp.mean(err, axis=-1)) if err.ndim else 0.5 * err


def _adamw(w, g, m, v):
    m = ADAM_B1 * m + (1.0 - ADAM_B1) * g
    v = ADAM_B2 * v + (1.0 - ADAM_B2) * _jnp.square(g)
    m_hat = m / (1.0 - ADAM_B1 ** ADAM_STEP)
    v_hat = v / (1.0 - ADAM_B2 ** ADAM_STEP)
    delta = -ADAM_LR * (m_hat / (_jnp.sqrt(v_hat) + ADAM_EPS) + ADAM_WD * w)
    return delta, m, v


def reference(x, mem, g_mix, w_in, conv_w, a_log, dt_bias, gdn_norm_g, fox_b_f, fox_q_norm, fox_k_norm, g_mem, w_mem_kv, mem_q_norm, mem_k_norm, w_up_gdn, w_up_fox, w_up_mem, w_out, g_mlp, w_ff1, w_ff2, loss_target, m_g_mix, m_w_in, m_conv_w, m_a_log, m_dt_bias, m_gdn_norm_g, m_fox_b_f, m_fox_q_norm, m_fox_k_norm, m_g_mem, m_w_mem_kv, m_mem_q_norm, m_mem_k_norm, m_w_up_gdn, m_w_up_fox, m_w_up_mem, m_w_out, m_g_mlp, m_w_ff1, m_w_ff2, v_g_mix, v_w_in, v_conv_w, v_a_log, v_dt_bias, v_gdn_norm_g, v_fox_b_f, v_fox_q_norm, v_fox_k_norm, v_g_mem, v_w_mem_kv, v_mem_q_norm, v_mem_k_norm, v_w_up_gdn, v_w_up_fox, v_w_up_mem, v_w_out, v_g_mlp, v_w_ff1, v_w_ff2):
    given = dict(x=x, mem=mem, g_mix=g_mix, w_in=w_in, conv_w=conv_w, a_log=a_log, dt_bias=dt_bias, gdn_norm_g=gdn_norm_g, fox_b_f=fox_b_f, fox_q_norm=fox_q_norm, fox_k_norm=fox_k_norm, g_mem=g_mem, w_mem_kv=w_mem_kv, mem_q_norm=mem_q_norm, mem_k_norm=mem_k_norm, w_up_gdn=w_up_gdn, w_up_fox=w_up_fox, w_up_mem=w_up_mem, w_out=w_out, g_mlp=g_mlp, w_ff1=w_ff1, w_ff2=w_ff2, loss_target=loss_target, m_g_mix=m_g_mix, m_w_in=m_w_in, m_conv_w=m_conv_w, m_a_log=m_a_log, m_dt_bias=m_dt_bias, m_gdn_norm_g=m_gdn_norm_g, m_fox_b_f=m_fox_b_f, m_fox_q_norm=m_fox_q_norm, m_fox_k_norm=m_fox_k_norm, m_g_mem=m_g_mem, m_w_mem_kv=m_w_mem_kv, m_mem_q_norm=m_mem_q_norm, m_mem_k_norm=m_mem_k_norm, m_w_up_gdn=m_w_up_gdn, m_w_up_fox=m_w_up_fox, m_w_up_mem=m_w_up_mem, m_w_out=m_w_out, m_g_mlp=m_g_mlp, m_w_ff1=m_w_ff1, m_w_ff2=m_w_ff2, v_g_mix=v_g_mix, v_w_in=v_w_in, v_conv_w=v_conv_w, v_a_log=v_a_log, v_dt_bias=v_dt_bias, v_gdn_norm_g=v_gdn_norm_g, v_fox_b_f=v_fox_b_f, v_fox_q_norm=v_fox_q_norm, v_fox_k_norm=v_fox_k_norm, v_g_mem=v_g_mem, v_w_mem_kv=v_w_mem_kv, v_mem_q_norm=v_mem_q_norm, v_mem_k_norm=v_mem_k_norm, v_w_up_gdn=v_w_up_gdn, v_w_up_fox=v_w_up_fox, v_w_up_mem=v_w_up_mem, v_w_out=v_w_out, v_g_mlp=v_g_mlp, v_w_ff1=v_w_ff1, v_w_ff2=v_w_ff2)
    weights = {n: given[n] for n in TWIN_WEIGHTS}
    shared = {n: given[n] for n in SHARED_INPUTS}
    per_example = {n: given[n] for n in ['x', 'mem']}
    grad_fn = _jax.value_and_grad(_loss, argnums=(0, 1))

    def one_microbatch(ex, loss_target):
        ex = dict(ex)
        diff = ex.pop(TWIN_DIFF_INPUT)
        return grad_fn(weights, diff, {**shared, **ex}, loss_target)

    if N_MICROBATCH == 1:
        loss, (grad_w, grad_x) = one_microbatch(per_example, given["loss_target"])
    else:
        def body(carry, xs):
            loss_sum, grad_sum = carry
            l_k, (gw_k, gx_k) = one_microbatch(xs[0], xs[1])
            with _jax.named_scope("update"):
                return (loss_sum + l_k, _jax.tree.map(_jnp.add, grad_sum, gw_k)), gx_k

        init = (_jnp.zeros((), _jnp.float32), _jax.tree.map(_jnp.zeros_like, weights))
        (loss, grad_w), grad_x = _jax.lax.scan(body, init, (per_example, given["loss_target"]))
    with _jax.named_scope("update"):
        delta_w, new_m, new_v = {}, {}, {}
        for n in TWIN_WEIGHTS:
            delta_w[n], new_m[n], new_v[n] = _adamw(weights[n], grad_w[n], given["m_" + n], given["v_" + n])
    return (loss, grad_x, *[grad_w[n] for n in TWIN_WEIGHTS], *[delta_w[n] for n in TWIN_WEIGHTS],
            *[new_m[n] for n in TWIN_WEIGHTS], *[new_v[n] for n in TWIN_WEIGHTS])
```

```python
import functools

import jax
import jax.numpy as jnp
from jax import lax
from jax.experimental import pallas as pl
from jax.experimental.pallas import tpu as pltpu

F32 = jnp.float32
BF16 = jnp.bfloat16
HI = lax.Precision.HIGHEST
MESH = pl.DeviceIdType.MESH

EPS = 1e-6
HEAD = 128
MEM_DH = 256
CONV_K = 4
CHUNK = 64
LANES = 128
SUB = 8
PACK_W = 1024
PACK_ROWS = 512
VMEM_LIMIT = 56 * 1024 * 1024
NEG = -1e30
SOLVE_PREC = None

ADAM_LR, ADAM_B1, ADAM_B2, ADAM_EPS, ADAM_WD, ADAM_STEP = 0.001, 0.9, 0.999, 1e-08, 0.01, 10


def _pcall(body, **kw):
    return pl.pallas_call(body, **kw)


def _cp(*sem):
    return pltpu.CompilerParams(dimension_semantics=sem, vmem_limit_bytes=VMEM_LIMIT)


def _dot(a, b, prec=None):
    return lax.dot_general(a, b, (((1,), (0,)), ((), ())), precision=prec, preferred_element_type=F32)


def _dot_nt(a, b, prec=None):
    return lax.dot_general(a, b, (((1,), (1,)), ((), ())), precision=prec, preferred_element_type=F32)


def _dot_tn(a, b, prec=None):
    return lax.dot_general(a, b, (((0,), (0,)), ((), ())), precision=prec, preferred_element_type=F32)


def _sigmoid(x):
    return 1.0 / (1.0 + jnp.exp(-x))


def _softplus(x):
    return jnp.maximum(x, 0.0) + jnp.log(1.0 + jnp.exp(-jnp.abs(x)))


def _silu(x):
    return x * _sigmoid(x)


def _rms(x, g):
    return x * lax.rsqrt(jnp.mean(x * x, axis=-1, keepdims=True) + EPS) * g


def _ru(a, m):
    return (a + m - 1) // m * m


def _mm(a, b, mode, name, out_dtypes=(F32,), epilogue=None, extras=(), tm=1024, tn=512, tk=512):
    if mode == "nn":
        (M, K), (K2, N) = a.shape, b.shape
    elif mode == "nt":
        (M, K), (N, K2) = a.shape, b.shape
    else:
        (K, M), (K2, N) = a.shape, b.shape
    assert K == K2, (a.shape, b.shape, mode)
    tm, tn, tk = min(tm, M), min(tn, N), min(tk, K)
    assert M % tm == 0 and N % tn == 0 and K % tk == 0, (M, N, K, tm, tn, tk)
    nk = K // tk
    n_ex, n_out = len(extras), len(out_dtypes)
    dims = {"nn": ((1,), (0,)), "nt": ((1,), (1,)), "tn": ((0,), (0,))}[mode]

    def body(a_ref, b_ref, *rest):
        ex_refs, out_refs, acc = rest[:n_ex], rest[n_ex:n_ex + n_out], rest[-1]
        k = pl.program_id(2)

        @pl.when(k == 0)
        def _():
            acc[...] = jnp.zeros_like(acc)

        acc[...] += lax.dot_general(a_ref[...], b_ref[...], (dims, ((), ())), preferred_element_type=F32)

        @pl.when(k == nk - 1)
        def _():
            res = acc[...]
            outs = epilogue(res, *[r[...] for r in ex_refs]) if epilogue is not None else (res,)
            for o_ref, o in zip(out_refs, outs):
                o_ref[...] = o.astype(o_ref.dtype)

    a_spec = pl.BlockSpec((tk, tm), lambda i, j, k: (k, i)) if mode == "tn" else pl.BlockSpec((tm, tk), lambda i, j, k: (i, k))
    b_spec = pl.BlockSpec((tn, tk), lambda i, j, k: (j, k)) if mode == "nt" else pl.BlockSpec((tk, tn), lambda i, j, k: (k, j))
    mn_spec = pl.BlockSpec((tm, tn), lambda i, j, k: (i, j))
    outs = _pcall(
        body, name=name, grid=(M // tm, N // tn, nk),
        in_specs=[a_spec, b_spec] + [mn_spec] * n_ex,
        out_specs=[mn_spec] * n_out,
        out_shape=[jax.ShapeDtypeStruct((M, N), dt) for dt in out_dtypes],
        scratch_shapes=[pltpu.VMEM((tm, tn), F32)],
        compiler_params=_cp("parallel", "parallel", "arbitrary"),
    )(a, b, *extras)
    return outs[0] if n_out == 1 else outs


def _rowwise(fn, T, tb, ins, outs, name, scratch=()):
    tb = min(tb, T)
    assert T % tb == 0 and (tb % SUB == 0 or tb == T)
    nblk = T // tb
    r8 = tb // SUB
    in_specs, arrs = [], []
    for spec in ins:
        kind, arr = spec[0], spec[1]
        arrs.append(arr)
        if kind == "full":
            nd = arr.ndim
            in_specs.append(pl.BlockSpec(arr.shape, lambda i, nd=nd: (0,) * nd))
            continue
        off, w = spec[2], spec[3]
        assert off % w == 0 and arr.shape[0] == T, (name, off, w, arr.shape)
        cb = off // w
        if kind == "row":
            in_specs.append(pl.BlockSpec((tb, w), lambda i, cb=cb: (i, cb)))
        elif kind == "prev":
            in_specs.append(pl.BlockSpec((SUB, w), lambda i, cb=cb: (jnp.maximum(i * r8 - 1, 0), cb)))
        else:
            in_specs.append(pl.BlockSpec((SUB, w), lambda i, cb=cb: (jnp.minimum((i + 1) * r8, T // SUB - 1), cb)))
    out_specs, out_shapes, is_acc = [], [], []
    for spec in outs:
        if spec[0] == "row":
            out_specs.append(pl.BlockSpec((tb, spec[1]), lambda i: (i, 0)))
            out_shapes.append(jax.ShapeDtypeStruct((T, spec[1]), spec[2]))
            is_acc.append(False)
        else:
            nd = len(spec[1])
            out_specs.append(pl.BlockSpec(spec[1], lambda i, nd=nd: (0,) * nd))
            out_shapes.append(jax.ShapeDtypeStruct(spec[1], spec[2]))
            is_acc.append(True)
    n_in, n_out = len(ins), len(outs)
    seq = any(is_acc) or len(scratch) > 0

    def body(*refs):
        in_refs, out_refs, scr = refs[:n_in], refs[n_in:n_in + n_out], refs[n_in + n_out:]
        i = pl.program_id(0)
        vals = fn(i, nblk, *[r[...] for r in in_refs], *scr)
        for o_ref, v, acc in zip(out_refs, vals, is_acc):
            if acc:
                @pl.when(i == 0)
                def _(o_ref=o_ref):
                    o_ref[...] = jnp.zeros_like(o_ref)

                o_ref[...] += v.astype(o_ref.dtype)
            else:
                o_ref[...] = v.astype(o_ref.dtype)

    res = _pcall(
        body, name=name, grid=(nblk,), in_specs=in_specs, out_specs=out_specs, out_shape=out_shapes,
        scratch_shapes=list(scratch), compiler_params=_cp("arbitrary" if seq else "parallel"),
    )(*arrs)
    return res


def _heads(x, width):
    return [x[:, h * width:(h + 1) * width] for h in range(x.shape[1] // width)]


def _cat(xs):
    return xs[0] if len(xs) == 1 else jnp.concatenate(xs, axis=1)


def _rms_fwd(x, g, name, tb=512):
    T, D = x.shape
    return _rowwise(lambda i, n, xv, gv: (_rms(xv, gv),), T, tb,
                    [("row", x, 0, D), ("full", g)], [("row", D, BF16)], name)[0]


def _rms_bwd(x, g, dh, dres, name, tb=256):
    T, D = x.shape

    def fn(i, n, xv, gv, dhv, drv):
        _, vjp = jax.vjp(_rms, xv, gv)
        dx, dg = vjp(dhv)
        tot = drv + dx
        return tot, tot, dg

    return _rowwise(fn, T, tb, [("row", x, 0, D), ("full", g), ("row", dh, 0, D), ("row", dres, 0, D)],
                    [("row", D, F32), ("row", D, BF16), ("acc", (1, D), F32)], name)


def _rms_dg(x, g, dh, name, tb=256):
    T, D = x.shape

    def fn(i, n, xv, gv, dhv):
        _, vjp = jax.vjp(lambda gg: _rms(xv, gg), gv)
        return vjp(dhv)

    return _rowwise(fn, T, tb, [("row", x, 0, D), ("full", g), ("row", dh, 0, D)], [("acc", (1, D), F32)], name)[0]


def _shift_down(x, halo, s, first):
    if s == 0:
        return x
    tb, c = x.shape
    xr = pltpu.roll(x, s, 0)
    hr = jnp.where(first, 0.0, pltpu.roll(halo, s, 0))
    hfull = hr if tb == SUB else jnp.concatenate([hr, jnp.zeros((tb - SUB, c), x.dtype)], axis=0)
    row = lax.broadcasted_iota(jnp.int32, x.shape, 0)
    return jnp.where(row < s, hfull, xr)


def _shift_up(z, halo, s, last):
    if s == 0:
        return z
    tb, c = z.shape
    zr = pltpu.roll(z, tb - s, 0)
    hr = jnp.where(last, 0.0, pltpu.roll(halo, SUB - s, 0))
    hfull = hr if tb == SUB else jnp.concatenate([jnp.zeros((tb - SUB, c), z.dtype), hr], axis=0)
    row = lax.broadcasted_iota(jnp.int32, z.shape, 0)
    return jnp.where(row >= tb - s, hfull, zr)


def _conv_pre(x, halo, cw, first):
    xs = [_shift_down(x, halo, s, first) for s in range(CONV_K)]
    y = cw[0:1, :] * xs[3]
    for i in range(1, CONV_K):
        y = y + cw[i:i + 1, :] * xs[CONV_K - 1 - i]
    return y, xs


def _qk_post(y, scale):
    a = _silu(y)
    return a * lax.rsqrt(jnp.sum(a * a, axis=-1, keepdims=True) + EPS) * scale


def _small_fn(s, alog, dtb, bf, gh, fh):
    lane = lax.broadcasted_iota(jnp.int32, s.shape, 1)
    beta = _sigmoid(s)
    g = -jnp.exp(alog) * _softplus(s + dtb)
    lf = -_softplus(-(s + bf))
    return jnp.where(lane < gh, beta, jnp.where(lane < 2 * gh, g, jnp.where(lane < 2 * gh + fh, lf, 0.0)))


def _gdn_prep(proj, offs, cws, pvecs, GH, FH, GW, tb=256):
    T = proj.shape[0]
    tb = min(tb, T)
    qscale = HEAD ** -0.5

    def fn(i, n, xq, hq, xk, hk, xv, hv, cwq, cwk, cwv, s, alog, dtb, bf, carry):
        first = i == 0
        yq, _ = _conv_pre(xq, hq, cwq, first)
        yk, _ = _conv_pre(xk, hk, cwk, first)
        yv, _ = _conv_pre(xv, hv, cwv, first)
        qn = _cat([_qk_post(y, qscale) for y in _heads(yq, HEAD)])
        kn = _cat([_qk_post(y, 1.0) for y in _heads(yk, HEAD)])
        vc = _silu(yv)
        gsm = _small_fn(s, alog, dtb, bf, GH, FH)

        @pl.when(first)
        def _():
            carry[...] = jnp.zeros_like(carry)

        ri = lax.broadcasted_iota(jnp.int32, (tb, tb), 0)
        ci = lax.broadcasted_iota(jnp.int32, (tb, tb), 1)
        cum = _dot((ri >= ci).astype(F32), gsm, HI) + carry[0:1, :]
        carry[...] += _dot(jnp.ones((SUB, tb), F32), gsm, HI)
        return qn, kn, vc, gsm, cum

    ins = []
    for key in ("q", "k", "v"):
        ins += [("row", proj, offs[key], GW), ("prev", proj, offs[key], GW)]
    ins += [("full", c) for c in cws] + [("row", proj, offs["small"], LANES)] + [("full", p) for p in pvecs]
    outs = [("row", GW, F32)] * 3 + [("row", LANES, F32)] * 2
    return _rowwise(fn, T, tb, ins, outs, "gdn_prep", scratch=[pltpu.VMEM((SUB, LANES), F32)])


def _gdn_prep_bwd_a(proj, offs, cws, pvecs, cts, dgsm_scan, dlf_sm, GH, FH, GW, tb=256):
    T = proj.shape[0]
    qscale = HEAD ** -0.5

    def one(x, halo, cw, ct, first, post):
        y, xs = _conv_pre(x, halo, cw, first)
        if post is None:
            _, vjp = jax.vjp(_silu, y)
            dy = vjp(ct)[0]
        else:
            dys = []
            for yh, cth in zip(_heads(y, HEAD), _heads(ct, HEAD)):
                _, vjp = jax.vjp(lambda t: _qk_post(t, post), yh)
                dys.append(vjp(cth)[0])
            dy = _cat(dys)
        row = lax.broadcasted_iota(jnp.int32, (SUB, x.shape[1]), 0)
        dcw = jnp.zeros((SUB, x.shape[1]), F32)
        for i in range(CONV_K):
            dcw = dcw + jnp.where(row == i, jnp.sum(dy * xs[CONV_K - 1 - i], axis=0, keepdims=True), 0.0)
        return dy, dcw

    def fn(i, n, xq, hq, xk, hk, xv, hv, cwq, cwk, cwv, cq, ck, cv, s, alog, dtb, bf, d1, d2):
        first = i == 0
        dyq, dcq = one(xq, hq, cwq, cq, first, qscale)
        dyk, dck = one(xk, hk, cwk, ck, first, 1.0)
        dyv, dcv = one(xv, hv, cwv, cv, first, None)
        _, vjp = jax.vjp(lambda a, b, c, d: _small_fn(a, b, c, d, GH, FH), s, alog, dtb, bf)
        ds, dalog, ddtb, dbf = vjp(d1 + d2)
        return dyq, dyk, dyv, dcq, dck, dcv, ds, dalog, ddtb, dbf

    ins = []
    for key in ("q", "k", "v"):
        ins += [("row", proj, offs[key], GW), ("prev", proj, offs[key], GW)]
    ins += [("full", c) for c in cws] + [("row", c, 0, GW) for c in cts]
    ins += [("row", proj, offs["small"], LANES)] + [("full", p) for p in pvecs]
    ins += [("row", dgsm_scan, 0, LANES), ("row", dlf_sm, 0, LANES)]
    outs = [("row", GW, F32)] * 3 + [("acc", (SUB, GW), F32)] * 3 + [("row", LANES, BF16)] + [("acc", (1, LANES), F32)] * 3
    return _rowwise(fn, T, tb, ins, outs, "gdn_prep_bwd_a")


def _gdn_prep_bwd_b(dys, cws, GW, tb=256):
    T = dys[0].shape[0]

    def fn(i, n, dq, nq, dk, nk, dv, nv, cwq, cwk, cwv):
        last = i == n - 1
        res = []
        for dy, nh, cw in ((dq, nq, cwq), (dk, nk, cwk), (dv, nv, cwv)):
            dx = cw[CONV_K - 1:CONV_K, :] * dy
            for t in range(CONV_K - 1):
                dx = dx + cw[t:t + 1, :] * _shift_up(dy, nh, CONV_K - 1 - t, last)
            res.append(dx)
        return tuple(res)

    ins = []
    for dy in dys:
        ins += [("row", dy, 0, GW), ("next", dy, 0, GW)]
    ins += [("full", c) for c in cws]
    return _rowwise(fn, T, tb, ins, [("row", GW, BF16)] * 3, "gdn_prep_bwd_b")


def _gdn_chunk(q, k, v, gcol, bcol, s0):
    c, d = q.shape
    ri = lax.broadcasted_iota(jnp.int32, (c, c), 0)
    ci = lax.broadcasted_iota(jnp.int32, (c, c), 1)
    incl, strict = ri >= ci, ri > ci
    eye = (ri == ci).astype(F32)
    lincl = incl.astype(F32)
    g_cc = gcol * jnp.ones((c, c), F32)
    g_cd = gcol * jnp.ones((c, d), F32)
    gam_cc = _dot(lincl, g_cc, HI)
    gam_t = _dot_nt(eye, gam_cc, HI)
    gam_cd = _dot(lincl, g_cd, HI)
    tot_cd = _dot(jnp.ones((c, c), F32), g_cd, HI)
    tot_dd = _dot(jnp.ones((d, c), F32), g_cd, HI)
    diff = gam_cc - gam_t
    dec_s = jnp.where(strict, jnp.exp(jnp.where(strict, diff, 0.0)), 0.0)
    dec_i = jnp.where(incl, jnp.exp(jnp.where(incl, diff, 0.0)), 0.0)
    m = bcol * _dot_nt(k, k) * dec_s
    b16 = (ri >> 4) == (ci >> 4)
    b32 = (ri >> 5) == (ci >> 5)
    m16 = jnp.where(b16, m, 0.0)
    m32 = jnp.where(b32 & ~b16, m, 0.0)
    m64 = jnp.where(b32, 0.0, m)
    p = eye - m16
    mp = m16
    for _ in range(3):
        mp = _dot(mp, mp, SOLVE_PREC)
        p = p + _dot(p, mp, SOLVE_PREC)
    p = p - _dot(_dot(p, m32, SOLVE_PREC), p, SOLVE_PREC)
    ainv = p - _dot(_dot(p, m64, SOLVE_PREC), p, SOLVE_PREC)
    eg = jnp.exp(gam_cd)
    w = _dot(ainv, bcol * eg * k, SOLVE_PREC)
    u0 = _dot(ainv, bcol * v, SOLVE_PREC)
    qk = _dot_nt(q, k) * dec_i
    u = u0 - _dot(w, s0)
    o = _dot(q * eg, s0) + _dot(qk, u)
    s1 = jnp.exp(tot_dd) * s0 + _dot_tn(k * jnp.exp(tot_cd - gam_cd), u)
    return o, s1


def _lane_col(x, lane_idx):
    lane = lax.broadcasted_iota(jnp.int32, x.shape, 1)
    return jnp.sum(jnp.where(lane == lane_idx, x, 0.0), axis=1, keepdims=True)


def _gdn_scan_fwd(qn, kn, vc, gsm, GH):
    T, GW = qn.shape
    nc = T // CHUNK

    def body(q_ref, k_ref, v_ref, g_ref, o_ref, sall_ref, s_scr):
        @pl.when(pl.program_id(0) == 0)
        def _():
            s_scr[...] = jnp.zeros_like(s_scr)

        gs = g_ref[...]
        for h in range(GH):
            sl = slice(h * HEAD, (h + 1) * HEAD)
            s0 = s_scr[h]
            sall_ref[0, h] = s0
            o, s1 = _gdn_chunk(q_ref[:, sl], k_ref[:, sl], v_ref[:, sl], _lane_col(gs, GH + h), _lane_col(gs, h), s0)
            o_ref[:, sl] = o
            s_scr[h] = s1

    row = pl.BlockSpec((CHUNK, GW), lambda i: (i, 0))
    return _pcall(
        body, name="gdn_scan_fwd", grid=(nc,),
        in_specs=[row, row, row, pl.BlockSpec((CHUNK, LANES), lambda i: (i, 0))],
        out_specs=[row, pl.BlockSpec((1, GH, HEAD, HEAD), lambda i: (i, 0, 0, 0))],
        out_shape=[jax.ShapeDtypeStruct((T, GW), F32), jax.ShapeDtypeStruct((nc, GH, HEAD, HEAD), F32)],
        scratch_shapes=[pltpu.VMEM((GH, HEAD, HEAD), F32)],
        compiler_params=_cp("arbitrary"),
    )(qn, kn, vc, gsm)


def _gdn_scan_bwd(qn, kn, vc, gsm, sall, do, GH):
    T, GW = qn.shape
    nc = T // CHUNK

    def body(q_ref, k_ref, v_ref, g_ref, sall_ref, do_ref, dq_ref, dk_ref, dv_ref, dg_ref, ds_scr):
        @pl.when(pl.program_id(0) == 0)
        def _():
            ds_scr[...] = jnp.zeros_like(ds_scr)

        gs = g_ref[...]
        lane = lax.broadcasted_iota(jnp.int32, gs.shape, 1)
        dgs = jnp.zeros_like(gs)
        for h in range(GH):
            sl = slice(h * HEAD, (h + 1) * HEAD)
            _, vjp = jax.vjp(_gdn_chunk, q_ref[:, sl], k_ref[:, sl], v_ref[:, sl],
                             _lane_col(gs, GH + h), _lane_col(gs, h), sall_ref[0, h])
            dq, dk, dv, dgc, dbc, ds0 = vjp((do_ref[:, sl], ds_scr[h]))
            dq_ref[:, sl] = dq
            dk_ref[:, sl] = dk
            dv_ref[:, sl] = dv
            dgs = dgs + jnp.where(lane == h, dbc, 0.0) + jnp.where(lane == GH + h, dgc, 0.0)
            ds_scr[h] = ds0
        dg_ref[...] = dgs

    row = pl.BlockSpec((CHUNK, GW), lambda i: (nc - 1 - i, 0))
    sm = pl.BlockSpec((CHUNK, LANES), lambda i: (nc - 1 - i, 0))
    return _pcall(
        body, name="gdn_scan_bwd", grid=(nc,),
        in_specs=[row, row, row, sm, pl.BlockSpec((1, GH, HEAD, HEAD), lambda i: (nc - 1 - i, 0, 0, 0)), row],
        out_specs=[row, row, row, sm],
        out_shape=[jax.ShapeDtypeStruct((T, GW), F32)] * 3 + [jax.ShapeDtypeStruct((T, LANES), F32)],
        scratch_shapes=[pltpu.VMEM((GH, HEAD, HEAD), F32)],
        compiler_params=_cp("arbitrary"),
    )(qn, kn, vc, gsm, sall, do)


def _gdn_post_fn(o, z, g):
    return _rms(o, g) * _silu(z)


def _gdn_post(o, proj, off_z, g, GW, tb=512):
    T = o.shape[0]

    def fn(i, n, ov, zv, gv):
        return (_cat([_gdn_post_fn(a, b, gv) for a, b in zip(_heads(ov, HEAD), _heads(zv, HEAD))]),)

    return _rowwise(fn, T, tb, [("row", o, 0, GW), ("row", proj, off_z, GW), ("full", g)], [("row", GW, BF16)], "gdn_post")[0]


def _gdn_post_bwd(o, proj, off_z, g, dout, GW, tb=256):
    T = o.shape[0]

    def fn(i, n, ov, zv, gv, dv):
        dos, dzs, dg = [], [], jnp.zeros_like(gv)
        for a, b, c in zip(_heads(ov, HEAD), _heads(zv, HEAD), _heads(dv, HEAD)):
            _, vjp = jax.vjp(_gdn_post_fn, a, b, gv)
            da, db, dgh = vjp(c)
            dos.append(da)
            dzs.append(db)
            dg = dg + dgh
        return _cat(dos), _cat(dzs), dg

    return _rowwise(fn, T, tb, [("row", o, 0, GW), ("row", proj, off_z, GW), ("full", g), ("row", dout, 0, GW)],
                    [("row", GW, F32), ("row", GW, BF16), ("acc", (1, HEAD), F32)], "gdn_post_bwd")


def _fox_prep(proj, offs, gq, gk, FW, tb=512):
    T = proj.shape[0]

    def fn(i, n, q, k, v, gqv, gkv):
        return (_cat([_rms(a, gqv) for a in _heads(q, HEAD)]), _cat([_rms(a, gkv) for a in _heads(k, HEAD)]), v)

    return _rowwise(fn, T, tb, [("row", proj, offs["fq"], FW), ("row", proj, offs["fk"], FW), ("row", proj, offs["fv"], FW),
                                ("full", gq), ("full", gk)], [("row", FW, BF16)] * 3, "fox_prep")


def _fox_prep_bwd(proj, offs, gq, gk, dq, dk, dv, FW, tb=256):
    T = proj.shape[0]

    def fn(i, n, q, k, gqv, gkv, dqv, dkv, dvv):
        res = []
        for x, g, d in ((q, gqv, dqv), (k, gkv, dkv)):
            dxs, dg = [], jnp.zeros_like(g)
            for a, c in zip(_heads(x, HEAD), _heads(d, HEAD)):
                _, vjp = jax.vjp(_rms, a, g)
                da, dgh = vjp(c)
                dxs.append(da)
                dg = dg + dgh
            res += [_cat(dxs), dg]
        return res[0], res[2], dvv, res[1], res[3]

    return _rowwise(fn, T, tb, [("row", proj, offs["fq"], FW), ("row", proj, offs["fk"], FW), ("full", gq), ("full", gk),
                                ("row", dq, 0, FW), ("row", dk, 0, FW), ("row", dv, 0, FW)],
                    [("row", FW, BF16)] * 3 + [("acc", (1, HEAD), F32)] * 2, "fox_prep_bwd")


def _sub_row(x, sub_idx):
    sub = lax.broadcasted_iota(jnp.int32, x.shape, 0)
    return jnp.sum(jnp.where(sub == sub_idx, x, 0.0), axis=0, keepdims=True)


def _fox_scores(q, k, cum_q, cumt_k, lane0, h, qi, kj, blk):
    cq = _lane_col(cum_q, lane0 + h)
    ck = _sub_row(cumt_k, (lane0 + h) % SUB)
    s = _dot_nt(q, k) * (HEAD ** -0.5) + (cq - ck)
    row = lax.broadcasted_iota(jnp.int32, s.shape, 0) + qi * blk
    col = lax.broadcasted_iota(jnp.int32, s.shape, 1) + kj * blk
    return jnp.where(row >= col, s, NEG)


def _flash_fwd(qb, kb, vb, cum, cumt, lane0, FH, blk):
    T, FW = qb.shape
    blk = min(blk, T)
    nb = T // blk

    def body(q_ref, k_ref, v_ref, cq_ref, ck_ref, ob_ref, lse_ref, m_scr, l_scr, acc):
        h, qi, kj = pl.program_id(0), pl.program_id(1), pl.program_id(2)

        @pl.when(kj == 0)
        def _():
            m_scr[...] = jnp.full_like(m_scr, NEG)
            l_scr[...] = jnp.zeros_like(l_scr)
            acc[...] = jnp.zeros_like(acc)

        @pl.when(kj <= qi)
        def _():
            s = _fox_scores(q_ref[...], k_ref[...], cq_ref[...], ck_ref[...], lane0, h, qi, kj, blk)
            m_old = m_scr[...]
            m_new = jnp.maximum(m_old, jnp.max(s, axis=1, keepdims=True))
            alpha = jnp.exp(m_old - m_new)
            p = jnp.exp(s - m_new)
            l_scr[...] = alpha * l_scr[...] + jnp.sum(p, axis=1, keepdims=True)
            acc[...] = alpha * acc[...] + _dot(p.astype(BF16), v_ref[...])
            m_scr[...] = m_new

        @pl.when(kj == qi)
        def _():
            ob_ref[...] = (acc[...] / l_scr[...]).astype(BF16)
            lse_ref[0] = m_scr[...] + jnp.log(l_scr[...])

    qs = pl.BlockSpec((blk, HEAD), lambda h, i, j: (i, h))
    ks = pl.BlockSpec((blk, HEAD), lambda h, i, j: (jnp.minimum(j, i), h))
    return _pcall(
        body, name="flash_fwd", grid=(FH, nb, nb),
        in_specs=[qs, ks, ks, pl.BlockSpec((blk, LANES), lambda h, i, j: (i, 0)),
                  pl.BlockSpec((SUB, blk), lambda h, i, j: ((lane0 + h) // SUB, jnp.minimum(j, i)))],
        out_specs=[qs, pl.BlockSpec((1, blk, 1), lambda h, i, j: (h, i, 0))],
        out_shape=[jax.ShapeDtypeStruct((T, FW), BF16), jax.ShapeDtypeStruct((FH, T, 1), F32)],
        scratch_shapes=[pltpu.VMEM((blk, 1), F32), pltpu.VMEM((blk, 1), F32), pltpu.VMEM((blk, HEAD), F32)],
        compiler_params=_cp("parallel", "parallel", "arbitrary"),
    )(qb, kb, vb, cum, cumt)


def _flash_bwd_q(qb, kb, vb, cum, cumt, lse, do, dl, lane0, FH, blk):
    T, FW = qb.shape
    blk = min(blk, T)
    nb = T // blk
    want_dq = dl is not None

    def body(q_ref, k_ref, v_ref, cq_ref, ck_ref, lse_ref, do_ref, *rest):
        if want_dq:
            dl_ref, out_ref, rs_ref, acc, rs_acc = rest
        else:
            out_ref, acc = rest
        h, qi, kj = pl.program_id(0), pl.program_id(1), pl.program_id(2)

        @pl.when(kj == 0)
        def _():
            acc[...] = jnp.zeros_like(acc)
            if want_dq:
                rs_acc[...] = jnp.zeros_like(rs_acc)

        @pl.when(kj <= qi)
        def _():
            s = _fox_scores(q_ref[...], k_ref[...], cq_ref[...], ck_ref[...], lane0, h, qi, kj, blk)
            p = jnp.exp(s - lse_ref[0])
            dp = _dot_nt(do_ref[...].astype(BF16), v_ref[...])
            if want_dq:
                ds = p * (dp - dl_ref[0])
                acc[...] += _dot(ds.astype(BF16), k_ref[...])
                rs_acc[...] += jnp.sum(ds, axis=1, keepdims=True)
            else:
                acc[...] += jnp.sum(p * dp, axis=1, keepdims=True)

        @pl.when(kj == qi)
        def _():
            if want_dq:
                out_ref[...] = acc[...] * (HEAD ** -0.5)
                rs_ref[0] = rs_acc[...]
            else:
                out_ref[0] = acc[...]

    qs = pl.BlockSpec((blk, HEAD), lambda h, i, j: (i, h))
    ks = pl.BlockSpec((blk, HEAD), lambda h, i, j: (jnp.minimum(j, i), h))
    col = pl.BlockSpec((1, blk, 1), lambda h, i, j: (h, i, 0))
    in_specs = [qs, ks, ks, pl.BlockSpec((blk, LANES), lambda h, i, j: (i, 0)),
                pl.BlockSpec((SUB, blk), lambda h, i, j: ((lane0 + h) // SUB, jnp.minimum(j, i))), col, qs]
    args = [qb, kb, vb, cum, cumt, lse, do]
    if want_dq:
        return _pcall(
            body, name="flash_bwd_dq", grid=(FH, nb, nb), in_specs=in_specs + [col], out_specs=[qs, col],
            out_shape=[jax.ShapeDtypeStruct((T, FW), F32), jax.ShapeDtypeStruct((FH, T, 1), F32)],
            scratch_shapes=[pltpu.VMEM((blk, HEAD), F32), pltpu.VMEM((blk, 1), F32)],
            compiler_params=_cp("parallel", "parallel", "arbitrary"))(*args, dl)
    return _pcall(
        body, name="flash_bwd_rowterm", grid=(FH, nb, nb), in_specs=in_specs, out_specs=col,
        out_shape=jax.ShapeDtypeStruct((FH, T, 1), F32), scratch_shapes=[pltpu.VMEM((blk, 1), F32)],
        compiler_params=_cp("parallel", "parallel", "arbitrary"))(*args)


def _flash_bwd_dkv(qb, kb, vb, cum, cumt, lse_row, dl_row, do, lane0, FH, blk):
    T, FW = qb.shape
    blk = min(blk, T)
    nb = T // blk

    def body(q_ref, k_ref, v_ref, ck_ref, cq_ref, lse_ref, dl_ref, do_ref, dk_ref, dv_ref, dc_ref, dk_acc, dv_acc, dc_acc):
        h, kj, qq = pl.program_id(0), pl.program_id(1), pl.program_id(2)
        qi = jnp.maximum(qq, kj)

        @pl.when(qq == 0)
        def _():
            dk_acc[...] = jnp.zeros_like(dk_acc)
            dv_acc[...] = jnp.zeros_like(dv_acc)
            dc_acc[...] = jnp.zeros_like(dc_acc)

        @pl.when(qq >= kj)
        def _():
            ck = _lane_col(ck_ref[...], lane0 + h)
            cq = _sub_row(cq_ref[...], (lane0 + h) % SUB)
            st = _dot_nt(k_ref[...], q_ref[...]) * (HEAD ** -0.5) + (cq - ck)
            krow = lax.broadcasted_iota(jnp.int32, st.shape, 0) + kj * blk
            qcol = lax.broadcasted_iota(jnp.int32, st.shape, 1) + qi * blk
            st = jnp.where(qcol >= krow, st, NEG)
            pt = jnp.exp(st - lse_ref[0])
            dob = do_ref[...].astype(BF16)
            dv_acc[...] += _dot(pt.astype(BF16), dob)
            dpt = _dot_nt(v_ref[...], dob)
            dst = pt * (dpt - dl_ref[0])
            dk_acc[...] += _dot(dst.astype(BF16), q_ref[...])
            dc_acc[...] -= jnp.sum(dst, axis=1, keepdims=True)

        @pl.when(qq == nb - 1)
        def _():
            dk_ref[...] = dk_acc[...] * (HEAD ** -0.5)
            dv_ref[...] = dv_acc[...]
            dc_ref[0] = dc_acc[...]

    ks = pl.BlockSpec((blk, HEAD), lambda h, j, i: (j, h))
    qs = pl.BlockSpec((blk, HEAD), lambda h, j, i: (jnp.maximum(i, j), h))
    rowq = pl.BlockSpec((1, 1, blk), lambda h, j, i: (h, 0, jnp.maximum(i, j)))
    return _pcall(
        body, name="flash_bwd_dkv", grid=(FH, nb, nb),
        in_specs=[qs, ks, ks, pl.BlockSpec((blk, LANES), lambda h, j, i: (j, 0)),
                  pl.BlockSpec((SUB, blk), lambda h, j, i: ((lane0 + h) // SUB, jnp.maximum(i, j))), rowq, rowq, qs],
        out_specs=[ks, ks, pl.BlockSpec((1, blk, 1), lambda h, j, i: (h, j, 0))],
        out_shape=[jax.ShapeDtypeStruct((T, FW), F32), jax.ShapeDtypeStruct((T, FW), F32),
                   jax.ShapeDtypeStruct((FH, T, 1), F32)],
        scratch_shapes=[pltpu.VMEM((blk, HEAD), F32), pltpu.VMEM((blk, HEAD), F32), pltpu.VMEM((blk, 1), F32)],
        compiler_params=_cp("parallel", "parallel", "arbitrary"),
    )(qb, kb, vb, cum, cumt, lse_row, dl_row, do)


def _rev_cumsum_rows(r1, r2, tb=512):
    H, T = r1.shape
    tb = min(tb, T)
    nb = T // tb

    def body(r1_ref, r2_ref, o_ref, carry):
        @pl.when(pl.program_id(0) == 0)
        def _():
            carry[...] = jnp.zeros_like(carry)

        rv = r1_ref[...] + r2_ref[...]
        si = lax.broadcasted_iota(jnp.int32, (tb, tb), 0)
        ti = lax.broadcasted_iota(jnp.int32, (tb, tb), 1)
        o_ref[...] = _dot(rv, (si >= ti).astype(F32), HI) + carry[...]
        carry[...] += jnp.sum(rv, axis=1, keepdims=True)

    spec = pl.BlockSpec((H, tb), lambda i: (0, nb - 1 - i))
    return _pcall(body, name="rev_cumsum", grid=(nb,), in_specs=[spec, spec], out_specs=spec,
                  out_shape=jax.ShapeDtypeStruct((H, T), F32), scratch_shapes=[pltpu.VMEM((H, 1), F32)],
                  compiler_params=_cp("arbitrary"))(r1, r2)


def _mem_head(q, k, v, gq, gk):
    logits = _dot_nt(_rms(q, gq), _rms(k, gk)) * (MEM_DH ** -0.5)
    mx = jnp.max(logits, axis=1, keepdims=True)
    e = jnp.exp(logits - mx)
    p = e / jnp.sum(e, axis=1, keepdims=True)
    return _dot(p, v)


def _mem_attn(proj, off_q, kv, gq, gk, MW, tb=512):
    T = proj.shape[0]
    MH = MW // MEM_DH

    def fn(i, n, q, kvv, gqv, gkv):
        ks, vs = _heads(kvv[:, :MW], MEM_DH), _heads(kvv[:, MW:], MEM_DH)
        return (_cat([_mem_head(a, b, c, gqv, gkv) for a, b, c in zip(_heads(q, MEM_DH), ks, vs)]),)

    return _rowwise(fn, T, tb, [("row", proj, off_q, MW), ("full", kv), ("full", gq), ("full", gk)],
                    [("row", MW, BF16)], "mem_attn")[0]


def _mem_attn_bwd(proj, off_q, kv, gq, gk, dout, MW, tb=256):
    T = proj.shape[0]
    ML = kv.shape[0]

    def fn(i, n, q, kvv, gqv, gkv, dv):
        ks, vs = _heads(kvv[:, :MW], MEM_DH), _heads(kvv[:, MW:], MEM_DH)
        dqs, dks, dvs = [], [], []
        dgq, dgk = jnp.zeros_like(gqv), jnp.zeros_like(gkv)
        for a, b, c, d in zip(_heads(q, MEM_DH), ks, vs, _heads(dv, MEM_DH)):
            _, vjp = jax.vjp(_mem_head, a, b, c, gqv, gkv)
            da, db, dc, dg1, dg2 = vjp(d)
            dqs.append(da)
            dks.append(db)
            dvs.append(dc)
            dgq, dgk = dgq + dg1, dgk + dg2
        return _cat(dqs), _cat(dks + dvs), dgq, dgk

    return _rowwise(fn, T, tb, [("row", proj, off_q, MW), ("full", kv), ("full", gq), ("full", gk), ("row", dout, 0, MW)],
                    [("row", MW, BF16), ("acc", (ML, 2 * MW), F32), ("acc", (1, MEM_DH), F32), ("acc", (1, MEM_DH), F32)],
                    "mem_attn_bwd")


def _merge_fn(ga, gb, gm, ua, ub, um):
    return _sigmoid(ga) * ua + _sigmoid(gb) * ub + _sigmoid(gm) * um


def _merge(proj, offs, ua, ub, um, D, tb=256):
    T = proj.shape[0]
    ins = [("row", proj, offs[k], D) for k in ("ga", "gb", "gm")] + [("row", u, 0, D) for u in (ua, ub, um)]
    return _rowwise(lambda i, n, *v: (_merge_fn(*v),), T, tb, ins, [("row", D, BF16)], "merge")[0]


def _merge_bwd(proj, offs, ua, ub, um, dy, D, tb=256):
    T = proj.shape[0]

    def fn(i, n, *v):
        _, vjp = jax.vjp(_merge_fn, *v[:6])
        return vjp(v[6])

    ins = [("row", proj, offs[k], D) for k in ("ga", "gb", "gm")] + [("row", u, 0, D) for u in (ua, ub, um)] + [("row", dy, 0, D)]
    return _rowwise(fn, T, tb, ins, [("row", D, BF16)] * 6, "merge_bwd")


def _loss_grad(x2, tgt, tb=256):
    T, D = x2.shape

    def fn(i, n, a, b):
        e = a - b
        part = jnp.sum(jnp.sum(e * e, axis=1, keepdims=True), axis=0, keepdims=True) * (0.5 / D)
        g = e * (1.0 / D)
        return g, g, part + jnp.zeros((SUB, LANES), F32)

    return _rowwise(fn, T, tb, [("row", x2, 0, D), ("row", tgt, 0, D)],
                    [("row", D, F32), ("row", D, BF16), ("acc", (SUB, LANES), F32)], "loss_grad")


def _adamw(w, g, m, v, name, tb=128):
    R, C = w.shape
    c1 = 1.0 / (1.0 - ADAM_B1 ** ADAM_STEP)
    c2 = 1.0 / (1.0 - ADAM_B2 ** ADAM_STEP)

    def fn(i, n, wv, gv, mv, vv):
        mn = ADAM_B1 * mv + (1.0 - ADAM_B1) * gv
        vn = ADAM_B2 * vv + (1.0 - ADAM_B2) * (gv * gv)
        delta = -ADAM_LR * ((mn * c1) / (jnp.sqrt(vn * c2) + ADAM_EPS) + ADAM_WD * wv)
        return delta, mn, vn

    return _rowwise(fn, R, tb, [("row", a, 0, C) for a in (w, g, m, v)], [("row", C, F32)] * 3, name)


def _coords():
    return lax.axis_index("x"), lax.axis_index("y"), lax.axis_index("c")


def _allgather_small(blk):
    m_per, n = blk.shape

    def body(x_ref, out_ref, send_sems, recv_sems, local_sem):
        x, y, c = _coords()
        me, sibling = (x, y, c), (x, y, 1 - c)
        chips = [(1 - x, y), (x, 1 - y), (1 - x, 1 - y)]

        def rows(px, py, pc):
            return out_ref.at[pl.ds((4 * px + 2 * py + pc) * m_per, m_per), :]

        def copy(k, block, to, src=None):
            return pltpu.make_async_remote_copy(
                src_ref=rows(*block) if src is None else src, dst_ref=rows(*block),
                send_sem=send_sems.at[k], recv_sem=recv_sems.at[k], device_id=to, device_id_type=MESH)

        mine = pltpu.make_async_copy(x_ref, rows(*me), local_sem)
        mine.start()
        first = [copy(0, me, sibling, src=x_ref)]
        first += [copy(1 + j, me, (*chip, c), src=x_ref) for j, chip in enumerate(chips)]
        for cp in first:
            cp.start()
        passed = [copy(4 + j, (*chip, c), sibling) for j, chip in enumerate(chips)]
        for j, chip in enumerate(chips):
            copy(1 + j, (*chip, c), me).wait_recv()
            passed[j].start()
        copy(0, sibling, me).wait_recv()
        for j, chip in enumerate(chips):
            copy(4 + j, (*chip, 1 - c), me).wait_recv()
        for cp in first + passed:
            cp.wait_send()
        mine.wait()

    return _pcall(
        body, name="allgather_small", out_shape=jax.ShapeDtypeStruct((8 * m_per, n), blk.dtype),
        in_specs=[pl.BlockSpec(memory_space=pltpu.VMEM)], out_specs=pl.BlockSpec(memory_space=pltpu.VMEM),
        scratch_shapes=[pltpu.SemaphoreType.DMA((7,)), pltpu.SemaphoreType.DMA((7,)), pltpu.SemaphoreType.DMA],
        compiler_params=pltpu.CompilerParams(vmem_limit_bytes=VMEM_LIMIT),
    )(blk)


def _sum8(g, m_per):
    n = g.shape[1]

    def body(g_ref, o_ref):
        acc = g_ref[pl.ds(0, m_per), :]
        for d in range(1, 8):
            acc = acc + g_ref[pl.ds(d * m_per, m_per), :]
        o_ref[...] = acc

    return _pcall(body, name="sum8", out_shape=jax.ShapeDtypeStruct((m_per, n), g.dtype))(g)


_ANY = pl.BlockSpec(memory_space=pl.ANY)


def _allgather_chips(buf):
    nr, w = buf.shape
    half = nr // 2

    def body(in_ref, out_ref, send_sems, recv_sems, local_sem):
        x, y, c = _coords()
        me = 2 * x + y
        chips = [(1 - x, y), (x, 1 - y), (1 - x, 1 - y)]
        mine_rows = pl.ds(pl.multiple_of(c * half, 16), half)
        other_rows = pl.ds(pl.multiple_of((1 - c) * half, 16), half)

        def copy(k, src, dst, to):
            return pltpu.make_async_remote_copy(src_ref=src, dst_ref=dst, send_sem=send_sems.at[k],
                                                recv_sem=recv_sems.at[k], device_id=to, device_id_type=MESH)

        local = pltpu.make_async_copy(in_ref, out_ref.at[me], local_sem)
        local.start()
        first = [copy(j, in_ref.at[mine_rows], out_ref.at[me, mine_rows], (cx, cy, c)) for j, (cx, cy) in enumerate(chips)]
        for cp in first:
            cp.start()
        passed = []
        for j, (cx, cy) in enumerate(chips):
            slot = out_ref.at[2 * cx + cy, mine_rows]
            copy(j, slot, slot, (cx, cy, c)).wait_recv()
            fwd = copy(3 + j, slot, slot, (x, y, 1 - c))
            fwd.start()
            passed.append(fwd)
        for j, (cx, cy) in enumerate(chips):
            slot = out_ref.at[2 * cx + cy, other_rows]
            copy(3 + j, slot, slot, (x, y, 1 - c)).wait_recv()
        for cp in first + passed:
            cp.wait_send()
        local.wait()

    return _pcall(
        body, name="allgather_chips", out_shape=jax.ShapeDtypeStruct((4, nr, w), buf.dtype),
        in_specs=[_ANY], out_specs=_ANY,
        scratch_shapes=[pltpu.SemaphoreType.DMA((6,)), pltpu.SemaphoreType.DMA((6,)), pltpu.SemaphoreType.DMA],
    )(buf)


def _rs_pair_exchange(g):
    _, nr, w = g.shape
    half = nr // 2

    def body(g_ref, rb_ref, send_sem, recv_sem):
        x, y, c = _coords()
        other_rows = pl.ds(pl.multiple_of((1 - c) * half, SUB), half)
        cp = pltpu.make_async_remote_copy(src_ref=g_ref.at[:, other_rows], dst_ref=rb_ref, send_sem=send_sem,
                                          recv_sem=recv_sem, device_id=(x, y, 1 - c), device_id_type=MESH)
        cp.start()
        cp.wait()

    return _pcall(body, name="rs_pair_exchange", out_shape=jax.ShapeDtypeStruct((4, half, w), g.dtype),
                  in_specs=[_ANY], out_specs=_ANY,
                  scratch_shapes=[pltpu.SemaphoreType.DMA, pltpu.SemaphoreType.DMA])(g)


def _rs_pair_add(g, rb, cidx, tb=256):
    _, nr, w = g.shape
    half = nr // 2
    tb = min(tb, half)
    assert half % tb == 0
    hb = half // tb

    def body(c_ref, g_ref, r_ref, o_ref):
        o_ref[...] = g_ref[...] + r_ref[...]

    gs = pltpu.PrefetchScalarGridSpec(
        num_scalar_prefetch=1, grid=(4, hb),
        in_specs=[pl.BlockSpec((1, tb, w), lambda j, i, c_ref: (j, c_ref[0] * hb + i, 0)),
                  pl.BlockSpec((1, tb, w), lambda j, i, c_ref: (j, i, 0))],
        out_specs=pl.BlockSpec((1, tb, w), lambda j, i, c_ref: (j, i, 0)))
    return _pcall(body, name="rs_pair_add", grid_spec=gs, out_shape=jax.ShapeDtypeStruct((4, half, w), g.dtype),
                  compiler_params=_cp("parallel", "parallel"))(cidx, g, rb)


def _rs_chip_exchange(p):
    _, h, w = p.shape

    def body(p_ref, rb_ref, send_sems, recv_sems, local_sem):
        x, y, c = _coords()
        me = 2 * x + y
        chips = [(1 - x, y), (x, 1 - y), (1 - x, 1 - y)]
        local = pltpu.make_async_copy(p_ref.at[me], rb_ref.at[me], local_sem)
        local.start()
        cps = [pltpu.make_async_remote_copy(src_ref=p_ref.at[2 * cx + cy], dst_ref=rb_ref.at[me], send_sem=send_sems.at[j],
                                            recv_sem=recv_sems.at[j], device_id=(cx, cy, c), device_id_type=MESH)
               for j, (cx, cy) in enumerate(chips)]
        for cp in cps:
            cp.start()
        for j, (cx, cy) in enumerate(chips):
            slot = rb_ref.at[2 * cx + cy]
            pltpu.make_async_remote_copy(src_ref=slot, dst_ref=slot, send_sem=send_sems.at[j], recv_sem=recv_sems.at[j],
                                         device_id=(cx, cy, c), device_id_type=MESH).wait_recv()
        for cp in cps:
            cp.wait_send()
        local.wait()

    return _pcall(body, name="rs_chip_exchange", out_shape=jax.ShapeDtypeStruct((4, h, w), p.dtype),
                  in_specs=[_ANY], out_specs=_ANY,
                  scratch_shapes=[pltpu.SemaphoreType.DMA((3,)), pltpu.SemaphoreType.DMA((3,)), pltpu.SemaphoreType.DMA])(p)


def _sum4(rb, tb=256):
    _, h, w = rb.shape
    tb = min(tb, h)
    assert h % tb == 0

    def body(r_ref, o_ref):
        o_ref[...] = ((r_ref[0] + r_ref[1]) + r_ref[2]) + r_ref[3]

    return _pcall(body, name="sum4", grid=(h // tb,), in_specs=[pl.BlockSpec((4, tb, w), lambda i: (0, i, 0))],
                  out_specs=pl.BlockSpec((tb, w), lambda i: (i, 0)), out_shape=jax.ShapeDtypeStruct((h, w), rb.dtype),
                  compiler_params=_cp("parallel"))(rb)


def _pair_allgather(f):
    h, w = f.shape

    def body(f_ref, out_ref, send_sem, recv_sem, local_sem):
        x, y, c = _coords()
        mine_rows = pl.ds(pl.multiple_of(c * h, SUB), h)
        local = pltpu.make_async_copy(f_ref, out_ref.at[mine_rows], local_sem)
        local.start()
        cp = pltpu.make_async_remote_copy(src_ref=f_ref, dst_ref=out_ref.at[mine_rows], send_sem=send_sem,
                                          recv_sem=recv_sem, device_id=(x, y, 1 - c), device_id_type=MESH)
        cp.start()
        cp.wait()
        local.wait()

    return _pcall(body, name="pair_allgather", out_shape=jax.ShapeDtypeStruct((2 * h, w), f.dtype),
                  in_specs=[_ANY], out_specs=_ANY,
                  scratch_shapes=[pltpu.SemaphoreType.DMA, pltpu.SemaphoreType.DMA, pltpu.SemaphoreType.DMA])(f)


def _pack(arrs, dtype, row_mult):
    flat = jnp.concatenate([a.astype(dtype).reshape(-1) for a in arrs])
    n = flat.shape[0]
    rows = _ru(-(-n // PACK_W), row_mult)
    return jnp.pad(flat, (0, rows * PACK_W - n)).reshape(rows, PACK_W)


def _unpack(buf, shapes):
    flat = buf.reshape(-1)
    out, off = [], 0
    for s in shapes:
        n = 1
        for d in s:
            n *= d
        out.append(flat[off:off + n].reshape(s))
        off += n
    return out


def _in_layout(D, GW, GH, FW, FH, MW, tn):
    o_z = 3 * GW
    o_beta = 4 * GW
    o_fq = o_beta + 2 * GH
    o_ff = o_fq + 3 * FW
    o_mq = o_ff + FH
    o_g = o_mq + MW
    orig = {"q": (0, GW), "k": (GW, GW), "v": (2 * GW, GW), "z": (o_z, GW), "beta": (o_beta, GH), "dec": (o_beta + GH, GH),
            "fq": (o_fq, FW), "fk": (o_fq + FW, FW), "fv": (o_fq + 2 * FW, FW), "ff": (o_ff, FH), "mq": (o_mq, MW),
            "ga": (o_g, D), "gb": (o_g + D, D), "gm": (o_g + 2 * D, D)}
    offs, cur = {}, 0
    for key, width in (("ga", D), ("gb", D), ("gm", D), ("q", GW), ("k", GW), ("v", GW), ("z", GW),
                       ("fq", FW), ("fk", FW), ("fv", FW), ("mq", MW), ("small", LANES)):
        cur = _ru(cur, width)
        offs[key] = cur
        cur += width
    total = _ru(cur, tn)
    pieces = [(offs[k], orig[k][0], orig[k][1]) for k in ("ga", "gb", "gm", "q", "k", "v", "z", "fq", "fk", "fv", "mq")]
    pieces += [(offs["small"], orig["beta"][0], GH), (offs["small"] + GH, orig["dec"][0], GH),
               (offs["small"] + 2 * GH, orig["ff"][0], FH)]
    return offs, total, pieces, o_g + 3 * D


def _pad_cols(w, pieces, total):
    parts, cur = [], 0
    for pstart, ostart, n in pieces:
        if pstart > cur:
            parts.append(jnp.zeros((w.shape[0], pstart - cur), w.dtype))
        parts.append(w[:, ostart:ostart + n])
        cur = pstart + n
    if total > cur:
        parts.append(jnp.zeros((w.shape[0], total - cur), w.dtype))
    return jnp.concatenate(parts, axis=1)


def _unpad_cols(wp, pieces):
    return jnp.concatenate([wp[:, pstart:pstart + n] for pstart, ostart, n in sorted(pieces, key=lambda t: t[1])], axis=1)


def _local_step(x, mem, tgt, W, flash_blk=512):
    T, D = x.shape
    GW = W["w_up_gdn"].shape[0]
    FW = W["w_up_fox"].shape[0]
    MW = W["w_up_mem"].shape[0]
    GH, FH = GW // HEAD, FW // HEAD
    offs, NP, pieces, d_in = _in_layout(D, GW, GH, FW, FH, MW, 512)
    assert W["w_in"].shape[1] == d_in
    w_in_p = _pad_cols(W["w_in"], pieces, NP)
    cw = W["conv_w"]
    cws = [cw[:, i * GW:(i + 1) * GW] for i in range(3)]
    zl = jnp.zeros((1, LANES), F32)
    pvecs = [lax.dynamic_update_slice(zl, W["a_log"], (0, GH)), lax.dynamic_update_slice(zl, W["dt_bias"], (0, GH)),
             lax.dynamic_update_slice(zl, W["fox_b_f"], (0, 2 * GH))]
    lane0 = 2 * GH

    h = _rms_fwd(x, W["g_mix"], "rms_mix")
    proj = _mm(h, w_in_p, "nn", "in_proj")
    qn, kn, vc, gsm, cum = _gdn_prep(proj, offs, cws, pvecs, GH, FH, GW)
    o_gdn, sall = _gdn_scan_fwd(qn, kn, vc, gsm, GH)
    o_a = _gdn_post(o_gdn, proj, offs["z"], W["gdn_norm_g"], GW)
    qb, kb, vb = _fox_prep(proj, offs, W["fox_q_norm"], W["fox_k_norm"], FW)
    cumt = cum.T
    o_b16, lse = _flash_fwd(qb, kb, vb, cum, cumt, lane0, FH, flash_blk)
    memn = _rms_fwd(mem, W["g_mem"], "rms_mem")
    kv = _mm(memn, W["w_mem_kv"], "nn", "mem_kv")
    o_m = _mem_attn(proj, offs["mq"], kv, W["mem_q_norm"], W["mem_k_norm"], MW)
    ua = _mm(o_a, W["w_up_gdn"], "nn", "up_gdn")
    ub = _mm(o_b16, W["w_up_fox"], "nn", "up_fox")
    um = _mm(o_m, W["w_up_mem"], "nn", "up_mem")
    y = _merge(proj, offs, ua, ub, um, D)
    x1 = _mm(y, W["w_out"], "nn", "out_proj", epilogue=lambda acc, r: (acc + r,), extras=(x,))
    h2 = _rms_fwd(x1, W["g_mlp"], "rms_mlp")
    u, a = _mm(h2, W["w_ff1"], "nn", "ff1", out_dtypes=(F32, BF16),
               epilogue=lambda acc: (acc, jnp.square(jnp.maximum(acc, 0.0))))
    x2 = _mm(a, W["w_ff2"], "nn", "ff2", epilogue=lambda acc, r: (acc + r,), extras=(x1,))
    dx2, dx2b, lpart = _loss_grad(x2, tgt)
    loss = lpart[0, 0]

    G = {}
    du = _mm(dx2b, W["w_ff2"], "nt", "ff2_dx", out_dtypes=(BF16,),
             epilogue=lambda acc, uu: (acc * (2.0 * jnp.maximum(uu, 0.0)),), extras=(u,))
    G["w_ff2"] = _mm(a, dx2b, "tn", "ff2_dw")
    dh2 = _mm(du, W["w_ff1"], "nt", "ff1_dx")
    G["w_ff1"] = _mm(h2, du, "tn", "ff1_dw")
    dx1, dx1b, G["g_mlp"] = _rms_bwd(x1, W["g_mlp"], dh2, dx2, "rms_mlp_bwd")
    dy = _mm(dx1b, W["w_out"], "nt", "out_dx")
    G["w_out"] = _mm(y, dx1b, "tn", "out_dw")
    dga, dgb, dgm, dua, dub, dum = _merge_bwd(proj, offs, ua, ub, um, dy, D)
    do_a = _mm(dua, W["w_up_gdn"], "nt", "up_gdn_dx")
    G["w_up_gdn"] = _mm(o_a, dua, "tn", "up_gdn_dw")
    do_b = _mm(dub, W["w_up_fox"], "nt", "up_fox_dx")
    G["w_up_fox"] = _mm(o_b16, dub, "tn", "up_fox_dw")
    do_m = _mm(dum, W["w_up_mem"], "nt", "up_mem_dx")
    G["w_up_mem"] = _mm(o_m, dum, "tn", "up_mem_dw")
    dmq, dkv, G["mem_q_norm"], G["mem_k_norm"] = _mem_attn_bwd(proj, offs["mq"], kv, W["mem_q_norm"], W["mem_k_norm"], do_m, MW)
    dkvb = dkv.astype(BF16)
    dmemn = _mm(dkvb, W["w_mem_kv"], "nt", "mem_kv_dx")
    G["w_mem_kv"] = _mm(memn, dkvb, "tn", "mem_kv_dw")
    G["g_mem"] = _rms_dg(mem, W["g_mem"], dmemn, "rms_mem_bwd")
    dl = _flash_bwd_q(qb, kb, vb, cum, cumt, lse, do_b, None, lane0, FH, flash_blk)
    dqb, dcq = _flash_bwd_q(qb, kb, vb, cum, cumt, lse, do_b, dl, lane0, FH, flash_blk)
    dkb, dvb, dck = _flash_bwd_dkv(qb, kb, vb, cum, cumt, lse.reshape(FH, 1, T), dl.reshape(FH, 1, T), do_b, lane0, FH, flash_blk)
    dlf = _rev_cumsum_rows(dcq.reshape(FH, T), dck.reshape(FH, T))
    dlf_sm = jnp.pad(dlf.T, ((0, 0), (lane0, LANES - lane0 - FH)))
    dfq, dfk, dfv, G["fox_q_norm"], G["fox_k_norm"] = _fox_prep_bwd(proj, offs, W["fox_q_norm"], W["fox_k_norm"], dqb, dkb, dvb, FW)
    do_gdn, dz, G["gdn_norm_g"] = _gdn_post_bwd(o_gdn, proj, offs["z"], W["gdn_norm_g"], do_a, GW)
    dqn, dkn, dvc, dgsm = _gdn_scan_bwd(qn, kn, vc, gsm, sall, do_gdn, GH)
    dyq, dyk, dyv, dcq, dck_w, dcv, dsmall, dalog, ddtb, dbf = _gdn_prep_bwd_a(
        proj, offs, cws, pvecs, (dqn, dkn, dvc), dgsm, dlf_sm, GH, FH, GW)
    dxq, dxk, dxv = _gdn_prep_bwd_b((dyq, dyk, dyv), cws, GW)
    G["conv_w"] = jnp.concatenate([dcq[:CONV_K], dck_w[:CONV_K], dcv[:CONV_K]], axis=1)
    G["a_log"] = dalog[:, GH:2 * GH]
    G["dt_bias"] = ddtb[:, GH:2 * GH]
    G["fox_b_f"] = dbf[:, lane0:lane0 + FH]
    segs = {"ga": dga, "gb": dgb, "gm": dgm, "q": dxq, "k": dxk, "v": dxv, "z": dz, "fq": dfq, "fk": dfk, "fv": dfv,
            "mq": dmq, "small": dsmall}
    parts, cur = [], 0
    for key in ("ga", "gb", "gm", "q", "k", "v", "z", "fq", "fk", "fv", "mq", "small"):
        if offs[key] > cur:
            parts.append(jnp.zeros((T, offs[key] - cur), BF16))
        parts.append(segs[key])
        cur = offs[key] + segs[key].shape[1]
    if NP > cur:
        parts.append(jnp.zeros((T, NP - cur), BF16))
    dproj = jnp.concatenate(parts, axis=1)
    dh = _mm(dproj, w_in_p, "nt", "in_dx")
    G["w_in"] = _unpad_cols(_mm(h, dproj, "tn", "in_dw"), pieces)
    grad_x, _, G["g_mix"] = _rms_bwd(x, W["g_mix"], dh, dx1, "rms_mix_bwd")
    return loss, grad_x, G


BIG = ["w_in", "w_mem_kv", "w_up_gdn", "w_up_fox", "w_up_mem", "w_out", "w_ff1", "w_ff2"]
SMALL = ["g_mix", "a_log", "dt_bias", "gdn_norm_g", "fox_b_f", "fox_q_norm", "fox_k_norm", "g_mem", "mem_q_norm",
         "mem_k_norm", "g_mlp"]
ORDER = ["g_mix", "w_in", "conv_w", "a_log", "dt_bias", "gdn_norm_g", "fox_b_f", "fox_q_norm", "fox_k_norm", "g_mem",
         "w_mem_kv", "mem_q_norm", "mem_k_norm", "w_up_gdn", "w_up_fox", "w_up_mem", "w_out", "g_mlp", "w_ff1", "w_ff2"]
SHARD_AXIS = {"w_in": 1, "w_mem_kv": 0, "w_up_gdn": 1, "w_up_fox": 1, "w_up_mem": 1, "w_out": 0, "w_ff1": 1, "w_ff2": 0}


def _step(x, mem, tgt, w, m, v, flash_blk=512):
    xi, yi, ci = _coords()
    chip = 2 * xi + yi

    shard_shapes = [w[n].shape for n in BIG]
    gathered = _allgather_chips(_pack([w[n] for n in BIG], BF16, PACK_ROWS))
    per_chip = [_unpack(gathered[j], shard_shapes) for j in range(4)]
    W = {n: jnp.concatenate([per_chip[j][i] for j in range(4)], axis=SHARD_AXIS[n]) for i, n in enumerate(BIG)}
    cw_rows = jnp.pad(w["conv_w"], ((0, SUB - CONV_K), (0, 0)))
    cw_all = _allgather_small(cw_rows)
    W["conv_w"] = jnp.concatenate([cw_all[16 * j:16 * j + CONV_K] for j in range(4)], axis=1)
    for n in SMALL:
        W[n] = w[n]

    loss, grad_x, G = _local_step(x, mem, tgt, W, flash_blk)
    loss = lax.psum(loss, ("x", "y", "c"))

    small_shapes = [G[n].shape for n in SMALL] + [G["conv_w"].shape]
    sm = _pack([G[n] for n in SMALL] + [G["conv_w"]], F32, SUB)
    sm_sum = _sum8(_allgather_small(sm), sm.shape[0])
    sm_list = _unpack(sm_sum, small_shapes)
    g = {n: sm_list[i] for i, n in enumerate(SMALL)}
    cw_full = sm_list[-1]
    gw4 = cw_full.shape[1] // 4
    g["conv_w"] = lax.dynamic_slice(cw_full, (0, chip * gw4), (CONV_K, gw4))

    by_dest = []
    for j in range(4):
        shards = []
        for n in BIG:
            size = w[n].shape[SHARD_AXIS[n]]
            shards.append(lax.slice_in_dim(G[n], j * size, (j + 1) * size, axis=SHARD_AXIS[n]))
        by_dest.append(_pack(shards, F32, PACK_ROWS))
    gflat = jnp.stack(by_dest)
    rb1 = _rs_pair_exchange(gflat)
    part = _rs_pair_add(gflat, rb1, jnp.reshape(ci, (1,)).astype(jnp.int32))
    rb2 = _rs_chip_exchange(part)
    mine = _pair_allgather(_sum4(rb2))
    for i, gv in enumerate(_unpack(mine, shard_shapes)):
        g[BIG[i]] = gv

    delta, new_m, new_v = {}, {}, {}
    for n in BIG:
        delta[n], new_m[n], new_v[n] = _adamw(w[n], g[n], m[n], v[n], "adamw_" + n)
    rest = SMALL + ["conv_w"]
    rest_shapes = [w[n].shape for n in rest]
    packed = [_pack([d[n] for n in rest], F32, SUB) for d in (w, g, m, v)]
    outs = _adamw(*packed, "adamw_small", tb=packed[0].shape[0])
    for d, buf in zip((delta, new_m, new_v), outs):
        for n, val in zip(rest, _unpack(buf, rest_shapes)):
            d[n] = val
    return loss, grad_x, g, delta, new_m, new_v


def kernel(x, mem, g_mix, w_in, conv_w, a_log, dt_bias, gdn_norm_g, fox_b_f, fox_q_norm, fox_k_norm, g_mem, w_mem_kv, mem_q_norm, mem_k_norm, w_up_gdn, w_up_fox, w_up_mem, w_out, g_mlp, w_ff1, w_ff2, loss_target, m_g_mix, m_w_in, m_conv_w, m_a_log, m_dt_bias, m_gdn_norm_g, m_fox_b_f, m_fox_q_norm, m_fox_k_norm, m_g_mem, m_w_mem_kv, m_mem_q_norm, m_mem_k_norm, m_w_up_gdn, m_w_up_fox, m_w_up_mem, m_w_out, m_g_mlp, m_w_ff1, m_w_ff2, v_g_mix, v_w_in, v_conv_w, v_a_log, v_dt_bias, v_gdn_norm_g, v_fox_b_f, v_fox_q_norm, v_fox_k_norm, v_g_mem, v_w_mem_kv, v_mem_q_norm, v_mem_k_norm, v_w_up_gdn, v_w_up_fox, v_w_up_mem, v_w_out, v_g_mlp, v_w_ff1, v_w_ff2):
    ws = (g_mix, w_in, conv_w, a_log, dt_bias, gdn_norm_g, fox_b_f, fox_q_norm, fox_k_norm, g_mem, w_mem_kv, mem_q_norm,
          mem_k_norm, w_up_gdn, w_up_fox, w_up_mem, w_out, g_mlp, w_ff1, w_ff2)
    ms = (m_g_mix, m_w_in, m_conv_w, m_a_log, m_dt_bias, m_gdn_norm_g, m_fox_b_f, m_fox_q_norm, m_fox_k_norm, m_g_mem,
          m_w_mem_kv, m_mem_q_norm, m_mem_k_norm, m_w_up_gdn, m_w_up_fox, m_w_up_mem, m_w_out, m_g_mlp, m_w_ff1, m_w_ff2)
    vs = (v_g_mix, v_w_in, v_conv_w, v_a_log, v_dt_bias, v_gdn_norm_g, v_fox_b_f, v_fox_q_norm, v_fox_k_norm, v_g_mem,
          v_w_mem_kv, v_mem_q_norm, v_mem_k_norm, v_w_up_gdn, v_w_up_fox, v_w_up_mem, v_w_out, v_g_mlp, v_w_ff1, v_w_ff2)
    drop = lambda a: a[0] if a.ndim == 3 else a
    w = {n: drop(a) for n, a in zip(ORDER, ws)}
    m = {n: drop(a) for n, a in zip(ORDER, ms)}
    v = {n: drop(a) for n, a in zip(ORDER, vs)}
    loss, grad_x, g, delta, new_m, new_v = _step(x[0], mem[0], loss_target[0], w, m, v)
    out = [loss, grad_x[None]]
    for d in (g, delta, new_m, new_v):
        out += [d[n].reshape(a.shape) for n, a in zip(ORDER, ws)]
    return tuple(out)
```

```python
import numpy as np

import jax
import jax.numpy as jnp
from jax import lax
from jax.experimental import pallas as pl
from jax.experimental.pallas import tpu as pltpu

F32 = jnp.float32
BF16 = jnp.bfloat16
HI = lax.Precision.HIGHEST
MESH = pl.DeviceIdType.MESH

EPS = 1e-6
HEAD = 128
MEM_DH = 256
CONV_K = 4
CHUNK = 64
CHUNK_SHIFT = 6
LANES = 128
SUB = 8
PACK_W = 1024
PACK_ROWS = 512
VMEM_LIMIT = 56 * 1024 * 1024
NEG = -1e30
SOLVE_PREC = None

ADAM_LR, ADAM_B1, ADAM_B2, ADAM_EPS, ADAM_WD, ADAM_STEP = 0.001, 0.9, 0.999, 1e-08, 0.01, 10


def _pcall(body, **kw):
    return pl.pallas_call(body, **kw)


def _cp(*sem):
    return pltpu.CompilerParams(dimension_semantics=sem, vmem_limit_bytes=VMEM_LIMIT)


def _dot(a, b, prec=None):
    return lax.dot_general(a, b, (((1,), (0,)), ((), ())), precision=prec, preferred_element_type=F32)


def _dot_nt(a, b, prec=None):
    return lax.dot_general(a, b, (((1,), (1,)), ((), ())), precision=prec, preferred_element_type=F32)


def _dot_tn(a, b, prec=None):
    return lax.dot_general(a, b, (((0,), (0,)), ((), ())), precision=prec, preferred_element_type=F32)


def _sigmoid(x):
    return 1.0 / (1.0 + jnp.exp(-x))


def _softplus(x):
    return jnp.maximum(x, 0.0) + jnp.log(1.0 + jnp.exp(-jnp.abs(x)))


def _silu(x):
    return x * _sigmoid(x)


def _rms(x, g):
    return x * lax.rsqrt(jnp.mean(x * x, axis=-1, keepdims=True) + EPS) * g


def _ru(a, m):
    return (a + m - 1) // m * m


def _mm(a, b, mode, name, out_dtypes=(F32,), epilogue=None, extras=(), tm=1024, tn=512, tk=2048):
    if mode == "nn":
        (M, K), (K2, N) = a.shape, b.shape
    elif mode == "nt":
        (M, K), (N, K2) = a.shape, b.shape
    else:
        (K, M), (K2, N) = a.shape, b.shape
    assert K == K2, (a.shape, b.shape, mode)
    tm, tn = min(tm, M), min(tn, N)
    tk = next((t for t in (tk, 1024, 512, 256, LANES) if t <= K and K % t == 0), K)
    assert M % tm == 0 and N % tn == 0 and K % tk == 0, (M, N, K, tm, tn, tk)
    nk = K // tk
    n_ex, n_out = len(extras), len(out_dtypes)
    dims = {"nn": ((1,), (0,)), "nt": ((1,), (1,)), "tn": ((0,), (0,))}[mode]

    def finish(res, ex_refs, out_refs):
        outs = epilogue(res, *[r[...] for r in ex_refs]) if epilogue is not None else (res,)
        for o_ref, o in zip(out_refs, outs):
            o_ref[...] = o.astype(o_ref.dtype)

    def body(a_ref, b_ref, *rest):
        ex_refs, out_refs = rest[:n_ex], rest[n_ex:n_ex + n_out]
        part = lax.dot_general(a_ref[...], b_ref[...], (dims, ((), ())), preferred_element_type=F32)
        if nk == 1:
            finish(part, ex_refs, out_refs)
            return
        acc = rest[-1]
        k = pl.program_id(2)

        @pl.when(k == 0)
        def _():
            acc[...] = part

        @pl.when(k > 0)
        def _():
            acc[...] += part

        @pl.when(k == nk - 1)
        def _():
            finish(acc[...], ex_refs, out_refs)

    a_spec = pl.BlockSpec((tk, tm), lambda i, j, k: (k, i)) if mode == "tn" else pl.BlockSpec((tm, tk), lambda i, j, k: (i, k))
    b_spec = pl.BlockSpec((tn, tk), lambda i, j, k: (j, k)) if mode == "nt" else pl.BlockSpec((tk, tn), lambda i, j, k: (k, j))
    mn_spec = pl.BlockSpec((tm, tn), lambda i, j, k: (i, j))
    outs = _pcall(
        body, name=name, grid=(M // tm, N // tn, nk),
        in_specs=[a_spec, b_spec] + [mn_spec] * n_ex,
        out_specs=[mn_spec] * n_out,
        out_shape=[jax.ShapeDtypeStruct((M, N), dt) for dt in out_dtypes],
        scratch_shapes=[pltpu.VMEM((tm, tn), F32)] if nk > 1 else [],
        compiler_params=_cp("parallel", "parallel", "arbitrary"),
    )(a, b, *extras)
    return outs[0] if n_out == 1 else outs


def _rowwise(fn, T, tb, ins, outs, name, scratch=()):
    tb = min(tb, T)
    assert T % tb == 0 and (tb % SUB == 0 or tb == T)
    nblk = T // tb
    r8 = tb // SUB
    in_specs, arrs = [], []
    for spec in ins:
        kind, arr = spec[0], spec[1]
        arrs.append(arr)
        if kind == "full":
            nd = arr.ndim
            in_specs.append(pl.BlockSpec(arr.shape, lambda i, nd=nd: (0,) * nd))
            continue
        off, w = spec[2], spec[3]
        assert off % w == 0 and arr.shape[0] == T, (name, off, w, arr.shape)
        cb = off // w
        if kind == "row":
            in_specs.append(pl.BlockSpec((tb, w), lambda i, cb=cb: (i, cb)))
        elif kind == "prev":
            in_specs.append(pl.BlockSpec((SUB, w), lambda i, cb=cb: (jnp.maximum(i * r8 - 1, 0), cb)))
        else:
            in_specs.append(pl.BlockSpec((SUB, w), lambda i, cb=cb: (jnp.minimum((i + 1) * r8, T // SUB - 1), cb)))
    out_specs, out_shapes, is_acc = [], [], []
    for spec in outs:
        if spec[0] == "row":
            out_specs.append(pl.BlockSpec((tb, spec[1]), lambda i: (i, 0)))
            out_shapes.append(jax.ShapeDtypeStruct((T, spec[1]), spec[2]))
            is_acc.append(False)
        else:
            nd = len(spec[1])
            out_specs.append(pl.BlockSpec(spec[1], lambda i, nd=nd: (0,) * nd))
            out_shapes.append(jax.ShapeDtypeStruct(spec[1], spec[2]))
            is_acc.append(True)
    n_in, n_out = len(ins), len(outs)
    seq = any(is_acc) or len(scratch) > 0

    def body(*refs):
        in_refs, out_refs, scr = refs[:n_in], refs[n_in:n_in + n_out], refs[n_in + n_out:]
        i = pl.program_id(0)
        vals = fn(i, nblk, *[r[...] for r in in_refs], *scr)
        for o_ref, v, acc in zip(out_refs, vals, is_acc):
            if acc:
                @pl.when(i == 0)
                def _(o_ref=o_ref):
                    o_ref[...] = jnp.zeros_like(o_ref)

                o_ref[...] += v.astype(o_ref.dtype)
            else:
                o_ref[...] = v.astype(o_ref.dtype)

    res = _pcall(
        body, name=name, grid=(nblk,), in_specs=in_specs, out_specs=out_specs, out_shape=out_shapes,
        scratch_shapes=list(scratch), compiler_params=_cp("arbitrary" if seq else "parallel"),
    )(*arrs)
    return res


def _heads(x, width):
    return [x[:, h * width:(h + 1) * width] for h in range(x.shape[1] // width)]


def _cat(xs):
    return xs[0] if len(xs) == 1 else jnp.concatenate(xs, axis=1)


def _rms_fwd(x, g, name, tb=512):
    T, D = x.shape
    return _rowwise(lambda i, n, xv, gv: (_rms(xv, gv),), T, tb,
                    [("row", x, 0, D), ("full", g)], [("row", D, BF16)], name)[0]


def _rms_bwd(x, g, dh, dres, name, tb=256):
    T, D = x.shape

    def fn(i, n, xv, gv, dhv, drv):
        _, vjp = jax.vjp(_rms, xv, gv)
        dx, dg = vjp(dhv)
        tot = drv + dx
        return tot, tot, dg

    return _rowwise(fn, T, tb, [("row", x, 0, D), ("full", g), ("row", dh, 0, D), ("row", dres, 0, D)],
                    [("row", D, F32), ("row", D, BF16), ("acc", (1, D), F32)], name)


def _rms_dg(x, g, dh, name, tb=256):
    T, D = x.shape

    def fn(i, n, xv, gv, dhv):
        _, vjp = jax.vjp(lambda gg: _rms(xv, gg), gv)
        return vjp(dhv)

    return _rowwise(fn, T, tb, [("row", x, 0, D), ("full", g), ("row", dh, 0, D)], [("acc", (1, D), F32)], name)[0]


def _shift_down(x, halo, s, first):
    if s == 0:
        return x
    tb, c = x.shape
    xr = pltpu.roll(x, s, 0)
    hr = jnp.where(first, 0.0, pltpu.roll(halo, s, 0))
    hfull = hr if tb == SUB else jnp.concatenate([hr, jnp.zeros((tb - SUB, c), x.dtype)], axis=0)
    row = lax.broadcasted_iota(jnp.int32, x.shape, 0)
    return jnp.where(row < s, hfull, xr)


def _shift_up(z, halo, s, last):
    if s == 0:
        return z
    tb, c = z.shape
    zr = pltpu.roll(z, tb - s, 0)
    hr = jnp.where(last, 0.0, pltpu.roll(halo, SUB - s, 0))
    hfull = hr if tb == SUB else jnp.concatenate([jnp.zeros((tb - SUB, c), z.dtype), hr], axis=0)
    row = lax.broadcasted_iota(jnp.int32, z.shape, 0)
    return jnp.where(row >= tb - s, hfull, zr)


def _conv_pre(x, halo, cw, first):
    xs = [_shift_down(x, halo, s, first) for s in range(CONV_K)]
    y = cw[0:1, :] * xs[3]
    for i in range(1, CONV_K):
        y = y + cw[i:i + 1, :] * xs[CONV_K - 1 - i]
    return y, xs


def _qk_post(y, scale):
    a = _silu(y)
    return a * lax.rsqrt(jnp.sum(a * a, axis=-1, keepdims=True) + EPS) * scale


def _small_fn(s, alog, dtb, bf, gh, fh):
    lane = lax.broadcasted_iota(jnp.int32, s.shape, 1)
    beta = _sigmoid(s)
    g = -jnp.exp(alog) * _softplus(s + dtb)
    lf = -_softplus(-(s + bf))
    return jnp.where(lane < gh, beta, jnp.where(lane < 2 * gh, g, jnp.where(lane < 2 * gh + fh, lf, 0.0)))


def _gdn_prep(proj, offs, cws, pvecs, GH, FH, GW, tb=256):
    T = proj.shape[0]
    tb = min(tb, T)
    qscale = HEAD ** -0.5

    def fn(i, n, xq, hq, xk, hk, xv, hv, cwq, cwk, cwv, s, alog, dtb, bf, carry):
        first = i == 0
        yq, _ = _conv_pre(xq, hq, cwq, first)
        yk, _ = _conv_pre(xk, hk, cwk, first)
        yv, _ = _conv_pre(xv, hv, cwv, first)
        qn = _cat([_qk_post(y, qscale) for y in _heads(yq, HEAD)])
        kn = _cat([_qk_post(y, 1.0) for y in _heads(yk, HEAD)])
        vc = _silu(yv)
        gsm = _small_fn(s, alog, dtb, bf, GH, FH)

        @pl.when(first)
        def _():
            carry[...] = jnp.zeros_like(carry)

        ri = lax.broadcasted_iota(jnp.int32, (tb, tb), 0)
        ci = lax.broadcasted_iota(jnp.int32, (tb, tb), 1)
        cum = _dot((ri >= ci).astype(F32), gsm, HI) + carry[0:1, :]
        carry[...] += _dot(jnp.ones((SUB, tb), F32), gsm, HI)
        in_chunk = (ri >= ci) & ((ri >> CHUNK_SHIFT) == (ci >> CHUNK_SHIFT))
        lane = lax.broadcasted_iota(jnp.int32, gsm.shape, 1)
        gbm = jnp.where(lane < GH, gsm, _dot(in_chunk.astype(F32), gsm, HI))
        return qn, kn, vc, cum, gbm

    ins = []
    for key in ("q", "k", "v"):
        ins += [("row", proj, offs[key], GW), ("prev", proj, offs[key], GW)]
    ins += [("full", c) for c in cws] + [("row", proj, offs["small"], LANES)] + [("full", p) for p in pvecs]
    outs = [("row", GW, F32)] * 3 + [("row", LANES, F32)] * 2
    return _rowwise(fn, T, tb, ins, outs, "gdn_prep", scratch=[pltpu.VMEM((SUB, LANES), F32)])


def _gdn_prep_bwd_a(proj, offs, cws, pvecs, cts, dgsm_scan, dlf_sm, GH, FH, GW, tb=256):
    T = proj.shape[0]
    qscale = HEAD ** -0.5

    def one(x, halo, cw, ct, first, post):
        y, xs = _conv_pre(x, halo, cw, first)
        if post is None:
            _, vjp = jax.vjp(_silu, y)
            dy = vjp(ct)[0]
        else:
            dys = []
            for yh, cth in zip(_heads(y, HEAD), _heads(ct, HEAD)):
                _, vjp = jax.vjp(lambda t: _qk_post(t, post), yh)
                dys.append(vjp(cth)[0])
            dy = _cat(dys)
        row = lax.broadcasted_iota(jnp.int32, (SUB, x.shape[1]), 0)
        dcw = jnp.zeros((SUB, x.shape[1]), F32)
        for i in range(CONV_K):
            dcw = dcw + jnp.where(row == i, jnp.sum(dy * xs[CONV_K - 1 - i], axis=0, keepdims=True), 0.0)
        return dy, dcw

    def fn(i, n, xq, hq, xk, hk, xv, hv, cwq, cwk, cwv, cq, ck, cv, s, alog, dtb, bf, d1, d2):
        first = i == 0
        dyq, dcq = one(xq, hq, cwq, cq, first, qscale)
        dyk, dck = one(xk, hk, cwk, ck, first, 1.0)
        dyv, dcv = one(xv, hv, cwv, cv, first, None)
        tb_ = d1.shape[0]
        ri = lax.broadcasted_iota(jnp.int32, (tb_, tb_), 0)
        ci = lax.broadcasted_iota(jnp.int32, (tb_, tb_), 1)
        later = (ci >= ri) & ((ri >> CHUNK_SHIFT) == (ci >> CHUNK_SHIFT))
        lane = lax.broadcasted_iota(jnp.int32, d1.shape, 1)
        d1 = jnp.where(lane < GH, d1, _dot(later.astype(F32), d1, HI))
        _, vjp = jax.vjp(lambda a, b, c, d: _small_fn(a, b, c, d, GH, FH), s, alog, dtb, bf)
        ds, dalog, ddtb, dbf = vjp(d1 + d2)
        return dyq, dyk, dyv, dcq, dck, dcv, ds, dalog, ddtb, dbf

    ins = []
    for key in ("q", "k", "v"):
        ins += [("row", proj, offs[key], GW), ("prev", proj, offs[key], GW)]
    ins += [("full", c) for c in cws] + [("row", c, 0, GW) for c in cts]
    ins += [("row", proj, offs["small"], LANES)] + [("full", p) for p in pvecs]
    ins += [("row", dgsm_scan, 0, LANES), ("row", dlf_sm, 0, LANES)]
    outs = [("row", GW, F32)] * 3 + [("acc", (SUB, GW), F32)] * 3 + [("row", LANES, BF16)] + [("acc", (1, LANES), F32)] * 3
    return _rowwise(fn, T, tb, ins, outs, "gdn_prep_bwd_a")


def _gdn_prep_bwd_b(dys, cws, GW, tb=256):
    T = dys[0].shape[0]

    def fn(i, n, dq, nq, dk, nk, dv, nv, cwq, cwk, cwv):
        last = i == n - 1
        res = []
        for dy, nh, cw in ((dq, nq, cwq), (dk, nk, cwk), (dv, nv, cwv)):
            dx = cw[CONV_K - 1:CONV_K, :] * dy
            for t in range(CONV_K - 1):
                dx = dx + cw[t:t + 1, :] * _shift_up(dy, nh, CONV_K - 1 - t, last)
            res.append(dx)
        return tuple(res)

    ins = []
    for dy in dys:
        ins += [("row", dy, 0, GW), ("next", dy, 0, GW)]
    ins += [("full", c) for c in cws]
    return _rowwise(fn, T, tb, ins, [("row", GW, BF16)] * 3, "gdn_prep_bwd_b")


def _gdn_chunk(q, k, v, gam, bcol, s0):
    c, d = q.shape
    ri = lax.broadcasted_iota(jnp.int32, (c, c), 0)
    ci = lax.broadcasted_iota(jnp.int32, (c, c), 1)
    incl, strict = ri >= ci, ri > ci
    eye = (ri == ci).astype(F32)
    gam_cc = gam * jnp.ones((c, c), F32)
    gam_t = _dot_nt(eye, gam_cc, HI)
    rows = lax.broadcasted_iota(jnp.int32, (c, 1), 0)
    glast = jnp.sum(jnp.where(rows == c - 1, gam, 0.0), axis=0, keepdims=True)
    diff = gam_cc - gam_t
    dec_s = jnp.where(strict, jnp.exp(jnp.where(strict, diff, 0.0)), 0.0)
    dec_i = jnp.where(incl, jnp.exp(jnp.where(incl, diff, 0.0)), 0.0)
    m = bcol * _dot_nt(k, k) * dec_s
    b16 = (ri >> 4) == (ci >> 4)
    b32 = (ri >> 5) == (ci >> 5)
    m16 = jnp.where(b16, m, 0.0)
    m32 = jnp.where(b32 & ~b16, m, 0.0)
    m64 = jnp.where(b32, 0.0, m)
    p = eye - m16
    mp = m16
    for _ in range(3):
        mp = _dot(mp, mp, SOLVE_PREC)
        p = p + _dot(p, mp, SOLVE_PREC)
    p = p - _dot(_dot(p, m32, SOLVE_PREC), p, SOLVE_PREC)
    ainv = p - _dot(_dot(p, m64, SOLVE_PREC), p, SOLVE_PREC)
    eg = jnp.exp(gam)
    w = _dot(ainv, (bcol * eg) * k, SOLVE_PREC)
    u0 = _dot(ainv, bcol * v, SOLVE_PREC)
    qk = _dot_nt(q, k) * dec_i
    u = u0 - _dot(w, s0)
    o = _dot(q * eg, s0) + _dot(qk, u)
    s1 = jnp.exp(glast) * s0 + _dot_tn(k * jnp.exp(glast - gam), u)
    return o, s1


def _lane_col(x, lane_idx):
    lane = lax.broadcasted_iota(jnp.int32, x.shape, 1)
    return jnp.sum(jnp.where(lane == lane_idx, x, 0.0), axis=1, keepdims=True)


def _gdn_scan_fwd(qn, kn, vc, gsm, GH):
    T, GW = qn.shape
    nc = T // CHUNK

    def body(q_ref, k_ref, v_ref, g_ref, o_ref, sall_ref, s_scr):
        @pl.when(pl.program_id(0) == 0)
        def _():
            s_scr[...] = jnp.zeros_like(s_scr)

        gs = g_ref[...]
        for h in range(GH):
            sl = slice(h * HEAD, (h + 1) * HEAD)
            s0 = s_scr[h]
            sall_ref[0, h] = s0
            o, s1 = _gdn_chunk(q_ref[:, sl], k_ref[:, sl], v_ref[:, sl], _lane_col(gs, GH + h), _lane_col(gs, h), s0)
            o_ref[:, sl] = o
            s_scr[h] = s1

    row = pl.BlockSpec((CHUNK, GW), lambda i: (i, 0))
    return _pcall(
        body, name="gdn_scan_fwd", grid=(nc,),
        in_specs=[row, row, row, pl.BlockSpec((CHUNK, LANES), lambda i: (i, 0))],
        out_specs=[row, pl.BlockSpec((1, GH, HEAD, HEAD), lambda i: (i, 0, 0, 0))],
        out_shape=[jax.ShapeDtypeStruct((T, GW), F32), jax.ShapeDtypeStruct((nc, GH, HEAD, HEAD), F32)],
        scratch_shapes=[pltpu.VMEM((GH, HEAD, HEAD), F32)],
        compiler_params=_cp("arbitrary"),
    )(qn, kn, vc, gsm)


def _gdn_scan_bwd(qn, kn, vc, gsm, sall, do, GH):
    T, GW = qn.shape
    nc = T // CHUNK

    def body(q_ref, k_ref, v_ref, g_ref, sall_ref, do_ref, dq_ref, dk_ref, dv_ref, dg_ref, ds_scr):
        @pl.when(pl.program_id(0) == 0)
        def _():
            ds_scr[...] = jnp.zeros_like(ds_scr)

        gs = g_ref[...]
        lane = lax.broadcasted_iota(jnp.int32, gs.shape, 1)
        dgs = jnp.zeros_like(gs)
        for h in range(GH):
            sl = slice(h * HEAD, (h + 1) * HEAD)
            _, vjp = jax.vjp(_gdn_chunk, q_ref[:, sl], k_ref[:, sl], v_ref[:, sl],
                             _lane_col(gs, GH + h), _lane_col(gs, h), sall_ref[0, h])
            dq, dk, dv, dgc, dbc, ds0 = vjp((do_ref[:, sl], ds_scr[h]))
            dq_ref[:, sl] = dq
            dk_ref[:, sl] = dk
            dv_ref[:, sl] = dv
            dgs = dgs + jnp.where(lane == h, dbc, 0.0) + jnp.where(lane == GH + h, dgc, 0.0)
            ds_scr[h] = ds0
        dg_ref[...] = dgs

    row = pl.BlockSpec((CHUNK, GW), lambda i: (nc - 1 - i, 0))
    sm = pl.BlockSpec((CHUNK, LANES), lambda i: (nc - 1 - i, 0))
    return _pcall(
        body, name="gdn_scan_bwd", grid=(nc,),
        in_specs=[row, row, row, sm, pl.BlockSpec((1, GH, HEAD, HEAD), lambda i: (nc - 1 - i, 0, 0, 0)), row],
        out_specs=[row, row, row, sm],
        out_shape=[jax.ShapeDtypeStruct((T, GW), F32)] * 3 + [jax.ShapeDtypeStruct((T, LANES), F32)],
        scratch_shapes=[pltpu.VMEM((GH, HEAD, HEAD), F32)],
        compiler_params=_cp("arbitrary"),
    )(qn, kn, vc, gsm, sall, do)


def _gdn_post_fn(o, z, g):
    return _rms(o, g) * _silu(z)


def _gdn_post(o, proj, off_z, g, GW, tb=512):
    T = o.shape[0]

    def fn(i, n, ov, zv, gv):
        return (_cat([_gdn_post_fn(a, b, gv) for a, b in zip(_heads(ov, HEAD), _heads(zv, HEAD))]),)

    return _rowwise(fn, T, tb, [("row", o, 0, GW), ("row", proj, off_z, GW), ("full", g)], [("row", GW, BF16)], "gdn_post")[0]


def _gdn_post_bwd(o, proj, off_z, g, dout, GW, tb=256):
    T = o.shape[0]

    def fn(i, n, ov, zv, gv, dv):
        dos, dzs, dg = [], [], jnp.zeros_like(gv)
        for a, b, c in zip(_heads(ov, HEAD), _heads(zv, HEAD), _heads(dv, HEAD)):
            _, vjp = jax.vjp(_gdn_post_fn, a, b, gv)
            da, db, dgh = vjp(c)
            dos.append(da)
            dzs.append(db)
            dg = dg + dgh
        return _cat(dos), _cat(dzs), dg

    return _rowwise(fn, T, tb, [("row", o, 0, GW), ("row", proj, off_z, GW), ("full", g), ("row", dout, 0, GW)],
                    [("row", GW, F32), ("row", GW, BF16), ("acc", (1, HEAD), F32)], "gdn_post_bwd")


def _fox_prep(proj, offs, gq, gk, FW, tb=512):
    T = proj.shape[0]

    def fn(i, n, q, k, v, gqv, gkv):
        return (_cat([_rms(a, gqv) for a in _heads(q, HEAD)]), _cat([_rms(a, gkv) for a in _heads(k, HEAD)]), v)

    return _rowwise(fn, T, tb, [("row", proj, offs["fq"], FW), ("row", proj, offs["fk"], FW), ("row", proj, offs["fv"], FW),
                                ("full", gq), ("full", gk)], [("row", FW, BF16)] * 3, "fox_prep")


def _fox_prep_bwd(proj, offs, gq, gk, dq, dk, dv, FW, tb=256):
    T = proj.shape[0]

    def fn(i, n, q, k, gqv, gkv, dqv, dkv, dvv):
        res = []
        for x, g, d in ((q, gqv, dqv), (k, gkv, dkv)):
            dxs, dg = [], jnp.zeros_like(g)
            for a, c in zip(_heads(x, HEAD), _heads(d, HEAD)):
                _, vjp = jax.vjp(_rms, a, g)
                da, dgh = vjp(c)
                dxs.append(da)
                dg = dg + dgh
            res += [_cat(dxs), dg]
        return res[0], res[2], dvv, res[1], res[3]

    return _rowwise(fn, T, tb, [("row", proj, offs["fq"], FW), ("row", proj, offs["fk"], FW), ("full", gq), ("full", gk),
                                ("row", dq, 0, FW), ("row", dk, 0, FW), ("row", dv, 0, FW)],
                    [("row", FW, BF16)] * 3 + [("acc", (1, HEAD), F32)] * 2, "fox_prep_bwd")


def _sub_row(x, sub_idx):
    sub = lax.broadcasted_iota(jnp.int32, x.shape, 0)
    return jnp.sum(jnp.where(sub == sub_idx, x, 0.0), axis=0, keepdims=True)


def _causal_pairs(nb, key_major):
    if key_major:
        pairs = [(i, j) for j in range(nb) for i in range(j, nb)]
    else:
        pairs = [(i, j) for i in range(nb) for j in range(i + 1)]
    return (jnp.asarray(np.array([p[0] for p in pairs], np.int32)),
            jnp.asarray(np.array([p[1] for p in pairs], np.int32)))


def _fox_scores(q, k, cq, ck, diagonal):
    s = _dot_nt(q, k) * (HEAD ** -0.5) + (cq - ck)
    if diagonal:
        row = lax.broadcasted_iota(jnp.int32, s.shape, 0)
        col = lax.broadcasted_iota(jnp.int32, s.shape, 1)
        s = jnp.where(row >= col, s, NEG)
    return s


def _flash_fwd(qb, kb, vb, cum, cumt, lane0, FH, blk):
    T, FW = qb.shape
    blk = min(blk, T)
    nb = T // blk
    qi_arr, kj_arr = _causal_pairs(nb, False)

    def body(qi_ref, kj_ref, q_ref, k_ref, v_ref, cq_ref, ck_ref, ob_ref, lse_ref, m_scr, l_scr, acc):
        h, t = pl.program_id(0), pl.program_id(1)
        qi, kj = qi_ref[t], kj_ref[t]

        @pl.when(kj == 0)
        def _():
            m_scr[...] = jnp.full_like(m_scr, NEG)
            l_scr[...] = jnp.zeros_like(l_scr)
            acc[...] = jnp.zeros_like(acc)

        def update(diagonal):
            cq = _lane_col(cq_ref[...], lane0 + h)
            ck = _sub_row(ck_ref[...], (lane0 + h) % SUB)
            s = _fox_scores(q_ref[...], k_ref[...], cq, ck, diagonal)
            m_old = m_scr[...]
            m_new = jnp.maximum(m_old, jnp.max(s, axis=1, keepdims=True))
            alpha = jnp.exp(m_old - m_new)
            p = jnp.exp(s - m_new)
            l_scr[...] = alpha * l_scr[...] + jnp.sum(p, axis=1, keepdims=True)
            acc[...] = alpha * acc[...] + _dot(p.astype(BF16), v_ref[...])
            m_scr[...] = m_new

        @pl.when(kj < qi)
        def _():
            update(False)

        @pl.when(kj == qi)
        def _():
            update(True)
            ob_ref[...] = (acc[...] / l_scr[...]).astype(BF16)
            lse_ref[0] = m_scr[...] + jnp.log(l_scr[...])

    qs = pl.BlockSpec((blk, HEAD), lambda h, t, qr, kr: (qr[t], h))
    ks = pl.BlockSpec((blk, HEAD), lambda h, t, qr, kr: (kr[t], h))
    gs = pltpu.PrefetchScalarGridSpec(
        num_scalar_prefetch=2, grid=(FH, qi_arr.shape[0]),
        in_specs=[qs, ks, ks, pl.BlockSpec((blk, LANES), lambda h, t, qr, kr: (qr[t], 0)),
                  pl.BlockSpec((SUB, blk), lambda h, t, qr, kr: ((lane0 + h) // SUB, kr[t]))],
        out_specs=[qs, pl.BlockSpec((1, blk, 1), lambda h, t, qr, kr: (h, qr[t], 0))],
        scratch_shapes=[pltpu.VMEM((blk, 1), F32), pltpu.VMEM((blk, 1), F32), pltpu.VMEM((blk, HEAD), F32)])
    return _pcall(
        body, name="flash_fwd", grid_spec=gs,
        out_shape=[jax.ShapeDtypeStruct((T, FW), BF16), jax.ShapeDtypeStruct((FH, T, 1), F32)],
        compiler_params=_cp("parallel", "arbitrary"),
    )(qi_arr, kj_arr, qb, kb, vb, cum, cumt)


def _flash_bwd_q(qb, kb, vb, cum, cumt, lse, do, dl, lane0, FH, blk):
    T, FW = qb.shape
    blk = min(blk, T)
    nb = T // blk
    qi_arr, kj_arr = _causal_pairs(nb, False)
    want_dq = dl is not None

    def body(qi_ref, kj_ref, q_ref, k_ref, v_ref, cq_ref, ck_ref, lse_ref, do_ref, *rest):
        if want_dq:
            dl_ref, dq_ref, rs_ref, acc, rs_acc = rest
        else:
            dl_ref, acc = rest
        h, t = pl.program_id(0), pl.program_id(1)
        qi, kj = qi_ref[t], kj_ref[t]

        @pl.when(kj == 0)
        def _():
            acc[...] = jnp.zeros_like(acc)
            if want_dq:
                rs_acc[...] = jnp.zeros_like(rs_acc)

        def update(diagonal):
            cq = _lane_col(cq_ref[...], lane0 + h)
            ck = _sub_row(ck_ref[...], (lane0 + h) % SUB)
            s = _fox_scores(q_ref[...], k_ref[...], cq, ck, diagonal)
            p = jnp.exp(s - lse_ref[0])
            dp = _dot_nt(do_ref[...].astype(BF16), v_ref[...])
            if want_dq:
                ds = p * (dp - dl_ref[0])
                acc[...] += _dot(ds.astype(BF16), k_ref[...])
                rs_acc[...] += jnp.sum(ds, axis=1, keepdims=True)
            else:
                acc[...] += jnp.sum(p * dp, axis=1, keepdims=True)

        @pl.when(kj < qi)
        def _():
            update(False)

        @pl.when(kj == qi)
        def _():
            update(True)
            if want_dq:
                dq_ref[...] = acc[...] * (HEAD ** -0.5)
                rs_ref[0] = rs_acc[...]
            else:
                dl_ref[0] = acc[...]

    qs = pl.BlockSpec((blk, HEAD), lambda h, t, qr, kr: (qr[t], h))
    ks = pl.BlockSpec((blk, HEAD), lambda h, t, qr, kr: (kr[t], h))
    col = pl.BlockSpec((1, blk, 1), lambda h, t, qr, kr: (h, qr[t], 0))
    in_specs = [qs, ks, ks, pl.BlockSpec((blk, LANES), lambda h, t, qr, kr: (qr[t], 0)),
                pl.BlockSpec((SUB, blk), lambda h, t, qr, kr: ((lane0 + h) // SUB, kr[t])), col, qs]
    args = [qi_arr, kj_arr, qb, kb, vb, cum, cumt, lse, do]
    colshape = jax.ShapeDtypeStruct((FH, T, 1), F32)
    if want_dq:
        gs = pltpu.PrefetchScalarGridSpec(
            num_scalar_prefetch=2, grid=(FH, qi_arr.shape[0]), in_specs=in_specs + [col], out_specs=[qs, col],
            scratch_shapes=[pltpu.VMEM((blk, HEAD), F32), pltpu.VMEM((blk, 1), F32)])
        return _pcall(body, name="flash_bwd_dq", grid_spec=gs,
                      out_shape=[jax.ShapeDtypeStruct((T, FW), F32), colshape],
                      compiler_params=_cp("parallel", "arbitrary"))(*args, dl)
    gs = pltpu.PrefetchScalarGridSpec(
        num_scalar_prefetch=2, grid=(FH, qi_arr.shape[0]), in_specs=in_specs, out_specs=col,
        scratch_shapes=[pltpu.VMEM((blk, 1), F32)])
    return _pcall(body, name="flash_bwd_rowterm", grid_spec=gs, out_shape=colshape,
                  compiler_params=_cp("parallel", "arbitrary"))(*args)


def _flash_bwd_dkv(qb, kb, vb, cum, cumt, lse_row, dl_row, do, lane0, FH, blk):
    T, FW = qb.shape
    blk = min(blk, T)
    nb = T // blk

    qi_arr, kj_arr = _causal_pairs(nb, True)

    def body(qi_ref, kj_ref, q_ref, k_ref, v_ref, ck_ref, cq_ref, lse_ref, dl_ref, do_ref,
             dk_ref, dv_ref, dc_ref, dk_acc, dv_acc, dc_acc):
        h, t = pl.program_id(0), pl.program_id(1)
        qi, kj = qi_ref[t], kj_ref[t]

        def update(diagonal):
            ck = _lane_col(ck_ref[...], lane0 + h)
            cq = _sub_row(cq_ref[...], (lane0 + h) % SUB)
            st = _dot_nt(k_ref[...], q_ref[...]) * (HEAD ** -0.5) + (cq - ck)
            if diagonal:
                krow = lax.broadcasted_iota(jnp.int32, st.shape, 0)
                qcol = lax.broadcasted_iota(jnp.int32, st.shape, 1)
                st = jnp.where(qcol >= krow, st, NEG)
            pt = jnp.exp(st - lse_ref[0])
            dob = do_ref[...].astype(BF16)
            dpt = _dot_nt(v_ref[...], dob)
            dst = pt * (dpt - dl_ref[0])
            return _dot(dst.astype(BF16), q_ref[...]), _dot(pt.astype(BF16), dob), -jnp.sum(dst, axis=1, keepdims=True)

        @pl.when(qi == kj)
        def _():
            dk, dv, dc = update(True)
            dk_acc[...] = dk
            dv_acc[...] = dv
            dc_acc[...] = dc

        @pl.when(qi > kj)
        def _():
            dk, dv, dc = update(False)
            dk_acc[...] += dk
            dv_acc[...] += dv
            dc_acc[...] += dc

        @pl.when(qi == nb - 1)
        def _():
            dk_ref[...] = dk_acc[...] * (HEAD ** -0.5)
            dv_ref[...] = dv_acc[...]
            dc_ref[0] = dc_acc[...]

    ks = pl.BlockSpec((blk, HEAD), lambda h, t, qr, kr: (kr[t], h))
    qs = pl.BlockSpec((blk, HEAD), lambda h, t, qr, kr: (qr[t], h))
    rowq = pl.BlockSpec((1, 1, blk), lambda h, t, qr, kr: (h, 0, qr[t]))
    gs = pltpu.PrefetchScalarGridSpec(
        num_scalar_prefetch=2, grid=(FH, qi_arr.shape[0]),
        in_specs=[qs, ks, ks, pl.BlockSpec((blk, LANES), lambda h, t, qr, kr: (kr[t], 0)),
                  pl.BlockSpec((SUB, blk), lambda h, t, qr, kr: ((lane0 + h) // SUB, qr[t])), rowq, rowq, qs],
        out_specs=[ks, ks, pl.BlockSpec((1, blk, 1), lambda h, t, qr, kr: (h, kr[t], 0))],
        scratch_shapes=[pltpu.VMEM((blk, HEAD), F32), pltpu.VMEM((blk, HEAD), F32), pltpu.VMEM((blk, 1), F32)])
    return _pcall(
        body, name="flash_bwd_dkv", grid_spec=gs,
        out_shape=[jax.ShapeDtypeStruct((T, FW), F32), jax.ShapeDtypeStruct((T, FW), F32),
                   jax.ShapeDtypeStruct((FH, T, 1), F32)],
        compiler_params=_cp("parallel", "arbitrary"),
    )(qi_arr, kj_arr, qb, kb, vb, cum, cumt, lse_row, dl_row, do)


def _rev_cumsum_rows(r1, r2, tb=512):
    H, T = r1.shape
    tb = min(tb, T)
    nb = T // tb

    def body(r1_ref, r2_ref, o_ref, carry):
        @pl.when(pl.program_id(0) == 0)
        def _():
            carry[...] = jnp.zeros_like(carry)

        rv = r1_ref[...] + r2_ref[...]
        si = lax.broadcasted_iota(jnp.int32, (tb, tb), 0)
        ti = lax.broadcasted_iota(jnp.int32, (tb, tb), 1)
        o_ref[...] = _dot(rv, (si >= ti).astype(F32), HI) + carry[...]
        carry[...] += jnp.sum(rv, axis=1, keepdims=True)

    spec = pl.BlockSpec((H, tb), lambda i: (0, nb - 1 - i))
    return _pcall(body, name="rev_cumsum", grid=(nb,), in_specs=[spec, spec], out_specs=spec,
                  out_shape=jax.ShapeDtypeStruct((H, T), F32), scratch_shapes=[pltpu.VMEM((H, 1), F32)],
                  compiler_params=_cp("arbitrary"))(r1, r2)


def _mem_head(q, k, v, gq, gk):
    logits = _dot_nt(_rms(q, gq), _rms(k, gk)) * (MEM_DH ** -0.5)
    mx = jnp.max(logits, axis=1, keepdims=True)
    e = jnp.exp(logits - mx)
    p = e / jnp.sum(e, axis=1, keepdims=True)
    return _dot(p, v)


def _mem_attn(proj, off_q, kv, gq, gk, MW, tb=512):
    T = proj.shape[0]
    MH = MW // MEM_DH

    def fn(i, n, q, kvv, gqv, gkv):
        ks, vs = _heads(kvv[:, :MW], MEM_DH), _heads(kvv[:, MW:], MEM_DH)
        return (_cat([_mem_head(a, b, c, gqv, gkv) for a, b, c in zip(_heads(q, MEM_DH), ks, vs)]),)

    return _rowwise(fn, T, tb, [("row", proj, off_q, MW), ("full", kv), ("full", gq), ("full", gk)],
                    [("row", MW, BF16)], "mem_attn")[0]


def _mem_attn_bwd(proj, off_q, kv, gq, gk, dout, MW, tb=256):
    T = proj.shape[0]
    ML = kv.shape[0]

    def fn(i, n, q, kvv, gqv, gkv, dv):
        ks, vs = _heads(kvv[:, :MW], MEM_DH), _heads(kvv[:, MW:], MEM_DH)
        dqs, dks, dvs = [], [], []
        dgq, dgk = jnp.zeros_like(gqv), jnp.zeros_like(gkv)
        for a, b, c, d in zip(_heads(q, MEM_DH), ks, vs, _heads(dv, MEM_DH)):
            _, vjp = jax.vjp(_mem_head, a, b, c, gqv, gkv)
            da, db, dc, dg1, dg2 = vjp(d)
            dqs.append(da)
            dks.append(db)
            dvs.append(dc)
            dgq, dgk = dgq + dg1, dgk + dg2
        return _cat(dqs), _cat(dks + dvs), dgq, dgk

    return _rowwise(fn, T, tb, [("row", proj, off_q, MW), ("full", kv), ("full", gq), ("full", gk), ("row", dout, 0, MW)],
                    [("row", MW, BF16), ("acc", (ML, 2 * MW), F32), ("acc", (1, MEM_DH), F32), ("acc", (1, MEM_DH), F32)],
                    "mem_attn_bwd")


def _merge_fn(ga, gb, gm, ua, ub, um):
    return _sigmoid(ga) * ua + _sigmoid(gb) * ub + _sigmoid(gm) * um


def _merge(proj, offs, ua, ub, um, D, tb=256):
    T = proj.shape[0]
    ins = [("row", proj, offs[k], D) for k in ("ga", "gb", "gm")] + [("row", u, 0, D) for u in (ua, ub, um)]
    return _rowwise(lambda i, n, *v: (_merge_fn(*v),), T, tb, ins, [("row", D, BF16)], "merge")[0]


def _merge_bwd(proj, offs, ua, ub, um, dy, D, tb=256):
    T = proj.shape[0]

    def fn(i, n, *v):
        _, vjp = jax.vjp(_merge_fn, *v[:6])
        return vjp(v[6])

    ins = [("row", proj, offs[k], D) for k in ("ga", "gb", "gm")] + [("row", u, 0, D) for u in (ua, ub, um)] + [("row", dy, 0, D)]
    return _rowwise(fn, T, tb, ins, [("row", D, BF16)] * 6, "merge_bwd")


def _loss_grad(x2, tgt, tb=256):
    T, D = x2.shape

    def fn(i, n, a, b):
        e = a - b
        part = jnp.sum(jnp.sum(e * e, axis=1, keepdims=True), axis=0, keepdims=True) * (0.5 / D)
        g = e * (1.0 / D)
        return g, g, part + jnp.zeros((SUB, LANES), F32)

    return _rowwise(fn, T, tb, [("row", x2, 0, D), ("row", tgt, 0, D)],
                    [("row", D, F32), ("row", D, BF16), ("acc", (SUB, LANES), F32)], "loss_grad")


def _adamw(w, g, m, v, name, tb=128):
    R, C = w.shape
    c1 = 1.0 / (1.0 - ADAM_B1 ** ADAM_STEP)
    c2 = 1.0 / (1.0 - ADAM_B2 ** ADAM_STEP)

    def fn(i, n, wv, gv, mv, vv):
        mn = ADAM_B1 * mv + (1.0 - ADAM_B1) * gv
        vn = ADAM_B2 * vv + (1.0 - ADAM_B2) * (gv * gv)
        delta = -ADAM_LR * ((mn * c1) / (jnp.sqrt(vn * c2) + ADAM_EPS) + ADAM_WD * wv)
        return delta, mn, vn

    return _rowwise(fn, R, tb, [("row", a, 0, C) for a in (w, g, m, v)], [("row", C, F32)] * 3, name)


def _coords():
    return lax.axis_index("x"), lax.axis_index("y"), lax.axis_index("c")


def _allgather_small(blk):
    m_per, n = blk.shape

    def body(x_ref, out_ref, send_sems, recv_sems, local_sem):
        x, y, c = _coords()
        me, sibling = (x, y, c), (x, y, 1 - c)
        chips = [(1 - x, y), (x, 1 - y), (1 - x, 1 - y)]

        def rows(px, py, pc):
            return out_ref.at[pl.ds((4 * px + 2 * py + pc) * m_per, m_per), :]

        def copy(k, block, to, src=None):
            return pltpu.make_async_remote_copy(
                src_ref=rows(*block) if src is None else src, dst_ref=rows(*block),
                send_sem=send_sems.at[k], recv_sem=recv_sems.at[k], device_id=to, device_id_type=MESH)

        mine = pltpu.make_async_copy(x_ref, rows(*me), local_sem)
        mine.start()
        first = [copy(0, me, sibling, src=x_ref)]
        first += [copy(1 + j, me, (*chip, c), src=x_ref) for j, chip in enumerate(chips)]
        for cp in first:
            cp.start()
        passed = [copy(4 + j, (*chip, c), sibling) for j, chip in enumerate(chips)]
        for j, chip in enumerate(chips):
            copy(1 + j, (*chip, c), me).wait_recv()
            passed[j].start()
        copy(0, sibling, me).wait_recv()
        for j, chip in enumerate(chips):
            copy(4 + j, (*chip, 1 - c), me).wait_recv()
        for cp in first + passed:
            cp.wait_send()
        mine.wait()

    return _pcall(
        body, name="allgather_small", out_shape=jax.ShapeDtypeStruct((8 * m_per, n), blk.dtype),
        in_specs=[pl.BlockSpec(memory_space=pltpu.VMEM)], out_specs=pl.BlockSpec(memory_space=pltpu.VMEM),
        scratch_shapes=[pltpu.SemaphoreType.DMA((7,)), pltpu.SemaphoreType.DMA((7,)), pltpu.SemaphoreType.DMA],
        compiler_params=pltpu.CompilerParams(vmem_limit_bytes=VMEM_LIMIT),
    )(blk)


def _sum8(g, m_per):
    n = g.shape[1]

    def body(g_ref, o_ref):
        acc = g_ref[pl.ds(0, m_per), :]
        for d in range(1, 8):
            acc = acc + g_ref[pl.ds(d * m_per, m_per), :]
        o_ref[...] = acc

    return _pcall(body, name="sum8", out_shape=jax.ShapeDtypeStruct((m_per, n), g.dtype))(g)


_ANY = pl.BlockSpec(memory_space=pl.ANY)


def _allgather_chips(buf):
    nr, w = buf.shape
    half = nr // 2

    def body(in_ref, out_ref, send_sems, recv_sems):
        x, y, c = _coords()
        me = 2 * x + y
        chips = [(1 - x, y), (x, 1 - y), (1 - x, 1 - y)]
        mine_rows = pl.ds(pl.multiple_of(c * half, 16), half)
        other_rows = pl.ds(pl.multiple_of((1 - c) * half, 16), half)

        def copy(k, src, dst, to):
            return pltpu.make_async_remote_copy(src_ref=src, dst_ref=dst, send_sem=send_sems.at[k],
                                                recv_sem=recv_sems.at[k], device_id=to, device_id_type=MESH)

        first = [copy(j, in_ref.at[mine_rows], out_ref.at[me, mine_rows], (cx, cy, c)) for j, (cx, cy) in enumerate(chips)]
        for cp in first:
            cp.start()
        passed = []
        for j, (cx, cy) in enumerate(chips):
            slot = out_ref.at[2 * cx + cy, mine_rows]
            copy(j, slot, slot, (cx, cy, c)).wait_recv()
            fwd = copy(3 + j, slot, slot, (x, y, 1 - c))
            fwd.start()
            passed.append(fwd)
        for j, (cx, cy) in enumerate(chips):
            slot = out_ref.at[2 * cx + cy, other_rows]
            copy(3 + j, slot, slot, (x, y, 1 - c)).wait_recv()
        for cp in first + passed:
            cp.wait_send()

    return _pcall(
        body, name="allgather_chips", out_shape=jax.ShapeDtypeStruct((4, nr, w), buf.dtype),
        in_specs=[_ANY], out_specs=_ANY,
        scratch_shapes=[pltpu.SemaphoreType.DMA((6,)), pltpu.SemaphoreType.DMA((6,))],
    )(buf)


def _rs_pair_exchange(g):
    _, nr, w = g.shape
    half = nr // 2

    def body(g_ref, rb_ref, send_sem, recv_sem):
        x, y, c = _coords()
        other_rows = pl.ds(pl.multiple_of((1 - c) * half, SUB), half)
        cp = pltpu.make_async_remote_copy(src_ref=g_ref.at[:, other_rows], dst_ref=rb_ref, send_sem=send_sem,
                                          recv_sem=recv_sem, device_id=(x, y, 1 - c), device_id_type=MESH)
        cp.start()
        cp.wait()

    return _pcall(body, name="rs_pair_exchange", out_shape=jax.ShapeDtypeStruct((4, half, w), g.dtype),
                  in_specs=[_ANY], out_specs=_ANY,
                  scratch_shapes=[pltpu.SemaphoreType.DMA, pltpu.SemaphoreType.DMA])(g)


def _rs_pair_add(g, rb, cidx, tb=256):
    _, nr, w = g.shape
    half = nr // 2
    tb = min(tb, half)
    assert half % tb == 0
    hb = half // tb

    def body(c_ref, g_ref, r_ref, o_ref):
        o_ref[...] = g_ref[...] + r_ref[...]

    gs = pltpu.PrefetchScalarGridSpec(
        num_scalar_prefetch=1, grid=(4, hb),
        in_specs=[pl.BlockSpec((1, tb, w), lambda j, i, c_ref: (j, c_ref[0] * hb + i, 0)),
                  pl.BlockSpec((1, tb, w), lambda j, i, c_ref: (j, i, 0))],
        out_specs=pl.BlockSpec((1, tb, w), lambda j, i, c_ref: (j, i, 0)))
    return _pcall(body, name="rs_pair_add", grid_spec=gs, out_shape=jax.ShapeDtypeStruct((4, half, w), g.dtype),
                  compiler_params=_cp("parallel", "parallel"))(cidx, g, rb)


def _rs_chip_exchange(p):
    _, h, w = p.shape

    def body(p_ref, rb_ref, send_sems, recv_sems):
        x, y, c = _coords()
        chips = [(1 - x, y), (x, 1 - y), (1 - x, 1 - y)]
        cps = [pltpu.make_async_remote_copy(src_ref=p_ref.at[2 * cx + cy], dst_ref=rb_ref.at[j], send_sem=send_sems.at[j],
                                            recv_sem=recv_sems.at[j], device_id=(cx, cy, c), device_id_type=MESH)
               for j, (cx, cy) in enumerate(chips)]
        for cp in cps:
            cp.start()
        for cp in cps:
            cp.wait()

    return _pcall(body, name="rs_chip_exchange", out_shape=jax.ShapeDtypeStruct((3, h, w), p.dtype),
                  in_specs=[_ANY], out_specs=_ANY,
                  scratch_shapes=[pltpu.SemaphoreType.DMA((3,)), pltpu.SemaphoreType.DMA((3,))])(p)


def _sum4(p, rb, chip_idx, tb=256):
    _, h, w = rb.shape
    tb = min(tb, h)
    assert h % tb == 0

    def body(m_ref, p_ref, r_ref, o_ref):
        o_ref[...] = ((p_ref[0] + r_ref[0]) + r_ref[1]) + r_ref[2]

    gs = pltpu.PrefetchScalarGridSpec(
        num_scalar_prefetch=1, grid=(h // tb,),
        in_specs=[pl.BlockSpec((1, tb, w), lambda i, m_ref: (m_ref[0], i, 0)),
                  pl.BlockSpec((3, tb, w), lambda i, m_ref: (0, i, 0))],
        out_specs=pl.BlockSpec((tb, w), lambda i, m_ref: (i, 0)))
    return _pcall(body, name="sum4", grid_spec=gs, out_shape=jax.ShapeDtypeStruct((h, w), rb.dtype),
                  compiler_params=_cp("parallel"))(chip_idx, p, rb)


def _pair_allgather(f):
    h, w = f.shape

    def body(f_ref, out_ref, send_sem, recv_sem):
        x, y, c = _coords()
        mine_rows = pl.ds(pl.multiple_of(c * h, SUB), h)
        other_rows = pl.ds(pl.multiple_of((1 - c) * h, SUB), h)
        send = pltpu.make_async_remote_copy(src_ref=f_ref, dst_ref=out_ref.at[mine_rows], send_sem=send_sem,
                                            recv_sem=recv_sem, device_id=(x, y, 1 - c), device_id_type=MESH)
        send.start()
        send.wait_send()
        pltpu.make_async_remote_copy(src_ref=f_ref, dst_ref=out_ref.at[other_rows], send_sem=send_sem,
                                     recv_sem=recv_sem, device_id=(x, y, 1 - c), device_id_type=MESH).wait_recv()

    return _pcall(body, name="pair_allgather", out_shape=jax.ShapeDtypeStruct((2 * h, w), f.dtype),
                  in_specs=[_ANY], out_specs=_ANY,
                  scratch_shapes=[pltpu.SemaphoreType.DMA, pltpu.SemaphoreType.DMA])(f)


def _pack(arrs, dtype, row_mult):
    flat = jnp.concatenate([a.astype(dtype).reshape(-1) for a in arrs])
    n = flat.shape[0]
    rows = _ru(-(-n // PACK_W), row_mult)
    return jnp.pad(flat, (0, rows * PACK_W - n)).reshape(rows, PACK_W)


def _unpack(buf, shapes):
    flat = buf.reshape(-1)
    out, off = [], 0
    for s in shapes:
        n = 1
        for d in s:
            n *= d
        out.append(flat[off:off + n].reshape(s))
        off += n
    return out


def _in_layout(D, GW, GH, FW, FH, MW, tn):
    o_z = 3 * GW
    o_beta = 4 * GW
    o_fq = o_beta + 2 * GH
    o_ff = o_fq + 3 * FW
    o_mq = o_ff + FH
    o_g = o_mq + MW
    orig = {"q": (0, GW), "k": (GW, GW), "v": (2 * GW, GW), "z": (o_z, GW), "beta": (o_beta, GH), "dec": (o_beta + GH, GH),
            "fq": (o_fq, FW), "fk": (o_fq + FW, FW), "fv": (o_fq + 2 * FW, FW), "ff": (o_ff, FH), "mq": (o_mq, MW),
            "ga": (o_g, D), "gb": (o_g + D, D), "gm": (o_g + 2 * D, D)}
    offs, cur = {}, 0
    for key, width in (("ga", D), ("gb", D), ("gm", D), ("q", GW), ("k", GW), ("v", GW), ("z", GW),
                       ("fq", FW), ("fk", FW), ("fv", FW), ("mq", MW), ("small", LANES)):
        cur = _ru(cur, width)
        offs[key] = cur
        cur += width
    total = _ru(cur, tn)
    pieces = [(offs[k], orig[k][0], orig[k][1]) for k in ("ga", "gb", "gm", "q", "k", "v", "z", "fq", "fk", "fv", "mq")]
    pieces += [(offs["small"], orig["beta"][0], GH), (offs["small"] + GH, orig["dec"][0], GH),
               (offs["small"] + 2 * GH, orig["ff"][0], FH)]
    return offs, total, pieces, o_g + 3 * D


def _pad_cols(w, pieces, total):
    parts, cur = [], 0
    for pstart, ostart, n in pieces:
        if pstart > cur:
            parts.append(jnp.zeros((w.shape[0], pstart - cur), w.dtype))
        parts.append(w[:, ostart:ostart + n])
        cur = pstart + n
    if total > cur:
        parts.append(jnp.zeros((w.shape[0], total - cur), w.dtype))
    return jnp.concatenate(parts, axis=1)


def _unpad_cols(wp, pieces):
    return jnp.concatenate([wp[:, pstart:pstart + n] for pstart, ostart, n in sorted(pieces, key=lambda t: t[1])], axis=1)


def _local_step(x, mem, tgt, W, flash_blk=512):
    T, D = x.shape
    GW = W["w_up_gdn"].shape[0]
    FW = W["w_up_fox"].shape[0]
    MW = W["w_up_mem"].shape[0]
    GH, FH = GW // HEAD, FW // HEAD
    offs, NP, pieces, d_in = _in_layout(D, GW, GH, FW, FH, MW, 1024)
    assert W["w_in"].shape[1] == d_in
    w_in_p = _pad_cols(W["w_in"], pieces, NP)
    cw = W["conv_w"]
    cws = [cw[:, i * GW:(i + 1) * GW] for i in range(3)]
    zl = jnp.zeros((1, LANES), F32)
    pvecs = [lax.dynamic_update_slice(zl, W["a_log"], (0, GH)), lax.dynamic_update_slice(zl, W["dt_bias"], (0, GH)),
             lax.dynamic_update_slice(zl, W["fox_b_f"], (0, 2 * GH))]
    lane0 = 2 * GH

    h = _rms_fwd(x, W["g_mix"], "rms_mix")
    proj = _mm(h, w_in_p, "nn", "in_proj")
    qn, kn, vc, cum, gbm = _gdn_prep(proj, offs, cws, pvecs, GH, FH, GW)
    o_gdn, sall = _gdn_scan_fwd(qn, kn, vc, gbm, GH)
    o_a = _gdn_post(o_gdn, proj, offs["z"], W["gdn_norm_g"], GW)
    qb, kb, vb = _fox_prep(proj, offs, W["fox_q_norm"], W["fox_k_norm"], FW)
    cumt = cum.T
    o_b16, lse = _flash_fwd(qb, kb, vb, cum, cumt, lane0, FH, flash_blk)
    memn = _rms_fwd(mem, W["g_mem"], "rms_mem")
    kv = _mm(memn, W["w_mem_kv"], "nn", "mem_kv")
    o_m = _mem_attn(proj, offs["mq"], kv, W["mem_q_norm"], W["mem_k_norm"], MW)
    ua = _mm(o_a, W["w_up_gdn"], "nn", "up_gdn")
    ub = _mm(o_b16, W["w_up_fox"], "nn", "up_fox")
    um = _mm(o_m, W["w_up_mem"], "nn", "up_mem")
    y = _merge(proj, offs, ua, ub, um, D)
    x1 = _mm(y, W["w_out"], "nn", "out_proj", epilogue=lambda acc, r: (acc + r,), extras=(x,))
    h2 = _rms_fwd(x1, W["g_mlp"], "rms_mlp")
    u, a = _mm(h2, W["w_ff1"], "nn", "ff1", out_dtypes=(F32, BF16),
               epilogue=lambda acc: (acc, jnp.square(jnp.maximum(acc, 0.0))))
    x2 = _mm(a, W["w_ff2"], "nn", "ff2", epilogue=lambda acc, r: (acc + r,), extras=(x1,))
    dx2, dx2b, lpart = _loss_grad(x2, tgt)
    loss = lpart[0, 0]

    G = {}
    du = _mm(dx2b, W["w_ff2"], "nt", "ff2_dx", out_dtypes=(BF16,),
             epilogue=lambda acc, uu: (acc * (2.0 * jnp.maximum(uu, 0.0)),), extras=(u,))
    G["w_ff2"] = _mm(a, dx2b, "tn", "ff2_dw")
    dh2 = _mm(du, W["w_ff1"], "nt", "ff1_dx")
    G["w_ff1"] = _mm(h2, du, "tn", "ff1_dw")
    dx1, dx1b, G["g_mlp"] = _rms_bwd(x1, W["g_mlp"], dh2, dx2, "rms_mlp_bwd")
    dy = _mm(dx1b, W["w_out"], "nt", "out_dx")
    G["w_out"] = _mm(y, dx1b, "tn", "out_dw")
    dga, dgb, dgm, dua, dub, dum = _merge_bwd(proj, offs, ua, ub, um, dy, D)
    do_a = _mm(dua, W["w_up_gdn"], "nt", "up_gdn_dx")
    G["w_up_gdn"] = _mm(o_a, dua, "tn", "up_gdn_dw")
    do_b = _mm(dub, W["w_up_fox"], "nt", "up_fox_dx")
    G["w_up_fox"] = _mm(o_b16, dub, "tn", "up_fox_dw")
    do_m = _mm(dum, W["w_up_mem"], "nt", "up_mem_dx")
    G["w_up_mem"] = _mm(o_m, dum, "tn", "up_mem_dw")
    dmq, dkv, G["mem_q_norm"], G["mem_k_norm"] = _mem_attn_bwd(proj, offs["mq"], kv, W["mem_q_norm"], W["mem_k_norm"], do_m, MW)
    dkvb = dkv.astype(BF16)
    dmemn = _mm(dkvb, W["w_mem_kv"], "nt", "mem_kv_dx")
    G["w_mem_kv"] = _mm(memn, dkvb, "tn", "mem_kv_dw")
    G["g_mem"] = _rms_dg(mem, W["g_mem"], dmemn, "rms_mem_bwd")
    dl = _flash_bwd_q(qb, kb, vb, cum, cumt, lse, do_b, None, lane0, FH, flash_blk)
    dqb, dcq = _flash_bwd_q(qb, kb, vb, cum, cumt, lse, do_b, dl, lane0, FH, flash_blk)
    dkb, dvb, dck = _flash_bwd_dkv(qb, kb, vb, cum, cumt, lse.reshape(FH, 1, T), dl.reshape(FH, 1, T), do_b, lane0, FH, flash_blk)
    dlf = _rev_cumsum_rows(dcq.reshape(FH, T), dck.reshape(FH, T))
    dlf_sm = jnp.pad(dlf.T, ((0, 0), (lane0, LANES - lane0 - FH)))
    dfq, dfk, dfv, G["fox_q_norm"], G["fox_k_norm"] = _fox_prep_bwd(proj, offs, W["fox_q_norm"], W["fox_k_norm"], dqb, dkb, dvb, FW)
    do_gdn, dz, G["gdn_norm_g"] = _gdn_post_bwd(o_gdn, proj, offs["z"], W["gdn_norm_g"], do_a, GW)
    dqn, dkn, dvc, dgsm = _gdn_scan_bwd(qn, kn, vc, gbm, sall, do_gdn, GH)
    dyq, dyk, dyv, dcq, dck_w, dcv, dsmall, dalog, ddtb, dbf = _gdn_prep_bwd_a(
        proj, offs, cws, pvecs, (dqn, dkn, dvc), dgsm, dlf_sm, GH, FH, GW)
    dxq, dxk, dxv = _gdn_prep_bwd_b((dyq, dyk, dyv), cws, GW)
    G["conv_w"] = jnp.concatenate([dcq[:CONV_K], dck_w[:CONV_K], dcv[:CONV_K]], axis=1)
    G["a_log"] = dalog[:, GH:2 * GH]
    G["dt_bias"] = ddtb[:, GH:2 * GH]
    G["fox_b_f"] = dbf[:, lane0:lane0 + FH]
    segs = {"ga": dga, "gb": dgb, "gm": dgm, "q": dxq, "k": dxk, "v": dxv, "z": dz, "fq": dfq, "fk": dfk, "fv": dfv,
            "mq": dmq, "small": dsmall}
    parts, cur = [], 0
    for key in ("ga", "gb", "gm", "q", "k", "v", "z", "fq", "fk", "fv", "mq", "small"):
        if offs[key] > cur:
            parts.append(jnp.zeros((T, offs[key] - cur), BF16))
        parts.append(segs[key])
        cur = offs[key] + segs[key].shape[1]
    if NP > cur:
        parts.append(jnp.zeros((T, NP - cur), BF16))
    dproj = jnp.concatenate(parts, axis=1)
    dh = _mm(dproj, w_in_p, "nt", "in_dx")
    G["w_in"] = _unpad_cols(_mm(h, dproj, "tn", "in_dw"), pieces)
    grad_x, _, G["g_mix"] = _rms_bwd(x, W["g_mix"], dh, dx1, "rms_mix_bwd")
    return loss, grad_x, G


BIG = ["w_in", "w_mem_kv", "w_up_gdn", "w_up_fox", "w_up_mem", "w_out", "w_ff1", "w_ff2"]
SMALL = ["g_mix", "a_log", "dt_bias", "gdn_norm_g", "fox_b_f", "fox_q_norm", "fox_k_norm", "g_mem", "mem_q_norm",
         "mem_k_norm", "g_mlp"]
ORDER = ["g_mix", "w_in", "conv_w", "a_log", "dt_bias", "gdn_norm_g", "fox_b_f", "fox_q_norm", "fox_k_norm", "g_mem",
         "w_mem_kv", "mem_q_norm", "mem_k_norm", "w_up_gdn", "w_up_fox", "w_up_mem", "w_out", "g_mlp", "w_ff1", "w_ff2"]
SHARD_AXIS = {"w_in": 1, "w_mem_kv": 0, "w_up_gdn": 1, "w_up_fox": 1, "w_up_mem": 1, "w_out": 0, "w_ff1": 1, "w_ff2": 0}


def _step(x, mem, tgt, w, m, v, flash_blk=512):
    xi, yi, ci = _coords()
    chip = 2 * xi + yi

    shard_shapes = [w[n].shape for n in BIG]
    packed_w = _pack([w[n] for n in BIG], BF16, PACK_ROWS)
    gathered = _allgather_chips(packed_w)
    gathered = lax.dynamic_update_slice(gathered, packed_w[None], (chip, 0, 0))
    per_chip = [_unpack(gathered[j], shard_shapes) for j in range(4)]
    W = {n: jnp.concatenate([per_chip[j][i] for j in range(4)], axis=SHARD_AXIS[n]) for i, n in enumerate(BIG)}
    cw_rows = jnp.pad(w["conv_w"], ((0, SUB - CONV_K), (0, 0)))
    cw_all = _allgather_small(cw_rows)
    W["conv_w"] = jnp.concatenate([cw_all[16 * j:16 * j + CONV_K] for j in range(4)], axis=1)
    for n in SMALL:
        W[n] = w[n]

    loss, grad_x, G = _local_step(x, mem, tgt, W, flash_blk)
    loss = lax.psum(loss, ("x", "y", "c"))

    small_shapes = [G[n].shape for n in SMALL] + [G["conv_w"].shape]
    sm = _pack([G[n] for n in SMALL] + [G["conv_w"]], F32, SUB)
    sm_sum = _sum8(_allgather_small(sm), sm.shape[0])
    sm_list = _unpack(sm_sum, small_shapes)
    g = {n: sm_list[i] for i, n in enumerate(SMALL)}
    cw_full = sm_list[-1]
    gw4 = cw_full.shape[1] // 4
    g["conv_w"] = lax.dynamic_slice(cw_full, (0, chip * gw4), (CONV_K, gw4))

    by_dest = []
    for j in range(4):
        shards = []
        for n in BIG:
            size = w[n].shape[SHARD_AXIS[n]]
            shards.append(lax.slice_in_dim(G[n], j * size, (j + 1) * size, axis=SHARD_AXIS[n]))
        by_dest.append(_pack(shards, F32, PACK_ROWS))
    gflat = jnp.stack(by_dest)
    rb1 = _rs_pair_exchange(gflat)
    part = _rs_pair_add(gflat, rb1, jnp.reshape(ci, (1,)).astype(jnp.int32))
    rb2 = _rs_chip_exchange(part)
    half_sum = _sum4(part, rb2, jnp.reshape(chip, (1,)).astype(jnp.int32))
    mine = _pair_allgather(half_sum)
    mine = lax.dynamic_update_slice(mine, half_sum, (ci * half_sum.shape[0], 0))
    for i, gv in enumerate(_unpack(mine, shard_shapes)):
        g[BIG[i]] = gv

    delta, new_m, new_v = {}, {}, {}
    for n in BIG:
        delta[n], new_m[n], new_v[n] = _adamw(w[n], g[n], m[n], v[n], "adamw_" + n)
    rest = SMALL + ["conv_w"]
    rest_shapes = [w[n].shape for n in rest]
    packed = [_pack([d[n] for n in rest], F32, SUB) for d in (w, g, m, v)]
    outs = _adamw(*packed, "adamw_small", tb=packed[0].shape[0])
    for d, buf in zip((delta, new_m, new_v), outs):
        for n, val in zip(rest, _unpack(buf, rest_shapes)):
            d[n] = val
    return loss, grad_x, g, delta, new_m, new_v


def kernel(x, mem, g_mix, w_in, conv_w, a_log, dt_bias, gdn_norm_g, fox_b_f, fox_q_norm, fox_k_norm, g_mem, w_mem_kv, mem_q_norm, mem_k_norm, w_up_gdn, w_up_fox, w_up_mem, w_out, g_mlp, w_ff1, w_ff2, loss_target, m_g_mix, m_w_in, m_conv_w, m_a_log, m_dt_bias, m_gdn_norm_g, m_fox_b_f, m_fox_q_norm, m_fox_k_norm, m_g_mem, m_w_mem_kv, m_mem_q_norm, m_mem_k_norm, m_w_up_gdn, m_w_up_fox, m_w_up_mem, m_w_out, m_g_mlp, m_w_ff1, m_w_ff2, v_g_mix, v_w_in, v_conv_w, v_a_log, v_dt_bias, v_gdn_norm_g, v_fox_b_f, v_fox_q_norm, v_fox_k_norm, v_g_mem, v_w_mem_kv, v_mem_q_norm, v_mem_k_norm, v_w_up_gdn, v_w_up_fox, v_w_up_mem, v_w_out, v_g_mlp, v_w_ff1, v_w_ff2):
    ws = (g_mix, w_in, conv_w, a_log, dt_bias, gdn_norm_g, fox_b_f, fox_q_norm, fox_k_norm, g_mem, w_mem_kv, mem_q_norm,
          mem_k_norm, w_up_gdn, w_up_fox, w_up_mem, w_out, g_mlp, w_ff1, w_ff2)
    ms = (m_g_mix, m_w_in, m_conv_w, m_a_log, m_dt_bias, m_gdn_norm_g, m_fox_b_f, m_fox_q_norm, m_fox_k_norm, m_g_mem,
          m_w_mem_kv, m_mem_q_norm, m_mem_k_norm, m_w_up_gdn, m_w_up_fox, m_w_up_mem, m_w_out, m_g_mlp, m_w_ff1, m_w_ff2)
    vs = (v_g_mix, v_w_in, v_conv_w, v_a_log, v_dt_bias, v_gdn_norm_g, v_fox_b_f, v_fox_q_norm, v_fox_k_norm, v_g_mem,
          v_w_mem_kv, v_mem_q_norm, v_mem_k_norm, v_w_up_gdn, v_w_up_fox, v_w_up_mem, v_w_out, v_g_mlp, v_w_ff1, v_w_ff2)
    drop = lambda a: a[0] if a.ndim == 3 else a
    w = {n: drop(a) for n, a in zip(ORDER, ws)}
    m = {n: drop(a) for n, a in zip(ORDER, ms)}
    v = {n: drop(a) for n, a in zip(ORDER, vs)}
    loss, grad_x, g, delta, new_m, new_v = _step(x[0], mem[0], loss_target[0], w, m, v)
    out = [loss, grad_x[None]]
    for d in (g, delta, new_m, new_v):
        out += [d[n].reshape(a.shape) for n, a in zip(ORDER, ws)]
    return tuple(out)
```

```python
import numpy as np

import jax
import jax.numpy as jnp
from jax import lax
from jax.experimental import pallas as pl
from jax.experimental.pallas import tpu as pltpu

F32 = jnp.float32
BF16 = jnp.bfloat16
HI = lax.Precision.HIGHEST
MESH = pl.DeviceIdType.MESH

EPS = 1e-6
HEAD = 128
MEM_DH = 256
CONV_K = 4
CHUNK = 64
CHUNK_SHIFT = 6
LANES = 128
SUB = 8
PACK_W = 1024
PACK_ROWS = 512
VMEM_LIMIT = 56 * 1024 * 1024
NEG = -1e30
SOLVE_PREC = None

ADAM_LR, ADAM_B1, ADAM_B2, ADAM_EPS, ADAM_WD, ADAM_STEP = 0.001, 0.9, 0.999, 1e-08, 0.01, 10


def _pcall(body, **kw):
    return pl.pallas_call(body, **kw)


def _cp(*sem):
    return pltpu.CompilerParams(dimension_semantics=sem, vmem_limit_bytes=VMEM_LIMIT)


def _dot(a, b, prec=None):
    return lax.dot_general(a, b, (((1,), (0,)), ((), ())), precision=prec, preferred_element_type=F32)


def _dot_nt(a, b, prec=None):
    return lax.dot_general(a, b, (((1,), (1,)), ((), ())), precision=prec, preferred_element_type=F32)


def _dot_tn(a, b, prec=None):
    return lax.dot_general(a, b, (((0,), (0,)), ((), ())), precision=prec, preferred_element_type=F32)


def _sigmoid(x):
    return 1.0 / (1.0 + jnp.exp(-x))


def _softplus(x):
    return jnp.maximum(x, 0.0) + jnp.log(1.0 + jnp.exp(-jnp.abs(x)))


def _silu(x):
    return x * _sigmoid(x)


def _rms(x, g):
    return x * lax.rsqrt(jnp.mean(x * x, axis=-1, keepdims=True) + EPS) * g


def _ru(a, m):
    return (a + m - 1) // m * m


def _mm(a, b, mode, name, out_dtypes=(F32,), epilogue=None, extras=(), tm=1024, tn=512, tk=2048):
    if mode == "nn":
        (M, K), (K2, N) = a.shape, b.shape
    elif mode == "nt":
        (M, K), (N, K2) = a.shape, b.shape
    else:
        (K, M), (K2, N) = a.shape, b.shape
    assert K == K2, (a.shape, b.shape, mode)
    tm, tn = min(tm, M), min(tn, N)
    tk = next((t for t in (tk, 1024, 512, 256, LANES) if t <= K and K % t == 0), K)
    assert M % tm == 0 and N % tn == 0 and K % tk == 0, (M, N, K, tm, tn, tk)
    nk = K // tk
    n_ex, n_out = len(extras), len(out_dtypes)
    dims = {"nn": ((1,), (0,)), "nt": ((1,), (1,)), "tn": ((0,), (0,))}[mode]

    def finish(res, ex_refs, out_refs):
        outs = epilogue(res, *[r[...] for r in ex_refs]) if epilogue is not None else (res,)
        for o_ref, o in zip(out_refs, outs):
            o_ref[...] = o.astype(o_ref.dtype)

    def body(a_ref, b_ref, *rest):
        ex_refs, out_refs = rest[:n_ex], rest[n_ex:n_ex + n_out]
        part = lax.dot_general(a_ref[...], b_ref[...], (dims, ((), ())), preferred_element_type=F32)
        if nk == 1:
            finish(part, ex_refs, out_refs)
            return
        acc = rest[-1]
        k = pl.program_id(2)

        @pl.when(k == 0)
        def _():
            acc[...] = part

        @pl.when(k > 0)
        def _():
            acc[...] += part

        @pl.when(k == nk - 1)
        def _():
            finish(acc[...], ex_refs, out_refs)

    a_spec = pl.BlockSpec((tk, tm), lambda i, j, k: (k, i)) if mode == "tn" else pl.BlockSpec((tm, tk), lambda i, j, k: (i, k))
    b_spec = pl.BlockSpec((tn, tk), lambda i, j, k: (j, k)) if mode == "nt" else pl.BlockSpec((tk, tn), lambda i, j, k: (k, j))
    mn_spec = pl.BlockSpec((tm, tn), lambda i, j, k: (i, j))
    outs = _pcall(
        body, name=name, grid=(M // tm, N // tn, nk),
        in_specs=[a_spec, b_spec] + [mn_spec] * n_ex,
        out_specs=[mn_spec] * n_out,
        out_shape=[jax.ShapeDtypeStruct((M, N), dt) for dt in out_dtypes],
        scratch_shapes=[pltpu.VMEM((tm, tn), F32)] if nk > 1 else [],
        compiler_params=_cp("parallel", "parallel", "arbitrary"),
    )(a, b, *extras)
    return outs[0] if n_out == 1 else outs


def _rowwise(fn, T, tb, ins, outs, name, scratch=()):
    tb = min(tb, T)
    assert T % tb == 0 and (tb % SUB == 0 or tb == T)
    nblk = T // tb
    r8 = tb // SUB
    in_specs, arrs = [], []
    for spec in ins:
        kind, arr = spec[0], spec[1]
        arrs.append(arr)
        if kind == "full":
            nd = arr.ndim
            in_specs.append(pl.BlockSpec(arr.shape, lambda i, nd=nd: (0,) * nd))
            continue
        off, w = spec[2], spec[3]
        assert off % w == 0 and arr.shape[0] == T, (name, off, w, arr.shape)
        cb = off // w
        if kind == "row":
            in_specs.append(pl.BlockSpec((tb, w), lambda i, cb=cb: (i, cb)))
        elif kind == "prev":
            in_specs.append(pl.BlockSpec((SUB, w), lambda i, cb=cb: (jnp.maximum(i * r8 - 1, 0), cb)))
        else:
            in_specs.append(pl.BlockSpec((SUB, w), lambda i, cb=cb: (jnp.minimum((i + 1) * r8, T // SUB - 1), cb)))
    out_specs, out_shapes, is_acc = [], [], []
    for spec in outs:
        if spec[0] == "row":
            out_specs.append(pl.BlockSpec((tb, spec[1]), lambda i: (i, 0)))
            out_shapes.append(jax.ShapeDtypeStruct((T, spec[1]), spec[2]))
            is_acc.append(False)
        else:
            nd = len(spec[1])
            out_specs.append(pl.BlockSpec(spec[1], lambda i, nd=nd: (0,) * nd))
            out_shapes.append(jax.ShapeDtypeStruct(spec[1], spec[2]))
            is_acc.append(True)
    n_in, n_out = len(ins), len(outs)
    seq = any(is_acc) or len(scratch) > 0

    def body(*refs):
        in_refs, out_refs, scr = refs[:n_in], refs[n_in:n_in + n_out], refs[n_in + n_out:]
        i = pl.program_id(0)
        vals = fn(i, nblk, *[r[...] for r in in_refs], *scr)
        for o_ref, v, acc in zip(out_refs, vals, is_acc):
            if acc:
                @pl.when(i == 0)
                def _(o_ref=o_ref):
                    o_ref[...] = jnp.zeros_like(o_ref)

                o_ref[...] += v.astype(o_ref.dtype)
            else:
                o_ref[...] = v.astype(o_ref.dtype)

    res = _pcall(
        body, name=name, grid=(nblk,), in_specs=in_specs, out_specs=out_specs, out_shape=out_shapes,
        scratch_shapes=list(scratch), compiler_params=_cp("arbitrary" if seq else "parallel"),
    )(*arrs)
    return res


def _heads(x, width):
    return [x[:, h * width:(h + 1) * width] for h in range(x.shape[1] // width)]


def _cat(xs):
    return xs[0] if len(xs) == 1 else jnp.concatenate(xs, axis=1)


def _rms_fwd(x, g, name, tb=512):
    T, D = x.shape
    return _rowwise(lambda i, n, xv, gv: (_rms(xv, gv),), T, tb,
                    [("row", x, 0, D), ("full", g)], [("row", D, BF16)], name)[0]


def _rms_bwd(x, g, dh, dres, name, tb=256):
    T, D = x.shape

    def fn(i, n, xv, gv, dhv, drv):
        _, vjp = jax.vjp(_rms, xv, gv)
        dx, dg = vjp(dhv)
        tot = drv + dx
        return tot, tot, dg

    return _rowwise(fn, T, tb, [("row", x, 0, D), ("full", g), ("row", dh, 0, D), ("row", dres, 0, D)],
                    [("row", D, F32), ("row", D, BF16), ("acc", (1, D), F32)], name)


def _rms_dg(x, g, dh, name, tb=256):
    T, D = x.shape

    def fn(i, n, xv, gv, dhv):
        _, vjp = jax.vjp(lambda gg: _rms(xv, gg), gv)
        return vjp(dhv)

    return _rowwise(fn, T, tb, [("row", x, 0, D), ("full", g), ("row", dh, 0, D)], [("acc", (1, D), F32)], name)[0]


def _shift_down(x, halo, s, first):
    if s == 0:
        return x
    tb, c = x.shape
    xr = pltpu.roll(x, s, 0)
    hr = jnp.where(first, 0.0, pltpu.roll(halo, s, 0))
    hfull = hr if tb == SUB else jnp.concatenate([hr, jnp.zeros((tb - SUB, c), x.dtype)], axis=0)
    row = lax.broadcasted_iota(jnp.int32, x.shape, 0)
    return jnp.where(row < s, hfull, xr)


def _shift_up(z, halo, s, last):
    if s == 0:
        return z
    tb, c = z.shape
    zr = pltpu.roll(z, tb - s, 0)
    hr = jnp.where(last, 0.0, pltpu.roll(halo, SUB - s, 0))
    hfull = hr if tb == SUB else jnp.concatenate([jnp.zeros((tb - SUB, c), z.dtype), hr], axis=0)
    row = lax.broadcasted_iota(jnp.int32, z.shape, 0)
    return jnp.where(row >= tb - s, hfull, zr)


def _conv_pre(x, halo, cw, first):
    xs = [_shift_down(x, halo, s, first) for s in range(CONV_K)]
    y = cw[0:1, :] * xs[3]
    for i in range(1, CONV_K):
        y = y + cw[i:i + 1, :] * xs[CONV_K - 1 - i]
    return y, xs


def _qk_post(y, scale):
    a = _silu(y)
    return a * lax.rsqrt(jnp.sum(a * a, axis=-1, keepdims=True) + EPS) * scale


def _small_fn(s, alog, dtb, bf, gh, fh):
    lane = lax.broadcasted_iota(jnp.int32, s.shape, 1)
    beta = _sigmoid(s)
    g = -jnp.exp(alog) * _softplus(s + dtb)
    lf = -_softplus(-(s + bf))
    return jnp.where(lane < gh, beta, jnp.where(lane < 2 * gh, g, jnp.where(lane < 2 * gh + fh, lf, 0.0)))


def _gdn_prep(proj, offs, cws, pvecs, GH, FH, GW, tb=256):
    T = proj.shape[0]
    tb = min(tb, T)
    qscale = HEAD ** -0.5

    def fn(i, n, xq, hq, xk, hk, xv, hv, cwq, cwk, cwv, s, alog, dtb, bf, carry):
        first = i == 0
        yq, _ = _conv_pre(xq, hq, cwq, first)
        yk, _ = _conv_pre(xk, hk, cwk, first)
        yv, _ = _conv_pre(xv, hv, cwv, first)
        qn = _cat([_qk_post(y, qscale) for y in _heads(yq, HEAD)])
        kn = _cat([_qk_post(y, 1.0) for y in _heads(yk, HEAD)])
        vc = _silu(yv)
        gsm = _small_fn(s, alog, dtb, bf, GH, FH)

        @pl.when(first)
        def _():
            carry[...] = jnp.zeros_like(carry)

        ri = lax.broadcasted_iota(jnp.int32, (tb, tb), 0)
        ci = lax.broadcasted_iota(jnp.int32, (tb, tb), 1)
        cum = _dot((ri >= ci).astype(F32), gsm, HI) + carry[0:1, :]
        carry[...] += _dot(jnp.ones((SUB, tb), F32), gsm, HI)
        in_chunk = (ri >= ci) & ((ri >> CHUNK_SHIFT) == (ci >> CHUNK_SHIFT))
        lane = lax.broadcasted_iota(jnp.int32, gsm.shape, 1)
        gbm = jnp.where(lane < GH, gsm, _dot(in_chunk.astype(F32), gsm, HI))
        return qn, kn, vc, cum, gbm

    ins = []
    for key in ("q", "k", "v"):
        ins += [("row", proj, offs[key], GW), ("prev", proj, offs[key], GW)]
    ins += [("full", c) for c in cws] + [("row", proj, offs["small"], LANES)] + [("full", p) for p in pvecs]
    outs = [("row", GW, F32)] * 3 + [("row", LANES, F32)] * 2
    return _rowwise(fn, T, tb, ins, outs, "gdn_prep", scratch=[pltpu.VMEM((SUB, LANES), F32)])


def _gdn_prep_bwd_a(proj, offs, cws, pvecs, cts, dgsm_scan, dlf_sm, GH, FH, GW, tb=256):
    T = proj.shape[0]
    qscale = HEAD ** -0.5

    def one(x, halo, cw, ct, first, post):
        y, xs = _conv_pre(x, halo, cw, first)
        if post is None:
            _, vjp = jax.vjp(_silu, y)
            dy = vjp(ct)[0]
        else:
            dys = []
            for yh, cth in zip(_heads(y, HEAD), _heads(ct, HEAD)):
                _, vjp = jax.vjp(lambda t: _qk_post(t, post), yh)
                dys.append(vjp(cth)[0])
            dy = _cat(dys)
        row = lax.broadcasted_iota(jnp.int32, (SUB, x.shape[1]), 0)
        dcw = jnp.zeros((SUB, x.shape[1]), F32)
        for i in range(CONV_K):
            dcw = dcw + jnp.where(row == i, jnp.sum(dy * xs[CONV_K - 1 - i], axis=0, keepdims=True), 0.0)
        return dy, dcw

    def fn(i, n, xq, hq, xk, hk, xv, hv, cwq, cwk, cwv, cq, ck, cv, s, alog, dtb, bf, d1, d2):
        first = i == 0
        dyq, dcq = one(xq, hq, cwq, cq, first, qscale)
        dyk, dck = one(xk, hk, cwk, ck, first, 1.0)
        dyv, dcv = one(xv, hv, cwv, cv, first, None)
        tb_ = d1.shape[0]
        ri = lax.broadcasted_iota(jnp.int32, (tb_, tb_), 0)
        ci = lax.broadcasted_iota(jnp.int32, (tb_, tb_), 1)
        later = (ci >= ri) & ((ri >> CHUNK_SHIFT) == (ci >> CHUNK_SHIFT))
        lane = lax.broadcasted_iota(jnp.int32, d1.shape, 1)
        d1 = jnp.where(lane < GH, d1, _dot(later.astype(F32), d1, HI))
        _, vjp = jax.vjp(lambda a, b, c, d: _small_fn(a, b, c, d, GH, FH), s, alog, dtb, bf)
        ds, dalog, ddtb, dbf = vjp(d1 + d2)
        return dyq, dyk, dyv, dcq, dck, dcv, ds, dalog, ddtb, dbf

    ins = []
    for key in ("q", "k", "v"):
        ins += [("row", proj, offs[key], GW), ("prev", proj, offs[key], GW)]
    ins += [("full", c) for c in cws] + [("row", c, 0, GW) for c in cts]
    ins += [("row", proj, offs["small"], LANES)] + [("full", p) for p in pvecs]
    ins += [("row", dgsm_scan, 0, LANES), ("row", dlf_sm, 0, LANES)]
    outs = [("row", GW, F32)] * 3 + [("acc", (SUB, GW), F32)] * 3 + [("row", LANES, BF16)] + [("acc", (1, LANES), F32)] * 3
    return _rowwise(fn, T, tb, ins, outs, "gdn_prep_bwd_a")


def _gdn_prep_bwd_b(dys, cws, GW, tb=256):
    T = dys[0].shape[0]

    def fn(i, n, dq, nq, dk, nk, dv, nv, cwq, cwk, cwv):
        last = i == n - 1
        res = []
        for dy, nh, cw in ((dq, nq, cwq), (dk, nk, cwk), (dv, nv, cwv)):
            dx = cw[CONV_K - 1:CONV_K, :] * dy
            for t in range(CONV_K - 1):
                dx = dx + cw[t:t + 1, :] * _shift_up(dy, nh, CONV_K - 1 - t, last)
            res.append(dx)
        return tuple(res)

    ins = []
    for dy in dys:
        ins += [("row", dy, 0, GW), ("next", dy, 0, GW)]
    ins += [("full", c) for c in cws]
    return _rowwise(fn, T, tb, ins, [("row", GW, BF16)] * 3, "gdn_prep_bwd_b")


def _gdn_chunks(qs, ks, vs, gams, bcols, s0s):
    c, d = qs[0].shape
    hs = range(len(qs))
    ri = lax.broadcasted_iota(jnp.int32, (c, c), 0)
    ci = lax.broadcasted_iota(jnp.int32, (c, c), 1)
    incl, strict = ri >= ci, ri > ci
    eye = (ri == ci).astype(F32)
    ones_cc = jnp.ones((c, c), F32)
    rows = lax.broadcasted_iota(jnp.int32, (c, 1), 0)
    b16 = (ri >> 4) == (ci >> 4)
    b32 = (ri >> 5) == (ci >> 5)
    gam_cc = [gams[h] * ones_cc for h in hs]
    gam_t = [_dot_nt(eye, gam_cc[h], HI) for h in hs]
    glast = [jnp.sum(jnp.where(rows == c - 1, gams[h], 0.0), axis=0, keepdims=True) for h in hs]
    dec_i = [jnp.where(incl, jnp.exp(jnp.where(incl, gam_cc[h] - gam_t[h], 0.0)), 0.0) for h in hs]
    kk = [_dot_nt(ks[h], ks[h]) for h in hs]
    m = [bcols[h] * kk[h] * jnp.where(strict, dec_i[h], 0.0) for h in hs]
    m32 = [jnp.where(b32 & ~b16, m[h], 0.0) for h in hs]
    m64 = [jnp.where(b32, 0.0, m[h]) for h in hs]
    mp = [jnp.where(b16, m[h], 0.0) for h in hs]
    p = [eye - mp[h] for h in hs]
    for _ in range(3):
        mp = [_dot(mp[h], mp[h], SOLVE_PREC) for h in hs]
        p = [p[h] + _dot(p[h], mp[h], SOLVE_PREC) for h in hs]
    t = [_dot(p[h], m32[h], SOLVE_PREC) for h in hs]
    p = [p[h] - _dot(t[h], p[h], SOLVE_PREC) for h in hs]
    t = [_dot(p[h], m64[h], SOLVE_PREC) for h in hs]
    ainv = [p[h] - _dot(t[h], p[h], SOLVE_PREC) for h in hs]
    eg = [jnp.exp(gams[h]) for h in hs]
    w = [_dot(ainv[h], (bcols[h] * eg[h]) * ks[h], SOLVE_PREC) for h in hs]
    u0 = [_dot(ainv[h], bcols[h] * vs[h], SOLVE_PREC) for h in hs]
    qk = [_dot_nt(qs[h], ks[h]) * dec_i[h] for h in hs]
    u = [u0[h] - _dot(w[h], s0s[h]) for h in hs]
    o = [_dot(qs[h] * eg[h], s0s[h]) + _dot(qk[h], u[h]) for h in hs]
    s1 = [jnp.exp(glast[h]) * s0s[h] + _dot_tn(ks[h] * jnp.exp(glast[h] - gams[h]), u[h]) for h in hs]
    return tuple(o), tuple(s1)


def _lane_col(x, lane_idx):
    lane = lax.broadcasted_iota(jnp.int32, x.shape, 1)
    return jnp.sum(jnp.where(lane == lane_idx, x, 0.0), axis=1, keepdims=True)


def _gdn_scan_fwd(qn, kn, vc, gsm, GH):
    T, GW = qn.shape
    nc = T // CHUNK

    def body(q_ref, k_ref, v_ref, g_ref, o_ref, sall_ref, s_scr):
        @pl.when(pl.program_id(0) == 0)
        def _():
            s_scr[...] = jnp.zeros_like(s_scr)

        gs = g_ref[...]
        sls = [slice(h * HEAD, (h + 1) * HEAD) for h in range(GH)]
        s0s = tuple(s_scr[h] for h in range(GH))
        os_, s1s = _gdn_chunks(tuple(q_ref[:, sl] for sl in sls), tuple(k_ref[:, sl] for sl in sls),
                               tuple(v_ref[:, sl] for sl in sls), tuple(_lane_col(gs, GH + h) for h in range(GH)),
                               tuple(_lane_col(gs, h) for h in range(GH)), s0s)
        for h in range(GH):
            sall_ref[0, h] = s0s[h]
            o_ref[:, sls[h]] = os_[h]
            s_scr[h] = s1s[h]

    row = pl.BlockSpec((CHUNK, GW), lambda i: (i, 0))
    return _pcall(
        body, name="gdn_scan_fwd", grid=(nc,),
        in_specs=[row, row, row, pl.BlockSpec((CHUNK, LANES), lambda i: (i, 0))],
        out_specs=[row, pl.BlockSpec((1, GH, HEAD, HEAD), lambda i: (i, 0, 0, 0))],
        out_shape=[jax.ShapeDtypeStruct((T, GW), F32), jax.ShapeDtypeStruct((nc, GH, HEAD, HEAD), F32)],
        scratch_shapes=[pltpu.VMEM((GH, HEAD, HEAD), F32)],
        compiler_params=_cp("arbitrary"),
    )(qn, kn, vc, gsm)


def _gdn_scan_bwd(qn, kn, vc, gsm, sall, do, GH):
    T, GW = qn.shape
    nc = T // CHUNK

    def body(q_ref, k_ref, v_ref, g_ref, sall_ref, do_ref, dq_ref, dk_ref, dv_ref, dg_ref, ds_scr):
        @pl.when(pl.program_id(0) == 0)
        def _():
            ds_scr[...] = jnp.zeros_like(ds_scr)

        gs = g_ref[...]
        lane = lax.broadcasted_iota(jnp.int32, gs.shape, 1)
        sls = [slice(h * HEAD, (h + 1) * HEAD) for h in range(GH)]
        _, vjp = jax.vjp(_gdn_chunks, tuple(q_ref[:, sl] for sl in sls), tuple(k_ref[:, sl] for sl in sls),
                         tuple(v_ref[:, sl] for sl in sls), tuple(_lane_col(gs, GH + h) for h in range(GH)),
                         tuple(_lane_col(gs, h) for h in range(GH)), tuple(sall_ref[0, h] for h in range(GH)))
        dq, dk, dv, dgc, dbc, ds0 = vjp((tuple(do_ref[:, sl] for sl in sls), tuple(ds_scr[h] for h in range(GH))))
        dgs = jnp.zeros_like(gs)
        for h in range(GH):
            dq_ref[:, sls[h]] = dq[h]
            dk_ref[:, sls[h]] = dk[h]
            dv_ref[:, sls[h]] = dv[h]
            dgs = dgs + jnp.where(lane == h, dbc[h], 0.0) + jnp.where(lane == GH + h, dgc[h], 0.0)
            ds_scr[h] = ds0[h]
        dg_ref[...] = dgs

    row = pl.BlockSpec((CHUNK, GW), lambda i: (nc - 1 - i, 0))
    sm = pl.BlockSpec((CHUNK, LANES), lambda i: (nc - 1 - i, 0))
    return _pcall(
        body, name="gdn_scan_bwd", grid=(nc,),
        in_specs=[row, row, row, sm, pl.BlockSpec((1, GH, HEAD, HEAD), lambda i: (nc - 1 - i, 0, 0, 0)), row],
        out_specs=[row, row, row, sm],
        out_shape=[jax.ShapeDtypeStruct((T, GW), F32)] * 3 + [jax.ShapeDtypeStruct((T, LANES), F32)],
        scratch_shapes=[pltpu.VMEM((GH, HEAD, HEAD), F32)],
        compiler_params=_cp("arbitrary"),
    )(qn, kn, vc, gsm, sall, do)


def _gdn_post_fn(o, z, g):
    return _rms(o, g) * _silu(z)


def _gdn_post(o, proj, off_z, g, GW, tb=512):
    T = o.shape[0]

    def fn(i, n, ov, zv, gv):
        return (_cat([_gdn_post_fn(a, b, gv) for a, b in zip(_heads(ov, HEAD), _heads(zv, HEAD))]),)

    return _rowwise(fn, T, tb, [("row", o, 0, GW), ("row", proj, off_z, GW), ("full", g)], [("row", GW, BF16)], "gdn_post")[0]


def _gdn_post_bwd(o, proj, off_z, g, dout, GW, tb=256):
    T = o.shape[0]

    def fn(i, n, ov, zv, gv, dv):
        dos, dzs, dg = [], [], jnp.zeros_like(gv)
        for a, b, c in zip(_heads(ov, HEAD), _heads(zv, HEAD), _heads(dv, HEAD)):
            _, vjp = jax.vjp(_gdn_post_fn, a, b, gv)
            da, db, dgh = vjp(c)
            dos.append(da)
            dzs.append(db)
            dg = dg + dgh
        return _cat(dos), _cat(dzs), dg

    return _rowwise(fn, T, tb, [("row", o, 0, GW), ("row", proj, off_z, GW), ("full", g), ("row", dout, 0, GW)],
                    [("row", GW, F32), ("row", GW, BF16), ("acc", (1, HEAD), F32)], "gdn_post_bwd")


def _fox_prep(proj, offs, gq, gk, FW, tb=512):
    T = proj.shape[0]

    def fn(i, n, q, k, v, gqv, gkv):
        return (_cat([_rms(a, gqv) for a in _heads(q, HEAD)]), _cat([_rms(a, gkv) for a in _heads(k, HEAD)]), v)

    return _rowwise(fn, T, tb, [("row", proj, offs["fq"], FW), ("row", proj, offs["fk"], FW), ("row", proj, offs["fv"], FW),
                                ("full", gq), ("full", gk)], [("row", FW, BF16)] * 3, "fox_prep")


def _fox_prep_bwd(proj, offs, gq, gk, dq, dk, dv, FW, tb=256):
    T = proj.shape[0]

    def fn(i, n, q, k, gqv, gkv, dqv, dkv, dvv):
        res = []
        for x, g, d in ((q, gqv, dqv), (k, gkv, dkv)):
            dxs, dg = [], jnp.zeros_like(g)
            for a, c in zip(_heads(x, HEAD), _heads(d, HEAD)):
                _, vjp = jax.vjp(_rms, a, g)
                da, dgh = vjp(c)
                dxs.append(da)
                dg = dg + dgh
            res += [_cat(dxs), dg]
        return res[0], res[2], dvv, res[1], res[3]

    return _rowwise(fn, T, tb, [("row", proj, offs["fq"], FW), ("row", proj, offs["fk"], FW), ("full", gq), ("full", gk),
                                ("row", dq, 0, FW), ("row", dk, 0, FW), ("row", dv, 0, FW)],
                    [("row", FW, BF16)] * 3 + [("acc", (1, HEAD), F32)] * 2, "fox_prep_bwd")


def _sub_row(x, sub_idx):
    sub = lax.broadcasted_iota(jnp.int32, x.shape, 0)
    return jnp.sum(jnp.where(sub == sub_idx, x, 0.0), axis=0, keepdims=True)


def _causal_pairs(nb, key_major):
    if key_major:
        pairs = [(i, j) for j in range(nb) for i in range(j, nb)]
    else:
        pairs = [(i, j) for i in range(nb) for j in range(i + 1)]
    return (jnp.asarray(np.array([p[0] for p in pairs], np.int32)),
            jnp.asarray(np.array([p[1] for p in pairs], np.int32)))


def _fox_scores(q, k, cq, ck, diagonal):
    s = _dot_nt(q, k) * (HEAD ** -0.5) + (cq - ck)
    if diagonal:
        row = lax.broadcasted_iota(jnp.int32, s.shape, 0)
        col = lax.broadcasted_iota(jnp.int32, s.shape, 1)
        s = jnp.where(row >= col, s, NEG)
    return s


def _flash_fwd(qb, kb, vb, cum, cumt, lane0, FH, blk):
    T, FW = qb.shape
    blk = min(blk, T)
    nb = T // blk
    qi_arr, kj_arr = _causal_pairs(nb, False)

    def body(qi_ref, kj_ref, q_ref, k_ref, v_ref, cq_ref, ck_ref, ob_ref, lse_ref, m_scr, l_scr, acc):
        h, t = pl.program_id(0), pl.program_id(1)
        qi, kj = qi_ref[t], kj_ref[t]

        @pl.when(kj == 0)
        def _():
            m_scr[...] = jnp.full_like(m_scr, NEG)
            l_scr[...] = jnp.zeros_like(l_scr)
            acc[...] = jnp.zeros_like(acc)

        def update(diagonal):
            cq, ck = cq_ref[0], ck_ref[0]
            s = _fox_scores(q_ref[...], k_ref[...], cq, ck, diagonal)
            m_old = m_scr[...]
            m_new = jnp.maximum(m_old, jnp.max(s, axis=1, keepdims=True))
            alpha = jnp.exp(m_old - m_new)
            p = jnp.exp(s - m_new)
            l_scr[...] = alpha * l_scr[...] + jnp.sum(p, axis=1, keepdims=True)
            acc[...] = alpha * acc[...] + _dot(p.astype(BF16), v_ref[...])
            m_scr[...] = m_new

        @pl.when(kj < qi)
        def _():
            update(False)

        @pl.when(kj == qi)
        def _():
            update(True)
            ob_ref[...] = (acc[...] / l_scr[...]).astype(BF16)
            lse_ref[0] = m_scr[...] + jnp.log(l_scr[...])

    qs = pl.BlockSpec((blk, HEAD), lambda h, t, qr, kr: (qr[t], h))
    ks = pl.BlockSpec((blk, HEAD), lambda h, t, qr, kr: (kr[t], h))
    gs = pltpu.PrefetchScalarGridSpec(
        num_scalar_prefetch=2, grid=(FH, qi_arr.shape[0]),
        in_specs=[qs, ks, ks, pl.BlockSpec((1, blk, 1), lambda h, t, qr, kr: (h, qr[t], 0)),
                  pl.BlockSpec((1, 1, blk), lambda h, t, qr, kr: (h, 0, kr[t]))],
        out_specs=[qs, pl.BlockSpec((1, blk, 1), lambda h, t, qr, kr: (h, qr[t], 0))],
        scratch_shapes=[pltpu.VMEM((blk, 1), F32), pltpu.VMEM((blk, 1), F32), pltpu.VMEM((blk, HEAD), F32)])
    return _pcall(
        body, name="flash_fwd", grid_spec=gs,
        out_shape=[jax.ShapeDtypeStruct((T, FW), BF16), jax.ShapeDtypeStruct((FH, T, 1), F32)],
        compiler_params=_cp("parallel", "arbitrary"),
    )(qi_arr, kj_arr, qb, kb, vb, cum, cumt)


def _flash_bwd_q(qb, kb, vb, cum, cumt, lse, do, dl, lane0, FH, blk):
    T, FW = qb.shape
    blk = min(blk, T)
    nb = T // blk
    qi_arr, kj_arr = _causal_pairs(nb, False)
    want_dq = dl is not None

    def body(qi_ref, kj_ref, q_ref, k_ref, v_ref, cq_ref, ck_ref, lse_ref, do_ref, *rest):
        if want_dq:
            dl_ref, dq_ref, rs_ref, acc, rs_acc = rest
        else:
            dl_ref, acc = rest
        h, t = pl.program_id(0), pl.program_id(1)
        qi, kj = qi_ref[t], kj_ref[t]

        @pl.when(kj == 0)
        def _():
            acc[...] = jnp.zeros_like(acc)
            if want_dq:
                rs_acc[...] = jnp.zeros_like(rs_acc)

        def update(diagonal):
            cq, ck = cq_ref[0], ck_ref[0]
            s = _fox_scores(q_ref[...], k_ref[...], cq, ck, diagonal)
            p = jnp.exp(s - lse_ref[0])
            dp = _dot_nt(do_ref[...].astype(BF16), v_ref[...])
            if want_dq:
                ds = p * (dp - dl_ref[0])
                acc[...] += _dot(ds.astype(BF16), k_ref[...])
                rs_acc[...] += jnp.sum(ds, axis=1, keepdims=True)
            else:
                acc[...] += jnp.sum(p * dp, axis=1, keepdims=True)

        @pl.when(kj < qi)
        def _():
            update(False)

        @pl.when(kj == qi)
        def _():
            update(True)
            if want_dq:
                dq_ref[...] = acc[...] * (HEAD ** -0.5)
                rs_ref[0] = rs_acc[...]
            else:
                dl_ref[0] = acc[...]

    qs = pl.BlockSpec((blk, HEAD), lambda h, t, qr, kr: (qr[t], h))
    ks = pl.BlockSpec((blk, HEAD), lambda h, t, qr, kr: (kr[t], h))
    col = pl.BlockSpec((1, blk, 1), lambda h, t, qr, kr: (h, qr[t], 0))
    in_specs = [qs, ks, ks, col, pl.BlockSpec((1, 1, blk), lambda h, t, qr, kr: (h, 0, kr[t])), col, qs]
    args = [qi_arr, kj_arr, qb, kb, vb, cum, cumt, lse, do]
    colshape = jax.ShapeDtypeStruct((FH, T, 1), F32)
    if want_dq:
        gs = pltpu.PrefetchScalarGridSpec(
            num_scalar_prefetch=2, grid=(FH, qi_arr.shape[0]), in_specs=in_specs + [col], out_specs=[qs, col],
            scratch_shapes=[pltpu.VMEM((blk, HEAD), F32), pltpu.VMEM((blk, 1), F32)])
        return _pcall(body, name="flash_bwd_dq", grid_spec=gs,
                      out_shape=[jax.ShapeDtypeStruct((T, FW), F32), colshape],
                      compiler_params=_cp("parallel", "arbitrary"))(*args, dl)
    gs = pltpu.PrefetchScalarGridSpec(
        num_scalar_prefetch=2, grid=(FH, qi_arr.shape[0]), in_specs=in_specs, out_specs=col,
        scratch_shapes=[pltpu.VMEM((blk, 1), F32)])
    return _pcall(body, name="flash_bwd_rowterm", grid_spec=gs, out_shape=colshape,
                  compiler_params=_cp("parallel", "arbitrary"))(*args)


def _flash_bwd_dkv(qb, kb, vb, cum, cumt, lse_row, dl_row, do, lane0, FH, blk):
    T, FW = qb.shape
    blk = min(blk, T)
    nb = T // blk

    qi_arr, kj_arr = _causal_pairs(nb, True)

    def body(qi_ref, kj_ref, q_ref, k_ref, v_ref, ck_ref, cq_ref, lse_ref, dl_ref, do_ref,
             dk_ref, dv_ref, dc_ref, dk_acc, dv_acc, dc_acc):
        h, t = pl.program_id(0), pl.program_id(1)
        qi, kj = qi_ref[t], kj_ref[t]

        def update(diagonal):
            ck, cq = ck_ref[0], cq_ref[0]
            st = _dot_nt(k_ref[...], q_ref[...]) * (HEAD ** -0.5) + (cq - ck)
            if diagonal:
                krow = lax.broadcasted_iota(jnp.int32, st.shape, 0)
                qcol = lax.broadcasted_iota(jnp.int32, st.shape, 1)
                st = jnp.where(qcol >= krow, st, NEG)
            pt = jnp.exp(st - lse_ref[0])
            dob = do_ref[...].astype(BF16)
            dpt = _dot_nt(v_ref[...], dob)
            dst = pt * (dpt - dl_ref[0])
            return _dot(dst.astype(BF16), q_ref[...]), _dot(pt.astype(BF16), dob), -jnp.sum(dst, axis=1, keepdims=True)

        @pl.when(qi == kj)
        def _():
            dk, dv, dc = update(True)
            dk_acc[...] = dk
            dv_acc[...] = dv
            dc_acc[...] = dc

        @pl.when(qi > kj)
        def _():
            dk, dv, dc = update(False)
            dk_acc[...] += dk
            dv_acc[...] += dv
            dc_acc[...] += dc

        @pl.when(qi == nb - 1)
        def _():
            dk_ref[...] = dk_acc[...] * (HEAD ** -0.5)
            dv_ref[...] = dv_acc[...]
            dc_ref[0] = dc_acc[...]

    ks = pl.BlockSpec((blk, HEAD), lambda h, t, qr, kr: (kr[t], h))
    qs = pl.BlockSpec((blk, HEAD), lambda h, t, qr, kr: (qr[t], h))
    rowq = pl.BlockSpec((1, 1, blk), lambda h, t, qr, kr: (h, 0, qr[t]))
    gs = pltpu.PrefetchScalarGridSpec(
        num_scalar_prefetch=2, grid=(FH, qi_arr.shape[0]),
        in_specs=[qs, ks, ks, pl.BlockSpec((1, blk, 1), lambda h, t, qr, kr: (h, kr[t], 0)), rowq, rowq, rowq, qs],
        out_specs=[ks, ks, pl.BlockSpec((1, blk, 1), lambda h, t, qr, kr: (h, kr[t], 0))],
        scratch_shapes=[pltpu.VMEM((blk, HEAD), F32), pltpu.VMEM((blk, HEAD), F32), pltpu.VMEM((blk, 1), F32)])
    return _pcall(
        body, name="flash_bwd_dkv", grid_spec=gs,
        out_shape=[jax.ShapeDtypeStruct((T, FW), F32), jax.ShapeDtypeStruct((T, FW), F32),
                   jax.ShapeDtypeStruct((FH, T, 1), F32)],
        compiler_params=_cp("parallel", "arbitrary"),
    )(qi_arr, kj_arr, qb, kb, vb, cum, cumt, lse_row, dl_row, do)


def _rev_cumsum_rows(r1, r2, tb=512):
    H, T = r1.shape
    tb = min(tb, T)
    nb = T // tb

    def body(r1_ref, r2_ref, o_ref, carry):
        @pl.when(pl.program_id(0) == 0)
        def _():
            carry[...] = jnp.zeros_like(carry)

        rv = r1_ref[...] + r2_ref[...]
        si = lax.broadcasted_iota(jnp.int32, (tb, tb), 0)
        ti = lax.broadcasted_iota(jnp.int32, (tb, tb), 1)
        o_ref[...] = _dot(rv, (si >= ti).astype(F32), HI) + carry[...]
        carry[...] += jnp.sum(rv, axis=1, keepdims=True)

    spec = pl.BlockSpec((H, tb), lambda i: (0, nb - 1 - i))
    return _pcall(body, name="rev_cumsum", grid=(nb,), in_specs=[spec, spec], out_specs=spec,
                  out_shape=jax.ShapeDtypeStruct((H, T), F32), scratch_shapes=[pltpu.VMEM((H, 1), F32)],
                  compiler_params=_cp("arbitrary"))(r1, r2)


def _mem_head(q, k, v, gq, gk):
    logits = _dot_nt(_rms(q, gq), _rms(k, gk)) * (MEM_DH ** -0.5)
    mx = jnp.max(logits, axis=1, keepdims=True)
    e = jnp.exp(logits - mx)
    p = e / jnp.sum(e, axis=1, keepdims=True)
    return _dot(p, v)


def _mem_attn(proj, off_q, kv, gq, gk, MW, tb=512):
    T = proj.shape[0]
    MH = MW // MEM_DH

    def fn(i, n, q, kvv, gqv, gkv):
        ks, vs = _heads(kvv[:, :MW], MEM_DH), _heads(kvv[:, MW:], MEM_DH)
        return (_cat([_mem_head(a, b, c, gqv, gkv) for a, b, c in zip(_heads(q, MEM_DH), ks, vs)]),)

    return _rowwise(fn, T, tb, [("row", proj, off_q, MW), ("full", kv), ("full", gq), ("full", gk)],
                    [("row", MW, BF16)], "mem_attn")[0]


def _mem_attn_bwd(proj, off_q, kv, gq, gk, dout, MW, tb=256):
    T = proj.shape[0]
    ML = kv.shape[0]

    def fn(i, n, q, kvv, gqv, gkv, dv):
        ks, vs = _heads(kvv[:, :MW], MEM_DH), _heads(kvv[:, MW:], MEM_DH)
        dqs, dks, dvs = [], [], []
        dgq, dgk = jnp.zeros_like(gqv), jnp.zeros_like(gkv)
        for a, b, c, d in zip(_heads(q, MEM_DH), ks, vs, _heads(dv, MEM_DH)):
            _, vjp = jax.vjp(_mem_head, a, b, c, gqv, gkv)
            da, db, dc, dg1, dg2 = vjp(d)
            dqs.append(da)
            dks.append(db)
            dvs.append(dc)
            dgq, dgk = dgq + dg1, dgk + dg2
        return _cat(dqs), _cat(dks + dvs), dgq, dgk

    return _rowwise(fn, T, tb, [("row", proj, off_q, MW), ("full", kv), ("full", gq), ("full", gk), ("row", dout, 0, MW)],
                    [("row", MW, BF16), ("acc", (ML, 2 * MW), F32), ("acc", (1, MEM_DH), F32), ("acc", (1, MEM_DH), F32)],
                    "mem_attn_bwd")


def _merge_fn(ga, gb, gm, ua, ub, um):
    return _sigmoid(ga) * ua + _sigmoid(gb) * ub + _sigmoid(gm) * um


def _merge(proj, offs, ua, ub, um, D, tb=256):
    T = proj.shape[0]
    ins = [("row", proj, offs[k], D) for k in ("ga", "gb", "gm")] + [("row", u, 0, D) for u in (ua, ub, um)]
    return _rowwise(lambda i, n, *v: (_merge_fn(*v),), T, tb, ins, [("row", D, BF16)], "merge")[0]


def _merge_bwd(proj, offs, ua, ub, um, dy, D, tb=256):
    T = proj.shape[0]

    def fn(i, n, *v):
        _, vjp = jax.vjp(_merge_fn, *v[:6])
        return vjp(v[6])

    ins = [("row", proj, offs[k], D) for k in ("ga", "gb", "gm")] + [("row", u, 0, D) for u in (ua, ub, um)] + [("row", dy, 0, D)]
    return _rowwise(fn, T, tb, ins, [("row", D, BF16)] * 6, "merge_bwd")


def _loss_grad(x2, tgt, tb=256):
    T, D = x2.shape

    def fn(i, n, a, b):
        e = a - b
        part = jnp.sum(jnp.sum(e * e, axis=1, keepdims=True), axis=0, keepdims=True) * (0.5 / D)
        g = e * (1.0 / D)
        return g, g, part + jnp.zeros((SUB, LANES), F32)

    return _rowwise(fn, T, tb, [("row", x2, 0, D), ("row", tgt, 0, D)],
                    [("row", D, F32), ("row", D, BF16), ("acc", (SUB, LANES), F32)], "loss_grad")


def _adamw(w, g, m, v, name, tb=128):
    R, C = w.shape
    c1 = 1.0 / (1.0 - ADAM_B1 ** ADAM_STEP)
    c2 = 1.0 / (1.0 - ADAM_B2 ** ADAM_STEP)

    def fn(i, n, wv, gv, mv, vv):
        mn = ADAM_B1 * mv + (1.0 - ADAM_B1) * gv
        vn = ADAM_B2 * vv + (1.0 - ADAM_B2) * (gv * gv)
        delta = -ADAM_LR * ((mn * c1) / (jnp.sqrt(vn * c2) + ADAM_EPS) + ADAM_WD * wv)
        return delta, mn, vn

    return _rowwise(fn, R, tb, [("row", a, 0, C) for a in (w, g, m, v)], [("row", C, F32)] * 3, name)


def _coords():
    return lax.axis_index("x"), lax.axis_index("y"), lax.axis_index("c")


def _allgather_small(blk):
    m_per, n = blk.shape

    def body(x_ref, out_ref, send_sems, recv_sems, local_sem):
        x, y, c = _coords()
        me, sibling = (x, y, c), (x, y, 1 - c)
        chips = [(1 - x, y), (x, 1 - y), (1 - x, 1 - y)]

        def rows(px, py, pc):
            return out_ref.at[pl.ds((4 * px + 2 * py + pc) * m_per, m_per), :]

        def copy(k, block, to, src=None):
            return pltpu.make_async_remote_copy(
                src_ref=rows(*block) if src is None else src, dst_ref=rows(*block),
                send_sem=send_sems.at[k], recv_sem=recv_sems.at[k], device_id=to, device_id_type=MESH)

        mine = pltpu.make_async_copy(x_ref, rows(*me), local_sem)
        mine.start()
        first = [copy(0, me, sibling, src=x_ref)]
        first += [copy(1 + j, me, (*chip, c), src=x_ref) for j, chip in enumerate(chips)]
        for cp in first:
            cp.start()
        passed = [copy(4 + j, (*chip, c), sibling) for j, chip in enumerate(chips)]
        for j, chip in enumerate(chips):
            copy(1 + j, (*chip, c), me).wait_recv()
            passed[j].start()
        copy(0, sibling, me).wait_recv()
        for j, chip in enumerate(chips):
            copy(4 + j, (*chip, 1 - c), me).wait_recv()
        for cp in first + passed:
            cp.wait_send()
        mine.wait()

    return _pcall(
        body, name="allgather_small", out_shape=jax.ShapeDtypeStruct((8 * m_per, n), blk.dtype),
        in_specs=[pl.BlockSpec(memory_space=pltpu.VMEM)], out_specs=pl.BlockSpec(memory_space=pltpu.VMEM),
        scratch_shapes=[pltpu.SemaphoreType.DMA((7,)), pltpu.SemaphoreType.DMA((7,)), pltpu.SemaphoreType.DMA],
        compiler_params=pltpu.CompilerParams(vmem_limit_bytes=VMEM_LIMIT),
    )(blk)


def _sum8(g, m_per):
    n = g.shape[1]

    def body(g_ref, o_ref):
        acc = g_ref[pl.ds(0, m_per), :]
        for d in range(1, 8):
            acc = acc + g_ref[pl.ds(d * m_per, m_per), :]
        o_ref[...] = acc

    return _pcall(body, name="sum8", out_shape=jax.ShapeDtypeStruct((m_per, n), g.dtype))(g)


_ANY = pl.BlockSpec(memory_space=pl.ANY)


def _allgather_chips(buf):
    nr, w = buf.shape
    half = nr // 2

    def body(in_ref, out_ref, send_sems, recv_sems):
        x, y, c = _coords()
        me = 2 * x + y
        chips = [(1 - x, y), (x, 1 - y), (1 - x, 1 - y)]
        mine_rows = pl.ds(pl.multiple_of(c * half, 16), half)
        other_rows = pl.ds(pl.multiple_of((1 - c) * half, 16), half)

        def copy(k, src, dst, to):
            return pltpu.make_async_remote_copy(src_ref=src, dst_ref=dst, send_sem=send_sems.at[k],
                                                recv_sem=recv_sems.at[k], device_id=to, device_id_type=MESH)

        first = [copy(j, in_ref.at[mine_rows], out_ref.at[me, mine_rows], (cx, cy, c)) for j, (cx, cy) in enumerate(chips)]
        for cp in first:
            cp.start()
        passed = []
        for j, (cx, cy) in enumerate(chips):
            slot = out_ref.at[2 * cx + cy, mine_rows]
            copy(j, slot, slot, (cx, cy, c)).wait_recv()
            fwd = copy(3 + j, slot, slot, (x, y, 1 - c))
            fwd.start()
            passed.append(fwd)
        for j, (cx, cy) in enumerate(chips):
            slot = out_ref.at[2 * cx + cy, other_rows]
            copy(3 + j, slot, slot, (x, y, 1 - c)).wait_recv()
        for cp in first + passed:
            cp.wait_send()

    return _pcall(
        body, name="allgather_chips", out_shape=jax.ShapeDtypeStruct((4, nr, w), buf.dtype),
        in_specs=[_ANY], out_specs=_ANY,
        scratch_shapes=[pltpu.SemaphoreType.DMA((6,)), pltpu.SemaphoreType.DMA((6,))],
    )(buf)


def _rs_pair_exchange(g):
    _, nr, w = g.shape
    half = nr // 2

    def body(g_ref, rb_ref, send_sem, recv_sem):
        x, y, c = _coords()
        other_rows = pl.ds(pl.multiple_of((1 - c) * half, SUB), half)
        cp = pltpu.make_async_remote_copy(src_ref=g_ref.at[:, other_rows], dst_ref=rb_ref, send_sem=send_sem,
                                          recv_sem=recv_sem, device_id=(x, y, 1 - c), device_id_type=MESH)
        cp.start()
        cp.wait()

    return _pcall(body, name="rs_pair_exchange", out_shape=jax.ShapeDtypeStruct((4, half, w), g.dtype),
                  in_specs=[_ANY], out_specs=_ANY,
                  scratch_shapes=[pltpu.SemaphoreType.DMA, pltpu.SemaphoreType.DMA])(g)


def _rs_pair_add(g, rb, cidx, tb=256):
    _, nr, w = g.shape
    half = nr // 2
    tb = min(tb, half)
    assert half % tb == 0
    hb = half // tb

    def body(c_ref, g_ref, r_ref, o_ref):
        o_ref[...] = (g_ref[...] + r_ref[...]).astype(o_ref.dtype)

    gs = pltpu.PrefetchScalarGridSpec(
        num_scalar_prefetch=1, grid=(4, hb),
        in_specs=[pl.BlockSpec((1, tb, w), lambda j, i, c_ref: (j, c_ref[0] * hb + i, 0)),
                  pl.BlockSpec((1, tb, w), lambda j, i, c_ref: (j, i, 0))],
        out_specs=pl.BlockSpec((1, tb, w), lambda j, i, c_ref: (j, i, 0)))
    return _pcall(body, name="rs_pair_add", grid_spec=gs, out_shape=jax.ShapeDtypeStruct((4, half, w), BF16),
                  compiler_params=_cp("parallel", "parallel"))(cidx, g, rb)


def _rs_chip_exchange(p):
    _, h, w = p.shape

    def body(p_ref, rb_ref, send_sems, recv_sems):
        x, y, c = _coords()
        chips = [(1 - x, y), (x, 1 - y), (1 - x, 1 - y)]
        cps = [pltpu.make_async_remote_copy(src_ref=p_ref.at[2 * cx + cy], dst_ref=rb_ref.at[j], send_sem=send_sems.at[j],
                                            recv_sem=recv_sems.at[j], device_id=(cx, cy, c), device_id_type=MESH)
               for j, (cx, cy) in enumerate(chips)]
        for cp in cps:
            cp.start()
        for cp in cps:
            cp.wait()

    return _pcall(body, name="rs_chip_exchange", out_shape=jax.ShapeDtypeStruct((3, h, w), p.dtype),
                  in_specs=[_ANY], out_specs=_ANY,
                  scratch_shapes=[pltpu.SemaphoreType.DMA((3,)), pltpu.SemaphoreType.DMA((3,))])(p)


def _sum4(p, rb, chip_idx, tb=256):
    _, h, w = rb.shape
    tb = min(tb, h)
    assert h % tb == 0

    def body(m_ref, p_ref, r_ref, o_ref):
        f = lambda t: t.astype(F32)
        o_ref[...] = ((f(p_ref[0]) + f(r_ref[0])) + f(r_ref[1])) + f(r_ref[2])

    gs = pltpu.PrefetchScalarGridSpec(
        num_scalar_prefetch=1, grid=(h // tb,),
        in_specs=[pl.BlockSpec((1, tb, w), lambda i, m_ref: (m_ref[0], i, 0)),
                  pl.BlockSpec((3, tb, w), lambda i, m_ref: (0, i, 0))],
        out_specs=pl.BlockSpec((tb, w), lambda i, m_ref: (i, 0)))
    return _pcall(body, name="sum4", grid_spec=gs, out_shape=jax.ShapeDtypeStruct((h, w), F32),
                  compiler_params=_cp("parallel"))(chip_idx, p, rb)


def _pair_allgather(f):
    h, w = f.shape

    def body(f_ref, out_ref, send_sem, recv_sem):
        x, y, c = _coords()
        mine_rows = pl.ds(pl.multiple_of(c * h, SUB), h)
        other_rows = pl.ds(pl.multiple_of((1 - c) * h, SUB), h)
        send = pltpu.make_async_remote_copy(src_ref=f_ref, dst_ref=out_ref.at[mine_rows], send_sem=send_sem,
                                            recv_sem=recv_sem, device_id=(x, y, 1 - c), device_id_type=MESH)
        send.start()
        send.wait_send()
        pltpu.make_async_remote_copy(src_ref=f_ref, dst_ref=out_ref.at[other_rows], send_sem=send_sem,
                                     recv_sem=recv_sem, device_id=(x, y, 1 - c), device_id_type=MESH).wait_recv()

    return _pcall(body, name="pair_allgather", out_shape=jax.ShapeDtypeStruct((2 * h, w), f.dtype),
                  in_specs=[_ANY], out_specs=_ANY,
                  scratch_shapes=[pltpu.SemaphoreType.DMA, pltpu.SemaphoreType.DMA])(f)


def _pack(arrs, dtype, row_mult):
    flat = jnp.concatenate([a.astype(dtype).reshape(-1) for a in arrs])
    n = flat.shape[0]
    rows = _ru(-(-n // PACK_W), row_mult)
    return jnp.pad(flat, (0, rows * PACK_W - n)).reshape(rows, PACK_W)


def _unpack(buf, shapes):
    flat = buf.reshape(-1)
    out, off = [], 0
    for s in shapes:
        n = 1
        for d in s:
            n *= d
        out.append(flat[off:off + n].reshape(s))
        off += n
    return out


def _in_layout(D, GW, GH, FW, FH, MW, tn):
    o_z = 3 * GW
    o_beta = 4 * GW
    o_fq = o_beta + 2 * GH
    o_ff = o_fq + 3 * FW
    o_mq = o_ff + FH
    o_g = o_mq + MW
    orig = {"q": (0, GW), "k": (GW, GW), "v": (2 * GW, GW), "z": (o_z, GW), "beta": (o_beta, GH), "dec": (o_beta + GH, GH),
            "fq": (o_fq, FW), "fk": (o_fq + FW, FW), "fv": (o_fq + 2 * FW, FW), "ff": (o_ff, FH), "mq": (o_mq, MW),
            "ga": (o_g, D), "gb": (o_g + D, D), "gm": (o_g + 2 * D, D)}
    offs, cur = {}, 0
    for key, width in (("ga", D), ("gb", D), ("gm", D), ("q", GW), ("k", GW), ("v", GW), ("z", GW),
                       ("fq", FW), ("fk", FW), ("fv", FW), ("mq", MW), ("small", LANES)):
        cur = _ru(cur, width)
        offs[key] = cur
        cur += width
    total = _ru(cur, tn)
    pieces = [(offs[k], orig[k][0], orig[k][1]) for k in ("ga", "gb", "gm", "q", "k", "v", "z", "fq", "fk", "fv", "mq")]
    pieces += [(offs["small"], orig["beta"][0], GH), (offs["small"] + GH, orig["dec"][0], GH),
               (offs["small"] + 2 * GH, orig["ff"][0], FH)]
    return offs, total, pieces, o_g + 3 * D


def _pad_cols(w, pieces, total):
    parts, cur = [], 0
    for pstart, ostart, n in pieces:
        if pstart > cur:
            parts.append(jnp.zeros((w.shape[0], pstart - cur), w.dtype))
        parts.append(w[:, ostart:ostart + n])
        cur = pstart + n
    if total > cur:
        parts.append(jnp.zeros((w.shape[0], total - cur), w.dtype))
    return jnp.concatenate(parts, axis=1)


def _unpad_cols(wp, pieces):
    return jnp.concatenate([wp[:, pstart:pstart + n] for pstart, ostart, n in sorted(pieces, key=lambda t: t[1])], axis=1)


def _local_step(x, mem, tgt, W, flash_blk=512):
    T, D = x.shape
    GW = W["w_up_gdn"].shape[0]
    FW = W["w_up_fox"].shape[0]
    MW = W["w_up_mem"].shape[0]
    GH, FH = GW // HEAD, FW // HEAD
    offs, NP, pieces, d_in = _in_layout(D, GW, GH, FW, FH, MW, 1024)
    assert W["w_in"].shape[1] == d_in
    w_in_p = _pad_cols(W["w_in"], pieces, NP)
    cw = W["conv_w"]
    cws = [cw[:, i * GW:(i + 1) * GW] for i in range(3)]
    zl = jnp.zeros((1, LANES), F32)
    pvecs = [lax.dynamic_update_slice(zl, W["a_log"], (0, GH)), lax.dynamic_update_slice(zl, W["dt_bias"], (0, GH)),
             lax.dynamic_update_slice(zl, W["fox_b_f"], (0, 2 * GH))]
    lane0 = 2 * GH

    h = _rms_fwd(x, W["g_mix"], "rms_mix")
    proj = _mm(h, w_in_p, "nn", "in_proj")
    qn, kn, vc, cum, gbm = _gdn_prep(proj, offs, cws, pvecs, GH, FH, GW)
    o_gdn, sall = _gdn_scan_fwd(qn, kn, vc, gbm, GH)
    o_a = _gdn_post(o_gdn, proj, offs["z"], W["gdn_norm_g"], GW)
    qb, kb, vb = _fox_prep(proj, offs, W["fox_q_norm"], W["fox_k_norm"], FW)
    cumh = cum[:, lane0:lane0 + FH].T
    cumc, cumr = cumh.reshape(FH, T, 1), cumh.reshape(FH, 1, T)
    o_b16, lse = _flash_fwd(qb, kb, vb, cumc, cumr, lane0, FH, flash_blk)
    memn = _rms_fwd(mem, W["g_mem"], "rms_mem")
    kv = _mm(memn, W["w_mem_kv"], "nn", "mem_kv")
    o_m = _mem_attn(proj, offs["mq"], kv, W["mem_q_norm"], W["mem_k_norm"], MW)
    ua = _mm(o_a, W["w_up_gdn"], "nn", "up_gdn")
    ub = _mm(o_b16, W["w_up_fox"], "nn", "up_fox")
    um = _mm(o_m, W["w_up_mem"], "nn", "up_mem")
    y = _merge(proj, offs, ua, ub, um, D)
    x1 = _mm(y, W["w_out"], "nn", "out_proj", epilogue=lambda acc, r: (acc + r,), extras=(x,))
    h2 = _rms_fwd(x1, W["g_mlp"], "rms_mlp")
    u, a = _mm(h2, W["w_ff1"], "nn", "ff1", out_dtypes=(F32, BF16),
               epilogue=lambda acc: (acc, jnp.square(jnp.maximum(acc, 0.0))))
    x2 = _mm(a, W["w_ff2"], "nn", "ff2", epilogue=lambda acc, r: (acc + r,), extras=(x1,))
    dx2, dx2b, lpart = _loss_grad(x2, tgt)
    loss = lpart[0, 0]

    G = {}
    du = _mm(dx2b, W["w_ff2"], "nt", "ff2_dx", out_dtypes=(BF16,),
             epilogue=lambda acc, uu: (acc * (2.0 * jnp.maximum(uu, 0.0)),), extras=(u,))
    G["w_ff2"] = _mm(a, dx2b, "tn", "ff2_dw")
    dh2 = _mm(du, W["w_ff1"], "nt", "ff1_dx")
    G["w_ff1"] = _mm(h2, du, "tn", "ff1_dw")
    dx1, dx1b, G["g_mlp"] = _rms_bwd(x1, W["g_mlp"], dh2, dx2, "rms_mlp_bwd")
    dy = _mm(dx1b, W["w_out"], "nt", "out_dx")
    G["w_out"] = _mm(y, dx1b, "tn", "out_dw")
    dga, dgb, dgm, dua, dub, dum = _merge_bwd(proj, offs, ua, ub, um, dy, D)
    do_a = _mm(dua, W["w_up_gdn"], "nt", "up_gdn_dx")
    G["w_up_gdn"] = _mm(o_a, dua, "tn", "up_gdn_dw")
    do_b = _mm(dub, W["w_up_fox"], "nt", "up_fox_dx")
    G["w_up_fox"] = _mm(o_b16, dub, "tn", "up_fox_dw")
    do_m = _mm(dum, W["w_up_mem"], "nt", "up_mem_dx")
    G["w_up_mem"] = _mm(o_m, dum, "tn", "up_mem_dw")
    dmq, dkv, G["mem_q_norm"], G["mem_k_norm"] = _mem_attn_bwd(proj, offs["mq"], kv, W["mem_q_norm"], W["mem_k_norm"], do_m, MW)
    dkvb = dkv.astype(BF16)
    dmemn = _mm(dkvb, W["w_mem_kv"], "nt", "mem_kv_dx")
    G["w_mem_kv"] = _mm(memn, dkvb, "tn", "mem_kv_dw")
    G["g_mem"] = _rms_dg(mem, W["g_mem"], dmemn, "rms_mem_bwd")
    dl = _flash_bwd_q(qb, kb, vb, cumc, cumr, lse, do_b, None, lane0, FH, flash_blk)
    dqb, dcq = _flash_bwd_q(qb, kb, vb, cumc, cumr, lse, do_b, dl, lane0, FH, flash_blk)
    dkb, dvb, dck = _flash_bwd_dkv(qb, kb, vb, cumc, cumr, lse.reshape(FH, 1, T), dl.reshape(FH, 1, T), do_b, lane0, FH, flash_blk)
    dlf = _rev_cumsum_rows(dcq.reshape(FH, T), dck.reshape(FH, T))
    dlf_sm = jnp.pad(dlf.T, ((0, 0), (lane0, LANES - lane0 - FH)))
    dfq, dfk, dfv, G["fox_q_norm"], G["fox_k_norm"] = _fox_prep_bwd(proj, offs, W["fox_q_norm"], W["fox_k_norm"], dqb, dkb, dvb, FW)
    do_gdn, dz, G["gdn_norm_g"] = _gdn_post_bwd(o_gdn, proj, offs["z"], W["gdn_norm_g"], do_a, GW)
    dqn, dkn, dvc, dgsm = _gdn_scan_bwd(qn, kn, vc, gbm, sall, do_gdn, GH)
    dyq, dyk, dyv, dcq, dck_w, dcv, dsmall, dalog, ddtb, dbf = _gdn_prep_bwd_a(
        proj, offs, cws, pvecs, (dqn, dkn, dvc), dgsm, dlf_sm, GH, FH, GW)
    dxq, dxk, dxv = _gdn_prep_bwd_b((dyq, dyk, dyv), cws, GW)
    G["conv_w"] = jnp.concatenate([dcq[:CONV_K], dck_w[:CONV_K], dcv[:CONV_K]], axis=1)
    G["a_log"] = dalog[:, GH:2 * GH]
    G["dt_bias"] = ddtb[:, GH:2 * GH]
    G["fox_b_f"] = dbf[:, lane0:lane0 + FH]
    segs = {"ga": dga, "gb": dgb, "gm": dgm, "q": dxq, "k": dxk, "v": dxv, "z": dz, "fq": dfq, "fk": dfk, "fv": dfv,
            "mq": dmq, "small": dsmall}
    parts, cur = [], 0
    for key in ("ga", "gb", "gm", "q", "k", "v", "z", "fq", "fk", "fv", "mq", "small"):
        if offs[key] > cur:
            parts.append(jnp.zeros((T, offs[key] - cur), BF16))
        parts.append(segs[key])
        cur = offs[key] + segs[key].shape[1]
    if NP > cur:
        parts.append(jnp.zeros((T, NP - cur), BF16))
    dproj = jnp.concatenate(parts, axis=1)
    dh = _mm(dproj, w_in_p, "nt", "in_dx")
    G["w_in"] = _unpad_cols(_mm(h, dproj, "tn", "in_dw"), pieces)
    grad_x, _, G["g_mix"] = _rms_bwd(x, W["g_mix"], dh, dx1, "rms_mix_bwd")
    return loss, grad_x, G


BIG = ["w_in", "w_mem_kv", "w_up_gdn", "w_up_fox", "w_up_mem", "w_out", "w_ff1", "w_ff2"]
SMALL = ["g_mix", "a_log", "dt_bias", "gdn_norm_g", "fox_b_f", "fox_q_norm", "fox_k_norm", "g_mem", "mem_q_norm",
         "mem_k_norm", "g_mlp"]
ORDER = ["g_mix", "w_in", "conv_w", "a_log", "dt_bias", "gdn_norm_g", "fox_b_f", "fox_q_norm", "fox_k_norm", "g_mem",
         "w_mem_kv", "mem_q_norm", "mem_k_norm", "w_up_gdn", "w_up_fox", "w_up_mem", "w_out", "g_mlp", "w_ff1", "w_ff2"]
SHARD_AXIS = {"w_in": 1, "w_mem_kv": 0, "w_up_gdn": 1, "w_up_fox": 1, "w_up_mem": 1, "w_out": 0, "w_ff1": 1, "w_ff2": 0}


def _step(x, mem, tgt, w, m, v, flash_blk=512):
    xi, yi, ci = _coords()
    chip = 2 * xi + yi

    shard_shapes = [w[n].shape for n in BIG]
    packed_w = _pack([w[n] for n in BIG], BF16, PACK_ROWS)
    gathered = _allgather_chips(packed_w)
    gathered = lax.dynamic_update_slice(gathered, packed_w[None], (chip, 0, 0))
    per_chip = [_unpack(gathered[j], shard_shapes) for j in range(4)]
    W = {n: jnp.concatenate([per_chip[j][i] for j in range(4)], axis=SHARD_AXIS[n]) for i, n in enumerate(BIG)}
    cw_rows = jnp.pad(w["conv_w"], ((0, SUB - CONV_K), (0, 0)))
    cw_all = _allgather_small(cw_rows)
    W["conv_w"] = jnp.concatenate([cw_all[16 * j:16 * j + CONV_K] for j in range(4)], axis=1)
    for n in SMALL:
        W[n] = w[n]

    loss, grad_x, G = _local_step(x, mem, tgt, W, flash_blk)
    loss = lax.psum(loss, ("x", "y", "c"))

    small_shapes = [G[n].shape for n in SMALL] + [G["conv_w"].shape]
    sm = _pack([G[n] for n in SMALL] + [G["conv_w"]], F32, SUB)
    sm_sum = _sum8(_allgather_small(sm), sm.shape[0])
    sm_list = _unpack(sm_sum, small_shapes)
    g = {n: sm_list[i] for i, n in enumerate(SMALL)}
    cw_full = sm_list[-1]
    gw4 = cw_full.shape[1] // 4
    g["conv_w"] = lax.dynamic_slice(cw_full, (0, chip * gw4), (CONV_K, gw4))

    by_dest = []
    for j in range(4):
        shards = []
        for n in BIG:
            size = w[n].shape[SHARD_AXIS[n]]
            shards.append(lax.slice_in_dim(G[n], j * size, (j + 1) * size, axis=SHARD_AXIS[n]))
        by_dest.append(_pack(shards, F32, PACK_ROWS))
    gflat = jnp.stack(by_dest)
    rb1 = _rs_pair_exchange(gflat)
    part = _rs_pair_add(gflat, rb1, jnp.reshape(ci, (1,)).astype(jnp.int32))
    rb2 = _rs_chip_exchange(part)
    half_sum = _sum4(part, rb2, jnp.reshape(chip, (1,)).astype(jnp.int32))
    mine = _pair_allgather(half_sum)
    mine = lax.dynamic_update_slice(mine, half_sum, (ci * half_sum.shape[0], 0))
    for i, gv in enumerate(_unpack(mine, shard_shapes)):
        g[BIG[i]] = gv

    delta, new_m, new_v = {}, {}, {}
    for n in BIG:
        delta[n], new_m[n], new_v[n] = _adamw(w[n], g[n], m[n], v[n], "adamw_" + n)
    rest = SMALL + ["conv_w"]
    rest_shapes = [w[n].shape for n in rest]
    packed = [_pack([d[n] for n in rest], F32, SUB) for d in (w, g, m, v)]
    outs = _adamw(*packed, "adamw_small", tb=packed[0].shape[0])
    for d, buf in zip((delta, new_m, new_v), outs):
        for n, val in zip(rest, _unpack(buf, rest_shapes)):
            d[n] = val
    return loss, grad_x, g, delta, new_m, new_v


def kernel(x, mem, g_mix, w_in, conv_w, a_log, dt_bias, gdn_norm_g, fox_b_f, fox_q_norm, fox_k_norm, g_mem, w_mem_kv, mem_q_norm, mem_k_norm, w_up_gdn, w_up_fox, w_up_mem, w_out, g_mlp, w_ff1, w_ff2, loss_target, m_g_mix, m_w_in, m_conv_w, m_a_log, m_dt_bias, m_gdn_norm_g, m_fox_b_f, m_fox_q_norm, m_fox_k_norm, m_g_mem, m_w_mem_kv, m_mem_q_norm, m_mem_k_norm, m_w_up_gdn, m_w_up_fox, m_w_up_mem, m_w_out, m_g_mlp, m_w_ff1, m_w_ff2, v_g_mix, v_w_in, v_conv_w, v_a_log, v_dt_bias, v_gdn_norm_g, v_fox_b_f, v_fox_q_norm, v_fox_k_norm, v_g_mem, v_w_mem_kv, v_mem_q_norm, v_mem_k_norm, v_w_up_gdn, v_w_up_fox, v_w_up_mem, v_w_out, v_g_mlp, v_w_ff1, v_w_ff2):
    ws = (g_mix, w_in, conv_w, a_log, dt_bias, gdn_norm_g, fox_b_f, fox_q_norm, fox_k_norm, g_mem, w_mem_kv, mem_q_norm,
          mem_k_norm, w_up_gdn, w_up_fox, w_up_mem, w_out, g_mlp, w_ff1, w_ff2)
    ms = (m_g_mix, m_w_in, m_conv_w, m_a_log, m_dt_bias, m_gdn_norm_g, m_fox_b_f, m_fox_q_norm, m_fox_k_norm, m_g_mem,
          m_w_mem_kv, m_mem_q_norm, m_mem_k_norm, m_w_up_gdn, m_w_up_fox, m_w_up_mem, m_w_out, m_g_mlp, m_w_ff1, m_w_ff2)
    vs = (v_g_mix, v_w_in, v_conv_w, v_a_log, v_dt_bias, v_gdn_norm_g, v_fox_b_f, v_fox_q_norm, v_fox_k_norm, v_g_mem,
          v_w_mem_kv, v_mem_q_norm, v_mem_k_norm, v_w_up_gdn, v_w_up_fox, v_w_up_mem, v_w_out, v_g_mlp, v_w_ff1, v_w_ff2)
    drop = lambda a: a[0] if a.ndim == 3 else a
    w = {n: drop(a) for n, a in zip(ORDER, ws)}
    m = {n: drop(a) for n, a in zip(ORDER, ms)}
    v = {n: drop(a) for n, a in zip(ORDER, vs)}
    loss, grad_x, g, delta, new_m, new_v = _step(x[0], mem[0], loss_target[0], w, m, v)
    out = [loss, grad_x[None]]
    for d in (g, delta, new_m, new_v):
        out += [d[n].reshape(a.shape) for n, a in zip(ORDER, ws)]
    return tuple(out)
```

```python
import numpy as np

import jax
import jax.numpy as jnp
from jax import lax
from jax.experimental import pallas as pl
from jax.experimental.pallas import tpu as pltpu

F32 = jnp.float32
BF16 = jnp.bfloat16
HI = lax.Precision.HIGHEST
MESH = pl.DeviceIdType.MESH

EPS = 1e-6
HEAD = 128
MEM_DH = 256
CONV_K = 4
CHUNK = 64
CHUNK_SHIFT = 6
LANES = 128
SUB = 8
PACK_W = 1024
PACK_ROWS = 512
VMEM_LIMIT = 56 * 1024 * 1024
NEG = -1e30
SOLVE_PREC = None
FLASH_HEADS_PER_STEP = 4

ADAM_LR, ADAM_B1, ADAM_B2, ADAM_EPS, ADAM_WD, ADAM_STEP = 0.001, 0.9, 0.999, 1e-08, 0.01, 10


def _pcall(body, **kw):
    return pl.pallas_call(body, **kw)


def _cp(*sem):
    return pltpu.CompilerParams(dimension_semantics=sem, vmem_limit_bytes=VMEM_LIMIT)


def _dot(a, b, prec=None):
    return lax.dot_general(a, b, (((1,), (0,)), ((), ())), precision=prec, preferred_element_type=F32)


def _dot_nt(a, b, prec=None):
    return lax.dot_general(a, b, (((1,), (1,)), ((), ())), precision=prec, preferred_element_type=F32)


def _dot_tn(a, b, prec=None):
    return lax.dot_general(a, b, (((0,), (0,)), ((), ())), precision=prec, preferred_element_type=F32)


def _sigmoid(x):
    return 1.0 / (1.0 + jnp.exp(-x))


def _softplus(x):
    return jnp.maximum(x, 0.0) + jnp.log(1.0 + jnp.exp(-jnp.abs(x)))


def _silu(x):
    return x * _sigmoid(x)


def _rms(x, g):
    return x * lax.rsqrt(jnp.mean(x * x, axis=-1, keepdims=True) + EPS) * g


def _ru(a, m):
    return (a + m - 1) // m * m


def _mm(a, b, mode, name, out_dtypes=(F32,), epilogue=None, extras=(), tm=1024, tn=512, tk=2048):
    if mode == "nn":
        (M, K), (K2, N) = a.shape, b.shape
    elif mode == "nt":
        (M, K), (N, K2) = a.shape, b.shape
    else:
        (K, M), (K2, N) = a.shape, b.shape
    assert K == K2, (a.shape, b.shape, mode)
    tm, tn = min(tm, M), min(tn, N)
    tk = next((t for t in (tk, 1024, 512, 256, LANES) if t <= K and K % t == 0), K)
    assert M % tm == 0 and N % tn == 0 and K % tk == 0, (M, N, K, tm, tn, tk)
    nk = K // tk
    n_ex, n_out = len(extras), len(out_dtypes)
    dims = {"nn": ((1,), (0,)), "nt": ((1,), (1,)), "tn": ((0,), (0,))}[mode]

    def finish(res, ex_refs, out_refs):
        outs = epilogue(res, *[r[...] for r in ex_refs]) if epilogue is not None else (res,)
        for o_ref, o in zip(out_refs, outs):
            o_ref[...] = o.astype(o_ref.dtype)

    def body(a_ref, b_ref, *rest):
        ex_refs, out_refs = rest[:n_ex], rest[n_ex:n_ex + n_out]
        part = lax.dot_general(a_ref[...], b_ref[...], (dims, ((), ())), preferred_element_type=F32)
        if nk == 1:
            finish(part, ex_refs, out_refs)
            return
        acc = rest[-1]
        k = pl.program_id(2)

        @pl.when(k == 0)
        def _():
            acc[...] = part

        @pl.when(k > 0)
        def _():
            acc[...] += part

        @pl.when(k == nk - 1)
        def _():
            finish(acc[...], ex_refs, out_refs)

    a_spec = pl.BlockSpec((tk, tm), lambda i, j, k: (k, i)) if mode == "tn" else pl.BlockSpec((tm, tk), lambda i, j, k: (i, k))
    b_spec = pl.BlockSpec((tn, tk), lambda i, j, k: (j, k)) if mode == "nt" else pl.BlockSpec((tk, tn), lambda i, j, k: (k, j))
    mn_spec = pl.BlockSpec((tm, tn), lambda i, j, k: (i, j))
    outs = _pcall(
        body, name=name, grid=(M // tm, N // tn, nk),
        in_specs=[a_spec, b_spec] + [mn_spec] * n_ex,
        out_specs=[mn_spec] * n_out,
        out_shape=[jax.ShapeDtypeStruct((M, N), dt) for dt in out_dtypes],
        scratch_shapes=[pltpu.VMEM((tm, tn), F32)] if nk > 1 else [],
        compiler_params=_cp("parallel", "parallel", "arbitrary"),
    )(a, b, *extras)
    return outs[0] if n_out == 1 else outs


def _rowwise(fn, T, tb, ins, outs, name, scratch=()):
    tb = min(tb, T)
    assert T % tb == 0 and (tb % SUB == 0 or tb == T)
    nblk = T // tb
    r8 = tb // SUB
    in_specs, arrs = [], []
    for spec in ins:
        kind, arr = spec[0], spec[1]
        arrs.append(arr)
        if kind == "full":
            nd = arr.ndim
            in_specs.append(pl.BlockSpec(arr.shape, lambda i, nd=nd: (0,) * nd))
            continue
        off, w = spec[2], spec[3]
        assert off % w == 0 and arr.shape[0] == T, (name, off, w, arr.shape)
        cb = off // w
        if kind == "row":
            in_specs.append(pl.BlockSpec((tb, w), lambda i, cb=cb: (i, cb)))
        elif kind == "prev":
            in_specs.append(pl.BlockSpec((SUB, w), lambda i, cb=cb: (jnp.maximum(i * r8 - 1, 0), cb)))
        else:
            in_specs.append(pl.BlockSpec((SUB, w), lambda i, cb=cb: (jnp.minimum((i + 1) * r8, T // SUB - 1), cb)))
    out_specs, out_shapes, is_acc = [], [], []
    for spec in outs:
        if spec[0] == "row":
            out_specs.append(pl.BlockSpec((tb, spec[1]), lambda i: (i, 0)))
            out_shapes.append(jax.ShapeDtypeStruct((T, spec[1]), spec[2]))
            is_acc.append(False)
        else:
            nd = len(spec[1])
            out_specs.append(pl.BlockSpec(spec[1], lambda i, nd=nd: (0,) * nd))
            out_shapes.append(jax.ShapeDtypeStruct(spec[1], spec[2]))
            is_acc.append(True)
    n_in, n_out = len(ins), len(outs)
    seq = any(is_acc) or len(scratch) > 0

    def body(*refs):
        in_refs, out_refs, scr = refs[:n_in], refs[n_in:n_in + n_out], refs[n_in + n_out:]
        i = pl.program_id(0)
        vals = fn(i, nblk, *[r[...] for r in in_refs], *scr)
        for o_ref, v, acc in zip(out_refs, vals, is_acc):
            if acc:
                @pl.when(i == 0)
                def _(o_ref=o_ref):
                    o_ref[...] = jnp.zeros_like(o_ref)

                o_ref[...] += v.astype(o_ref.dtype)
            else:
                o_ref[...] = v.astype(o_ref.dtype)

    res = _pcall(
        body, name=name, grid=(nblk,), in_specs=in_specs, out_specs=out_specs, out_shape=out_shapes,
        scratch_shapes=list(scratch), compiler_params=_cp("arbitrary" if seq else "parallel"),
    )(*arrs)
    return res


def _heads(x, width):
    return [x[:, h * width:(h + 1) * width] for h in range(x.shape[1] // width)]


def _cat(xs):
    return xs[0] if len(xs) == 1 else jnp.concatenate(xs, axis=1)


def _rms_fwd(x, g, name, tb=512):
    T, D = x.shape
    return _rowwise(lambda i, n, xv, gv: (_rms(xv, gv),), T, tb,
                    [("row", x, 0, D), ("full", g)], [("row", D, BF16)], name)[0]


def _rms_bwd(x, g, dh, dres, name, tb=256):
    T, D = x.shape

    def fn(i, n, xv, gv, dhv, drv):
        _, vjp = jax.vjp(_rms, xv, gv)
        dx, dg = vjp(dhv)
        tot = drv + dx
        return tot, tot, dg

    return _rowwise(fn, T, tb, [("row", x, 0, D), ("full", g), ("row", dh, 0, D), ("row", dres, 0, D)],
                    [("row", D, F32), ("row", D, BF16), ("acc", (1, D), F32)], name)


def _rms_dg(x, g, dh, name, tb=256):
    T, D = x.shape

    def fn(i, n, xv, gv, dhv):
        _, vjp = jax.vjp(lambda gg: _rms(xv, gg), gv)
        return vjp(dhv)

    return _rowwise(fn, T, tb, [("row", x, 0, D), ("full", g), ("row", dh, 0, D)], [("acc", (1, D), F32)], name)[0]


def _shift_down(x, halo, s, first):
    if s == 0:
        return x
    tb, c = x.shape
    xr = pltpu.roll(x, s, 0)
    hr = jnp.where(first, 0.0, pltpu.roll(halo, s, 0))
    hfull = hr if tb == SUB else jnp.concatenate([hr, jnp.zeros((tb - SUB, c), x.dtype)], axis=0)
    row = lax.broadcasted_iota(jnp.int32, x.shape, 0)
    return jnp.where(row < s, hfull, xr)


def _shift_up(z, halo, s, last):
    if s == 0:
        return z
    tb, c = z.shape
    zr = pltpu.roll(z, tb - s, 0)
    hr = jnp.where(last, 0.0, pltpu.roll(halo, SUB - s, 0))
    hfull = hr if tb == SUB else jnp.concatenate([jnp.zeros((tb - SUB, c), z.dtype), hr], axis=0)
    row = lax.broadcasted_iota(jnp.int32, z.shape, 0)
    return jnp.where(row >= tb - s, hfull, zr)


def _conv_pre(x, halo, cw, first):
    xs = [_shift_down(x, halo, s, first) for s in range(CONV_K)]
    y = cw[0:1, :] * xs[3]
    for i in range(1, CONV_K):
        y = y + cw[i:i + 1, :] * xs[CONV_K - 1 - i]
    return y, xs


def _qk_post(y, scale):
    a = _silu(y)
    return a * lax.rsqrt(jnp.sum(a * a, axis=-1, keepdims=True) + EPS) * scale


def _small_fn(s, alog, dtb, bf, gh, fh):
    lane = lax.broadcasted_iota(jnp.int32, s.shape, 1)
    beta = _sigmoid(s)
    g = -jnp.exp(alog) * _softplus(s + dtb)
    lf = -_softplus(-(s + bf))
    return jnp.where(lane < gh, beta, jnp.where(lane < 2 * gh, g, jnp.where(lane < 2 * gh + fh, lf, 0.0)))


def _gdn_prep(proj, offs, cws, pvecs, GH, FH, GW, tb=256):
    T = proj.shape[0]
    tb = min(tb, T)
    qscale = HEAD ** -0.5

    def fn(i, n, xq, hq, xk, hk, xv, hv, cwq, cwk, cwv, s, alog, dtb, bf, carry):
        first = i == 0
        yq, _ = _conv_pre(xq, hq, cwq, first)
        yk, _ = _conv_pre(xk, hk, cwk, first)
        yv, _ = _conv_pre(xv, hv, cwv, first)
        qn = _cat([_qk_post(y, qscale) for y in _heads(yq, HEAD)])
        kn = _cat([_qk_post(y, 1.0) for y in _heads(yk, HEAD)])
        vc = _silu(yv)
        gsm = _small_fn(s, alog, dtb, bf, GH, FH)

        @pl.when(first)
        def _():
            carry[...] = jnp.zeros_like(carry)

        ri = lax.broadcasted_iota(jnp.int32, (tb, tb), 0)
        ci = lax.broadcasted_iota(jnp.int32, (tb, tb), 1)
        cum = _dot((ri >= ci).astype(F32), gsm, HI) + carry[0:1, :]
        carry[...] += _dot(jnp.ones((SUB, tb), F32), gsm, HI)
        in_chunk = (ri >= ci) & ((ri >> CHUNK_SHIFT) == (ci >> CHUNK_SHIFT))
        lane = lax.broadcasted_iota(jnp.int32, gsm.shape, 1)
        gbm = jnp.where(lane < GH, gsm, _dot(in_chunk.astype(F32), gsm, HI))
        return qn, kn, vc, cum, gbm

    ins = []
    for key in ("q", "k", "v"):
        ins += [("row", proj, offs[key], GW), ("prev", proj, offs[key], GW)]
    ins += [("full", c) for c in cws] + [("row", proj, offs["small"], LANES)] + [("full", p) for p in pvecs]
    outs = [("row", GW, F32)] * 3 + [("row", LANES, F32)] * 2
    return _rowwise(fn, T, tb, ins, outs, "gdn_prep", scratch=[pltpu.VMEM((SUB, LANES), F32)])


def _gdn_prep_bwd_a(proj, offs, cws, pvecs, cts, dgsm_scan, dlf_sm, GH, FH, GW, tb=256):
    T = proj.shape[0]
    qscale = HEAD ** -0.5

    def one(x, halo, cw, ct, first, post):
        y, xs = _conv_pre(x, halo, cw, first)
        if post is None:
            _, vjp = jax.vjp(_silu, y)
            dy = vjp(ct)[0]
        else:
            dys = []
            for yh, cth in zip(_heads(y, HEAD), _heads(ct, HEAD)):
                _, vjp = jax.vjp(lambda t: _qk_post(t, post), yh)
                dys.append(vjp(cth)[0])
            dy = _cat(dys)
        row = lax.broadcasted_iota(jnp.int32, (SUB, x.shape[1]), 0)
        dcw = jnp.zeros((SUB, x.shape[1]), F32)
        for i in range(CONV_K):
            dcw = dcw + jnp.where(row == i, jnp.sum(dy * xs[CONV_K - 1 - i], axis=0, keepdims=True), 0.0)
        return dy, dcw

    def fn(i, n, xq, hq, xk, hk, xv, hv, cwq, cwk, cwv, cq, ck, cv, s, alog, dtb, bf, d1, d2):
        first = i == 0
        dyq, dcq = one(xq, hq, cwq, cq, first, qscale)
        dyk, dck = one(xk, hk, cwk, ck, first, 1.0)
        dyv, dcv = one(xv, hv, cwv, cv, first, None)
        tb_ = d1.shape[0]
        ri = lax.broadcasted_iota(jnp.int32, (tb_, tb_), 0)
        ci = lax.broadcasted_iota(jnp.int32, (tb_, tb_), 1)
        later = (ci >= ri) & ((ri >> CHUNK_SHIFT) == (ci >> CHUNK_SHIFT))
        lane = lax.broadcasted_iota(jnp.int32, d1.shape, 1)
        d1 = jnp.where(lane < GH, d1, _dot(later.astype(F32), d1, HI))
        _, vjp = jax.vjp(lambda a, b, c, d: _small_fn(a, b, c, d, GH, FH), s, alog, dtb, bf)
        ds, dalog, ddtb, dbf = vjp(d1 + d2)
        return dyq, dyk, dyv, dcq, dck, dcv, ds, dalog, ddtb, dbf

    ins = []
    for key in ("q", "k", "v"):
        ins += [("row", proj, offs[key], GW), ("prev", proj, offs[key], GW)]
    ins += [("full", c) for c in cws] + [("row", c, 0, GW) for c in cts]
    ins += [("row", proj, offs["small"], LANES)] + [("full", p) for p in pvecs]
    ins += [("row", dgsm_scan, 0, LANES), ("row", dlf_sm, 0, LANES)]
    outs = [("row", GW, F32)] * 3 + [("acc", (SUB, GW), F32)] * 3 + [("row", LANES, BF16)] + [("acc", (1, LANES), F32)] * 3
    return _rowwise(fn, T, tb, ins, outs, "gdn_prep_bwd_a")


def _gdn_prep_bwd_b(dys, cws, GW, tb=256):
    T = dys[0].shape[0]

    def fn(i, n, dq, nq, dk, nk, dv, nv, cwq, cwk, cwv):
        last = i == n - 1
        res = []
        for dy, nh, cw in ((dq, nq, cwq), (dk, nk, cwk), (dv, nv, cwv)):
            dx = cw[CONV_K - 1:CONV_K, :] * dy
            for t in range(CONV_K - 1):
                dx = dx + cw[t:t + 1, :] * _shift_up(dy, nh, CONV_K - 1 - t, last)
            res.append(dx)
        return tuple(res)

    ins = []
    for dy in dys:
        ins += [("row", dy, 0, GW), ("next", dy, 0, GW)]
    ins += [("full", c) for c in cws]
    return _rowwise(fn, T, tb, ins, [("row", GW, BF16)] * 3, "gdn_prep_bwd_b")


def _gdn_chunks(qs, ks, vs, gams, bcols, s0s):
    c, d = qs[0].shape
    hs = range(len(qs))
    ri = lax.broadcasted_iota(jnp.int32, (c, c), 0)
    ci = lax.broadcasted_iota(jnp.int32, (c, c), 1)
    incl, strict = ri >= ci, ri > ci
    eye = (ri == ci).astype(F32)
    ones_cc = jnp.ones((c, c), F32)
    rows = lax.broadcasted_iota(jnp.int32, (c, 1), 0)
    b16 = (ri >> 4) == (ci >> 4)
    b32 = (ri >> 5) == (ci >> 5)
    gam_cc = [gams[h] * ones_cc for h in hs]
    gam_t = [_dot_nt(eye, gam_cc[h], HI) for h in hs]
    glast = [jnp.sum(jnp.where(rows == c - 1, gams[h], 0.0), axis=0, keepdims=True) for h in hs]
    dec_i = [jnp.where(incl, jnp.exp(jnp.where(incl, gam_cc[h] - gam_t[h], 0.0)), 0.0) for h in hs]
    kk = [_dot_nt(ks[h], ks[h]) for h in hs]
    m = [bcols[h] * kk[h] * jnp.where(strict, dec_i[h], 0.0) for h in hs]
    m32 = [jnp.where(b32 & ~b16, m[h], 0.0) for h in hs]
    m64 = [jnp.where(b32, 0.0, m[h]) for h in hs]
    mp = [jnp.where(b16, m[h], 0.0) for h in hs]
    p = [eye - mp[h] for h in hs]
    for _ in range(3):
        mp = [_dot(mp[h], mp[h], SOLVE_PREC) for h in hs]
        p = [p[h] + _dot(p[h], mp[h], SOLVE_PREC) for h in hs]
    t = [_dot(p[h], m32[h], SOLVE_PREC) for h in hs]
    p = [p[h] - _dot(t[h], p[h], SOLVE_PREC) for h in hs]
    t = [_dot(p[h], m64[h], SOLVE_PREC) for h in hs]
    ainv = [p[h] - _dot(t[h], p[h], SOLVE_PREC) for h in hs]
    eg = [jnp.exp(gams[h]) for h in hs]
    w = [_dot(ainv[h], (bcols[h] * eg[h]) * ks[h], SOLVE_PREC) for h in hs]
    u0 = [_dot(ainv[h], bcols[h] * vs[h], SOLVE_PREC) for h in hs]
    qk = [_dot_nt(qs[h], ks[h]) * dec_i[h] for h in hs]
    u = [u0[h] - _dot(w[h], s0s[h]) for h in hs]
    o = [_dot(qs[h] * eg[h], s0s[h]) + _dot(qk[h], u[h]) for h in hs]
    s1 = [jnp.exp(glast[h]) * s0s[h] + _dot_tn(ks[h] * jnp.exp(glast[h] - gams[h]), u[h]) for h in hs]
    return tuple(o), tuple(s1)


def _lane_col(x, lane_idx):
    lane = lax.broadcasted_iota(jnp.int32, x.shape, 1)
    return jnp.sum(jnp.where(lane == lane_idx, x, 0.0), axis=1, keepdims=True)


def _gdn_scan_fwd(qn, kn, vc, gsm, GH):
    T, GW = qn.shape
    nc = T // CHUNK

    def body(q_ref, k_ref, v_ref, g_ref, o_ref, sall_ref, s_scr):
        @pl.when(pl.program_id(0) == 0)
        def _():
            s_scr[...] = jnp.zeros_like(s_scr)

        gs = g_ref[...]
        sls = [slice(h * HEAD, (h + 1) * HEAD) for h in range(GH)]
        s0s = tuple(s_scr[h] for h in range(GH))
        os_, s1s = _gdn_chunks(tuple(q_ref[:, sl] for sl in sls), tuple(k_ref[:, sl] for sl in sls),
                               tuple(v_ref[:, sl] for sl in sls), tuple(_lane_col(gs, GH + h) for h in range(GH)),
                               tuple(_lane_col(gs, h) for h in range(GH)), s0s)
        for h in range(GH):
            sall_ref[0, h] = s0s[h]
            o_ref[:, sls[h]] = os_[h]
            s_scr[h] = s1s[h]

    row = pl.BlockSpec((CHUNK, GW), lambda i: (i, 0))
    return _pcall(
        body, name="gdn_scan_fwd", grid=(nc,),
        in_specs=[row, row, row, pl.BlockSpec((CHUNK, LANES), lambda i: (i, 0))],
        out_specs=[row, pl.BlockSpec((1, GH, HEAD, HEAD), lambda i: (i, 0, 0, 0))],
        out_shape=[jax.ShapeDtypeStruct((T, GW), F32), jax.ShapeDtypeStruct((nc, GH, HEAD, HEAD), F32)],
        scratch_shapes=[pltpu.VMEM((GH, HEAD, HEAD), F32)],
        compiler_params=_cp("arbitrary"),
    )(qn, kn, vc, gsm)


def _gdn_scan_bwd(qn, kn, vc, gsm, sall, do, GH):
    T, GW = qn.shape
    nc = T // CHUNK

    def body(q_ref, k_ref, v_ref, g_ref, sall_ref, do_ref, dq_ref, dk_ref, dv_ref, dg_ref, ds_scr):
        @pl.when(pl.program_id(0) == 0)
        def _():
            ds_scr[...] = jnp.zeros_like(ds_scr)

        gs = g_ref[...]
        lane = lax.broadcasted_iota(jnp.int32, gs.shape, 1)
        sls = [slice(h * HEAD, (h + 1) * HEAD) for h in range(GH)]
        _, vjp = jax.vjp(_gdn_chunks, tuple(q_ref[:, sl] for sl in sls), tuple(k_ref[:, sl] for sl in sls),
                         tuple(v_ref[:, sl] for sl in sls), tuple(_lane_col(gs, GH + h) for h in range(GH)),
                         tuple(_lane_col(gs, h) for h in range(GH)), tuple(sall_ref[0, h] for h in range(GH)))
        dq, dk, dv, dgc, dbc, ds0 = vjp((tuple(do_ref[:, sl] for sl in sls), tuple(ds_scr[h] for h in range(GH))))
        dgs = jnp.zeros_like(gs)
        for h in range(GH):
            dq_ref[:, sls[h]] = dq[h]
            dk_ref[:, sls[h]] = dk[h]
            dv_ref[:, sls[h]] = dv[h]
            dgs = dgs + jnp.where(lane == h, dbc[h], 0.0) + jnp.where(lane == GH + h, dgc[h], 0.0)
            ds_scr[h] = ds0[h]
        dg_ref[...] = dgs

    row = pl.BlockSpec((CHUNK, GW), lambda i: (nc - 1 - i, 0))
    sm = pl.BlockSpec((CHUNK, LANES), lambda i: (nc - 1 - i, 0))
    return _pcall(
        body, name="gdn_scan_bwd", grid=(nc,),
        in_specs=[row, row, row, sm, pl.BlockSpec((1, GH, HEAD, HEAD), lambda i: (nc - 1 - i, 0, 0, 0)), row],
        out_specs=[row, row, row, sm],
        out_shape=[jax.ShapeDtypeStruct((T, GW), F32)] * 3 + [jax.ShapeDtypeStruct((T, LANES), F32)],
        scratch_shapes=[pltpu.VMEM((GH, HEAD, HEAD), F32)],
        compiler_params=_cp("arbitrary"),
    )(qn, kn, vc, gsm, sall, do)


def _gdn_post_fn(o, z, g):
    return _rms(o, g) * _silu(z)


def _gdn_post(o, proj, off_z, g, GW, tb=512):
    T = o.shape[0]

    def fn(i, n, ov, zv, gv):
        return (_cat([_gdn_post_fn(a, b, gv) for a, b in zip(_heads(ov, HEAD), _heads(zv, HEAD))]),)

    return _rowwise(fn, T, tb, [("row", o, 0, GW), ("row", proj, off_z, GW), ("full", g)], [("row", GW, BF16)], "gdn_post")[0]


def _gdn_post_bwd(o, proj, off_z, g, dout, GW, tb=256):
    T = o.shape[0]

    def fn(i, n, ov, zv, gv, dv):
        dos, dzs, dg = [], [], jnp.zeros_like(gv)
        for a, b, c in zip(_heads(ov, HEAD), _heads(zv, HEAD), _heads(dv, HEAD)):
            _, vjp = jax.vjp(_gdn_post_fn, a, b, gv)
            da, db, dgh = vjp(c)
            dos.append(da)
            dzs.append(db)
            dg = dg + dgh
        return _cat(dos), _cat(dzs), dg

    return _rowwise(fn, T, tb, [("row", o, 0, GW), ("row", proj, off_z, GW), ("full", g), ("row", dout, 0, GW)],
                    [("row", GW, F32), ("row", GW, BF16), ("acc", (1, HEAD), F32)], "gdn_post_bwd")


def _fox_prep(proj, offs, gq, gk, FW, tb=512):
    T = proj.shape[0]

    def fn(i, n, q, k, v, gqv, gkv):
        return (_cat([_rms(a, gqv) for a in _heads(q, HEAD)]), _cat([_rms(a, gkv) for a in _heads(k, HEAD)]), v)

    return _rowwise(fn, T, tb, [("row", proj, offs["fq"], FW), ("row", proj, offs["fk"], FW), ("row", proj, offs["fv"], FW),
                                ("full", gq), ("full", gk)], [("row", FW, BF16)] * 3, "fox_prep")


def _fox_prep_bwd(proj, offs, gq, gk, dq, dk, dv, FW, tb=256):
    T = proj.shape[0]

    def fn(i, n, q, k, gqv, gkv, dqv, dkv, dvv):
        res = []
        for x, g, d in ((q, gqv, dqv), (k, gkv, dkv)):
            dxs, dg = [], jnp.zeros_like(g)
            for a, c in zip(_heads(x, HEAD), _heads(d, HEAD)):
                _, vjp = jax.vjp(_rms, a, g)
                da, dgh = vjp(c)
                dxs.append(da)
                dg = dg + dgh
            res += [_cat(dxs), dg]
        return res[0], res[2], dvv, res[1], res[3]

    return _rowwise(fn, T, tb, [("row", proj, offs["fq"], FW), ("row", proj, offs["fk"], FW), ("full", gq), ("full", gk),
                                ("row", dq, 0, FW), ("row", dk, 0, FW), ("row", dv, 0, FW)],
                    [("row", FW, BF16)] * 3 + [("acc", (1, HEAD), F32)] * 2, "fox_prep_bwd")


def _sub_row(x, sub_idx):
    sub = lax.broadcasted_iota(jnp.int32, x.shape, 0)
    return jnp.sum(jnp.where(sub == sub_idx, x, 0.0), axis=0, keepdims=True)


def _causal_pairs(nb, key_major):
    if key_major:
        pairs = [(i, j) for j in range(nb) for i in range(j, nb)]
    else:
        pairs = [(i, j) for i in range(nb) for j in range(i + 1)]
    return (jnp.asarray(np.array([p[0] for p in pairs], np.int32)),
            jnp.asarray(np.array([p[1] for p in pairs], np.int32)))


def _fox_scores(q, k, cq, ck, diagonal):
    s = _dot_nt(q, k) * (HEAD ** -0.5) + (cq - ck)
    if diagonal:
        row = lax.broadcasted_iota(jnp.int32, s.shape, 0)
        col = lax.broadcasted_iota(jnp.int32, s.shape, 1)
        s = jnp.where(row >= col, s, NEG)
    return s


def _flash_fwd(qb, kb, vb, cum, cumt, lane0, FH, blk):
    T, FW = qb.shape
    blk = min(blk, T)
    nb = T // blk
    qi_arr, kj_arr = _causal_pairs(nb, False)

    hps = FLASH_HEADS_PER_STEP
    assert FH % hps == 0
    us = range(hps)
    sl = [slice(u * HEAD, (u + 1) * HEAD) for u in us]

    def body(qi_ref, kj_ref, q_ref, k_ref, v_ref, cq_ref, ck_ref, ob_ref, lse_ref, m_scr, l_scr, acc):
        t = pl.program_id(1)
        qi, kj = qi_ref[t], kj_ref[t]

        @pl.when(kj == 0)
        def _():
            m_scr[...] = jnp.full_like(m_scr, NEG)
            l_scr[...] = jnp.zeros_like(l_scr)
            acc[...] = jnp.zeros_like(acc)

        def update(diagonal):
            s = [_fox_scores(q_ref[:, sl[u]], k_ref[:, sl[u]], cq_ref[u], ck_ref[u], diagonal) for u in us]
            m_old = [m_scr[u] for u in us]
            m_new = [jnp.maximum(m_old[u], jnp.max(s[u], axis=1, keepdims=True)) for u in us]
            alpha = [jnp.exp(m_old[u] - m_new[u]) for u in us]
            p = [jnp.exp(s[u] - m_new[u]) for u in us]
            pv = [_dot(p[u].astype(BF16), v_ref[:, sl[u]]) for u in us]
            for u in us:
                l_scr[u] = alpha[u] * l_scr[u] + jnp.sum(p[u], axis=1, keepdims=True)
                acc[:, sl[u]] = alpha[u] * acc[:, sl[u]] + pv[u]
                m_scr[u] = m_new[u]

        @pl.when(kj < qi)
        def _():
            update(False)

        @pl.when(kj == qi)
        def _():
            update(True)
            for u in us:
                ob_ref[:, sl[u]] = (acc[:, sl[u]] / l_scr[u]).astype(BF16)
                lse_ref[u] = m_scr[u] + jnp.log(l_scr[u])

    qs = pl.BlockSpec((blk, hps * HEAD), lambda g, t, qr, kr: (qr[t], g))
    ks = pl.BlockSpec((blk, hps * HEAD), lambda g, t, qr, kr: (kr[t], g))
    col = pl.BlockSpec((hps, blk, 1), lambda g, t, qr, kr: (g, qr[t], 0))
    gs = pltpu.PrefetchScalarGridSpec(
        num_scalar_prefetch=2, grid=(FH // hps, qi_arr.shape[0]),
        in_specs=[qs, ks, ks, col, pl.BlockSpec((hps, 1, blk), lambda g, t, qr, kr: (g, 0, kr[t]))],
        out_specs=[qs, col],
        scratch_shapes=[pltpu.VMEM((hps, blk, 1), F32), pltpu.VMEM((hps, blk, 1), F32), pltpu.VMEM((blk, hps * HEAD), F32)])
    return _pcall(
        body, name="flash_fwd", grid_spec=gs,
        out_shape=[jax.ShapeDtypeStruct((T, FW), BF16), jax.ShapeDtypeStruct((FH, T, 1), F32)],
        compiler_params=_cp("parallel", "arbitrary"),
    )(qi_arr, kj_arr, qb, kb, vb, cum, cumt)


def _flash_bwd_q(qb, kb, vb, cum, cumt, lse, do, dl, lane0, FH, blk):
    T, FW = qb.shape
    blk = min(blk, T)
    nb = T // blk
    qi_arr, kj_arr = _causal_pairs(nb, False)
    want_dq = dl is not None
    hps = FLASH_HEADS_PER_STEP
    assert FH % hps == 0
    us = range(hps)
    sl = [slice(u * HEAD, (u + 1) * HEAD) for u in us]

    def body(qi_ref, kj_ref, q_ref, k_ref, v_ref, cq_ref, ck_ref, lse_ref, do_ref, *rest):
        if want_dq:
            dl_ref, dq_ref, rs_ref, acc, rs_acc = rest
        else:
            dl_ref, acc = rest
        t = pl.program_id(1)
        qi, kj = qi_ref[t], kj_ref[t]

        @pl.when(kj == 0)
        def _():
            acc[...] = jnp.zeros_like(acc)
            if want_dq:
                rs_acc[...] = jnp.zeros_like(rs_acc)

        def update(diagonal):
            s = [_fox_scores(q_ref[:, sl[u]], k_ref[:, sl[u]], cq_ref[u], ck_ref[u], diagonal) for u in us]
            p = [jnp.exp(s[u] - lse_ref[u]) for u in us]
            dp = [_dot_nt(do_ref[:, sl[u]].astype(BF16), v_ref[:, sl[u]]) for u in us]
            if want_dq:
                ds = [p[u] * (dp[u] - dl_ref[u]) for u in us]
                dqp = [_dot(ds[u].astype(BF16), k_ref[:, sl[u]]) for u in us]
                for u in us:
                    acc[:, sl[u]] += dqp[u]
                    rs_acc[u] += jnp.sum(ds[u], axis=1, keepdims=True)
            else:
                for u in us:
                    acc[u] += jnp.sum(p[u] * dp[u], axis=1, keepdims=True)

        @pl.when(kj < qi)
        def _():
            update(False)

        @pl.when(kj == qi)
        def _():
            update(True)
            if want_dq:
                dq_ref[...] = acc[...] * (HEAD ** -0.5)
                rs_ref[...] = rs_acc[...]
            else:
                dl_ref[...] = acc[...]

    qs = pl.BlockSpec((blk, hps * HEAD), lambda g, t, qr, kr: (qr[t], g))
    ks = pl.BlockSpec((blk, hps * HEAD), lambda g, t, qr, kr: (kr[t], g))
    col = pl.BlockSpec((hps, blk, 1), lambda g, t, qr, kr: (g, qr[t], 0))
    in_specs = [qs, ks, ks, col, pl.BlockSpec((hps, 1, blk), lambda g, t, qr, kr: (g, 0, kr[t])), col, qs]
    args = [qi_arr, kj_arr, qb, kb, vb, cum, cumt, lse, do]
    colshape = jax.ShapeDtypeStruct((FH, T, 1), F32)
    if want_dq:
        gs = pltpu.PrefetchScalarGridSpec(
            num_scalar_prefetch=2, grid=(FH // hps, qi_arr.shape[0]), in_specs=in_specs + [col], out_specs=[qs, col],
            scratch_shapes=[pltpu.VMEM((blk, hps * HEAD), F32), pltpu.VMEM((hps, blk, 1), F32)])
        return _pcall(body, name="flash_bwd_dq", grid_spec=gs,
                      out_shape=[jax.ShapeDtypeStruct((T, FW), F32), colshape],
                      compiler_params=_cp("parallel", "arbitrary"))(*args, dl)
    gs = pltpu.PrefetchScalarGridSpec(
        num_scalar_prefetch=2, grid=(FH // hps, qi_arr.shape[0]), in_specs=in_specs, out_specs=col,
        scratch_shapes=[pltpu.VMEM((hps, blk, 1), F32)])
    return _pcall(body, name="flash_bwd_rowterm", grid_spec=gs, out_shape=colshape,
                  compiler_params=_cp("parallel", "arbitrary"))(*args)


def _flash_bwd_dkv(qb, kb, vb, cum, cumt, lse_row, dl_row, do, lane0, FH, blk):
    T, FW = qb.shape
    blk = min(blk, T)
    nb = T // blk

    qi_arr, kj_arr = _causal_pairs(nb, True)

    hps = FLASH_HEADS_PER_STEP
    assert FH % hps == 0
    us = range(hps)
    sl = [slice(u * HEAD, (u + 1) * HEAD) for u in us]

    def body(qi_ref, kj_ref, q_ref, k_ref, v_ref, ck_ref, cq_ref, lse_ref, dl_ref, do_ref,
             dk_ref, dv_ref, dc_ref, dk_acc, dv_acc, dc_acc):
        t = pl.program_id(1)
        qi, kj = qi_ref[t], kj_ref[t]

        def update(diagonal, first):
            st = [_dot_nt(k_ref[:, sl[u]], q_ref[:, sl[u]]) * (HEAD ** -0.5) + (cq_ref[u] - ck_ref[u]) for u in us]
            if diagonal:
                krow = lax.broadcasted_iota(jnp.int32, st[0].shape, 0)
                qcol = lax.broadcasted_iota(jnp.int32, st[0].shape, 1)
                st = [jnp.where(qcol >= krow, st[u], NEG) for u in us]
            pt = [jnp.exp(st[u] - lse_ref[u]) for u in us]
            dob = [do_ref[:, sl[u]].astype(BF16) for u in us]
            dpt = [_dot_nt(v_ref[:, sl[u]], dob[u]) for u in us]
            dst = [pt[u] * (dpt[u] - dl_ref[u]) for u in us]
            dk = [_dot(dst[u].astype(BF16), q_ref[:, sl[u]]) for u in us]
            dv = [_dot(pt[u].astype(BF16), dob[u]) for u in us]
            for u in us:
                dc = -jnp.sum(dst[u], axis=1, keepdims=True)
                if first:
                    dk_acc[:, sl[u]] = dk[u]
                    dv_acc[:, sl[u]] = dv[u]
                    dc_acc[u] = dc
                else:
                    dk_acc[:, sl[u]] += dk[u]
                    dv_acc[:, sl[u]] += dv[u]
                    dc_acc[u] += dc

        @pl.when(qi == kj)
        def _():
            update(True, True)

        @pl.when(qi > kj)
        def _():
            update(False, False)

        @pl.when(qi == nb - 1)
        def _():
            dk_ref[...] = dk_acc[...] * (HEAD ** -0.5)
            dv_ref[...] = dv_acc[...]
            dc_ref[...] = dc_acc[...]

    ks = pl.BlockSpec((blk, hps * HEAD), lambda g, t, qr, kr: (kr[t], g))
    qs = pl.BlockSpec((blk, hps * HEAD), lambda g, t, qr, kr: (qr[t], g))
    rowq = pl.BlockSpec((hps, 1, blk), lambda g, t, qr, kr: (g, 0, qr[t]))
    colk = pl.BlockSpec((hps, blk, 1), lambda g, t, qr, kr: (g, kr[t], 0))
    gs = pltpu.PrefetchScalarGridSpec(
        num_scalar_prefetch=2, grid=(FH // hps, qi_arr.shape[0]),
        in_specs=[qs, ks, ks, colk, rowq, rowq, rowq, qs],
        out_specs=[ks, ks, colk],
        scratch_shapes=[pltpu.VMEM((blk, hps * HEAD), F32), pltpu.VMEM((blk, hps * HEAD), F32),
                        pltpu.VMEM((hps, blk, 1), F32)])
    return _pcall(
        body, name="flash_bwd_dkv", grid_spec=gs,
        out_shape=[jax.ShapeDtypeStruct((T, FW), F32), jax.ShapeDtypeStruct((T, FW), F32),
                   jax.ShapeDtypeStruct((FH, T, 1), F32)],
        compiler_params=_cp("parallel", "arbitrary"),
    )(qi_arr, kj_arr, qb, kb, vb, cum, cumt, lse_row, dl_row, do)


def _rev_cumsum_rows(r1, r2, tb=512):
    H, T = r1.shape
    tb = min(tb, T)
    nb = T // tb

    def body(r1_ref, r2_ref, o_ref, carry):
        @pl.when(pl.program_id(0) == 0)
        def _():
            carry[...] = jnp.zeros_like(carry)

        rv = r1_ref[...] + r2_ref[...]
        si = lax.broadcasted_iota(jnp.int32, (tb, tb), 0)
        ti = lax.broadcasted_iota(jnp.int32, (tb, tb), 1)
        o_ref[...] = _dot(rv, (si >= ti).astype(F32), HI) + carry[...]
        carry[...] += jnp.sum(rv, axis=1, keepdims=True)

    spec = pl.BlockSpec((H, tb), lambda i: (0, nb - 1 - i))
    return _pcall(body, name="rev_cumsum", grid=(nb,), in_specs=[spec, spec], out_specs=spec,
                  out_shape=jax.ShapeDtypeStruct((H, T), F32), scratch_shapes=[pltpu.VMEM((H, 1), F32)],
                  compiler_params=_cp("arbitrary"))(r1, r2)


def _mem_head(q, k, v, gq, gk):
    logits = _dot_nt(_rms(q, gq), _rms(k, gk)) * (MEM_DH ** -0.5)
    mx = jnp.max(logits, axis=1, keepdims=True)
    e = jnp.exp(logits - mx)
    p = e / jnp.sum(e, axis=1, keepdims=True)
    return _dot(p, v)


def _mem_attn(proj, off_q, kv, gq, gk, MW, tb=512):
    T = proj.shape[0]
    MH = MW // MEM_DH

    def fn(i, n, q, kvv, gqv, gkv):
        ks, vs = _heads(kvv[:, :MW], MEM_DH), _heads(kvv[:, MW:], MEM_DH)
        return (_cat([_mem_head(a, b, c, gqv, gkv) for a, b, c in zip(_heads(q, MEM_DH), ks, vs)]),)

    return _rowwise(fn, T, tb, [("row", proj, off_q, MW), ("full", kv), ("full", gq), ("full", gk)],
                    [("row", MW, BF16)], "mem_attn")[0]


def _mem_attn_bwd(proj, off_q, kv, gq, gk, dout, MW, tb=256):
    T = proj.shape[0]
    ML = kv.shape[0]

    def fn(i, n, q, kvv, gqv, gkv, dv):
        ks, vs = _heads(kvv[:, :MW], MEM_DH), _heads(kvv[:, MW:], MEM_DH)
        dqs, dks, dvs = [], [], []
        dgq, dgk = jnp.zeros_like(gqv), jnp.zeros_like(gkv)
        for a, b, c, d in zip(_heads(q, MEM_DH), ks, vs, _heads(dv, MEM_DH)):
            _, vjp = jax.vjp(_mem_head, a, b, c, gqv, gkv)
            da, db, dc, dg1, dg2 = vjp(d)
            dqs.append(da)
            dks.append(db)
            dvs.append(dc)
            dgq, dgk = dgq + dg1, dgk + dg2
        return _cat(dqs), _cat(dks + dvs), dgq, dgk

    return _rowwise(fn, T, tb, [("row", proj, off_q, MW), ("full", kv), ("full", gq), ("full", gk), ("row", dout, 0, MW)],
                    [("row", MW, BF16), ("acc", (ML, 2 * MW), F32), ("acc", (1, MEM_DH), F32), ("acc", (1, MEM_DH), F32)],
                    "mem_attn_bwd")


def _merge_fn(ga, gb, gm, ua, ub, um):
    return _sigmoid(ga) * ua + _sigmoid(gb) * ub + _sigmoid(gm) * um


def _merge(proj, offs, ua, ub, um, D, tb=256):
    T = proj.shape[0]
    ins = [("row", proj, offs[k], D) for k in ("ga", "gb", "gm")] + [("row", u, 0, D) for u in (ua, ub, um)]
    return _rowwise(lambda i, n, *v: (_merge_fn(*v),), T, tb, ins, [("row", D, BF16)], "merge")[0]


def _merge_bwd(proj, offs, ua, ub, um, dy, D, tb=256):
    T = proj.shape[0]

    def fn(i, n, *v):
        _, vjp = jax.vjp(_merge_fn, *v[:6])
        return vjp(v[6])

    ins = [("row", proj, offs[k], D) for k in ("ga", "gb", "gm")] + [("row", u, 0, D) for u in (ua, ub, um)] + [("row", dy, 0, D)]
    return _rowwise(fn, T, tb, ins, [("row", D, BF16)] * 6, "merge_bwd")


def _loss_grad(x2, tgt, tb=256):
    T, D = x2.shape

    def fn(i, n, a, b):
        e = a - b
        part = jnp.sum(jnp.sum(e * e, axis=1, keepdims=True), axis=0, keepdims=True) * (0.5 / D)
        g = e * (1.0 / D)
        return g, g, part + jnp.zeros((SUB, LANES), F32)

    return _rowwise(fn, T, tb, [("row", x2, 0, D), ("row", tgt, 0, D)],
                    [("row", D, F32), ("row", D, BF16), ("acc", (SUB, LANES), F32)], "loss_grad")


def _adamw(w, g, m, v, name, tb=128):
    R, C = w.shape
    c1 = 1.0 / (1.0 - ADAM_B1 ** ADAM_STEP)
    c2 = 1.0 / (1.0 - ADAM_B2 ** ADAM_STEP)

    def fn(i, n, wv, gv, mv, vv):
        mn = ADAM_B1 * mv + (1.0 - ADAM_B1) * gv
        vn = ADAM_B2 * vv + (1.0 - ADAM_B2) * (gv * gv)
        delta = -ADAM_LR * ((mn * c1) / (jnp.sqrt(vn * c2) + ADAM_EPS) + ADAM_WD * wv)
        return delta, mn, vn

    return _rowwise(fn, R, tb, [("row", a, 0, C) for a in (w, g, m, v)], [("row", C, F32)] * 3, name)


def _coords():
    return lax.axis_index("x"), lax.axis_index("y"), lax.axis_index("c")


def _allgather_small(blk):
    m_per, n = blk.shape

    def body(x_ref, out_ref, send_sems, recv_sems, local_sem):
        x, y, c = _coords()
        me, sibling = (x, y, c), (x, y, 1 - c)
        chips = [(1 - x, y), (x, 1 - y), (1 - x, 1 - y)]

        def rows(px, py, pc):
            return out_ref.at[pl.ds((4 * px + 2 * py + pc) * m_per, m_per), :]

        def copy(k, block, to, src=None):
            return pltpu.make_async_remote_copy(
                src_ref=rows(*block) if src is None else src, dst_ref=rows(*block),
                send_sem=send_sems.at[k], recv_sem=recv_sems.at[k], device_id=to, device_id_type=MESH)

        mine = pltpu.make_async_copy(x_ref, rows(*me), local_sem)
        mine.start()
        first = [copy(0, me, sibling, src=x_ref)]
        first += [copy(1 + j, me, (*chip, c), src=x_ref) for j, chip in enumerate(chips)]
        for cp in first:
            cp.start()
        passed = [copy(4 + j, (*chip, c), sibling) for j, chip in enumerate(chips)]
        for j, chip in enumerate(chips):
            copy(1 + j, (*chip, c), me).wait_recv()
            passed[j].start()
        copy(0, sibling, me).wait_recv()
        for j, chip in enumerate(chips):
            copy(4 + j, (*chip, 1 - c), me).wait_recv()
        for cp in first + passed:
            cp.wait_send()
        mine.wait()

    return _pcall(
        body, name="allgather_small", out_shape=jax.ShapeDtypeStruct((8 * m_per, n), blk.dtype),
        in_specs=[pl.BlockSpec(memory_space=pltpu.VMEM)], out_specs=pl.BlockSpec(memory_space=pltpu.VMEM),
        scratch_shapes=[pltpu.SemaphoreType.DMA((7,)), pltpu.SemaphoreType.DMA((7,)), pltpu.SemaphoreType.DMA],
        compiler_params=pltpu.CompilerParams(vmem_limit_bytes=VMEM_LIMIT),
    )(blk)


def _sum8(g, m_per):
    n = g.shape[1]

    def body(g_ref, o_ref):
        acc = g_ref[pl.ds(0, m_per), :]
        for d in range(1, 8):
            acc = acc + g_ref[pl.ds(d * m_per, m_per), :]
        o_ref[...] = acc

    return _pcall(body, name="sum8", out_shape=jax.ShapeDtypeStruct((m_per, n), g.dtype))(g)


_ANY = pl.BlockSpec(memory_space=pl.ANY)


def _allgather_chips(buf):
    nr, w = buf.shape
    half = nr // 2

    def body(in_ref, out_ref, send_sems, recv_sems):
        x, y, c = _coords()
        me = 2 * x + y
        chips = [(1 - x, y), (x, 1 - y), (1 - x, 1 - y)]
        mine_rows = pl.ds(pl.multiple_of(c * half, 16), half)
        other_rows = pl.ds(pl.multiple_of((1 - c) * half, 16), half)

        def copy(k, src, dst, to):
            return pltpu.make_async_remote_copy(src_ref=src, dst_ref=dst, send_sem=send_sems.at[k],
                                                recv_sem=recv_sems.at[k], device_id=to, device_id_type=MESH)

        first = [copy(j, in_ref.at[mine_rows], out_ref.at[me, mine_rows], (cx, cy, c)) for j, (cx, cy) in enumerate(chips)]
        for cp in first:
            cp.start()
        passed = []
        for j, (cx, cy) in enumerate(chips):
            slot = out_ref.at[2 * cx + cy, mine_rows]
            copy(j, slot, slot, (cx, cy, c)).wait_recv()
            fwd = copy(3 + j, slot, slot, (x, y, 1 - c))
            fwd.start()
            passed.append(fwd)
        for j, (cx, cy) in enumerate(chips):
            slot = out_ref.at[2 * cx + cy, other_rows]
            copy(3 + j, slot, slot, (x, y, 1 - c)).wait_recv()
        for cp in first + passed:
            cp.wait_send()

    return _pcall(
        body, name="allgather_chips", out_shape=jax.ShapeDtypeStruct((4, nr, w), buf.dtype),
        in_specs=[_ANY], out_specs=_ANY,
        scratch_shapes=[pltpu.SemaphoreType.DMA((6,)), pltpu.SemaphoreType.DMA((6,))],
    )(buf)


def _rs_pair_exchange(g):
    _, nr, w = g.shape
    half = nr // 2

    def body(g_ref, rb_ref, send_sem, recv_sem):
        x, y, c = _coords()
        other_rows = pl.ds(pl.multiple_of((1 - c) * half, SUB), half)
        cp = pltpu.make_async_remote_copy(src_ref=g_ref.at[:, other_rows], dst_ref=rb_ref, send_sem=send_sem,
                                          recv_sem=recv_sem, device_id=(x, y, 1 - c), device_id_type=MESH)
        cp.start()
        cp.wait()

    return _pcall(body, name="rs_pair_exchange", out_shape=jax.ShapeDtypeStruct((4, half, w), g.dtype),
                  in_specs=[_ANY], out_specs=_ANY,
                  scratch_shapes=[pltpu.SemaphoreType.DMA, pltpu.SemaphoreType.DMA])(g)


def _rs_pair_add(g, rb, cidx, tb=256):
    _, nr, w = g.shape
    half = nr // 2
    tb = min(tb, half)
    assert half % tb == 0
    hb = half // tb

    def body(c_ref, g_ref, r_ref, o_ref):
        o_ref[...] = (g_ref[...] + r_ref[...]).astype(o_ref.dtype)

    gs = pltpu.PrefetchScalarGridSpec(
        num_scalar_prefetch=1, grid=(4, hb),
        in_specs=[pl.BlockSpec((1, tb, w), lambda j, i, c_ref: (j, c_ref[0] * hb + i, 0)),
                  pl.BlockSpec((1, tb, w), lambda j, i, c_ref: (j, i, 0))],
        out_specs=pl.BlockSpec((1, tb, w), lambda j, i, c_ref: (j, i, 0)))
    return _pcall(body, name="rs_pair_add", grid_spec=gs, out_shape=jax.ShapeDtypeStruct((4, half, w), BF16),
                  compiler_params=_cp("parallel", "parallel"))(cidx, g, rb)


def _rs_chip_exchange(p):
    _, h, w = p.shape

    def body(p_ref, rb_ref, send_sems, recv_sems):
        x, y, c = _coords()
        chips = [(1 - x, y), (x, 1 - y), (1 - x, 1 - y)]
        cps = [pltpu.make_async_remote_copy(src_ref=p_ref.at[2 * cx + cy], dst_ref=rb_ref.at[j], send_sem=send_sems.at[j],
                                            recv_sem=recv_sems.at[j], device_id=(cx, cy, c), device_id_type=MESH)
               for j, (cx, cy) in enumerate(chips)]
        for cp in cps:
            cp.start()
        for cp in cps:
            cp.wait()

    return _pcall(body, name="rs_chip_exchange", out_shape=jax.ShapeDtypeStruct((3, h, w), p.dtype),
                  in_specs=[_ANY], out_specs=_ANY,
                  scratch_shapes=[pltpu.SemaphoreType.DMA((3,)), pltpu.SemaphoreType.DMA((3,))])(p)


def _sum4(p, rb, chip_idx, tb=256):
    _, h, w = rb.shape
    tb = min(tb, h)
    assert h % tb == 0

    def body(m_ref, p_ref, r_ref, o_ref):
        f = lambda t: t.astype(F32)
        o_ref[...] = ((f(p_ref[0]) + f(r_ref[0])) + f(r_ref[1])) + f(r_ref[2])

    gs = pltpu.PrefetchScalarGridSpec(
        num_scalar_prefetch=1, grid=(h // tb,),
        in_specs=[pl.BlockSpec((1, tb, w), lambda i, m_ref: (m_ref[0], i, 0)),
                  pl.BlockSpec((3, tb, w), lambda i, m_ref: (0, i, 0))],
        out_specs=pl.BlockSpec((tb, w), lambda i, m_ref: (i, 0)))
    return _pcall(body, name="sum4", grid_spec=gs, out_shape=jax.ShapeDtypeStruct((h, w), F32),
                  compiler_params=_cp("parallel"))(chip_idx, p, rb)


def _pair_allgather(f):
    h, w = f.shape

    def body(f_ref, out_ref, send_sem, recv_sem):
        x, y, c = _coords()
        mine_rows = pl.ds(pl.multiple_of(c * h, SUB), h)
        other_rows = pl.ds(pl.multiple_of((1 - c) * h, SUB), h)
        send = pltpu.make_async_remote_copy(src_ref=f_ref, dst_ref=out_ref.at[mine_rows], send_sem=send_sem,
                                            recv_sem=recv_sem, device_id=(x, y, 1 - c), device_id_type=MESH)
        send.start()
        send.wait_send()
        pltpu.make_async_remote_copy(src_ref=f_ref, dst_ref=out_ref.at[other_rows], send_sem=send_sem,
                                     recv_sem=recv_sem, device_id=(x, y, 1 - c), device_id_type=MESH).wait_recv()

    return _pcall(body, name="pair_allgather", out_shape=jax.ShapeDtypeStruct((2 * h, w), f.dtype),
                  in_specs=[_ANY], out_specs=_ANY,
                  scratch_shapes=[pltpu.SemaphoreType.DMA, pltpu.SemaphoreType.DMA])(f)


def _size(shape):
    n = 1
    for d in shape:
        n *= d
    return n


PACK_ALIGN = 16


def _pack(arrs, dtype, row_mult):
    parts = []
    for a in arrs:
        flat = a.astype(dtype).reshape(-1)
        n = flat.shape[0]
        full = _ru(n, PACK_W * PACK_ALIGN)
        if full > n:
            flat = jnp.pad(flat, (0, full - n))
        parts.append(flat.reshape(-1, PACK_W))
    rows = sum(p.shape[0] for p in parts)
    if rows % row_mult:
        parts.append(jnp.zeros((_ru(rows, row_mult) - rows, PACK_W), dtype))
    return jnp.concatenate(parts, axis=0)


def _unpack(buf, shapes):
    out, off = [], 0
    for s in shapes:
        n = _size(s)
        r = _ru(-(-n // PACK_W), PACK_ALIGN)
        part = buf[off:off + r]
        out.append(part.reshape(s) if n == r * PACK_W else part.reshape(-1)[:n].reshape(s))
        off += r
    return out


def _pack_flat(arrs, dtype, row_mult):
    flat = jnp.concatenate([a.astype(dtype).reshape(-1) for a in arrs])
    n = flat.shape[0]
    rows = _ru(-(-n // PACK_W), row_mult)
    return jnp.pad(flat, (0, rows * PACK_W - n)).reshape(rows, PACK_W)


def _unpack_flat(buf, shapes):
    flat = buf.reshape(-1)
    out, off = [], 0
    for s in shapes:
        n = _size(s)
        out.append(flat[off:off + n].reshape(s))
        off += n
    return out


def _in_layout(D, GW, GH, FW, FH, MW, tn):
    o_z = 3 * GW
    o_beta = 4 * GW
    o_fq = o_beta + 2 * GH
    o_ff = o_fq + 3 * FW
    o_mq = o_ff + FH
    o_g = o_mq + MW
    orig = {"q": (0, GW), "k": (GW, GW), "v": (2 * GW, GW), "z": (o_z, GW), "beta": (o_beta, GH), "dec": (o_beta + GH, GH),
            "fq": (o_fq, FW), "fk": (o_fq + FW, FW), "fv": (o_fq + 2 * FW, FW), "ff": (o_ff, FH), "mq": (o_mq, MW),
            "ga": (o_g, D), "gb": (o_g + D, D), "gm": (o_g + 2 * D, D)}
    offs, cur = {}, 0
    for key, width in (("ga", D), ("gb", D), ("gm", D), ("q", GW), ("k", GW), ("v", GW), ("z", GW),
                       ("fq", FW), ("fk", FW), ("fv", FW), ("mq", MW), ("small", LANES)):
        cur = _ru(cur, width)
        offs[key] = cur
        cur += width
    total = _ru(cur, tn)
    pieces = [(offs[k], orig[k][0], orig[k][1]) for k in ("ga", "gb", "gm", "q", "k", "v", "z", "fq", "fk", "fv", "mq")]
    pieces += [(offs["small"], orig["beta"][0], GH), (offs["small"] + GH, orig["dec"][0], GH),
               (offs["small"] + 2 * GH, orig["ff"][0], FH)]
    return offs, total, pieces, o_g + 3 * D


def _pad_cols(w, pieces, total):
    parts, cur = [], 0
    for pstart, ostart, n in pieces:
        if pstart > cur:
            parts.append(jnp.zeros((w.shape[0], pstart - cur), w.dtype))
        parts.append(w[:, ostart:ostart + n])
        cur = pstart + n
    if total > cur:
        parts.append(jnp.zeros((w.shape[0], total - cur), w.dtype))
    return jnp.concatenate(parts, axis=1)


def _unpad_cols(wp, pieces):
    return jnp.concatenate([wp[:, pstart:pstart + n] for pstart, ostart, n in sorted(pieces, key=lambda t: t[1])], axis=1)


def _local_step(x, mem, tgt, W, flash_blk=512):
    T, D = x.shape
    GW = W["w_up_gdn"].shape[0]
    FW = W["w_up_fox"].shape[0]
    MW = W["w_up_mem"].shape[0]
    GH, FH = GW // HEAD, FW // HEAD
    offs, NP, pieces, d_in = _in_layout(D, GW, GH, FW, FH, MW, 1024)
    assert W["w_in"].shape[1] == d_in
    w_in_p = _pad_cols(W["w_in"], pieces, NP)
    cw = W["conv_w"]
    cws = [cw[:, i * GW:(i + 1) * GW] for i in range(3)]
    zl = jnp.zeros((1, LANES), F32)
    pvecs = [lax.dynamic_update_slice(zl, W["a_log"], (0, GH)), lax.dynamic_update_slice(zl, W["dt_bias"], (0, GH)),
             lax.dynamic_update_slice(zl, W["fox_b_f"], (0, 2 * GH))]
    lane0 = 2 * GH

    h = _rms_fwd(x, W["g_mix"], "rms_mix")
    proj = _mm(h, w_in_p, "nn", "in_proj")
    qn, kn, vc, cum, gbm = _gdn_prep(proj, offs, cws, pvecs, GH, FH, GW)
    o_gdn, sall = _gdn_scan_fwd(qn, kn, vc, gbm, GH)
    o_a = _gdn_post(o_gdn, proj, offs["z"], W["gdn_norm_g"], GW)
    qb, kb, vb = _fox_prep(proj, offs, W["fox_q_norm"], W["fox_k_norm"], FW)
    cumh = cum[:, lane0:lane0 + FH].T
    cumc, cumr = cumh.reshape(FH, T, 1), cumh.reshape(FH, 1, T)
    o_b16, lse = _flash_fwd(qb, kb, vb, cumc, cumr, lane0, FH, flash_blk)
    memn = _rms_fwd(mem, W["g_mem"], "rms_mem")
    kv = _mm(memn, W["w_mem_kv"], "nn", "mem_kv")
    o_m = _mem_attn(proj, offs["mq"], kv, W["mem_q_norm"], W["mem_k_norm"], MW)
    ua = _mm(o_a, W["w_up_gdn"], "nn", "up_gdn")
    ub = _mm(o_b16, W["w_up_fox"], "nn", "up_fox")
    um = _mm(o_m, W["w_up_mem"], "nn", "up_mem")
    y = _merge(proj, offs, ua, ub, um, D)
    x1 = _mm(y, W["w_out"], "nn", "out_proj", epilogue=lambda acc, r: (acc + r,), extras=(x,))
    h2 = _rms_fwd(x1, W["g_mlp"], "rms_mlp")
    u, a = _mm(h2, W["w_ff1"], "nn", "ff1", out_dtypes=(F32, BF16),
               epilogue=lambda acc: (acc, jnp.square(jnp.maximum(acc, 0.0))))
    x2 = _mm(a, W["w_ff2"], "nn", "ff2", epilogue=lambda acc, r: (acc + r,), extras=(x1,))
    dx2, dx2b, lpart = _loss_grad(x2, tgt)
    loss = lpart[0, 0]

    G = {}
    du = _mm(dx2b, W["w_ff2"], "nt", "ff2_dx", out_dtypes=(BF16,),
             epilogue=lambda acc, uu: (acc * (2.0 * jnp.maximum(uu, 0.0)),), extras=(u,))
    G["w_ff2"] = _mm(a, dx2b, "tn", "ff2_dw")
    dh2 = _mm(du, W["w_ff1"], "nt", "ff1_dx")
    G["w_ff1"] = _mm(h2, du, "tn", "ff1_dw")
    dx1, dx1b, G["g_mlp"] = _rms_bwd(x1, W["g_mlp"], dh2, dx2, "rms_mlp_bwd")
    dy = _mm(dx1b, W["w_out"], "nt", "out_dx")
    G["w_out"] = _mm(y, dx1b, "tn", "out_dw")
    dga, dgb, dgm, dua, dub, dum = _merge_bwd(proj, offs, ua, ub, um, dy, D)
    do_a = _mm(dua, W["w_up_gdn"], "nt", "up_gdn_dx")
    G["w_up_gdn"] = _mm(o_a, dua, "tn", "up_gdn_dw")
    do_b = _mm(dub, W["w_up_fox"], "nt", "up_fox_dx")
    G["w_up_fox"] = _mm(o_b16, dub, "tn", "up_fox_dw")
    do_m = _mm(dum, W["w_up_mem"], "nt", "up_mem_dx")
    G["w_up_mem"] = _mm(o_m, dum, "tn", "up_mem_dw")
    dmq, dkv, G["mem_q_norm"], G["mem_k_norm"] = _mem_attn_bwd(proj, offs["mq"], kv, W["mem_q_norm"], W["mem_k_norm"], do_m, MW)
    dkvb = dkv.astype(BF16)
    dmemn = _mm(dkvb, W["w_mem_kv"], "nt", "mem_kv_dx")
    G["w_mem_kv"] = _mm(memn, dkvb, "tn", "mem_kv_dw")
    G["g_mem"] = _rms_dg(mem, W["g_mem"], dmemn, "rms_mem_bwd")
    dl = _flash_bwd_q(qb, kb, vb, cumc, cumr, lse, do_b, None, lane0, FH, flash_blk)
    dqb, dcq = _flash_bwd_q(qb, kb, vb, cumc, cumr, lse, do_b, dl, lane0, FH, flash_blk)
    dkb, dvb, dck = _flash_bwd_dkv(qb, kb, vb, cumc, cumr, lse.reshape(FH, 1, T), dl.reshape(FH, 1, T), do_b, lane0, FH, flash_blk)
    dlf = _rev_cumsum_rows(dcq.reshape(FH, T), dck.reshape(FH, T))
    dlf_sm = jnp.pad(dlf.T, ((0, 0), (lane0, LANES - lane0 - FH)))
    dfq, dfk, dfv, G["fox_q_norm"], G["fox_k_norm"] = _fox_prep_bwd(proj, offs, W["fox_q_norm"], W["fox_k_norm"], dqb, dkb, dvb, FW)
    do_gdn, dz, G["gdn_norm_g"] = _gdn_post_bwd(o_gdn, proj, offs["z"], W["gdn_norm_g"], do_a, GW)
    dqn, dkn, dvc, dgsm = _gdn_scan_bwd(qn, kn, vc, gbm, sall, do_gdn, GH)
    dyq, dyk, dyv, dcq, dck_w, dcv, dsmall, dalog, ddtb, dbf = _gdn_prep_bwd_a(
        proj, offs, cws, pvecs, (dqn, dkn, dvc), dgsm, dlf_sm, GH, FH, GW)
    dxq, dxk, dxv = _gdn_prep_bwd_b((dyq, dyk, dyv), cws, GW)
    G["conv_w"] = jnp.concatenate([dcq[:CONV_K], dck_w[:CONV_K], dcv[:CONV_K]], axis=1)
    G["a_log"] = dalog[:, GH:2 * GH]
    G["dt_bias"] = ddtb[:, GH:2 * GH]
    G["fox_b_f"] = dbf[:, lane0:lane0 + FH]
    segs = {"ga": dga, "gb": dgb, "gm": dgm, "q": dxq, "k": dxk, "v": dxv, "z": dz, "fq": dfq, "fk": dfk, "fv": dfv,
            "mq": dmq, "small": dsmall}
    parts, cur = [], 0
    for key in ("ga", "gb", "gm", "q", "k", "v", "z", "fq", "fk", "fv", "mq", "small"):
        if offs[key] > cur:
            parts.append(jnp.zeros((T, offs[key] - cur), BF16))
        parts.append(segs[key])
        cur = offs[key] + segs[key].shape[1]
    if NP > cur:
        parts.append(jnp.zeros((T, NP - cur), BF16))
    dproj = jnp.concatenate(parts, axis=1)
    dh = _mm(dproj, w_in_p, "nt", "in_dx")
    G["w_in"] = _unpad_cols(_mm(h, dproj, "tn", "in_dw"), pieces)
    grad_x, _, G["g_mix"] = _rms_bwd(x, W["g_mix"], dh, dx1, "rms_mix_bwd")
    return loss, grad_x, G


BIG = ["w_in", "w_mem_kv", "w_up_gdn", "w_up_fox", "w_up_mem", "w_out", "w_ff1", "w_ff2"]
SMALL = ["g_mix", "a_log", "dt_bias", "gdn_norm_g", "fox_b_f", "fox_q_norm", "fox_k_norm", "g_mem", "mem_q_norm",
         "mem_k_norm", "g_mlp"]
ORDER = ["g_mix", "w_in", "conv_w", "a_log", "dt_bias", "gdn_norm_g", "fox_b_f", "fox_q_norm", "fox_k_norm", "g_mem",
         "w_mem_kv", "mem_q_norm", "mem_k_norm", "w_up_gdn", "w_up_fox", "w_up_mem", "w_out", "g_mlp", "w_ff1", "w_ff2"]
SHARD_AXIS = {"w_in": 1, "w_mem_kv": 0, "w_up_gdn": 1, "w_up_fox": 1, "w_up_mem": 1, "w_out": 0, "w_ff1": 1, "w_ff2": 0}


def _step(x, mem, tgt, w, m, v, flash_blk=512):
    xi, yi, ci = _coords()
    chip = 2 * xi + yi

    shard_shapes = [w[n].shape for n in BIG]
    packed_w = _pack([w[n] for n in BIG], BF16, PACK_ROWS)
    gathered = _allgather_chips(packed_w)
    gathered = lax.dynamic_update_slice(gathered, packed_w[None], (chip, 0, 0))
    per_chip = [_unpack(gathered[j], shard_shapes) for j in range(4)]
    W = {n: jnp.concatenate([per_chip[j][i] for j in range(4)], axis=SHARD_AXIS[n]) for i, n in enumerate(BIG)}
    cw_rows = jnp.pad(w["conv_w"], ((0, SUB - CONV_K), (0, 0)))
    cw_all = _allgather_small(cw_rows)
    W["conv_w"] = jnp.concatenate([cw_all[16 * j:16 * j + CONV_K] for j in range(4)], axis=1)
    for n in SMALL:
        W[n] = w[n]

    loss, grad_x, G = _local_step(x, mem, tgt, W, flash_blk)
    loss = lax.psum(loss, ("x", "y", "c"))

    small_shapes = [G[n].shape for n in SMALL] + [G["conv_w"].shape]
    sm = _pack_flat([G[n] for n in SMALL] + [G["conv_w"]], F32, SUB)
    sm_sum = _sum8(_allgather_small(sm), sm.shape[0])
    sm_list = _unpack_flat(sm_sum, small_shapes)
    g = {n: sm_list[i] for i, n in enumerate(SMALL)}
    cw_full = sm_list[-1]
    gw4 = cw_full.shape[1] // 4
    g["conv_w"] = lax.dynamic_slice(cw_full, (0, chip * gw4), (CONV_K, gw4))

    by_dest = []
    for j in range(4):
        shards = []
        for n in BIG:
            size = w[n].shape[SHARD_AXIS[n]]
            shards.append(lax.slice_in_dim(G[n], j * size, (j + 1) * size, axis=SHARD_AXIS[n]))
        by_dest.append(_pack(shards, F32, PACK_ROWS))
    gflat = jnp.stack(by_dest)
    rb1 = _rs_pair_exchange(gflat)
    part = _rs_pair_add(gflat, rb1, jnp.reshape(ci, (1,)).astype(jnp.int32))
    rb2 = _rs_chip_exchange(part)
    half_sum = _sum4(part, rb2, jnp.reshape(chip, (1,)).astype(jnp.int32))
    mine = _pair_allgather(half_sum)
    mine = lax.dynamic_update_slice(mine, half_sum, (ci * half_sum.shape[0], 0))
    for i, gv in enumerate(_unpack(mine, shard_shapes)):
        g[BIG[i]] = gv

    delta, new_m, new_v = {}, {}, {}
    for n in BIG:
        delta[n], new_m[n], new_v[n] = _adamw(w[n], g[n], m[n], v[n], "adamw_" + n)
    rest = SMALL + ["conv_w"]
    rest_shapes = [w[n].shape for n in rest]
    packed = [_pack_flat([d[n] for n in rest], F32, SUB) for d in (w, g, m, v)]
    outs = _adamw(*packed, "adamw_small", tb=packed[0].shape[0])
    for d, buf in zip((delta, new_m, new_v), outs):
        for n, val in zip(rest, _unpack_flat(buf, rest_shapes)):
            d[n] = val
    return loss, grad_x, g, delta, new_m, new_v


def kernel(x, mem, g_mix, w_in, conv_w, a_log, dt_bias, gdn_norm_g, fox_b_f, fox_q_norm, fox_k_norm, g_mem, w_mem_kv, mem_q_norm, mem_k_norm, w_up_gdn, w_up_fox, w_up_mem, w_out, g_mlp, w_ff1, w_ff2, loss_target, m_g_mix, m_w_in, m_conv_w, m_a_log, m_dt_bias, m_gdn_norm_g, m_fox_b_f, m_fox_q_norm, m_fox_k_norm, m_g_mem, m_w_mem_kv, m_mem_q_norm, m_mem_k_norm, m_w_up_gdn, m_w_up_fox, m_w_up_mem, m_w_out, m_g_mlp, m_w_ff1, m_w_ff2, v_g_mix, v_w_in, v_conv_w, v_a_log, v_dt_bias, v_gdn_norm_g, v_fox_b_f, v_fox_q_norm, v_fox_k_norm, v_g_mem, v_w_mem_kv, v_mem_q_norm, v_mem_k_norm, v_w_up_gdn, v_w_up_fox, v_w_up_mem, v_w_out, v_g_mlp, v_w_ff1, v_w_ff2):
    ws = (g_mix, w_in, conv_w, a_log, dt_bias, gdn_norm_g, fox_b_f, fox_q_norm, fox_k_norm, g_mem, w_mem_kv, mem_q_norm,
          mem_k_norm, w_up_gdn, w_up_fox, w_up_mem, w_out, g_mlp, w_ff1, w_ff2)
    ms = (m_g_mix, m_w_in, m_conv_w, m_a_log, m_dt_bias, m_gdn_norm_g, m_fox_b_f, m_fox_q_norm, m_fox_k_norm, m_g_mem,
          m_w_mem_kv, m_mem_q_norm, m_mem_k_norm, m_w_up_gdn, m_w_up_fox, m_w_up_mem, m_w_out, m_g_mlp, m_w_ff1, m_w_ff2)
    vs = (v_g_mix, v_w_in, v_conv_w, v_a_log, v_dt_bias, v_gdn_norm_g, v_fox_b_f, v_fox_q_norm, v_fox_k_norm, v_g_mem,
          v_w_mem_kv, v_mem_q_norm, v_mem_k_norm, v_w_up_gdn, v_w_up_fox, v_w_up_mem, v_w_out, v_g_mlp, v_w_ff1, v_w_ff2)
    drop = lambda a: a[0] if a.ndim == 3 else a
    w = {n: drop(a) for n, a in zip(ORDER, ws)}
    m = {n: drop(a) for n, a in zip(ORDER, ms)}
    v = {n: drop(a) for n, a in zip(ORDER, vs)}
    loss, grad_x, g, delta, new_m, new_v = _step(x[0], mem[0], loss_target[0], w, m, v)
    out = [loss, grad_x[None]]
    for d in (g, delta, new_m, new_v):
        out += [d[n].reshape(a.shape) for n, a in zip(ORDER, ws)]
    return tuple(out)
```

```python
import numpy as np

import jax
import jax.numpy as jnp
from jax import lax
from jax.experimental import pallas as pl
from jax.experimental.pallas import tpu as pltpu

F32 = jnp.float32
BF16 = jnp.bfloat16
HI = lax.Precision.HIGHEST
MESH = pl.DeviceIdType.MESH

EPS = 1e-6
HEAD = 128
MEM_DH = 256
CONV_K = 4
CHUNK = 64
CHUNK_SHIFT = 6
LANES = 128
SUB = 8
PACK_W = 1024
PACK_ROWS = 512
VMEM_LIMIT = 56 * 1024 * 1024
NEG = -1e30
SOLVE_PREC = None
FLASH_HEADS_PER_STEP = 4

ADAM_LR, ADAM_B1, ADAM_B2, ADAM_EPS, ADAM_WD, ADAM_STEP = 0.001, 0.9, 0.999, 1e-08, 0.01, 10


def _pcall(body, **kw):
    return pl.pallas_call(body, **kw)


def _cp(*sem):
    return pltpu.CompilerParams(dimension_semantics=sem, vmem_limit_bytes=VMEM_LIMIT)


def _dot(a, b, prec=None):
    return lax.dot_general(a, b, (((1,), (0,)), ((), ())), precision=prec, preferred_element_type=F32)


def _dot_nt(a, b, prec=None):
    return lax.dot_general(a, b, (((1,), (1,)), ((), ())), precision=prec, preferred_element_type=F32)


def _dot_tn(a, b, prec=None):
    return lax.dot_general(a, b, (((0,), (0,)), ((), ())), precision=prec, preferred_element_type=F32)


def _sigmoid(x):
    return 1.0 / (1.0 + jnp.exp(-x))


def _softplus(x):
    return jnp.maximum(x, 0.0) + jnp.log(1.0 + jnp.exp(-jnp.abs(x)))


def _silu(x):
    return x * _sigmoid(x)


def _rms(x, g):
    return x * lax.rsqrt(jnp.mean(x * x, axis=-1, keepdims=True) + EPS) * g


def _ru(a, m):
    return (a + m - 1) // m * m


def _mm(a, b, mode, name, out_dtypes=(F32,), epilogue=None, extras=(), tm=1024, tn=1024, tk=2048):
    if mode == "nn":
        (M, K), (K2, N) = a.shape, b.shape
    elif mode == "nt":
        (M, K), (N, K2) = a.shape, b.shape
    else:
        (K, M), (K2, N) = a.shape, b.shape
    assert K == K2, (a.shape, b.shape, mode)
    tm, tn = min(tm, M), min(tn, N)
    tk = next((t for t in (tk, 1024, 512, 256, LANES) if t <= K and K % t == 0), K)
    assert M % tm == 0 and N % tn == 0 and K % tk == 0, (M, N, K, tm, tn, tk)
    nk = K // tk
    n_ex, n_out = len(extras), len(out_dtypes)
    dims = {"nn": ((1,), (0,)), "nt": ((1,), (1,)), "tn": ((0,), (0,))}[mode]

    def finish(res, ex_refs, out_refs):
        outs = epilogue(res, *[r[...] for r in ex_refs]) if epilogue is not None else (res,)
        for o_ref, o in zip(out_refs, outs):
            o_ref[...] = o.astype(o_ref.dtype)

    def body(a_ref, b_ref, *rest):
        ex_refs, out_refs = rest[:n_ex], rest[n_ex:n_ex + n_out]
        part = lax.dot_general(a_ref[...], b_ref[...], (dims, ((), ())), preferred_element_type=F32)
        if nk == 1:
            finish(part, ex_refs, out_refs)
            return
        acc = rest[-1]
        k = pl.program_id(2)

        @pl.when(k == 0)
        def _():
            acc[...] = part

        @pl.when(k > 0)
        def _():
            acc[...] += part

        @pl.when(k == nk - 1)
        def _():
            finish(acc[...], ex_refs, out_refs)

    a_spec = pl.BlockSpec((tk, tm), lambda i, j, k: (k, i)) if mode == "tn" else pl.BlockSpec((tm, tk), lambda i, j, k: (i, k))
    b_spec = pl.BlockSpec((tn, tk), lambda i, j, k: (j, k)) if mode == "nt" else pl.BlockSpec((tk, tn), lambda i, j, k: (k, j))
    mn_spec = pl.BlockSpec((tm, tn), lambda i, j, k: (i, j))
    outs = _pcall(
        body, name=name, grid=(M // tm, N // tn, nk),
        in_specs=[a_spec, b_spec] + [mn_spec] * n_ex,
        out_specs=[mn_spec] * n_out,
        out_shape=[jax.ShapeDtypeStruct((M, N), dt) for dt in out_dtypes],
        scratch_shapes=[pltpu.VMEM((tm, tn), F32)] if nk > 1 else [],
        compiler_params=_cp("parallel", "parallel", "arbitrary"),
    )(a, b, *extras)
    return outs[0] if n_out == 1 else outs


def _rowwise(fn, T, tb, ins, outs, name, scratch=()):
    tb = min(tb, T)
    assert T % tb == 0 and (tb % SUB == 0 or tb == T)
    nblk = T // tb
    r8 = tb // SUB
    in_specs, arrs = [], []
    for spec in ins:
        kind, arr = spec[0], spec[1]
        arrs.append(arr)
        if kind == "full":
            nd = arr.ndim
            in_specs.append(pl.BlockSpec(arr.shape, lambda i, nd=nd: (0,) * nd))
            continue
        off, w = spec[2], spec[3]
        assert off % w == 0 and arr.shape[0] == T, (name, off, w, arr.shape)
        cb = off // w
        if kind == "row":
            in_specs.append(pl.BlockSpec((tb, w), lambda i, cb=cb: (i, cb)))
        elif kind == "prev":
            in_specs.append(pl.BlockSpec((SUB, w), lambda i, cb=cb: (jnp.maximum(i * r8 - 1, 0), cb)))
        else:
            in_specs.append(pl.BlockSpec((SUB, w), lambda i, cb=cb: (jnp.minimum((i + 1) * r8, T // SUB - 1), cb)))
    out_specs, out_shapes, is_acc = [], [], []
    for spec in outs:
        if spec[0] == "row":
            out_specs.append(pl.BlockSpec((tb, spec[1]), lambda i: (i, 0)))
            out_shapes.append(jax.ShapeDtypeStruct((T, spec[1]), spec[2]))
            is_acc.append(False)
        else:
            nd = len(spec[1])
            out_specs.append(pl.BlockSpec(spec[1], lambda i, nd=nd: (0,) * nd))
            out_shapes.append(jax.ShapeDtypeStruct(spec[1], spec[2]))
            is_acc.append(True)
    n_in, n_out = len(ins), len(outs)
    seq = any(is_acc) or len(scratch) > 0

    def body(*refs):
        in_refs, out_refs, scr = refs[:n_in], refs[n_in:n_in + n_out], refs[n_in + n_out:]
        i = pl.program_id(0)
        vals = fn(i, nblk, *[r[...] for r in in_refs], *scr)
        for o_ref, v, acc in zip(out_refs, vals, is_acc):
            if acc:
                @pl.when(i == 0)
                def _(o_ref=o_ref):
                    o_ref[...] = jnp.zeros_like(o_ref)

                o_ref[...] += v.astype(o_ref.dtype)
            else:
                o_ref[...] = v.astype(o_ref.dtype)

    res = _pcall(
        body, name=name, grid=(nblk,), in_specs=in_specs, out_specs=out_specs, out_shape=out_shapes,
        scratch_shapes=list(scratch), compiler_params=_cp("arbitrary" if seq else "parallel"),
    )(*arrs)
    return res


def _heads(x, width):
    return [x[:, h * width:(h + 1) * width] for h in range(x.shape[1] // width)]


def _cat(xs):
    return xs[0] if len(xs) == 1 else jnp.concatenate(xs, axis=1)


def _rms_fwd(x, g, name, tb=512):
    T, D = x.shape
    return _rowwise(lambda i, n, xv, gv: (_rms(xv, gv),), T, tb,
                    [("row", x, 0, D), ("full", g)], [("row", D, BF16)], name)[0]


def _rms_bwd(x, g, dh, dres, name, tb=256):
    T, D = x.shape

    def fn(i, n, xv, gv, dhv, drv):
        _, vjp = jax.vjp(_rms, xv, gv)
        dx, dg = vjp(dhv)
        tot = drv + dx
        return tot, tot, dg

    return _rowwise(fn, T, tb, [("row", x, 0, D), ("full", g), ("row", dh, 0, D), ("row", dres, 0, D)],
                    [("row", D, F32), ("row", D, BF16), ("acc", (1, D), F32)], name)


def _rms_dg(x, g, dh, name, tb=256):
    T, D = x.shape

    def fn(i, n, xv, gv, dhv):
        _, vjp = jax.vjp(lambda gg: _rms(xv, gg), gv)
        return vjp(dhv)

    return _rowwise(fn, T, tb, [("row", x, 0, D), ("full", g), ("row", dh, 0, D)], [("acc", (1, D), F32)], name)[0]


def _shift_down(x, halo, s, first):
    if s == 0:
        return x
    tb, c = x.shape
    xr = pltpu.roll(x, s, 0)
    hr = jnp.where(first, 0.0, pltpu.roll(halo, s, 0))
    hfull = hr if tb == SUB else jnp.concatenate([hr, jnp.zeros((tb - SUB, c), x.dtype)], axis=0)
    row = lax.broadcasted_iota(jnp.int32, x.shape, 0)
    return jnp.where(row < s, hfull, xr)


def _shift_up(z, halo, s, last):
    if s == 0:
        return z
    tb, c = z.shape
    zr = pltpu.roll(z, tb - s, 0)
    hr = jnp.where(last, 0.0, pltpu.roll(halo, SUB - s, 0))
    hfull = hr if tb == SUB else jnp.concatenate([jnp.zeros((tb - SUB, c), z.dtype), hr], axis=0)
    row = lax.broadcasted_iota(jnp.int32, z.shape, 0)
    return jnp.where(row >= tb - s, hfull, zr)


def _conv_pre(x, halo, cw, first):
    xs = [_shift_down(x, halo, s, first) for s in range(CONV_K)]
    y = cw[0:1, :] * xs[3]
    for i in range(1, CONV_K):
        y = y + cw[i:i + 1, :] * xs[CONV_K - 1 - i]
    return y, xs


def _qk_post(y, scale):
    a = _silu(y)
    return a * lax.rsqrt(jnp.sum(a * a, axis=-1, keepdims=True) + EPS) * scale


def _small_fn(s, alog, dtb, bf, gh, fh):
    lane = lax.broadcasted_iota(jnp.int32, s.shape, 1)
    beta = _sigmoid(s)
    g = -jnp.exp(alog) * _softplus(s + dtb)
    lf = -_softplus(-(s + bf))
    return jnp.where(lane < gh, beta, jnp.where(lane < 2 * gh, g, jnp.where(lane < 2 * gh + fh, lf, 0.0)))


def _gdn_prep(proj, offs, cws, pvecs, GH, FH, GW, tb=256):
    T = proj.shape[0]
    tb = min(tb, T)
    qscale = HEAD ** -0.5

    def fn(i, n, xq, hq, xk, hk, xv, hv, cwq, cwk, cwv, s, alog, dtb, bf, carry):
        first = i == 0
        yq, _ = _conv_pre(xq, hq, cwq, first)
        yk, _ = _conv_pre(xk, hk, cwk, first)
        yv, _ = _conv_pre(xv, hv, cwv, first)
        qn = _cat([_qk_post(y, qscale) for y in _heads(yq, HEAD)])
        kn = _cat([_qk_post(y, 1.0) for y in _heads(yk, HEAD)])
        vc = _silu(yv)
        gsm = _small_fn(s, alog, dtb, bf, GH, FH)

        @pl.when(first)
        def _():
            carry[...] = jnp.zeros_like(carry)

        ri = lax.broadcasted_iota(jnp.int32, (tb, tb), 0)
        ci = lax.broadcasted_iota(jnp.int32, (tb, tb), 1)
        cum = _dot((ri >= ci).astype(F32), gsm, HI) + carry[0:1, :]
        carry[...] += _dot(jnp.ones((SUB, tb), F32), gsm, HI)
        in_chunk = (ri >= ci) & ((ri >> CHUNK_SHIFT) == (ci >> CHUNK_SHIFT))
        lane = lax.broadcasted_iota(jnp.int32, gsm.shape, 1)
        gbm = jnp.where(lane < GH, gsm, _dot(in_chunk.astype(F32), gsm, HI))
        return qn, kn, vc, cum, gbm

    ins = []
    for key in ("q", "k", "v"):
        ins += [("row", proj, offs[key], GW), ("prev", proj, offs[key], GW)]
    ins += [("full", c) for c in cws] + [("row", proj, offs["small"], LANES)] + [("full", p) for p in pvecs]
    outs = [("row", GW, F32)] * 3 + [("row", LANES, F32)] * 2
    return _rowwise(fn, T, tb, ins, outs, "gdn_prep", scratch=[pltpu.VMEM((SUB, LANES), F32)])


def _gdn_prep_bwd_a(proj, offs, cws, pvecs, cts, dgsm_scan, dlf_sm, GH, FH, GW, tb=256):
    T = proj.shape[0]
    qscale = HEAD ** -0.5

    def one(x, halo, cw, ct, first, post):
        y, xs = _conv_pre(x, halo, cw, first)
        if post is None:
            _, vjp = jax.vjp(_silu, y)
            dy = vjp(ct)[0]
        else:
            dys = []
            for yh, cth in zip(_heads(y, HEAD), _heads(ct, HEAD)):
                _, vjp = jax.vjp(lambda t: _qk_post(t, post), yh)
                dys.append(vjp(cth)[0])
            dy = _cat(dys)
        row = lax.broadcasted_iota(jnp.int32, (SUB, x.shape[1]), 0)
        dcw = jnp.zeros((SUB, x.shape[1]), F32)
        for i in range(CONV_K):
            dcw = dcw + jnp.where(row == i, jnp.sum(dy * xs[CONV_K - 1 - i], axis=0, keepdims=True), 0.0)
        return dy, dcw

    def fn(i, n, xq, hq, xk, hk, xv, hv, cwq, cwk, cwv, cq, ck, cv, s, alog, dtb, bf, d1, d2):
        first = i == 0
        dyq, dcq = one(xq, hq, cwq, cq, first, qscale)
        dyk, dck = one(xk, hk, cwk, ck, first, 1.0)
        dyv, dcv = one(xv, hv, cwv, cv, first, None)
        tb_ = d1.shape[0]
        ri = lax.broadcasted_iota(jnp.int32, (tb_, tb_), 0)
        ci = lax.broadcasted_iota(jnp.int32, (tb_, tb_), 1)
        later = (ci >= ri) & ((ri >> CHUNK_SHIFT) == (ci >> CHUNK_SHIFT))
        lane = lax.broadcasted_iota(jnp.int32, d1.shape, 1)
        d1 = jnp.where(lane < GH, d1, _dot(later.astype(F32), d1, HI))
        _, vjp = jax.vjp(lambda a, b, c, d: _small_fn(a, b, c, d, GH, FH), s, alog, dtb, bf)
        ds, dalog, ddtb, dbf = vjp(d1 + d2)
        return dyq, dyk, dyv, dcq, dck, dcv, ds, dalog, ddtb, dbf

    ins = []
    for key in ("q", "k", "v"):
        ins += [("row", proj, offs[key], GW), ("prev", proj, offs[key], GW)]
    ins += [("full", c) for c in cws] + [("row", c, 0, GW) for c in cts]
    ins += [("row", proj, offs["small"], LANES)] + [("full", p) for p in pvecs]
    ins += [("row", dgsm_scan, 0, LANES), ("row", dlf_sm, 0, LANES)]
    outs = [("row", GW, F32)] * 3 + [("acc", (SUB, GW), F32)] * 3 + [("row", LANES, BF16)] + [("acc", (1, LANES), F32)] * 3
    return _rowwise(fn, T, tb, ins, outs, "gdn_prep_bwd_a")


def _gdn_prep_bwd_b(dys, cws, GW, tb=256):
    T = dys[0].shape[0]

    def fn(i, n, dq, nq, dk, nk, dv, nv, cwq, cwk, cwv):
        last = i == n - 1
        res = []
        for dy, nh, cw in ((dq, nq, cwq), (dk, nk, cwk), (dv, nv, cwv)):
            dx = cw[CONV_K - 1:CONV_K, :] * dy
            for t in range(CONV_K - 1):
                dx = dx + cw[t:t + 1, :] * _shift_up(dy, nh, CONV_K - 1 - t, last)
            res.append(dx)
        return tuple(res)

    ins = []
    for dy in dys:
        ins += [("row", dy, 0, GW), ("next", dy, 0, GW)]
    ins += [("full", c) for c in cws]
    return _rowwise(fn, T, tb, ins, [("row", GW, BF16)] * 3, "gdn_prep_bwd_b")


def _gdn_chunks(qs, ks, vs, gams, bcols, s0s):
    c, d = qs[0].shape
    hs = range(len(qs))
    ri = lax.broadcasted_iota(jnp.int32, (c, c), 0)
    ci = lax.broadcasted_iota(jnp.int32, (c, c), 1)
    incl, strict = ri >= ci, ri > ci
    eye = (ri == ci).astype(F32)
    ones_cc = jnp.ones((c, c), F32)
    rows = lax.broadcasted_iota(jnp.int32, (c, 1), 0)
    b16 = (ri >> 4) == (ci >> 4)
    b32 = (ri >> 5) == (ci >> 5)
    gam_cc = [gams[h] * ones_cc for h in hs]
    gam_t = [_dot_nt(eye, gam_cc[h], HI) for h in hs]
    glast = [jnp.sum(jnp.where(rows == c - 1, gams[h], 0.0), axis=0, keepdims=True) for h in hs]
    dec_i = [jnp.where(incl, jnp.exp(jnp.where(incl, gam_cc[h] - gam_t[h], 0.0)), 0.0) for h in hs]
    kk = [_dot_nt(ks[h], ks[h]) for h in hs]
    m = [bcols[h] * kk[h] * jnp.where(strict, dec_i[h], 0.0) for h in hs]
    m32 = [jnp.where(b32 & ~b16, m[h], 0.0) for h in hs]
    m64 = [jnp.where(b32, 0.0, m[h]) for h in hs]
    mp = [jnp.where(b16, m[h], 0.0) for h in hs]
    p = [eye - mp[h] for h in hs]
    for _ in range(3):
        mp = [_dot(mp[h], mp[h], SOLVE_PREC) for h in hs]
        p = [p[h] + _dot(p[h], mp[h], SOLVE_PREC) for h in hs]
    t = [_dot(p[h], m32[h], SOLVE_PREC) for h in hs]
    p = [p[h] - _dot(t[h], p[h], SOLVE_PREC) for h in hs]
    t = [_dot(p[h], m64[h], SOLVE_PREC) for h in hs]
    ainv = [p[h] - _dot(t[h], p[h], SOLVE_PREC) for h in hs]
    eg = [jnp.exp(gams[h]) for h in hs]
    w = [_dot(ainv[h], (bcols[h] * eg[h]) * ks[h], SOLVE_PREC) for h in hs]
    u0 = [_dot(ainv[h], bcols[h] * vs[h], SOLVE_PREC) for h in hs]
    qk = [_dot_nt(qs[h], ks[h]) * dec_i[h] for h in hs]
    u = [u0[h] - _dot(w[h], s0s[h]) for h in hs]
    o = [_dot(qs[h] * eg[h], s0s[h]) + _dot(qk[h], u[h]) for h in hs]
    s1 = [jnp.exp(glast[h]) * s0s[h] + _dot_tn(ks[h] * jnp.exp(glast[h] - gams[h]), u[h]) for h in hs]
    return tuple(o), tuple(s1)


def _lane_col(x, lane_idx):
    lane = lax.broadcasted_iota(jnp.int32, x.shape, 1)
    return jnp.sum(jnp.where(lane == lane_idx, x, 0.0), axis=1, keepdims=True)


def _gdn_scan_fwd(qn, kn, vc, gsm, GH):
    T, GW = qn.shape
    nc = T // CHUNK

    def body(q_ref, k_ref, v_ref, g_ref, o_ref, sall_ref, s_scr):
        @pl.when(pl.program_id(0) == 0)
        def _():
            s_scr[...] = jnp.zeros_like(s_scr)

        gs = g_ref[...]
        sls = [slice(h * HEAD, (h + 1) * HEAD) for h in range(GH)]
        s0s = tuple(s_scr[h] for h in range(GH))
        os_, s1s = _gdn_chunks(tuple(q_ref[:, sl] for sl in sls), tuple(k_ref[:, sl] for sl in sls),
                               tuple(v_ref[:, sl] for sl in sls), tuple(_lane_col(gs, GH + h) for h in range(GH)),
                               tuple(_lane_col(gs, h) for h in range(GH)), s0s)
        for h in range(GH):
            sall_ref[0, h] = s0s[h]
            o_ref[:, sls[h]] = os_[h]
            s_scr[h] = s1s[h]

    row = pl.BlockSpec((CHUNK, GW), lambda i: (i, 0))
    return _pcall(
        body, name="gdn_scan_fwd", grid=(nc,),
        in_specs=[row, row, row, pl.BlockSpec((CHUNK, LANES), lambda i: (i, 0))],
        out_specs=[row, pl.BlockSpec((1, GH, HEAD, HEAD), lambda i: (i, 0, 0, 0))],
        out_shape=[jax.ShapeDtypeStruct((T, GW), F32), jax.ShapeDtypeStruct((nc, GH, HEAD, HEAD), F32)],
        scratch_shapes=[pltpu.VMEM((GH, HEAD, HEAD), F32)],
        compiler_params=_cp("arbitrary"),
    )(qn, kn, vc, gsm)


def _gdn_scan_bwd(qn, kn, vc, gsm, sall, do, GH):
    T, GW = qn.shape
    nc = T // CHUNK

    def body(q_ref, k_ref, v_ref, g_ref, sall_ref, do_ref, dq_ref, dk_ref, dv_ref, dg_ref, ds_scr):
        @pl.when(pl.program_id(0) == 0)
        def _():
            ds_scr[...] = jnp.zeros_like(ds_scr)

        gs = g_ref[...]
        lane = lax.broadcasted_iota(jnp.int32, gs.shape, 1)
        sls = [slice(h * HEAD, (h + 1) * HEAD) for h in range(GH)]
        _, vjp = jax.vjp(_gdn_chunks, tuple(q_ref[:, sl] for sl in sls), tuple(k_ref[:, sl] for sl in sls),
                         tuple(v_ref[:, sl] for sl in sls), tuple(_lane_col(gs, GH + h) for h in range(GH)),
                         tuple(_lane_col(gs, h) for h in range(GH)), tuple(sall_ref[0, h] for h in range(GH)))
        dq, dk, dv, dgc, dbc, ds0 = vjp((tuple(do_ref[:, sl] for sl in sls), tuple(ds_scr[h] for h in range(GH))))
        dgs = jnp.zeros_like(gs)
        for h in range(GH):
            dq_ref[:, sls[h]] = dq[h]
            dk_ref[:, sls[h]] = dk[h]
            dv_ref[:, sls[h]] = dv[h]
            dgs = dgs + jnp.where(lane == h, dbc[h], 0.0) + jnp.where(lane == GH + h, dgc[h], 0.0)
            ds_scr[h] = ds0[h]
        dg_ref[...] = dgs

    row = pl.BlockSpec((CHUNK, GW), lambda i: (nc - 1 - i, 0))
    sm = pl.BlockSpec((CHUNK, LANES), lambda i: (nc - 1 - i, 0))
    return _pcall(
        body, name="gdn_scan_bwd", grid=(nc,),
        in_specs=[row, row, row, sm, pl.BlockSpec((1, GH, HEAD, HEAD), lambda i: (nc - 1 - i, 0, 0, 0)), row],
        out_specs=[row, row, row, sm],
        out_shape=[jax.ShapeDtypeStruct((T, GW), F32)] * 3 + [jax.ShapeDtypeStruct((T, LANES), F32)],
        scratch_shapes=[pltpu.VMEM((GH, HEAD, HEAD), F32)],
        compiler_params=_cp("arbitrary"),
    )(qn, kn, vc, gsm, sall, do)


def _gdn_post_fn(o, z, g):
    return _rms(o, g) * _silu(z)


def _gdn_post(o, proj, off_z, g, GW, tb=512):
    T = o.shape[0]

    def fn(i, n, ov, zv, gv):
        return (_cat([_gdn_post_fn(a, b, gv) for a, b in zip(_heads(ov, HEAD), _heads(zv, HEAD))]),)

    return _rowwise(fn, T, tb, [("row", o, 0, GW), ("row", proj, off_z, GW), ("full", g)], [("row", GW, BF16)], "gdn_post")[0]


def _gdn_post_bwd(o, proj, off_z, g, dout, GW, tb=256):
    T = o.shape[0]

    def fn(i, n, ov, zv, gv, dv):
        dos, dzs, dg = [], [], jnp.zeros_like(gv)
        for a, b, c in zip(_heads(ov, HEAD), _heads(zv, HEAD), _heads(dv, HEAD)):
            _, vjp = jax.vjp(_gdn_post_fn, a, b, gv)
            da, db, dgh = vjp(c)
            dos.append(da)
            dzs.append(db)
            dg = dg + dgh
        return _cat(dos), _cat(dzs), dg

    return _rowwise(fn, T, tb, [("row", o, 0, GW), ("row", proj, off_z, GW), ("full", g), ("row", dout, 0, GW)],
                    [("row", GW, F32), ("row", GW, BF16), ("acc", (1, HEAD), F32)], "gdn_post_bwd")


def _fox_prep(proj, offs, gq, gk, FW, tb=512):
    T = proj.shape[0]

    def fn(i, n, q, k, v, gqv, gkv):
        return (_cat([_rms(a, gqv) * (HEAD ** -0.5) for a in _heads(q, HEAD)]),
                _cat([_rms(a, gkv) for a in _heads(k, HEAD)]), v)

    return _rowwise(fn, T, tb, [("row", proj, offs["fq"], FW), ("row", proj, offs["fk"], FW), ("row", proj, offs["fv"], FW),
                                ("full", gq), ("full", gk)], [("row", FW, BF16)] * 3, "fox_prep")


def _fox_prep_bwd(proj, offs, gq, gk, dq, dk, dv, FW, tb=256):
    T = proj.shape[0]

    def fn(i, n, q, k, gqv, gkv, dqv, dkv, dvv):
        res = []
        for x, g, d in ((q, gqv, dqv), (k, gkv, dkv)):
            dxs, dg = [], jnp.zeros_like(g)
            for a, c in zip(_heads(x, HEAD), _heads(d, HEAD)):
                _, vjp = jax.vjp(_rms, a, g)
                da, dgh = vjp(c)
                dxs.append(da)
                dg = dg + dgh
            res += [_cat(dxs), dg]
        return res[0], res[2], dvv, res[1], res[3]

    return _rowwise(fn, T, tb, [("row", proj, offs["fq"], FW), ("row", proj, offs["fk"], FW), ("full", gq), ("full", gk),
                                ("row", dq, 0, FW), ("row", dk, 0, FW), ("row", dv, 0, FW)],
                    [("row", FW, BF16)] * 3 + [("acc", (1, HEAD), F32)] * 2, "fox_prep_bwd")


def _sub_row(x, sub_idx):
    sub = lax.broadcasted_iota(jnp.int32, x.shape, 0)
    return jnp.sum(jnp.where(sub == sub_idx, x, 0.0), axis=0, keepdims=True)


def _causal_pairs(nb, key_major):
    if key_major:
        pairs = [(i, j) for j in range(nb) for i in range(j, nb)]
    else:
        pairs = [(i, j) for i in range(nb) for j in range(i + 1)]
    return (jnp.asarray(np.array([p[0] for p in pairs], np.int32)),
            jnp.asarray(np.array([p[1] for p in pairs], np.int32)))


def _fox_scores(q, k, ck, diagonal):
    s = _dot_nt(q, k) - ck
    if diagonal:
        row = lax.broadcasted_iota(jnp.int32, s.shape, 0)
        col = lax.broadcasted_iota(jnp.int32, s.shape, 1)
        s = jnp.where(row >= col, s, NEG)
    return s


def _flash_fwd(qb, kb, vb, cumr, FH, blk):
    T, FW = qb.shape
    blk = min(blk, T)
    nb = T // blk
    qi_arr, kj_arr = _causal_pairs(nb, False)

    hps = FLASH_HEADS_PER_STEP
    assert FH % hps == 0
    us = range(hps)
    sl = [slice(u * HEAD, (u + 1) * HEAD) for u in us]

    def body(qi_ref, kj_ref, q_ref, k_ref, v_ref, ck_ref, ob_ref, lse_ref, m_scr, l_scr, acc):
        t = pl.program_id(1)
        qi, kj = qi_ref[t], kj_ref[t]

        @pl.when(kj == 0)
        def _():
            m_scr[...] = jnp.full_like(m_scr, NEG)
            l_scr[...] = jnp.zeros_like(l_scr)
            acc[...] = jnp.zeros_like(acc)

        def update(diagonal):
            s = [_fox_scores(q_ref[:, sl[u]], k_ref[:, sl[u]], ck_ref[u], diagonal) for u in us]
            m_old = [m_scr[u] for u in us]
            m_new = [jnp.maximum(m_old[u], jnp.max(s[u], axis=1, keepdims=True)) for u in us]
            alpha = [jnp.exp(m_old[u] - m_new[u]) for u in us]
            p = [jnp.exp(s[u] - m_new[u]) for u in us]
            pv = [_dot(p[u].astype(BF16), v_ref[:, sl[u]]) for u in us]
            for u in us:
                l_scr[u] = alpha[u] * l_scr[u] + jnp.sum(p[u], axis=1, keepdims=True)
                acc[:, sl[u]] = alpha[u] * acc[:, sl[u]] + pv[u]
                m_scr[u] = m_new[u]

        @pl.when(kj < qi)
        def _():
            update(False)

        @pl.when(kj == qi)
        def _():
            update(True)
            for u in us:
                ob_ref[:, sl[u]] = (acc[:, sl[u]] / l_scr[u]).astype(BF16)
                lse_ref[u] = m_scr[u] + jnp.log(l_scr[u])

    qs = pl.BlockSpec((blk, hps * HEAD), lambda g, t, qr, kr: (qr[t], g))
    ks = pl.BlockSpec((blk, hps * HEAD), lambda g, t, qr, kr: (kr[t], g))
    col = pl.BlockSpec((hps, blk, 1), lambda g, t, qr, kr: (g, qr[t], 0))
    gs = pltpu.PrefetchScalarGridSpec(
        num_scalar_prefetch=2, grid=(FH // hps, qi_arr.shape[0]),
        in_specs=[qs, ks, ks, pl.BlockSpec((hps, 1, blk), lambda g, t, qr, kr: (g, 0, kr[t]))],
        out_specs=[qs, col],
        scratch_shapes=[pltpu.VMEM((hps, blk, 1), F32), pltpu.VMEM((hps, blk, 1), F32), pltpu.VMEM((blk, hps * HEAD), F32)])
    return _pcall(
        body, name="flash_fwd", grid_spec=gs,
        out_shape=[jax.ShapeDtypeStruct((T, FW), BF16), jax.ShapeDtypeStruct((FH, T, 1), F32)],
        compiler_params=_cp("parallel", "arbitrary"),
    )(qi_arr, kj_arr, qb, kb, vb, cumr)


def _flash_bwd_q(qb, kb, vb, cumr, lse, do, dl, FH, blk):
    T, FW = qb.shape
    blk = min(blk, T)
    nb = T // blk
    qi_arr, kj_arr = _causal_pairs(nb, False)
    want_dq = dl is not None
    hps = FLASH_HEADS_PER_STEP
    assert FH % hps == 0
    us = range(hps)
    sl = [slice(u * HEAD, (u + 1) * HEAD) for u in us]

    def body(qi_ref, kj_ref, q_ref, k_ref, v_ref, ck_ref, lse_ref, do_ref, *rest):
        if want_dq:
            dl_ref, dq_ref, rs_ref, acc, rs_acc = rest
        else:
            dl_ref, acc = rest
        t = pl.program_id(1)
        qi, kj = qi_ref[t], kj_ref[t]

        @pl.when(kj == 0)
        def _():
            acc[...] = jnp.zeros_like(acc)
            if want_dq:
                rs_acc[...] = jnp.zeros_like(rs_acc)

        def update(diagonal):
            s = [_fox_scores(q_ref[:, sl[u]], k_ref[:, sl[u]], ck_ref[u], diagonal) for u in us]
            p = [jnp.exp(s[u] - lse_ref[u]) for u in us]
            dp = [_dot_nt(do_ref[:, sl[u]].astype(BF16), v_ref[:, sl[u]]) for u in us]
            if want_dq:
                ds = [p[u] * (dp[u] - dl_ref[u]) for u in us]
                dqp = [_dot(ds[u].astype(BF16), k_ref[:, sl[u]]) for u in us]
                for u in us:
                    acc[:, sl[u]] += dqp[u]
                    rs_acc[u] += jnp.sum(ds[u], axis=1, keepdims=True)
            else:
                for u in us:
                    acc[u] += jnp.sum(p[u] * dp[u], axis=1, keepdims=True)

        @pl.when(kj < qi)
        def _():
            update(False)

        @pl.when(kj == qi)
        def _():
            update(True)
            if want_dq:
                dq_ref[...] = acc[...] * (HEAD ** -0.5)
                rs_ref[...] = rs_acc[...]
            else:
                dl_ref[...] = acc[...]

    qs = pl.BlockSpec((blk, hps * HEAD), lambda g, t, qr, kr: (qr[t], g))
    ks = pl.BlockSpec((blk, hps * HEAD), lambda g, t, qr, kr: (kr[t], g))
    col = pl.BlockSpec((hps, blk, 1), lambda g, t, qr, kr: (g, qr[t], 0))
    in_specs = [qs, ks, ks, pl.BlockSpec((hps, 1, blk), lambda g, t, qr, kr: (g, 0, kr[t])), col, qs]
    args = [qi_arr, kj_arr, qb, kb, vb, cumr, lse, do]
    colshape = jax.ShapeDtypeStruct((FH, T, 1), F32)
    if want_dq:
        gs = pltpu.PrefetchScalarGridSpec(
            num_scalar_prefetch=2, grid=(FH // hps, qi_arr.shape[0]), in_specs=in_specs + [col], out_specs=[qs, col],
            scratch_shapes=[pltpu.VMEM((blk, hps * HEAD), F32), pltpu.VMEM((hps, blk, 1), F32)])
        return _pcall(body, name="flash_bwd_dq", grid_spec=gs,
                      out_shape=[jax.ShapeDtypeStruct((T, FW), F32), colshape],
                      compiler_params=_cp("parallel", "arbitrary"))(*args, dl)
    gs = pltpu.PrefetchScalarGridSpec(
        num_scalar_prefetch=2, grid=(FH // hps, qi_arr.shape[0]), in_specs=in_specs, out_specs=col,
        scratch_shapes=[pltpu.VMEM((hps, blk, 1), F32)])
    return _pcall(body, name="flash_bwd_rowterm", grid_spec=gs, out_shape=colshape,
                  compiler_params=_cp("parallel", "arbitrary"))(*args)


def _flash_bwd_dkv(qb, kb, vb, cumc, lse_row, dl_row, do, FH, blk):
    T, FW = qb.shape
    blk = min(blk, T)
    nb = T // blk

    qi_arr, kj_arr = _causal_pairs(nb, True)

    hps = FLASH_HEADS_PER_STEP
    assert FH % hps == 0
    us = range(hps)
    sl = [slice(u * HEAD, (u + 1) * HEAD) for u in us]

    def body(qi_ref, kj_ref, q_ref, k_ref, v_ref, ck_ref, lse_ref, dl_ref, do_ref,
             dk_ref, dv_ref, dc_ref, dk_acc, dv_acc, dc_acc):
        t = pl.program_id(1)
        qi, kj = qi_ref[t], kj_ref[t]

        def update(diagonal, first):
            st = [_dot_nt(k_ref[:, sl[u]], q_ref[:, sl[u]]) - ck_ref[u] for u in us]
            if diagonal:
                krow = lax.broadcasted_iota(jnp.int32, st[0].shape, 0)
                qcol = lax.broadcasted_iota(jnp.int32, st[0].shape, 1)
                st = [jnp.where(qcol >= krow, st[u], NEG) for u in us]
            pt = [jnp.exp(st[u] - lse_ref[u]) for u in us]
            dob = [do_ref[:, sl[u]].astype(BF16) for u in us]
            dpt = [_dot_nt(v_ref[:, sl[u]], dob[u]) for u in us]
            dst = [pt[u] * (dpt[u] - dl_ref[u]) for u in us]
            dk = [_dot(dst[u].astype(BF16), q_ref[:, sl[u]]) for u in us]
            dv = [_dot(pt[u].astype(BF16), dob[u]) for u in us]
            for u in us:
                dc = -jnp.sum(dst[u], axis=1, keepdims=True)
                if first:
                    dk_acc[:, sl[u]] = dk[u]
                    dv_acc[:, sl[u]] = dv[u]
                    dc_acc[u] = dc
                else:
                    dk_acc[:, sl[u]] += dk[u]
                    dv_acc[:, sl[u]] += dv[u]
                    dc_acc[u] += dc

        @pl.when(qi == kj)
        def _():
            update(True, True)

        @pl.when(qi > kj)
        def _():
            update(False, False)

        @pl.when(qi == nb - 1)
        def _():
            dk_ref[...] = dk_acc[...]
            dv_ref[...] = dv_acc[...]
            dc_ref[...] = dc_acc[...]

    ks = pl.BlockSpec((blk, hps * HEAD), lambda g, t, qr, kr: (kr[t], g))
    qs = pl.BlockSpec((blk, hps * HEAD), lambda g, t, qr, kr: (qr[t], g))
    rowq = pl.BlockSpec((hps, 1, blk), lambda g, t, qr, kr: (g, 0, qr[t]))
    colk = pl.BlockSpec((hps, blk, 1), lambda g, t, qr, kr: (g, kr[t], 0))
    gs = pltpu.PrefetchScalarGridSpec(
        num_scalar_prefetch=2, grid=(FH // hps, qi_arr.shape[0]),
        in_specs=[qs, ks, ks, colk, rowq, rowq, qs],
        out_specs=[ks, ks, colk],
        scratch_shapes=[pltpu.VMEM((blk, hps * HEAD), F32), pltpu.VMEM((blk, hps * HEAD), F32),
                        pltpu.VMEM((hps, blk, 1), F32)])
    return _pcall(
        body, name="flash_bwd_dkv", grid_spec=gs,
        out_shape=[jax.ShapeDtypeStruct((T, FW), F32), jax.ShapeDtypeStruct((T, FW), F32),
                   jax.ShapeDtypeStruct((FH, T, 1), F32)],
        compiler_params=_cp("parallel", "arbitrary"),
    )(qi_arr, kj_arr, qb, kb, vb, cumc, lse_row, dl_row, do)


def _rev_cumsum_rows(r1, r2, tb=512):
    H, T = r1.shape
    tb = min(tb, T)
    nb = T // tb

    def body(r1_ref, r2_ref, o_ref, carry):
        @pl.when(pl.program_id(0) == 0)
        def _():
            carry[...] = jnp.zeros_like(carry)

        rv = r1_ref[...] + r2_ref[...]
        si = lax.broadcasted_iota(jnp.int32, (tb, tb), 0)
        ti = lax.broadcasted_iota(jnp.int32, (tb, tb), 1)
        o_ref[...] = _dot(rv, (si >= ti).astype(F32), HI) + carry[...]
        carry[...] += jnp.sum(rv, axis=1, keepdims=True)

    spec = pl.BlockSpec((H, tb), lambda i: (0, nb - 1 - i))
    return _pcall(body, name="rev_cumsum", grid=(nb,), in_specs=[spec, spec], out_specs=spec,
                  out_shape=jax.ShapeDtypeStruct((H, T), F32), scratch_shapes=[pltpu.VMEM((H, 1), F32)],
                  compiler_params=_cp("arbitrary"))(r1, r2)


def _mem_head(q, k, v, gq, gk):
    logits = _dot_nt(_rms(q, gq), _rms(k, gk)) * (MEM_DH ** -0.5)
    mx = jnp.max(logits, axis=1, keepdims=True)
    e = jnp.exp(logits - mx)
    p = e / jnp.sum(e, axis=1, keepdims=True)
    return _dot(p, v)


def _mem_attn(proj, off_q, kv, gq, gk, MW, tb=512):
    T = proj.shape[0]
    MH = MW // MEM_DH

    def fn(i, n, q, kvv, gqv, gkv):
        ks, vs = _heads(kvv[:, :MW], MEM_DH), _heads(kvv[:, MW:], MEM_DH)
        return (_cat([_mem_head(a, b, c, gqv, gkv) for a, b, c in zip(_heads(q, MEM_DH), ks, vs)]),)

    return _rowwise(fn, T, tb, [("row", proj, off_q, MW), ("full", kv), ("full", gq), ("full", gk)],
                    [("row", MW, BF16)], "mem_attn")[0]


def _mem_attn_bwd(proj, off_q, kv, gq, gk, dout, MW, tb=256):
    T = proj.shape[0]
    ML = kv.shape[0]

    def fn(i, n, q, kvv, gqv, gkv, dv):
        ks, vs = _heads(kvv[:, :MW], MEM_DH), _heads(kvv[:, MW:], MEM_DH)
        dqs, dks, dvs = [], [], []
        dgq, dgk = jnp.zeros_like(gqv), jnp.zeros_like(gkv)
        for a, b, c, d in zip(_heads(q, MEM_DH), ks, vs, _heads(dv, MEM_DH)):
            _, vjp = jax.vjp(_mem_head, a, b, c, gqv, gkv)
            da, db, dc, dg1, dg2 = vjp(d)
            dqs.append(da)
            dks.append(db)
            dvs.append(dc)
            dgq, dgk = dgq + dg1, dgk + dg2
        return _cat(dqs), _cat(dks + dvs), dgq, dgk

    return _rowwise(fn, T, tb, [("row", proj, off_q, MW), ("full", kv), ("full", gq), ("full", gk), ("row", dout, 0, MW)],
                    [("row", MW, BF16), ("acc", (ML, 2 * MW), F32), ("acc", (1, MEM_DH), F32), ("acc", (1, MEM_DH), F32)],
                    "mem_attn_bwd")


def _merge_fn(ga, gb, gm, ua, ub, um):
    return _sigmoid(ga) * ua + _sigmoid(gb) * ub + _sigmoid(gm) * um


def _merge(proj, offs, ua, ub, um, D, tb=256):
    T = proj.shape[0]
    ins = [("row", proj, offs[k], D) for k in ("ga", "gb", "gm")] + [("row", u, 0, D) for u in (ua, ub, um)]
    return _rowwise(lambda i, n, *v: (_merge_fn(*v),), T, tb, ins, [("row", D, BF16)], "merge")[0]


def _merge_bwd(proj, offs, ua, ub, um, dy, D, tb=256):
    T = proj.shape[0]

    def fn(i, n, *v):
        _, vjp = jax.vjp(_merge_fn, *v[:6])
        return vjp(v[6])

    ins = [("row", proj, offs[k], D) for k in ("ga", "gb", "gm")] + [("row", u, 0, D) for u in (ua, ub, um)] + [("row", dy, 0, D)]
    return _rowwise(fn, T, tb, ins, [("row", D, BF16)] * 6, "merge_bwd")


def _loss_grad(x2, tgt, tb=256):
    T, D = x2.shape

    def fn(i, n, a, b):
        e = a - b
        part = jnp.sum(jnp.sum(e * e, axis=1, keepdims=True), axis=0, keepdims=True) * (0.5 / D)
        g = e * (1.0 / D)
        return g, g, part + jnp.zeros((SUB, LANES), F32)

    return _rowwise(fn, T, tb, [("row", x2, 0, D), ("row", tgt, 0, D)],
                    [("row", D, F32), ("row", D, BF16), ("acc", (SUB, LANES), F32)], "loss_grad")


def _adamw(w, g, m, v, name, tb=128):
    R, C = w.shape
    c1 = 1.0 / (1.0 - ADAM_B1 ** ADAM_STEP)
    c2 = 1.0 / (1.0 - ADAM_B2 ** ADAM_STEP)

    def fn(i, n, wv, gv, mv, vv):
        mn = ADAM_B1 * mv + (1.0 - ADAM_B1) * gv
        vn = ADAM_B2 * vv + (1.0 - ADAM_B2) * (gv * gv)
        delta = -ADAM_LR * ((mn * c1) / (jnp.sqrt(vn * c2) + ADAM_EPS) + ADAM_WD * wv)
        return delta, mn, vn

    return _rowwise(fn, R, tb, [("row", a, 0, C) for a in (w, g, m, v)], [("row", C, F32)] * 3, name)


def _coords():
    return lax.axis_index("x"), lax.axis_index("y"), lax.axis_index("c")


def _allgather_small(blk):
    m_per, n = blk.shape

    def body(x_ref, out_ref, send_sems, recv_sems, local_sem):
        x, y, c = _coords()
        me, sibling = (x, y, c), (x, y, 1 - c)
        chips = [(1 - x, y), (x, 1 - y), (1 - x, 1 - y)]

        def rows(px, py, pc):
            return out_ref.at[pl.ds((4 * px + 2 * py + pc) * m_per, m_per), :]

        def copy(k, block, to, src=None):
            return pltpu.make_async_remote_copy(
                src_ref=rows(*block) if src is None else src, dst_ref=rows(*block),
                send_sem=send_sems.at[k], recv_sem=recv_sems.at[k], device_id=to, device_id_type=MESH)

        mine = pltpu.make_async_copy(x_ref, rows(*me), local_sem)
        mine.start()
        first = [copy(0, me, sibling, src=x_ref)]
        first += [copy(1 + j, me, (*chip, c), src=x_ref) for j, chip in enumerate(chips)]
        for cp in first:
            cp.start()
        passed = [copy(4 + j, (*chip, c), sibling) for j, chip in enumerate(chips)]
        for j, chip in enumerate(chips):
            copy(1 + j, (*chip, c), me).wait_recv()
            passed[j].start()
        copy(0, sibling, me).wait_recv()
        for j, chip in enumerate(chips):
            copy(4 + j, (*chip, 1 - c), me).wait_recv()
        for cp in first + passed:
            cp.wait_send()
        mine.wait()

    return _pcall(
        body, name="allgather_small", out_shape=jax.ShapeDtypeStruct((8 * m_per, n), blk.dtype),
        in_specs=[pl.BlockSpec(memory_space=pltpu.VMEM)], out_specs=pl.BlockSpec(memory_space=pltpu.VMEM),
        scratch_shapes=[pltpu.SemaphoreType.DMA((7,)), pltpu.SemaphoreType.DMA((7,)), pltpu.SemaphoreType.DMA],
        compiler_params=pltpu.CompilerParams(vmem_limit_bytes=VMEM_LIMIT),
    )(blk)


def _sum8(g, m_per):
    n = g.shape[1]

    def body(g_ref, o_ref):
        acc = g_ref[pl.ds(0, m_per), :]
        for d in range(1, 8):
            acc = acc + g_ref[pl.ds(d * m_per, m_per), :]
        o_ref[...] = acc

    return _pcall(body, name="sum8", out_shape=jax.ShapeDtypeStruct((m_per, n), g.dtype))(g)


_ANY = pl.BlockSpec(memory_space=pl.ANY)


def _allgather_chips(buf):
    nr, w = buf.shape
    half = nr // 2

    def body(in_ref, out_ref, send_sems, recv_sems):
        x, y, c = _coords()
        me = 2 * x + y
        chips = [(1 - x, y), (x, 1 - y), (1 - x, 1 - y)]
        mine_rows = pl.ds(pl.multiple_of(c * half, 16), half)
        other_rows = pl.ds(pl.multiple_of((1 - c) * half, 16), half)

        def copy(k, src, dst, to):
            return pltpu.make_async_remote_copy(src_ref=src, dst_ref=dst, send_sem=send_sems.at[k],
                                                recv_sem=recv_sems.at[k], device_id=to, device_id_type=MESH)

        first = [copy(j, in_ref.at[mine_rows], out_ref.at[me, mine_rows], (cx, cy, c)) for j, (cx, cy) in enumerate(chips)]
        for cp in first:
            cp.start()
        passed = []
        for j, (cx, cy) in enumerate(chips):
            slot = out_ref.at[2 * cx + cy, mine_rows]
            copy(j, slot, slot, (cx, cy, c)).wait_recv()
            fwd = copy(3 + j, slot, slot, (x, y, 1 - c))
            fwd.start()
            passed.append(fwd)
        for j, (cx, cy) in enumerate(chips):
            slot = out_ref.at[2 * cx + cy, other_rows]
            copy(3 + j, slot, slot, (x, y, 1 - c)).wait_recv()
        for cp in first + passed:
            cp.wait_send()

    return _pcall(
        body, name="allgather_chips", out_shape=jax.ShapeDtypeStruct((4, nr, w), buf.dtype),
        in_specs=[_ANY], out_specs=_ANY,
        scratch_shapes=[pltpu.SemaphoreType.DMA((6,)), pltpu.SemaphoreType.DMA((6,))],
    )(buf)


def _rs_pair_exchange(g):
    _, nr, w = g.shape
    half = nr // 2

    def body(g_ref, rb_ref, send_sem, recv_sem):
        x, y, c = _coords()
        other_rows = pl.ds(pl.multiple_of((1 - c) * half, SUB), half)
        cp = pltpu.make_async_remote_copy(src_ref=g_ref.at[:, other_rows], dst_ref=rb_ref, send_sem=send_sem,
                                          recv_sem=recv_sem, device_id=(x, y, 1 - c), device_id_type=MESH)
        cp.start()
        cp.wait()

    return _pcall(body, name="rs_pair_exchange", out_shape=jax.ShapeDtypeStruct((4, half, w), g.dtype),
                  in_specs=[_ANY], out_specs=_ANY,
                  scratch_shapes=[pltpu.SemaphoreType.DMA, pltpu.SemaphoreType.DMA])(g)


def _rs_pair_add(g, rb, cidx, tb=256):
    _, nr, w = g.shape
    half = nr // 2
    tb = min(tb, half)
    assert half % tb == 0
    hb = half // tb

    def body(c_ref, g_ref, r_ref, o_ref):
        o_ref[...] = (g_ref[...] + r_ref[...]).astype(o_ref.dtype)

    gs = pltpu.PrefetchScalarGridSpec(
        num_scalar_prefetch=1, grid=(4, hb),
        in_specs=[pl.BlockSpec((1, tb, w), lambda j, i, c_ref: (j, c_ref[0] * hb + i, 0)),
                  pl.BlockSpec((1, tb, w), lambda j, i, c_ref: (j, i, 0))],
        out_specs=pl.BlockSpec((1, tb, w), lambda j, i, c_ref: (j, i, 0)))
    return _pcall(body, name="rs_pair_add", grid_spec=gs, out_shape=jax.ShapeDtypeStruct((4, half, w), BF16),
                  compiler_params=_cp("parallel", "parallel"))(cidx, g, rb)


def _rs_chip_exchange(p):
    _, h, w = p.shape

    def body(p_ref, rb_ref, send_sems, recv_sems):
        x, y, c = _coords()
        chips = [(1 - x, y), (x, 1 - y), (1 - x, 1 - y)]
        cps = [pltpu.make_async_remote_copy(src_ref=p_ref.at[2 * cx + cy], dst_ref=rb_ref.at[j], send_sem=send_sems.at[j],
                                            recv_sem=recv_sems.at[j], device_id=(cx, cy, c), device_id_type=MESH)
               for j, (cx, cy) in enumerate(chips)]
        for cp in cps:
            cp.start()
        for cp in cps:
            cp.wait()

    return _pcall(body, name="rs_chip_exchange", out_shape=jax.ShapeDtypeStruct((3, h, w), p.dtype),
                  in_specs=[_ANY], out_specs=_ANY,
                  scratch_shapes=[pltpu.SemaphoreType.DMA((3,)), pltpu.SemaphoreType.DMA((3,))])(p)


def _sum4(p, rb, chip_idx, tb=256):
    _, h, w = rb.shape
    tb = min(tb, h)
    assert h % tb == 0

    def body(m_ref, p_ref, r_ref, o_ref):
        f = lambda t: t.astype(F32)
        o_ref[...] = ((f(p_ref[0]) + f(r_ref[0])) + f(r_ref[1])) + f(r_ref[2])

    gs = pltpu.PrefetchScalarGridSpec(
        num_scalar_prefetch=1, grid=(h // tb,),
        in_specs=[pl.BlockSpec((1, tb, w), lambda i, m_ref: (m_ref[0], i, 0)),
                  pl.BlockSpec((3, tb, w), lambda i, m_ref: (0, i, 0))],
        out_specs=pl.BlockSpec((tb, w), lambda i, m_ref: (i, 0)))
    return _pcall(body, name="sum4", grid_spec=gs, out_shape=jax.ShapeDtypeStruct((h, w), F32),
                  compiler_params=_cp("parallel"))(chip_idx, p, rb)


def _pair_allgather(f):
    h, w = f.shape

    def body(f_ref, out_ref, send_sem, recv_sem):
        x, y, c = _coords()
        mine_rows = pl.ds(pl.multiple_of(c * h, SUB), h)
        other_rows = pl.ds(pl.multiple_of((1 - c) * h, SUB), h)
        send = pltpu.make_async_remote_copy(src_ref=f_ref, dst_ref=out_ref.at[mine_rows], send_sem=send_sem,
                                            recv_sem=recv_sem, device_id=(x, y, 1 - c), device_id_type=MESH)
        send.start()
        send.wait_send()
        pltpu.make_async_remote_copy(src_ref=f_ref, dst_ref=out_ref.at[other_rows], send_sem=send_sem,
                                     recv_sem=recv_sem, device_id=(x, y, 1 - c), device_id_type=MESH).wait_recv()

    return _pcall(body, name="pair_allgather", out_shape=jax.ShapeDtypeStruct((2 * h, w), f.dtype),
                  in_specs=[_ANY], out_specs=_ANY,
                  scratch_shapes=[pltpu.SemaphoreType.DMA, pltpu.SemaphoreType.DMA])(f)


def _size(shape):
    n = 1
    for d in shape:
        n *= d
    return n


PACK_ALIGN = 16


def _pack(arrs, dtype, row_mult):
    parts = []
    for a in arrs:
        flat = a.astype(dtype).reshape(-1)
        n = flat.shape[0]
        full = _ru(n, PACK_W * PACK_ALIGN)
        if full > n:
            flat = jnp.pad(flat, (0, full - n))
        parts.append(flat.reshape(-1, PACK_W))
    rows = sum(p.shape[0] for p in parts)
    if rows % row_mult:
        parts.append(jnp.zeros((_ru(rows, row_mult) - rows, PACK_W), dtype))
    return jnp.concatenate(parts, axis=0)


def _unpack(buf, shapes):
    out, off = [], 0
    for s in shapes:
        n = _size(s)
        r = _ru(-(-n // PACK_W), PACK_ALIGN)
        part = buf[off:off + r]
        out.append(part.reshape(s) if n == r * PACK_W else part.reshape(-1)[:n].reshape(s))
        off += r
    return out


def _pack_flat(arrs, dtype, row_mult):
    flat = jnp.concatenate([a.astype(dtype).reshape(-1) for a in arrs])
    n = flat.shape[0]
    rows = _ru(-(-n // PACK_W), row_mult)
    return jnp.pad(flat, (0, rows * PACK_W - n)).reshape(rows, PACK_W)


def _unpack_flat(buf, shapes):
    flat = buf.reshape(-1)
    out, off = [], 0
    for s in shapes:
        n = _size(s)
        out.append(flat[off:off + n].reshape(s))
        off += n
    return out


def _in_layout(D, GW, GH, FW, FH, MW, tn):
    o_z = 3 * GW
    o_beta = 4 * GW
    o_fq = o_beta + 2 * GH
    o_ff = o_fq + 3 * FW
    o_mq = o_ff + FH
    o_g = o_mq + MW
    orig = {"q": (0, GW), "k": (GW, GW), "v": (2 * GW, GW), "z": (o_z, GW), "beta": (o_beta, GH), "dec": (o_beta + GH, GH),
            "fq": (o_fq, FW), "fk": (o_fq + FW, FW), "fv": (o_fq + 2 * FW, FW), "ff": (o_ff, FH), "mq": (o_mq, MW),
            "ga": (o_g, D), "gb": (o_g + D, D), "gm": (o_g + 2 * D, D)}
    offs, cur = {}, 0
    for key, width in (("ga", D), ("gb", D), ("gm", D), ("q", GW), ("k", GW), ("v", GW), ("z", GW),
                       ("fq", FW), ("fk", FW), ("fv", FW), ("mq", MW), ("small", LANES)):
        cur = _ru(cur, width)
        offs[key] = cur
        cur += width
    total = _ru(cur, tn)
    pieces = [(offs[k], orig[k][0], orig[k][1]) for k in ("ga", "gb", "gm", "q", "k", "v", "z", "fq", "fk", "fv", "mq")]
    pieces += [(offs["small"], orig["beta"][0], GH), (offs["small"] + GH, orig["dec"][0], GH),
               (offs["small"] + 2 * GH, orig["ff"][0], FH)]
    return offs, total, pieces, o_g + 3 * D


def _pad_cols(w, pieces, total):
    parts, cur = [], 0
    for pstart, ostart, n in pieces:
        if pstart > cur:
            parts.append(jnp.zeros((w.shape[0], pstart - cur), w.dtype))
        parts.append(w[:, ostart:ostart + n])
        cur = pstart + n
    if total > cur:
        parts.append(jnp.zeros((w.shape[0], total - cur), w.dtype))
    return jnp.concatenate(parts, axis=1)


def _unpad_cols(wp, pieces):
    return jnp.concatenate([wp[:, pstart:pstart + n] for pstart, ostart, n in sorted(pieces, key=lambda t: t[1])], axis=1)


def _local_step(x, mem, tgt, W, flash_blk=512):
    T, D = x.shape
    GW = W["w_up_gdn"].shape[0]
    FW = W["w_up_fox"].shape[0]
    MW = W["w_up_mem"].shape[0]
    GH, FH = GW // HEAD, FW // HEAD
    offs, NP, pieces, d_in = _in_layout(D, GW, GH, FW, FH, MW, 1024)
    assert W["w_in"].shape[1] == d_in
    w_in_p = _pad_cols(W["w_in"], pieces, NP)
    cw = W["conv_w"]
    cws = [cw[:, i * GW:(i + 1) * GW] for i in range(3)]
    zl = jnp.zeros((1, LANES), F32)
    pvecs = [lax.dynamic_update_slice(zl, W["a_log"], (0, GH)), lax.dynamic_update_slice(zl, W["dt_bias"], (0, GH)),
             lax.dynamic_update_slice(zl, W["fox_b_f"], (0, 2 * GH))]
    lane0 = 2 * GH

    h = _rms_fwd(x, W["g_mix"], "rms_mix")
    proj = _mm(h, w_in_p, "nn", "in_proj")
    qn, kn, vc, cum, gbm = _gdn_prep(proj, offs, cws, pvecs, GH, FH, GW)
    o_gdn, sall = _gdn_scan_fwd(qn, kn, vc, gbm, GH)
    o_a = _gdn_post(o_gdn, proj, offs["z"], W["gdn_norm_g"], GW)
    qb, kb, vb = _fox_prep(proj, offs, W["fox_q_norm"], W["fox_k_norm"], FW)
    cumh = cum[:, lane0:lane0 + FH].T
    cumc, cumr = cumh.reshape(FH, T, 1), cumh.reshape(FH, 1, T)
    o_b16, lse = _flash_fwd(qb, kb, vb, cumr, FH, flash_blk)
    memn = _rms_fwd(mem, W["g_mem"], "rms_mem")
    kv = _mm(memn, W["w_mem_kv"], "nn", "mem_kv")
    o_m = _mem_attn(proj, offs["mq"], kv, W["mem_q_norm"], W["mem_k_norm"], MW)
    ua = _mm(o_a, W["w_up_gdn"], "nn", "up_gdn")
    ub = _mm(o_b16, W["w_up_fox"], "nn", "up_fox")
    um = _mm(o_m, W["w_up_mem"], "nn", "up_mem")
    y = _merge(proj, offs, ua, ub, um, D)
    x1 = _mm(y, W["w_out"], "nn", "out_proj", epilogue=lambda acc, r: (acc + r,), extras=(x,))
    h2 = _rms_fwd(x1, W["g_mlp"], "rms_mlp")
    u, a = _mm(h2, W["w_ff1"], "nn", "ff1", out_dtypes=(F32, BF16),
               epilogue=lambda acc: (acc, jnp.square(jnp.maximum(acc, 0.0))))
    x2 = _mm(a, W["w_ff2"], "nn", "ff2", epilogue=lambda acc, r: (acc + r,), extras=(x1,))
    dx2, dx2b, lpart = _loss_grad(x2, tgt)
    loss = lpart[0, 0]

    G = {}
    du = _mm(dx2b, W["w_ff2"], "nt", "ff2_dx", out_dtypes=(BF16,),
             epilogue=lambda acc, uu: (acc * (2.0 * jnp.maximum(uu, 0.0)),), extras=(u,))
    G["w_ff2"] = _mm(a, dx2b, "tn", "ff2_dw")
    dh2 = _mm(du, W["w_ff1"], "nt", "ff1_dx")
    G["w_ff1"] = _mm(h2, du, "tn", "ff1_dw")
    dx1, dx1b, G["g_mlp"] = _rms_bwd(x1, W["g_mlp"], dh2, dx2, "rms_mlp_bwd")
    dy = _mm(dx1b, W["w_out"], "nt", "out_dx")
    G["w_out"] = _mm(y, dx1b, "tn", "out_dw")
    dga, dgb, dgm, dua, dub, dum = _merge_bwd(proj, offs, ua, ub, um, dy, D)
    do_a = _mm(dua, W["w_up_gdn"], "nt", "up_gdn_dx")
    G["w_up_gdn"] = _mm(o_a, dua, "tn", "up_gdn_dw")
    do_b = _mm(dub, W["w_up_fox"], "nt", "up_fox_dx")
    G["w_up_fox"] = _mm(o_b16, dub, "tn", "up_fox_dw")
    do_m = _mm(dum, W["w_up_mem"], "nt", "up_mem_dx")
    G["w_up_mem"] = _mm(o_m, dum, "tn", "up_mem_dw")
    dmq, dkv, G["mem_q_norm"], G["mem_k_norm"] = _mem_attn_bwd(proj, offs["mq"], kv, W["mem_q_norm"], W["mem_k_norm"], do_m, MW)
    dkvb = dkv.astype(BF16)
    dmemn = _mm(dkvb, W["w_mem_kv"], "nt", "mem_kv_dx")
    G["w_mem_kv"] = _mm(memn, dkvb, "tn", "mem_kv_dw")
    G["g_mem"] = _rms_dg(mem, W["g_mem"], dmemn, "rms_mem_bwd")
    dl = _flash_bwd_q(qb, kb, vb, cumr, lse, do_b, None, FH, flash_blk)
    dqb, dcq = _flash_bwd_q(qb, kb, vb, cumr, lse, do_b, dl, FH, flash_blk)
    dkb, dvb, dck = _flash_bwd_dkv(qb, kb, vb, cumc, lse.reshape(FH, 1, T), dl.reshape(FH, 1, T), do_b, FH, flash_blk)
    dlf = _rev_cumsum_rows(dcq.reshape(FH, T), dck.reshape(FH, T))
    dlf_sm = jnp.pad(dlf.T, ((0, 0), (lane0, LANES - lane0 - FH)))
    dfq, dfk, dfv, G["fox_q_norm"], G["fox_k_norm"] = _fox_prep_bwd(proj, offs, W["fox_q_norm"], W["fox_k_norm"], dqb, dkb, dvb, FW)
    do_gdn, dz, G["gdn_norm_g"] = _gdn_post_bwd(o_gdn, proj, offs["z"], W["gdn_norm_g"], do_a, GW)
    dqn, dkn, dvc, dgsm = _gdn_scan_bwd(qn, kn, vc, gbm, sall, do_gdn, GH)
    dyq, dyk, dyv, dcq, dck_w, dcv, dsmall, dalog, ddtb, dbf = _gdn_prep_bwd_a(
        proj, offs, cws, pvecs, (dqn, dkn, dvc), dgsm, dlf_sm, GH, FH, GW)
    dxq, dxk, dxv = _gdn_prep_bwd_b((dyq, dyk, dyv), cws, GW)
    G["conv_w"] = jnp.concatenate([dcq[:CONV_K], dck_w[:CONV_K], dcv[:CONV_K]], axis=1)
    G["a_log"] = dalog[:, GH:2 * GH]
    G["dt_bias"] = ddtb[:, GH:2 * GH]
    G["fox_b_f"] = dbf[:, lane0:lane0 + FH]
    segs = {"ga": dga, "gb": dgb, "gm": dgm, "q": dxq, "k": dxk, "v": dxv, "z": dz, "fq": dfq, "fk": dfk, "fv": dfv,
            "mq": dmq, "small": dsmall}
    parts, cur = [], 0
    for key in ("ga", "gb", "gm", "q", "k", "v", "z", "fq", "fk", "fv", "mq", "small"):
        if offs[key] > cur:
            parts.append(jnp.zeros((T, offs[key] - cur), BF16))
        parts.append(segs[key])
        cur = offs[key] + segs[key].shape[1]
    if NP > cur:
        parts.append(jnp.zeros((T, NP - cur), BF16))
    dproj = jnp.concatenate(parts, axis=1)
    dh = _mm(dproj, w_in_p, "nt", "in_dx")
    G["w_in"] = _unpad_cols(_mm(h, dproj, "tn", "in_dw"), pieces)
    grad_x, _, G["g_mix"] = _rms_bwd(x, W["g_mix"], dh, dx1, "rms_mix_bwd")
    return loss, grad_x, G


BIG = ["w_in", "w_mem_kv", "w_up_gdn", "w_up_fox", "w_up_mem", "w_out", "w_ff1", "w_ff2"]
SMALL = ["g_mix", "a_log", "dt_bias", "gdn_norm_g", "fox_b_f", "fox_q_norm", "fox_k_norm", "g_mem", "mem_q_norm",
         "mem_k_norm", "g_mlp"]
ORDER = ["g_mix", "w_in", "conv_w", "a_log", "dt_bias", "gdn_norm_g", "fox_b_f", "fox_q_norm", "fox_k_norm", "g_mem",
         "w_mem_kv", "mem_q_norm", "mem_k_norm", "w_up_gdn", "w_up_fox", "w_up_mem", "w_out", "g_mlp", "w_ff1", "w_ff2"]
SHARD_AXIS = {"w_in": 1, "w_mem_kv": 0, "w_up_gdn": 1, "w_up_fox": 1, "w_up_mem": 1, "w_out": 0, "w_ff1": 1, "w_ff2": 0}


def _step(x, mem, tgt, w, m, v, flash_blk=512):
    xi, yi, ci = _coords()
    chip = 2 * xi + yi

    shard_shapes = [w[n].shape for n in BIG]
    packed_w = _pack([w[n] for n in BIG], BF16, PACK_ROWS)
    gathered = _allgather_chips(packed_w)
    gathered = lax.dynamic_update_slice(gathered, packed_w[None], (chip, 0, 0))
    per_chip = [_unpack(gathered[j], shard_shapes) for j in range(4)]
    W = {n: jnp.concatenate([per_chip[j][i] for j in range(4)], axis=SHARD_AXIS[n]) for i, n in enumerate(BIG)}
    cw_rows = jnp.pad(w["conv_w"], ((0, SUB - CONV_K), (0, 0)))
    cw_all = _allgather_small(cw_rows)
    W["conv_w"] = jnp.concatenate([cw_all[16 * j:16 * j + CONV_K] for j in range(4)], axis=1)
    for n in SMALL:
        W[n] = w[n]

    loss, grad_x, G = _local_step(x, mem, tgt, W, flash_blk)
    loss = lax.psum(loss, ("x", "y", "c"))

    small_shapes = [G[n].shape for n in SMALL] + [G["conv_w"].shape]
    sm = _pack_flat([G[n] for n in SMALL] + [G["conv_w"]], F32, SUB)
    sm_sum = _sum8(_allgather_small(sm), sm.shape[0])
    sm_list = _unpack_flat(sm_sum, small_shapes)
    g = {n: sm_list[i] for i, n in enumerate(SMALL)}
    cw_full = sm_list[-1]
    gw4 = cw_full.shape[1] // 4
    g["conv_w"] = lax.dynamic_slice(cw_full, (0, chip * gw4), (CONV_K, gw4))

    by_dest = []
    for j in range(4):
        shards = []
        for n in BIG:
            size = w[n].shape[SHARD_AXIS[n]]
            shards.append(lax.slice_in_dim(G[n], j * size, (j + 1) * size, axis=SHARD_AXIS[n]))
        by_dest.append(_pack(shards, F32, PACK_ROWS))
    gflat = jnp.stack(by_dest)
    rb1 = _rs_pair_exchange(gflat)
    part = _rs_pair_add(gflat, rb1, jnp.reshape(ci, (1,)).astype(jnp.int32))
    rb2 = _rs_chip_exchange(part)
    half_sum = _sum4(part, rb2, jnp.reshape(chip, (1,)).astype(jnp.int32))
    mine = _pair_allgather(half_sum)
    mine = lax.dynamic_update_slice(mine, half_sum, (ci * half_sum.shape[0], 0))
    for i, gv in enumerate(_unpack(mine, shard_shapes)):
        g[BIG[i]] = gv

    delta, new_m, new_v = {}, {}, {}
    for n in BIG:
        delta[n], new_m[n], new_v[n] = _adamw(w[n], g[n], m[n], v[n], "adamw_" + n)
    rest = SMALL + ["conv_w"]
    rest_shapes = [w[n].shape for n in rest]
    packed = [_pack_flat([d[n] for n in rest], F32, SUB) for d in (w, g, m, v)]
    outs = _adamw(*packed, "adamw_small", tb=packed[0].shape[0])
    for d, buf in zip((delta, new_m, new_v), outs):
        for n, val in zip(rest, _unpack_flat(buf, rest_shapes)):
            d[n] = val
    return loss, grad_x, g, delta, new_m, new_v


def kernel(x, mem, g_mix, w_in, conv_w, a_log, dt_bias, gdn_norm_g, fox_b_f, fox_q_norm, fox_k_norm, g_mem, w_mem_kv, mem_q_norm, mem_k_norm, w_up_gdn, w_up_fox, w_up_mem, w_out, g_mlp, w_ff1, w_ff2, loss_target, m_g_mix, m_w_in, m_conv_w, m_a_log, m_dt_bias, m_gdn_norm_g, m_fox_b_f, m_fox_q_norm, m_fox_k_norm, m_g_mem, m_w_mem_kv, m_mem_q_norm, m_mem_k_norm, m_w_up_gdn, m_w_up_fox, m_w_up_mem, m_w_out, m_g_mlp, m_w_ff1, m_w_ff2, v_g_mix, v_w_in, v_conv_w, v_a_log, v_dt_bias, v_gdn_norm_g, v_fox_b_f, v_fox_q_norm, v_fox_k_norm, v_g_mem, v_w_mem_kv, v_mem_q_norm, v_mem_k_norm, v_w_up_gdn, v_w_up_fox, v_w_up_mem, v_w_out, v_g_mlp, v_w_ff1, v_w_ff2):
    ws = (g_mix, w_in, conv_w, a_log, dt_bias, gdn_norm_g, fox_b_f, fox_q_norm, fox_k_norm, g_mem, w_mem_kv, mem_q_norm,
          mem_k_norm, w_up_gdn, w_up_fox, w_up_mem, w_out, g_mlp, w_ff1, w_ff2)
    ms = (m_g_mix, m_w_in, m_conv_w, m_a_log, m_dt_bias, m_gdn_norm_g, m_fox_b_f, m_fox_q_norm, m_fox_k_norm, m_g_mem,
          m_w_mem_kv, m_mem_q_norm, m_mem_k_norm, m_w_up_gdn, m_w_up_fox, m_w_up_mem, m_w_out, m_g_mlp, m_w_ff1, m_w_ff2)
    vs = (v_g_mix, v_w_in, v_conv_w, v_a_log, v_dt_bias, v_gdn_norm_g, v_fox_b_f, v_fox_q_norm, v_fox_k_norm, v_g_mem,
          v_w_mem_kv, v_mem_q_norm, v_mem_k_norm, v_w_up_gdn, v_w_up_fox, v_w_up_mem, v_w_out, v_g_mlp, v_w_ff1, v_w_ff2)
    drop = lambda a: a[0] if a.ndim == 3 else a
    w = {n: drop(a) for n, a in zip(ORDER, ws)}
    m = {n: drop(a) for n, a in zip(ORDER, ms)}
    v = {n: drop(a) for n, a in zip(ORDER, vs)}
    loss, grad_x, g, delta, new_m, new_v = _step(x[0], mem[0], loss_target[0], w, m, v)
    out = [loss, grad_x[None]]
    for d in (g, delta, new_m, new_v):
        out += [d[n].reshape(a.shape) for n, a in zip(ORDER, ws)]
    return tuple(out)
```

```python
import numpy as np

import jax
import jax.numpy as jnp
from jax import lax
from jax.experimental import pallas as pl
from jax.experimental.pallas import tpu as pltpu

F32 = jnp.float32
BF16 = jnp.bfloat16
HI = lax.Precision.HIGHEST
MESH = pl.DeviceIdType.MESH

EPS = 1e-6
HEAD = 128
MEM_DH = 256
CONV_K = 4
CHUNK = 64
CHUNK_SHIFT = 6
LANES = 128
SUB = 8
PACK_W = 1024
PACK_ROWS = 512
VMEM_LIMIT = 56 * 1024 * 1024
NEG = -1e30
SOLVE_PREC = None
FLASH_HEADS_PER_STEP = 4

ADAM_LR, ADAM_B1, ADAM_B2, ADAM_EPS, ADAM_WD, ADAM_STEP = 0.001, 0.9, 0.999, 1e-08, 0.01, 10


def _pcall(body, **kw):
    return pl.pallas_call(body, **kw)


def _cp(*sem):
    return pltpu.CompilerParams(dimension_semantics=sem, vmem_limit_bytes=VMEM_LIMIT)


def _dot(a, b, prec=None):
    return lax.dot_general(a, b, (((1,), (0,)), ((), ())), precision=prec, preferred_element_type=F32)


def _dot_nt(a, b, prec=None):
    return lax.dot_general(a, b, (((1,), (1,)), ((), ())), precision=prec, preferred_element_type=F32)


def _dot_tn(a, b, prec=None):
    return lax.dot_general(a, b, (((0,), (0,)), ((), ())), precision=prec, preferred_element_type=F32)


def _sigmoid(x):
    return 1.0 / (1.0 + jnp.exp(-x))


def _softplus(x):
    return jnp.maximum(x, 0.0) + jnp.log(1.0 + jnp.exp(-jnp.abs(x)))


def _silu(x):
    return x * _sigmoid(x)


def _rms(x, g):
    return x * lax.rsqrt(jnp.mean(x * x, axis=-1, keepdims=True) + EPS) * g


def _ru(a, m):
    return (a + m - 1) // m * m


def _mm(a, b, mode, name, out_dtypes=(F32,), epilogue=None, extras=(), tm=1024, tn=1024, tk=2048):
    if mode == "nn":
        (M, K), (K2, N) = a.shape, b.shape
    elif mode == "nt":
        (M, K), (N, K2) = a.shape, b.shape
    else:
        (K, M), (K2, N) = a.shape, b.shape
    assert K == K2, (a.shape, b.shape, mode)
    tm, tn = min(tm, M), min(tn, N)
    tk = next((t for t in (tk, 1024, 512, 256, LANES) if t <= K and K % t == 0), K)
    assert M % tm == 0 and N % tn == 0 and K % tk == 0, (M, N, K, tm, tn, tk)
    nk = K // tk
    n_ex, n_out = len(extras), len(out_dtypes)
    dims = {"nn": ((1,), (0,)), "nt": ((1,), (1,)), "tn": ((0,), (0,))}[mode]

    def finish(res, ex_refs, out_refs):
        outs = epilogue(res, *[r[...] for r in ex_refs]) if epilogue is not None else (res,)
        for o_ref, o in zip(out_refs, outs):
            o_ref[...] = o.astype(o_ref.dtype)

    def body(a_ref, b_ref, *rest):
        ex_refs, out_refs = rest[:n_ex], rest[n_ex:n_ex + n_out]
        part = lax.dot_general(a_ref[...], b_ref[...], (dims, ((), ())), preferred_element_type=F32)
        if nk == 1:
            finish(part, ex_refs, out_refs)
            return
        acc = rest[-1]
        k = pl.program_id(2)

        @pl.when(k == 0)
        def _():
            acc[...] = part

        @pl.when(k > 0)
        def _():
            acc[...] += part

        @pl.when(k == nk - 1)
        def _():
            finish(acc[...], ex_refs, out_refs)

    a_spec = pl.BlockSpec((tk, tm), lambda i, j, k: (k, i)) if mode == "tn" else pl.BlockSpec((tm, tk), lambda i, j, k: (i, k))
    b_spec = pl.BlockSpec((tn, tk), lambda i, j, k: (j, k)) if mode == "nt" else pl.BlockSpec((tk, tn), lambda i, j, k: (k, j))
    mn_spec = pl.BlockSpec((tm, tn), lambda i, j, k: (i, j))
    outs = _pcall(
        body, name=name, grid=(M // tm, N // tn, nk),
        in_specs=[a_spec, b_spec] + [mn_spec] * n_ex,
        out_specs=[mn_spec] * n_out,
        out_shape=[jax.ShapeDtypeStruct((M, N), dt) for dt in out_dtypes],
        scratch_shapes=[pltpu.VMEM((tm, tn), F32)] if nk > 1 else [],
        compiler_params=_cp("parallel", "parallel", "arbitrary"),
    )(a, b, *extras)
    return outs[0] if n_out == 1 else outs


def _rowwise(fn, T, tb, ins, outs, name, scratch=()):
    tb = min(tb, T)
    assert T % tb == 0 and (tb % SUB == 0 or tb == T)
    nblk = T // tb
    r8 = tb // SUB
    in_specs, arrs = [], []
    for spec in ins:
        kind, arr = spec[0], spec[1]
        arrs.append(arr)
        if kind == "full":
            nd = arr.ndim
            in_specs.append(pl.BlockSpec(arr.shape, lambda i, nd=nd: (0,) * nd))
            continue
        off, w = spec[2], spec[3]
        assert off % w == 0 and arr.shape[0] == T, (name, off, w, arr.shape)
        cb = off // w
        if kind == "row":
            in_specs.append(pl.BlockSpec((tb, w), lambda i, cb=cb: (i, cb)))
        elif kind == "prev":
            in_specs.append(pl.BlockSpec((SUB, w), lambda i, cb=cb: (jnp.maximum(i * r8 - 1, 0), cb)))
        else:
            in_specs.append(pl.BlockSpec((SUB, w), lambda i, cb=cb: (jnp.minimum((i + 1) * r8, T // SUB - 1), cb)))
    out_specs, out_shapes, is_acc = [], [], []
    for spec in outs:
        if spec[0] == "row":
            out_specs.append(pl.BlockSpec((tb, spec[1]), lambda i: (i, 0)))
            out_shapes.append(jax.ShapeDtypeStruct((T, spec[1]), spec[2]))
            is_acc.append(False)
        else:
            nd = len(spec[1])
            out_specs.append(pl.BlockSpec(spec[1], lambda i, nd=nd: (0,) * nd))
            out_shapes.append(jax.ShapeDtypeStruct(spec[1], spec[2]))
            is_acc.append(True)
    n_in, n_out = len(ins), len(outs)
    seq = any(is_acc) or len(scratch) > 0

    def body(*refs):
        in_refs, out_refs, scr = refs[:n_in], refs[n_in:n_in + n_out], refs[n_in + n_out:]
        i = pl.program_id(0)
        vals = fn(i, nblk, *[r[...] for r in in_refs], *scr)
        for o_ref, v, acc in zip(out_refs, vals, is_acc):
            if acc:
                @pl.when(i == 0)
                def _(o_ref=o_ref):
                    o_ref[...] = jnp.zeros_like(o_ref)

                o_ref[...] += v.astype(o_ref.dtype)
            else:
                o_ref[...] = v.astype(o_ref.dtype)

    res = _pcall(
        body, name=name, grid=(nblk,), in_specs=in_specs, out_specs=out_specs, out_shape=out_shapes,
        scratch_shapes=list(scratch), compiler_params=_cp("arbitrary" if seq else "parallel"),
    )(*arrs)
    return res


def _heads(x, width):
    return [x[:, h * width:(h + 1) * width] for h in range(x.shape[1] // width)]


def _cat(xs):
    return xs[0] if len(xs) == 1 else jnp.concatenate(xs, axis=1)


def _rms_fwd(x, g, name, tb=512):
    T, D = x.shape
    return _rowwise(lambda i, n, xv, gv: (_rms(xv, gv),), T, tb,
                    [("row", x, 0, D), ("full", g)], [("row", D, BF16)], name)[0]


def _rms_bwd(x, g, dh, dres, name, tb=256):
    T, D = x.shape

    def fn(i, n, xv, gv, dhv, drv):
        _, vjp = jax.vjp(_rms, xv, gv)
        dx, dg = vjp(dhv)
        tot = drv + dx
        return tot, tot, dg

    return _rowwise(fn, T, tb, [("row", x, 0, D), ("full", g), ("row", dh, 0, D), ("row", dres, 0, D)],
                    [("row", D, F32), ("row", D, BF16), ("acc", (1, D), F32)], name)


def _rms_dg(x, g, dh, name, tb=256):
    T, D = x.shape

    def fn(i, n, xv, gv, dhv):
        _, vjp = jax.vjp(lambda gg: _rms(xv, gg), gv)
        return vjp(dhv)

    return _rowwise(fn, T, tb, [("row", x, 0, D), ("full", g), ("row", dh, 0, D)], [("acc", (1, D), F32)], name)[0]


def _shift_down(x, halo, s, first):
    if s == 0:
        return x
    tb, c = x.shape
    xr = pltpu.roll(x, s, 0)
    hr = jnp.where(first, 0.0, pltpu.roll(halo, s, 0))
    hfull = hr if tb == SUB else jnp.concatenate([hr, jnp.zeros((tb - SUB, c), x.dtype)], axis=0)
    row = lax.broadcasted_iota(jnp.int32, x.shape, 0)
    return jnp.where(row < s, hfull, xr)


def _shift_up(z, halo, s, last):
    if s == 0:
        return z
    tb, c = z.shape
    zr = pltpu.roll(z, tb - s, 0)
    hr = jnp.where(last, 0.0, pltpu.roll(halo, SUB - s, 0))
    hfull = hr if tb == SUB else jnp.concatenate([jnp.zeros((tb - SUB, c), z.dtype), hr], axis=0)
    row = lax.broadcasted_iota(jnp.int32, z.shape, 0)
    return jnp.where(row >= tb - s, hfull, zr)


def _conv_pre(x, halo, cw, first):
    xs = [_shift_down(x, halo, s, first) for s in range(CONV_K)]
    y = cw[0:1, :] * xs[3]
    for i in range(1, CONV_K):
        y = y + cw[i:i + 1, :] * xs[CONV_K - 1 - i]
    return y, xs


def _qk_post(y, scale):
    a = _silu(y)
    return a * lax.rsqrt(jnp.sum(a * a, axis=-1, keepdims=True) + EPS) * scale


def _small_fn(s, alog, dtb, bf, gh, fh):
    lane = lax.broadcasted_iota(jnp.int32, s.shape, 1)
    beta = _sigmoid(s)
    g = -jnp.exp(alog) * _softplus(s + dtb)
    lf = -_softplus(-(s + bf))
    return jnp.where(lane < gh, beta, jnp.where(lane < 2 * gh, g, jnp.where(lane < 2 * gh + fh, lf, 0.0)))


def _gdn_prep(proj, offs, cws, pvecs, GH, FH, GW, tb=256):
    T = proj.shape[0]
    tb = min(tb, T)
    qscale = HEAD ** -0.5

    def fn(i, n, xq, hq, xk, hk, xv, hv, cwq, cwk, cwv, s, alog, dtb, bf, carry):
        first = i == 0
        yq, _ = _conv_pre(xq, hq, cwq, first)
        yk, _ = _conv_pre(xk, hk, cwk, first)
        yv, _ = _conv_pre(xv, hv, cwv, first)
        qn = _cat([_qk_post(y, qscale) for y in _heads(yq, HEAD)])
        kn = _cat([_qk_post(y, 1.0) for y in _heads(yk, HEAD)])
        vc = _silu(yv)
        gsm = _small_fn(s, alog, dtb, bf, GH, FH)

        @pl.when(first)
        def _():
            carry[...] = jnp.zeros_like(carry)

        ri = lax.broadcasted_iota(jnp.int32, (tb, tb), 0)
        ci = lax.broadcasted_iota(jnp.int32, (tb, tb), 1)
        cum = _dot((ri >= ci).astype(F32), gsm, HI) + carry[0:1, :]
        carry[...] += _dot(jnp.ones((SUB, tb), F32), gsm, HI)
        in_chunk = (ri >= ci) & ((ri >> CHUNK_SHIFT) == (ci >> CHUNK_SHIFT))
        lane = lax.broadcasted_iota(jnp.int32, gsm.shape, 1)
        gbm = jnp.where(lane < GH, gsm, _dot(in_chunk.astype(F32), gsm, HI))
        return qn, kn, vc, cum, gbm

    ins = []
    for key in ("q", "k", "v"):
        ins += [("row", proj, offs[key], GW), ("prev", proj, offs[key], GW)]
    ins += [("full", c) for c in cws] + [("row", proj, offs["small"], LANES)] + [("full", p) for p in pvecs]
    outs = [("row", GW, F32)] * 3 + [("row", LANES, F32)] * 2
    return _rowwise(fn, T, tb, ins, outs, "gdn_prep", scratch=[pltpu.VMEM((SUB, LANES), F32)])


def _gdn_prep_bwd_a(proj, offs, cws, pvecs, cts, dgsm_scan, dlf_sm, GH, FH, GW, tb=256):
    T = proj.shape[0]
    qscale = HEAD ** -0.5

    def one(x, halo, cw, ct, first, post):
        y, xs = _conv_pre(x, halo, cw, first)
        if post is None:
            _, vjp = jax.vjp(_silu, y)
            dy = vjp(ct)[0]
        else:
            dys = []
            for yh, cth in zip(_heads(y, HEAD), _heads(ct, HEAD)):
                _, vjp = jax.vjp(lambda t: _qk_post(t, post), yh)
                dys.append(vjp(cth)[0])
            dy = _cat(dys)
        row = lax.broadcasted_iota(jnp.int32, (SUB, x.shape[1]), 0)
        dcw = jnp.zeros((SUB, x.shape[1]), F32)
        for i in range(CONV_K):
            dcw = dcw + jnp.where(row == i, jnp.sum(dy * xs[CONV_K - 1 - i], axis=0, keepdims=True), 0.0)
        return dy, dcw

    def fn(i, n, xq, hq, xk, hk, xv, hv, cwq, cwk, cwv, cq, ck, cv, s, alog, dtb, bf, d1, d2):
        first = i == 0
        dyq, dcq = one(xq, hq, cwq, cq, first, qscale)
        dyk, dck = one(xk, hk, cwk, ck, first, 1.0)
        dyv, dcv = one(xv, hv, cwv, cv, first, None)
        tb_ = d1.shape[0]
        ri = lax.broadcasted_iota(jnp.int32, (tb_, tb_), 0)
        ci = lax.broadcasted_iota(jnp.int32, (tb_, tb_), 1)
        later = (ci >= ri) & ((ri >> CHUNK_SHIFT) == (ci >> CHUNK_SHIFT))
        lane = lax.broadcasted_iota(jnp.int32, d1.shape, 1)
        d1 = jnp.where(lane < GH, d1, _dot(later.astype(F32), d1, HI))
        _, vjp = jax.vjp(lambda a, b, c, d: _small_fn(a, b, c, d, GH, FH), s, alog, dtb, bf)
        ds, dalog, ddtb, dbf = vjp(d1 + d2)
        return dyq, dyk, dyv, dcq, dck, dcv, ds, dalog, ddtb, dbf

    ins = []
    for key in ("q", "k", "v"):
        ins += [("row", proj, offs[key], GW), ("prev", proj, offs[key], GW)]
    ins += [("full", c) for c in cws] + [("row", c, 0, GW) for c in cts]
    ins += [("row", proj, offs["small"], LANES)] + [("full", p) for p in pvecs]
    ins += [("row", dgsm_scan, 0, LANES), ("row", dlf_sm, 0, LANES)]
    outs = [("row", GW, F32)] * 3 + [("acc", (SUB, GW), F32)] * 3 + [("row", LANES, BF16)] + [("acc", (1, LANES), F32)] * 3
    return _rowwise(fn, T, tb, ins, outs, "gdn_prep_bwd_a")


def _gdn_prep_bwd_b(dys, cws, GW, tb=256):
    T = dys[0].shape[0]

    def fn(i, n, dq, nq, dk, nk, dv, nv, cwq, cwk, cwv):
        last = i == n - 1
        res = []
        for dy, nh, cw in ((dq, nq, cwq), (dk, nk, cwk), (dv, nv, cwv)):
            dx = cw[CONV_K - 1:CONV_K, :] * dy
            for t in range(CONV_K - 1):
                dx = dx + cw[t:t + 1, :] * _shift_up(dy, nh, CONV_K - 1 - t, last)
            res.append(dx)
        return tuple(res)

    ins = []
    for dy in dys:
        ins += [("row", dy, 0, GW), ("next", dy, 0, GW)]
    ins += [("full", c) for c in cws]
    return _rowwise(fn, T, tb, ins, [("row", GW, BF16)] * 3, "gdn_prep_bwd_b")


def _gdn_chunks(qs, ks, vs, gams, bcols, s0s):
    c, d = qs[0].shape
    hs = range(len(qs))
    ri = lax.broadcasted_iota(jnp.int32, (c, c), 0)
    ci = lax.broadcasted_iota(jnp.int32, (c, c), 1)
    incl, strict = ri >= ci, ri > ci
    eye = (ri == ci).astype(F32)
    ones_cc = jnp.ones((c, c), F32)
    rows = lax.broadcasted_iota(jnp.int32, (c, 1), 0)
    b16 = (ri >> 4) == (ci >> 4)
    b32 = (ri >> 5) == (ci >> 5)
    gam_cc = [gams[h] * ones_cc for h in hs]
    gam_t = [_dot_nt(eye, gam_cc[h], HI) for h in hs]
    glast = [jnp.sum(jnp.where(rows == c - 1, gams[h], 0.0), axis=0, keepdims=True) for h in hs]
    dec_i = [jnp.where(incl, jnp.exp(jnp.where(incl, gam_cc[h] - gam_t[h], 0.0)), 0.0) for h in hs]
    kk = [_dot_nt(ks[h], ks[h]) for h in hs]
    m = [bcols[h] * kk[h] * jnp.where(strict, dec_i[h], 0.0) for h in hs]
    m32 = [jnp.where(b32 & ~b16, m[h], 0.0) for h in hs]
    m64 = [jnp.where(b32, 0.0, m[h]) for h in hs]
    mp = [jnp.where(b16, m[h], 0.0) for h in hs]
    p = [eye - mp[h] for h in hs]
    for _ in range(3):
        mp = [_dot(mp[h], mp[h], SOLVE_PREC) for h in hs]
        p = [p[h] + _dot(p[h], mp[h], SOLVE_PREC) for h in hs]
    t = [_dot(p[h], m32[h], SOLVE_PREC) for h in hs]
    p = [p[h] - _dot(t[h], p[h], SOLVE_PREC) for h in hs]
    t = [_dot(p[h], m64[h], SOLVE_PREC) for h in hs]
    ainv = [p[h] - _dot(t[h], p[h], SOLVE_PREC) for h in hs]
    eg = [jnp.exp(gams[h]) for h in hs]
    w = [_dot(ainv[h], (bcols[h] * eg[h]) * ks[h], SOLVE_PREC) for h in hs]
    u0 = [_dot(ainv[h], bcols[h] * vs[h], SOLVE_PREC) for h in hs]
    qk = [_dot_nt(qs[h], ks[h]) * dec_i[h] for h in hs]
    u = [u0[h] - _dot(w[h], s0s[h]) for h in hs]
    o = [_dot(qs[h] * eg[h], s0s[h]) + _dot(qk[h], u[h]) for h in hs]
    s1 = [jnp.exp(glast[h]) * s0s[h] + _dot_tn(ks[h] * jnp.exp(glast[h] - gams[h]), u[h]) for h in hs]
    return tuple(o), tuple(s1)


def _lane_col(x, lane_idx):
    lane = lax.broadcasted_iota(jnp.int32, x.shape, 1)
    return jnp.sum(jnp.where(lane == lane_idx, x, 0.0), axis=1, keepdims=True)


def _gdn_scan_fwd(qn, kn, vc, gsm, GH):
    T, GW = qn.shape
    nc = T // CHUNK

    def body(q_ref, k_ref, v_ref, g_ref, o_ref, sall_ref, s_scr):
        @pl.when(pl.program_id(0) == 0)
        def _():
            s_scr[...] = jnp.zeros_like(s_scr)

        gs = g_ref[...]
        sls = [slice(h * HEAD, (h + 1) * HEAD) for h in range(GH)]
        s0s = tuple(s_scr[h] for h in range(GH))
        os_, s1s = _gdn_chunks(tuple(q_ref[:, sl] for sl in sls), tuple(k_ref[:, sl] for sl in sls),
                               tuple(v_ref[:, sl] for sl in sls), tuple(_lane_col(gs, GH + h) for h in range(GH)),
                               tuple(_lane_col(gs, h) for h in range(GH)), s0s)
        for h in range(GH):
            sall_ref[0, h] = s0s[h]
            o_ref[:, sls[h]] = os_[h]
            s_scr[h] = s1s[h]

    row = pl.BlockSpec((CHUNK, GW), lambda i: (i, 0))
    return _pcall(
        body, name="gdn_scan_fwd", grid=(nc,),
        in_specs=[row, row, row, pl.BlockSpec((CHUNK, LANES), lambda i: (i, 0))],
        out_specs=[row, pl.BlockSpec((1, GH, HEAD, HEAD), lambda i: (i, 0, 0, 0))],
        out_shape=[jax.ShapeDtypeStruct((T, GW), F32), jax.ShapeDtypeStruct((nc, GH, HEAD, HEAD), F32)],
        scratch_shapes=[pltpu.VMEM((GH, HEAD, HEAD), F32)],
        compiler_params=_cp("arbitrary"),
    )(qn, kn, vc, gsm)


def _gdn_scan_bwd(qn, kn, vc, gsm, sall, do, GH):
    T, GW = qn.shape
    nc = T // CHUNK

    def body(q_ref, k_ref, v_ref, g_ref, sall_ref, do_ref, dq_ref, dk_ref, dv_ref, dg_ref, ds_scr):
        @pl.when(pl.program_id(0) == 0)
        def _():
            ds_scr[...] = jnp.zeros_like(ds_scr)

        gs = g_ref[...]
        lane = lax.broadcasted_iota(jnp.int32, gs.shape, 1)
        sls = [slice(h * HEAD, (h + 1) * HEAD) for h in range(GH)]
        _, vjp = jax.vjp(_gdn_chunks, tuple(q_ref[:, sl] for sl in sls), tuple(k_ref[:, sl] for sl in sls),
                         tuple(v_ref[:, sl] for sl in sls), tuple(_lane_col(gs, GH + h) for h in range(GH)),
                         tuple(_lane_col(gs, h) for h in range(GH)), tuple(sall_ref[0, h] for h in range(GH)))
        dq, dk, dv, dgc, dbc, ds0 = vjp((tuple(do_ref[:, sl] for sl in sls), tuple(ds_scr[h] for h in range(GH))))
        dgs = jnp.zeros_like(gs)
        for h in range(GH):
            dq_ref[:, sls[h]] = dq[h]
            dk_ref[:, sls[h]] = dk[h]
            dv_ref[:, sls[h]] = dv[h]
            dgs = dgs + jnp.where(lane == h, dbc[h], 0.0) + jnp.where(lane == GH + h, dgc[h], 0.0)
            ds_scr[h] = ds0[h]
        dg_ref[...] = dgs

    row = pl.BlockSpec((CHUNK, GW), lambda i: (nc - 1 - i, 0))
    sm = pl.BlockSpec((CHUNK, LANES), lambda i: (nc - 1 - i, 0))
    return _pcall(
        body, name="gdn_scan_bwd", grid=(nc,),
        in_specs=[row, row, row, sm, pl.BlockSpec((1, GH, HEAD, HEAD), lambda i: (nc - 1 - i, 0, 0, 0)), row],
        out_specs=[row, row, row, sm],
        out_shape=[jax.ShapeDtypeStruct((T, GW), F32)] * 3 + [jax.ShapeDtypeStruct((T, LANES), F32)],
        scratch_shapes=[pltpu.VMEM((GH, HEAD, HEAD), F32)],
        compiler_params=_cp("arbitrary"),
    )(qn, kn, vc, gsm, sall, do)


def _gdn_post_fn(o, z, g):
    return _rms(o, g) * _silu(z)


def _gdn_post(o, proj, off_z, g, GW, tb=512):
    T = o.shape[0]

    def fn(i, n, ov, zv, gv):
        return (_cat([_gdn_post_fn(a, b, gv) for a, b in zip(_heads(ov, HEAD), _heads(zv, HEAD))]),)

    return _rowwise(fn, T, tb, [("row", o, 0, GW), ("row", proj, off_z, GW), ("full", g)], [("row", GW, BF16)], "gdn_post")[0]


def _gdn_post_bwd(o, proj, off_z, g, dout, GW, tb=256):
    T = o.shape[0]

    def fn(i, n, ov, zv, gv, dv):
        dos, dzs, dg = [], [], jnp.zeros_like(gv)
        for a, b, c in zip(_heads(ov, HEAD), _heads(zv, HEAD), _heads(dv, HEAD)):
            _, vjp = jax.vjp(_gdn_post_fn, a, b, gv)
            da, db, dgh = vjp(c)
            dos.append(da)
            dzs.append(db)
            dg = dg + dgh
        return _cat(dos), _cat(dzs), dg

    return _rowwise(fn, T, tb, [("row", o, 0, GW), ("row", proj, off_z, GW), ("full", g), ("row", dout, 0, GW)],
                    [("row", GW, F32), ("row", GW, BF16), ("acc", (1, HEAD), F32)], "gdn_post_bwd")


def _fox_prep(proj, offs, gq, gk, FW, tb=512):
    T = proj.shape[0]

    def fn(i, n, q, k, v, gqv, gkv):
        return (_cat([_rms(a, gqv) * (HEAD ** -0.5) for a in _heads(q, HEAD)]),
                _cat([_rms(a, gkv) for a in _heads(k, HEAD)]), v)

    return _rowwise(fn, T, tb, [("row", proj, offs["fq"], FW), ("row", proj, offs["fk"], FW), ("row", proj, offs["fv"], FW),
                                ("full", gq), ("full", gk)], [("row", FW, BF16)] * 3, "fox_prep")


def _fox_prep_bwd(proj, offs, gq, gk, dq, dk, dv, FW, tb=256):
    T = proj.shape[0]

    def fn(i, n, q, k, gqv, gkv, dqv, dkv, dvv):
        res = []
        for x, g, d in ((q, gqv, dqv), (k, gkv, dkv)):
            dxs, dg = [], jnp.zeros_like(g)
            for a, c in zip(_heads(x, HEAD), _heads(d, HEAD)):
                _, vjp = jax.vjp(_rms, a, g)
                da, dgh = vjp(c)
                dxs.append(da)
                dg = dg + dgh
            res += [_cat(dxs), dg]
        return res[0], res[2], dvv, res[1], res[3]

    return _rowwise(fn, T, tb, [("row", proj, offs["fq"], FW), ("row", proj, offs["fk"], FW), ("full", gq), ("full", gk),
                                ("row", dq, 0, FW), ("row", dk, 0, FW), ("row", dv, 0, FW)],
                    [("row", FW, BF16)] * 3 + [("acc", (1, HEAD), F32)] * 2, "fox_prep_bwd")


def _sub_row(x, sub_idx):
    sub = lax.broadcasted_iota(jnp.int32, x.shape, 0)
    return jnp.sum(jnp.where(sub == sub_idx, x, 0.0), axis=0, keepdims=True)


def _causal_pairs(nb, key_major):
    if key_major:
        pairs = [(i, j) for j in range(nb) for i in range(j, nb)]
    else:
        pairs = [(i, j) for i in range(nb) for j in range(i + 1)]
    return (jnp.asarray(np.array([p[0] for p in pairs], np.int32)),
            jnp.asarray(np.array([p[1] for p in pairs], np.int32)))


def _fox_scores(q, k, ck, diagonal):
    s = _dot_nt(q, k) - ck
    if diagonal:
        row = lax.broadcasted_iota(jnp.int32, s.shape, 0)
        col = lax.broadcasted_iota(jnp.int32, s.shape, 1)
        s = jnp.where(row >= col, s, NEG)
    return s


def _flash_fwd(qb, kb, vb, cumr, FH, blk):
    T, FW = qb.shape
    blk = min(blk, T)
    nb = T // blk
    qi_arr, kj_arr = _causal_pairs(nb, False)

    hps = FLASH_HEADS_PER_STEP
    assert FH % hps == 0
    us = range(hps)
    sl = [slice(u * HEAD, (u + 1) * HEAD) for u in us]

    def body(qi_ref, kj_ref, q_ref, k_ref, v_ref, ck_ref, ob_ref, lse_ref, m_scr, l_scr, acc):
        t = pl.program_id(1)
        qi, kj = qi_ref[t], kj_ref[t]

        @pl.when(kj == 0)
        def _():
            m_scr[...] = jnp.full_like(m_scr, NEG)
            l_scr[...] = jnp.zeros_like(l_scr)
            acc[...] = jnp.zeros_like(acc)

        def update(diagonal):
            s = [_fox_scores(q_ref[:, sl[u]], k_ref[:, sl[u]], ck_ref[u], diagonal) for u in us]
            m_old = [m_scr[u] for u in us]
            m_new = [jnp.maximum(m_old[u], jnp.max(s[u], axis=1, keepdims=True)) for u in us]
            alpha = [jnp.exp(m_old[u] - m_new[u]) for u in us]
            p = [jnp.exp(s[u] - m_new[u]) for u in us]
            pv = [_dot(p[u].astype(BF16), v_ref[:, sl[u]]) for u in us]
            for u in us:
                l_scr[u] = alpha[u] * l_scr[u] + jnp.sum(p[u], axis=1, keepdims=True)
                acc[:, sl[u]] = alpha[u] * acc[:, sl[u]] + pv[u]
                m_scr[u] = m_new[u]

        @pl.when(kj < qi)
        def _():
            update(False)

        @pl.when(kj == qi)
        def _():
            update(True)
            for u in us:
                ob_ref[:, sl[u]] = (acc[:, sl[u]] / l_scr[u]).astype(BF16)
                lse_ref[u] = m_scr[u] + jnp.log(l_scr[u])

    qs = pl.BlockSpec((blk, hps * HEAD), lambda g, t, qr, kr: (qr[t], g))
    ks = pl.BlockSpec((blk, hps * HEAD), lambda g, t, qr, kr: (kr[t], g))
    col = pl.BlockSpec((hps, blk, 1), lambda g, t, qr, kr: (g, qr[t], 0))
    gs = pltpu.PrefetchScalarGridSpec(
        num_scalar_prefetch=2, grid=(FH // hps, qi_arr.shape[0]),
        in_specs=[qs, ks, ks, pl.BlockSpec((hps, 1, blk), lambda g, t, qr, kr: (g, 0, kr[t]))],
        out_specs=[qs, col],
        scratch_shapes=[pltpu.VMEM((hps, blk, 1), F32), pltpu.VMEM((hps, blk, 1), F32), pltpu.VMEM((blk, hps * HEAD), F32)])
    return _pcall(
        body, name="flash_fwd", grid_spec=gs,
        out_shape=[jax.ShapeDtypeStruct((T, FW), BF16), jax.ShapeDtypeStruct((FH, T, 1), F32)],
        compiler_params=_cp("parallel", "arbitrary"),
    )(qi_arr, kj_arr, qb, kb, vb, cumr)


def _flash_bwd_q(qb, kb, vb, cumr, lse, do, dl, FH, blk):
    T, FW = qb.shape
    blk = min(blk, T)
    nb = T // blk
    qi_arr, kj_arr = _causal_pairs(nb, False)
    want_dq = dl is not None
    hps = FLASH_HEADS_PER_STEP
    assert FH % hps == 0
    us = range(hps)
    sl = [slice(u * HEAD, (u + 1) * HEAD) for u in us]

    def body(qi_ref, kj_ref, q_ref, k_ref, v_ref, ck_ref, lse_ref, do_ref, *rest):
        if want_dq:
            dl_ref, dq_ref, rs_ref, acc, rs_acc = rest
        else:
            dl_ref, acc = rest
        t = pl.program_id(1)
        qi, kj = qi_ref[t], kj_ref[t]

        @pl.when(kj == 0)
        def _():
            acc[...] = jnp.zeros_like(acc)
            if want_dq:
                rs_acc[...] = jnp.zeros_like(rs_acc)

        def update(diagonal):
            s = [_fox_scores(q_ref[:, sl[u]], k_ref[:, sl[u]], ck_ref[u], diagonal) for u in us]
            p = [jnp.exp(s[u] - lse_ref[u]) for u in us]
            dp = [_dot_nt(do_ref[:, sl[u]].astype(BF16), v_ref[:, sl[u]]) for u in us]
            if want_dq:
                ds = [p[u] * (dp[u] - dl_ref[u]) for u in us]
                dqp = [_dot(ds[u].astype(BF16), k_ref[:, sl[u]]) for u in us]
                for u in us:
                    acc[:, sl[u]] += dqp[u]
                    rs_acc[u] += jnp.sum(ds[u], axis=1, keepdims=True)
            else:
                for u in us:
                    acc[u] += jnp.sum(p[u] * dp[u], axis=1, keepdims=True)

        @pl.when(kj < qi)
        def _():
            update(False)

        @pl.when(kj == qi)
        def _():
            update(True)
            if want_dq:
                dq_ref[...] = acc[...] * (HEAD ** -0.5)
                rs_ref[...] = rs_acc[...]
            else:
                dl_ref[...] = acc[...]

    qs = pl.BlockSpec((blk, hps * HEAD), lambda g, t, qr, kr: (qr[t], g))
    ks = pl.BlockSpec((blk, hps * HEAD), lambda g, t, qr, kr: (kr[t], g))
    col = pl.BlockSpec((hps, blk, 1), lambda g, t, qr, kr: (g, qr[t], 0))
    in_specs = [qs, ks, ks, pl.BlockSpec((hps, 1, blk), lambda g, t, qr, kr: (g, 0, kr[t])), col, qs]
    args = [qi_arr, kj_arr, qb, kb, vb, cumr, lse, do]
    colshape = jax.ShapeDtypeStruct((FH, T, 1), F32)
    if want_dq:
        gs = pltpu.PrefetchScalarGridSpec(
            num_scalar_prefetch=2, grid=(FH // hps, qi_arr.shape[0]), in_specs=in_specs + [col], out_specs=[qs, col],
            scratch_shapes=[pltpu.VMEM((blk, hps * HEAD), F32), pltpu.VMEM((hps, blk, 1), F32)])
        return _pcall(body, name="flash_bwd_dq", grid_spec=gs,
                      out_shape=[jax.ShapeDtypeStruct((T, FW), F32), colshape],
                      compiler_params=_cp("parallel", "arbitrary"))(*args, dl)
    gs = pltpu.PrefetchScalarGridSpec(
        num_scalar_prefetch=2, grid=(FH // hps, qi_arr.shape[0]), in_specs=in_specs, out_specs=col,
        scratch_shapes=[pltpu.VMEM((hps, blk, 1), F32)])
    return _pcall(body, name="flash_bwd_rowterm", grid_spec=gs, out_shape=colshape,
                  compiler_params=_cp("parallel", "arbitrary"))(*args)


def _flash_bwd_dkv(qb, kb, vb, cumc, lse_row, dl_row, do, FH, blk):
    T, FW = qb.shape
    blk = min(blk, T)
    nb = T // blk

    qi_arr, kj_arr = _causal_pairs(nb, True)

    hps = FLASH_HEADS_PER_STEP
    assert FH % hps == 0
    us = range(hps)
    sl = [slice(u * HEAD, (u + 1) * HEAD) for u in us]

    def body(qi_ref, kj_ref, q_ref, k_ref, v_ref, ck_ref, lse_ref, dl_ref, do_ref,
             dk_ref, dv_ref, dc_ref, dk_acc, dv_acc, dc_acc):
        t = pl.program_id(1)
        qi, kj = qi_ref[t], kj_ref[t]

        def update(diagonal, first):
            st = [_dot_nt(k_ref[:, sl[u]], q_ref[:, sl[u]]) - ck_ref[u] for u in us]
            if diagonal:
                krow = lax.broadcasted_iota(jnp.int32, st[0].shape, 0)
                qcol = lax.broadcasted_iota(jnp.int32, st[0].shape, 1)
                st = [jnp.where(qcol >= krow, st[u], NEG) for u in us]
            pt = [jnp.exp(st[u] - lse_ref[u]) for u in us]
            dob = [do_ref[:, sl[u]].astype(BF16) for u in us]
            dpt = [_dot_nt(v_ref[:, sl[u]], dob[u]) for u in us]
            dst = [pt[u] * (dpt[u] - dl_ref[u]) for u in us]
            dk = [_dot(dst[u].astype(BF16), q_ref[:, sl[u]]) for u in us]
            dv = [_dot(pt[u].astype(BF16), dob[u]) for u in us]
            for u in us:
                dc = -jnp.sum(dst[u], axis=1, keepdims=True)
                if first:
                    dk_acc[:, sl[u]] = dk[u]
                    dv_acc[:, sl[u]] = dv[u]
                    dc_acc[u] = dc
                else:
                    dk_acc[:, sl[u]] += dk[u]
                    dv_acc[:, sl[u]] += dv[u]
                    dc_acc[u] += dc

        @pl.when(qi == kj)
        def _():
            update(True, True)

        @pl.when(qi > kj)
        def _():
            update(False, False)

        @pl.when(qi == nb - 1)
        def _():
            dk_ref[...] = dk_acc[...]
            dv_ref[...] = dv_acc[...]
            dc_ref[...] = dc_acc[...]

    ks = pl.BlockSpec((blk, hps * HEAD), lambda g, t, qr, kr: (kr[t], g))
    qs = pl.BlockSpec((blk, hps * HEAD), lambda g, t, qr, kr: (qr[t], g))
    rowq = pl.BlockSpec((hps, 1, blk), lambda g, t, qr, kr: (g, 0, qr[t]))
    colk = pl.BlockSpec((hps, blk, 1), lambda g, t, qr, kr: (g, kr[t], 0))
    gs = pltpu.PrefetchScalarGridSpec(
        num_scalar_prefetch=2, grid=(FH // hps, qi_arr.shape[0]),
        in_specs=[qs, ks, ks, colk, rowq, rowq, qs],
        out_specs=[ks, ks, colk],
        scratch_shapes=[pltpu.VMEM((blk, hps * HEAD), F32), pltpu.VMEM((blk, hps * HEAD), F32),
                        pltpu.VMEM((hps, blk, 1), F32)])
    return _pcall(
        body, name="flash_bwd_dkv", grid_spec=gs,
        out_shape=[jax.ShapeDtypeStruct((T, FW), F32), jax.ShapeDtypeStruct((T, FW), F32),
                   jax.ShapeDtypeStruct((FH, T, 1), F32)],
        compiler_params=_cp("parallel", "arbitrary"),
    )(qi_arr, kj_arr, qb, kb, vb, cumc, lse_row, dl_row, do)


def _rev_cumsum_rows(r1, r2, tb=512):
    H, T = r1.shape
    tb = min(tb, T)
    nb = T // tb

    def body(r1_ref, r2_ref, o_ref, carry):
        @pl.when(pl.program_id(0) == 0)
        def _():
            carry[...] = jnp.zeros_like(carry)

        rv = r1_ref[...] + r2_ref[...]
        si = lax.broadcasted_iota(jnp.int32, (tb, tb), 0)
        ti = lax.broadcasted_iota(jnp.int32, (tb, tb), 1)
        o_ref[...] = _dot(rv, (si >= ti).astype(F32), HI) + carry[...]
        carry[...] += jnp.sum(rv, axis=1, keepdims=True)

    spec = pl.BlockSpec((H, tb), lambda i: (0, nb - 1 - i))
    return _pcall(body, name="rev_cumsum", grid=(nb,), in_specs=[spec, spec], out_specs=spec,
                  out_shape=jax.ShapeDtypeStruct((H, T), F32), scratch_shapes=[pltpu.VMEM((H, 1), F32)],
                  compiler_params=_cp("arbitrary"))(r1, r2)


def _mem_head(q, k, v, gq, gk):
    logits = _dot_nt(_rms(q, gq), _rms(k, gk)) * (MEM_DH ** -0.5)
    mx = jnp.max(logits, axis=1, keepdims=True)
    e = jnp.exp(logits - mx)
    p = e / jnp.sum(e, axis=1, keepdims=True)
    return _dot(p, v)


def _mem_attn(proj, off_q, kv, gq, gk, MW, tb=512):
    T = proj.shape[0]
    MH = MW // MEM_DH

    def fn(i, n, q, kvv, gqv, gkv):
        ks, vs = _heads(kvv[:, :MW], MEM_DH), _heads(kvv[:, MW:], MEM_DH)
        return (_cat([_mem_head(a, b, c, gqv, gkv) for a, b, c in zip(_heads(q, MEM_DH), ks, vs)]),)

    return _rowwise(fn, T, tb, [("row", proj, off_q, MW), ("full", kv), ("full", gq), ("full", gk)],
                    [("row", MW, BF16)], "mem_attn")[0]


def _mem_attn_bwd(proj, off_q, kv, gq, gk, dout, MW, tb=256):
    T = proj.shape[0]
    ML = kv.shape[0]

    def fn(i, n, q, kvv, gqv, gkv, dv):
        ks, vs = _heads(kvv[:, :MW], MEM_DH), _heads(kvv[:, MW:], MEM_DH)
        dqs, dks, dvs = [], [], []
        dgq, dgk = jnp.zeros_like(gqv), jnp.zeros_like(gkv)
        for a, b, c, d in zip(_heads(q, MEM_DH), ks, vs, _heads(dv, MEM_DH)):
            _, vjp = jax.vjp(_mem_head, a, b, c, gqv, gkv)
            da, db, dc, dg1, dg2 = vjp(d)
            dqs.append(da)
            dks.append(db)
            dvs.append(dc)
            dgq, dgk = dgq + dg1, dgk + dg2
        return _cat(dqs), _cat(dks + dvs), dgq, dgk

    return _rowwise(fn, T, tb, [("row", proj, off_q, MW), ("full", kv), ("full", gq), ("full", gk), ("row", dout, 0, MW)],
                    [("row", MW, BF16), ("acc", (ML, 2 * MW), F32), ("acc", (1, MEM_DH), F32), ("acc", (1, MEM_DH), F32)],
                    "mem_attn_bwd")


def _merge_fn(ga, gb, gm, ua, ub, um):
    return _sigmoid(ga) * ua + _sigmoid(gb) * ub + _sigmoid(gm) * um


def _merge(proj, offs, ua, ub, um, D, tb=256):
    T = proj.shape[0]
    ins = [("row", proj, offs[k], D) for k in ("ga", "gb", "gm")] + [("row", u, 0, D) for u in (ua, ub, um)]
    return _rowwise(lambda i, n, *v: (_merge_fn(*v),), T, tb, ins, [("row", D, BF16)], "merge")[0]


def _merge_bwd(proj, offs, ua, ub, um, dy, D, tb=256):
    T = proj.shape[0]

    def fn(i, n, *v):
        _, vjp = jax.vjp(_merge_fn, *v[:6])
        return vjp(v[6])

    ins = [("row", proj, offs[k], D) for k in ("ga", "gb", "gm")] + [("row", u, 0, D) for u in (ua, ub, um)] + [("row", dy, 0, D)]
    return _rowwise(fn, T, tb, ins, [("row", D, BF16)] * 6, "merge_bwd")


def _loss_grad(x2, tgt, tb=256):
    T, D = x2.shape

    def fn(i, n, a, b):
        e = a - b
        part = jnp.sum(jnp.sum(e * e, axis=1, keepdims=True), axis=0, keepdims=True) * (0.5 / D)
        g = e * (1.0 / D)
        return g, g, part + jnp.zeros((SUB, LANES), F32)

    return _rowwise(fn, T, tb, [("row", x2, 0, D), ("row", tgt, 0, D)],
                    [("row", D, F32), ("row", D, BF16), ("acc", (SUB, LANES), F32)], "loss_grad")


def _adamw(w, g, m, v, name, tb=128):
    R, C = w.shape
    c1 = 1.0 / (1.0 - ADAM_B1 ** ADAM_STEP)
    c2 = 1.0 / (1.0 - ADAM_B2 ** ADAM_STEP)

    def fn(i, n, wv, gv, mv, vv):
        mn = ADAM_B1 * mv + (1.0 - ADAM_B1) * gv
        vn = ADAM_B2 * vv + (1.0 - ADAM_B2) * (gv * gv)
        delta = -ADAM_LR * ((mn * c1) / (jnp.sqrt(vn * c2) + ADAM_EPS) + ADAM_WD * wv)
        return delta, mn, vn

    return _rowwise(fn, R, tb, [("row", a, 0, C) for a in (w, g, m, v)], [("row", C, F32)] * 3, name)


def _coords():
    return lax.axis_index("x"), lax.axis_index("y"), lax.axis_index("c")


def _allgather_small(blk):
    m_per, n = blk.shape

    def body(x_ref, out_ref, send_sems, recv_sems, local_sem):
        x, y, c = _coords()
        me, sibling = (x, y, c), (x, y, 1 - c)
        chips = [(1 - x, y), (x, 1 - y), (1 - x, 1 - y)]

        def rows(px, py, pc):
            return out_ref.at[pl.ds((4 * px + 2 * py + pc) * m_per, m_per), :]

        def copy(k, block, to, src=None):
            return pltpu.make_async_remote_copy(
                src_ref=rows(*block) if src is None else src, dst_ref=rows(*block),
                send_sem=send_sems.at[k], recv_sem=recv_sems.at[k], device_id=to, device_id_type=MESH)

        mine = pltpu.make_async_copy(x_ref, rows(*me), local_sem)
        mine.start()
        first = [copy(0, me, sibling, src=x_ref)]
        first += [copy(1 + j, me, (*chip, c), src=x_ref) for j, chip in enumerate(chips)]
        for cp in first:
            cp.start()
        passed = [copy(4 + j, (*chip, c), sibling) for j, chip in enumerate(chips)]
        for j, chip in enumerate(chips):
            copy(1 + j, (*chip, c), me).wait_recv()
            passed[j].start()
        copy(0, sibling, me).wait_recv()
        for j, chip in enumerate(chips):
            copy(4 + j, (*chip, 1 - c), me).wait_recv()
        for cp in first + passed:
            cp.wait_send()
        mine.wait()

    return _pcall(
        body, name="allgather_small", out_shape=jax.ShapeDtypeStruct((8 * m_per, n), blk.dtype),
        in_specs=[pl.BlockSpec(memory_space=pltpu.VMEM)], out_specs=pl.BlockSpec(memory_space=pltpu.VMEM),
        scratch_shapes=[pltpu.SemaphoreType.DMA((7,)), pltpu.SemaphoreType.DMA((7,)), pltpu.SemaphoreType.DMA],
        compiler_params=pltpu.CompilerParams(vmem_limit_bytes=VMEM_LIMIT),
    )(blk)


def _sum8(g, m_per):
    n = g.shape[1]

    def body(g_ref, o_ref):
        acc = g_ref[pl.ds(0, m_per), :]
        for d in range(1, 8):
            acc = acc + g_ref[pl.ds(d * m_per, m_per), :]
        o_ref[...] = acc

    return _pcall(body, name="sum8", out_shape=jax.ShapeDtypeStruct((m_per, n), g.dtype))(g)


_ANY = pl.BlockSpec(memory_space=pl.ANY)


def _allgather_chips(buf):
    nr, w = buf.shape
    half = nr // 2

    def body(in_ref, out_ref, send_sems, recv_sems):
        x, y, c = _coords()
        me = 2 * x + y
        chips = [(1 - x, y), (x, 1 - y), (1 - x, 1 - y)]
        mine_rows = pl.ds(pl.multiple_of(c * half, 16), half)
        other_rows = pl.ds(pl.multiple_of((1 - c) * half, 16), half)

        def copy(k, src, dst, to):
            return pltpu.make_async_remote_copy(src_ref=src, dst_ref=dst, send_sem=send_sems.at[k],
                                                recv_sem=recv_sems.at[k], device_id=to, device_id_type=MESH)

        first = [copy(j, in_ref.at[mine_rows], out_ref.at[me, mine_rows], (cx, cy, c)) for j, (cx, cy) in enumerate(chips)]
        for cp in first:
            cp.start()
        passed = []
        for j, (cx, cy) in enumerate(chips):
            slot = out_ref.at[2 * cx + cy, mine_rows]
            copy(j, slot, slot, (cx, cy, c)).wait_recv()
            fwd = copy(3 + j, slot, slot, (x, y, 1 - c))
            fwd.start()
            passed.append(fwd)
        for j, (cx, cy) in enumerate(chips):
            slot = out_ref.at[2 * cx + cy, other_rows]
            copy(3 + j, slot, slot, (x, y, 1 - c)).wait_recv()
        for cp in first + passed:
            cp.wait_send()

    return _pcall(
        body, name="allgather_chips", out_shape=jax.ShapeDtypeStruct((4, nr, w), buf.dtype),
        in_specs=[_ANY], out_specs=_ANY,
        scratch_shapes=[pltpu.SemaphoreType.DMA((6,)), pltpu.SemaphoreType.DMA((6,))],
    )(buf)


def _rs_pair_exchange(g):
    _, nr, w = g.shape
    half = nr // 2

    def body(g_ref, rb_ref, send_sem, recv_sem):
        x, y, c = _coords()
        other_rows = pl.ds(pl.multiple_of((1 - c) * half, SUB), half)
        cp = pltpu.make_async_remote_copy(src_ref=g_ref.at[:, other_rows], dst_ref=rb_ref, send_sem=send_sem,
                                          recv_sem=recv_sem, device_id=(x, y, 1 - c), device_id_type=MESH)
        cp.start()
        cp.wait()

    return _pcall(body, name="rs_pair_exchange", out_shape=jax.ShapeDtypeStruct((4, half, w), g.dtype),
                  in_specs=[_ANY], out_specs=_ANY,
                  scratch_shapes=[pltpu.SemaphoreType.DMA, pltpu.SemaphoreType.DMA])(g)


def _rs_pair_add(g, rb, cidx, tb=256):
    _, nr, w = g.shape
    half = nr // 2
    tb = min(tb, half)
    assert half % tb == 0
    hb = half // tb

    def body(c_ref, g_ref, r_ref, o_ref):
        o_ref[...] = (g_ref[...].astype(F32) + r_ref[...].astype(F32)).astype(o_ref.dtype)

    gs = pltpu.PrefetchScalarGridSpec(
        num_scalar_prefetch=1, grid=(4, hb),
        in_specs=[pl.BlockSpec((1, tb, w), lambda j, i, c_ref: (j, c_ref[0] * hb + i, 0)),
                  pl.BlockSpec((1, tb, w), lambda j, i, c_ref: (j, i, 0))],
        out_specs=pl.BlockSpec((1, tb, w), lambda j, i, c_ref: (j, i, 0)))
    return _pcall(body, name="rs_pair_add", grid_spec=gs, out_shape=jax.ShapeDtypeStruct((4, half, w), BF16),
                  compiler_params=_cp("parallel", "parallel"))(cidx, g, rb)


def _rs_chip_exchange(p):
    _, h, w = p.shape

    def body(p_ref, rb_ref, send_sems, recv_sems):
        x, y, c = _coords()
        chips = [(1 - x, y), (x, 1 - y), (1 - x, 1 - y)]
        cps = [pltpu.make_async_remote_copy(src_ref=p_ref.at[2 * cx + cy], dst_ref=rb_ref.at[j], send_sem=send_sems.at[j],
                                            recv_sem=recv_sems.at[j], device_id=(cx, cy, c), device_id_type=MESH)
               for j, (cx, cy) in enumerate(chips)]
        for cp in cps:
            cp.start()
        for cp in cps:
            cp.wait()

    return _pcall(body, name="rs_chip_exchange", out_shape=jax.ShapeDtypeStruct((3, h, w), p.dtype),
                  in_specs=[_ANY], out_specs=_ANY,
                  scratch_shapes=[pltpu.SemaphoreType.DMA((3,)), pltpu.SemaphoreType.DMA((3,))])(p)


def _sum4(p, rb, chip_idx, tb=256):
    _, h, w = rb.shape
    tb = min(tb, h)
    assert h % tb == 0

    def body(m_ref, p_ref, r_ref, o_ref):
        f = lambda t: t.astype(F32)
        o_ref[...] = ((f(p_ref[0]) + f(r_ref[0])) + f(r_ref[1])) + f(r_ref[2])

    gs = pltpu.PrefetchScalarGridSpec(
        num_scalar_prefetch=1, grid=(h // tb,),
        in_specs=[pl.BlockSpec((1, tb, w), lambda i, m_ref: (m_ref[0], i, 0)),
                  pl.BlockSpec((3, tb, w), lambda i, m_ref: (0, i, 0))],
        out_specs=pl.BlockSpec((tb, w), lambda i, m_ref: (i, 0)))
    return _pcall(body, name="sum4", grid_spec=gs, out_shape=jax.ShapeDtypeStruct((h, w), F32),
                  compiler_params=_cp("parallel"))(chip_idx, p, rb)


def _pair_allgather(f):
    h, w = f.shape

    def body(f_ref, out_ref, send_sem, recv_sem):
        x, y, c = _coords()
        mine_rows = pl.ds(pl.multiple_of(c * h, SUB), h)
        other_rows = pl.ds(pl.multiple_of((1 - c) * h, SUB), h)
        send = pltpu.make_async_remote_copy(src_ref=f_ref, dst_ref=out_ref.at[mine_rows], send_sem=send_sem,
                                            recv_sem=recv_sem, device_id=(x, y, 1 - c), device_id_type=MESH)
        send.start()
        send.wait_send()
        pltpu.make_async_remote_copy(src_ref=f_ref, dst_ref=out_ref.at[other_rows], send_sem=send_sem,
                                     recv_sem=recv_sem, device_id=(x, y, 1 - c), device_id_type=MESH).wait_recv()

    return _pcall(body, name="pair_allgather", out_shape=jax.ShapeDtypeStruct((2 * h, w), f.dtype),
                  in_specs=[_ANY], out_specs=_ANY,
                  scratch_shapes=[pltpu.SemaphoreType.DMA, pltpu.SemaphoreType.DMA])(f)


def _size(shape):
    n = 1
    for d in shape:
        n *= d
    return n


PACK_ALIGN = 16


def _pack(arrs, dtype, row_mult):
    parts = []
    for a in arrs:
        flat = a.astype(dtype).reshape(-1)
        n = flat.shape[0]
        full = _ru(n, PACK_W * PACK_ALIGN)
        if full > n:
            flat = jnp.pad(flat, (0, full - n))
        parts.append(flat.reshape(-1, PACK_W))
    rows = sum(p.shape[0] for p in parts)
    if rows % row_mult:
        parts.append(jnp.zeros((_ru(rows, row_mult) - rows, PACK_W), dtype))
    return jnp.concatenate(parts, axis=0)


def _unpack(buf, shapes):
    out, off = [], 0
    for s in shapes:
        n = _size(s)
        r = _ru(-(-n // PACK_W), PACK_ALIGN)
        part = buf[off:off + r]
        out.append(part.reshape(s) if n == r * PACK_W else part.reshape(-1)[:n].reshape(s))
        off += r
    return out


def _unpack_chips(gathered, shapes, axes):
    out, off = [], 0
    for s, ax in zip(shapes, axes):
        n = _size(s)
        r = _ru(-(-n // PACK_W), PACK_ALIGN)
        part = gathered[:, off:off + r]
        if n != r * PACK_W:
            part = part.reshape(4, -1)[:, :n]
        part = part.reshape((4,) + tuple(s))
        out.append(part.reshape(4 * s[0], s[1]) if ax == 0 else jnp.moveaxis(part, 0, 1).reshape(s[0], 4 * s[1]))
        off += r
    return out


def _pack_chips(arrs, shard_shapes, axes, dtype, row_mult):
    parts = []
    for a, s, ax in zip(arrs, shard_shapes, axes):
        a = a.astype(dtype)
        p = a.reshape((4,) + tuple(s)) if ax == 0 else jnp.moveaxis(a.reshape(s[0], 4, s[1]), 1, 0)
        p = p.reshape(4, -1)
        n = p.shape[1]
        full = _ru(n, PACK_W * PACK_ALIGN)
        if full > n:
            p = jnp.pad(p, ((0, 0), (0, full - n)))
        parts.append(p.reshape(4, -1, PACK_W))
    rows = sum(p.shape[1] for p in parts)
    if rows % row_mult:
        parts.append(jnp.zeros((4, _ru(rows, row_mult) - rows, PACK_W), dtype))
    return jnp.concatenate(parts, axis=1)


def _pack_flat(arrs, dtype, row_mult):
    flat = jnp.concatenate([a.astype(dtype).reshape(-1) for a in arrs])
    n = flat.shape[0]
    rows = _ru(-(-n // PACK_W), row_mult)
    return jnp.pad(flat, (0, rows * PACK_W - n)).reshape(rows, PACK_W)


def _unpack_flat(buf, shapes):
    flat = buf.reshape(-1)
    out, off = [], 0
    for s in shapes:
        n = _size(s)
        out.append(flat[off:off + n].reshape(s))
        off += n
    return out


def _in_layout(D, GW, GH, FW, FH, MW, tn):
    o_z = 3 * GW
    o_beta = 4 * GW
    o_fq = o_beta + 2 * GH
    o_ff = o_fq + 3 * FW
    o_mq = o_ff + FH
    o_g = o_mq + MW
    orig = {"q": (0, GW), "k": (GW, GW), "v": (2 * GW, GW), "z": (o_z, GW), "beta": (o_beta, GH), "dec": (o_beta + GH, GH),
            "fq": (o_fq, FW), "fk": (o_fq + FW, FW), "fv": (o_fq + 2 * FW, FW), "ff": (o_ff, FH), "mq": (o_mq, MW),
            "ga": (o_g, D), "gb": (o_g + D, D), "gm": (o_g + 2 * D, D)}
    offs, cur = {}, 0
    for key, width in (("ga", D), ("gb", D), ("gm", D), ("q", GW), ("k", GW), ("v", GW), ("z", GW),
                       ("fq", FW), ("fk", FW), ("fv", FW), ("mq", MW), ("small", LANES)):
        cur = _ru(cur, width)
        offs[key] = cur
        cur += width
    total = _ru(cur, tn)
    pieces = [(offs[k], orig[k][0], orig[k][1]) for k in ("ga", "gb", "gm", "q", "k", "v", "z", "fq", "fk", "fv", "mq")]
    pieces += [(offs["small"], orig["beta"][0], GH), (offs["small"] + GH, orig["dec"][0], GH),
               (offs["small"] + 2 * GH, orig["ff"][0], FH)]
    return offs, total, pieces, o_g + 3 * D


def _pad_cols(w, pieces, total):
    parts, cur = [], 0
    for pstart, ostart, n in pieces:
        if pstart > cur:
            parts.append(jnp.zeros((w.shape[0], pstart - cur), w.dtype))
        parts.append(w[:, ostart:ostart + n])
        cur = pstart + n
    if total > cur:
        parts.append(jnp.zeros((w.shape[0], total - cur), w.dtype))
    return jnp.concatenate(parts, axis=1)


def _unpad_cols(wp, pieces):
    return jnp.concatenate([wp[:, pstart:pstart + n] for pstart, ostart, n in sorted(pieces, key=lambda t: t[1])], axis=1)


def _local_step(x, mem, tgt, W, flash_blk=512):
    T, D = x.shape
    GW = W["w_up_gdn"].shape[0]
    FW = W["w_up_fox"].shape[0]
    MW = W["w_up_mem"].shape[0]
    GH, FH = GW // HEAD, FW // HEAD
    offs, NP, pieces, d_in = _in_layout(D, GW, GH, FW, FH, MW, 1024)
    assert W["w_in"].shape[1] == d_in
    w_in_p = _pad_cols(W["w_in"], pieces, NP)
    cw = W["conv_w"]
    cws = [cw[:, i * GW:(i + 1) * GW] for i in range(3)]
    zl = jnp.zeros((1, LANES), F32)
    pvecs = [lax.dynamic_update_slice(zl, W["a_log"], (0, GH)), lax.dynamic_update_slice(zl, W["dt_bias"], (0, GH)),
             lax.dynamic_update_slice(zl, W["fox_b_f"], (0, 2 * GH))]
    lane0 = 2 * GH

    h = _rms_fwd(x, W["g_mix"], "rms_mix")
    proj = _mm(h, w_in_p, "nn", "in_proj")
    qn, kn, vc, cum, gbm = _gdn_prep(proj, offs, cws, pvecs, GH, FH, GW)
    o_gdn, sall = _gdn_scan_fwd(qn, kn, vc, gbm, GH)
    o_a = _gdn_post(o_gdn, proj, offs["z"], W["gdn_norm_g"], GW)
    qb, kb, vb = _fox_prep(proj, offs, W["fox_q_norm"], W["fox_k_norm"], FW)
    cumh = cum[:, lane0:lane0 + FH].T
    cumc, cumr = cumh.reshape(FH, T, 1), cumh.reshape(FH, 1, T)
    o_b16, lse = _flash_fwd(qb, kb, vb, cumr, FH, flash_blk)
    memn = _rms_fwd(mem, W["g_mem"], "rms_mem")
    kv = _mm(memn, W["w_mem_kv"], "nn", "mem_kv")
    o_m = _mem_attn(proj, offs["mq"], kv, W["mem_q_norm"], W["mem_k_norm"], MW)
    ua = _mm(o_a, W["w_up_gdn"], "nn", "up_gdn")
    ub = _mm(o_b16, W["w_up_fox"], "nn", "up_fox")
    um = _mm(o_m, W["w_up_mem"], "nn", "up_mem")
    y = _merge(proj, offs, ua, ub, um, D)
    x1 = _mm(y, W["w_out"], "nn", "out_proj", epilogue=lambda acc, r: (acc + r,), extras=(x,))
    h2 = _rms_fwd(x1, W["g_mlp"], "rms_mlp")
    u, a = _mm(h2, W["w_ff1"], "nn", "ff1", out_dtypes=(F32, BF16),
               epilogue=lambda acc: (acc, jnp.square(jnp.maximum(acc, 0.0))))
    x2 = _mm(a, W["w_ff2"], "nn", "ff2", epilogue=lambda acc, r: (acc + r,), extras=(x1,))
    dx2, dx2b, lpart = _loss_grad(x2, tgt)
    loss = lpart[0, 0]

    G = {}
    du = _mm(dx2b, W["w_ff2"], "nt", "ff2_dx", out_dtypes=(BF16,),
             epilogue=lambda acc, uu: (acc * (2.0 * jnp.maximum(uu, 0.0)),), extras=(u,))
    G["w_ff2"] = _mm(a, dx2b, "tn", "ff2_dw", out_dtypes=(BF16,))
    dh2 = _mm(du, W["w_ff1"], "nt", "ff1_dx")
    G["w_ff1"] = _mm(h2, du, "tn", "ff1_dw", out_dtypes=(BF16,))
    dx1, dx1b, G["g_mlp"] = _rms_bwd(x1, W["g_mlp"], dh2, dx2, "rms_mlp_bwd")
    dy = _mm(dx1b, W["w_out"], "nt", "out_dx")
    G["w_out"] = _mm(y, dx1b, "tn", "out_dw", out_dtypes=(BF16,))
    dga, dgb, dgm, dua, dub, dum = _merge_bwd(proj, offs, ua, ub, um, dy, D)
    do_a = _mm(dua, W["w_up_gdn"], "nt", "up_gdn_dx")
    G["w_up_gdn"] = _mm(o_a, dua, "tn", "up_gdn_dw", out_dtypes=(BF16,))
    do_b = _mm(dub, W["w_up_fox"], "nt", "up_fox_dx")
    G["w_up_fox"] = _mm(o_b16, dub, "tn", "up_fox_dw", out_dtypes=(BF16,))
    do_m = _mm(dum, W["w_up_mem"], "nt", "up_mem_dx")
    G["w_up_mem"] = _mm(o_m, dum, "tn", "up_mem_dw", out_dtypes=(BF16,))
    dmq, dkv, G["mem_q_norm"], G["mem_k_norm"] = _mem_attn_bwd(proj, offs["mq"], kv, W["mem_q_norm"], W["mem_k_norm"], do_m, MW)
    dkvb = dkv.astype(BF16)
    dmemn = _mm(dkvb, W["w_mem_kv"], "nt", "mem_kv_dx")
    G["w_mem_kv"] = _mm(memn, dkvb, "tn", "mem_kv_dw", out_dtypes=(BF16,))
    G["g_mem"] = _rms_dg(mem, W["g_mem"], dmemn, "rms_mem_bwd")
    dl = _flash_bwd_q(qb, kb, vb, cumr, lse, do_b, None, FH, flash_blk)
    dqb, dcq = _flash_bwd_q(qb, kb, vb, cumr, lse, do_b, dl, FH, flash_blk)
    dkb, dvb, dck = _flash_bwd_dkv(qb, kb, vb, cumc, lse.reshape(FH, 1, T), dl.reshape(FH, 1, T), do_b, FH, flash_blk)
    dlf = _rev_cumsum_rows(dcq.reshape(FH, T), dck.reshape(FH, T))
    dlf_sm = jnp.pad(dlf.T, ((0, 0), (lane0, LANES - lane0 - FH)))
    dfq, dfk, dfv, G["fox_q_norm"], G["fox_k_norm"] = _fox_prep_bwd(proj, offs, W["fox_q_norm"], W["fox_k_norm"], dqb, dkb, dvb, FW)
    do_gdn, dz, G["gdn_norm_g"] = _gdn_post_bwd(o_gdn, proj, offs["z"], W["gdn_norm_g"], do_a, GW)
    dqn, dkn, dvc, dgsm = _gdn_scan_bwd(qn, kn, vc, gbm, sall, do_gdn, GH)
    dyq, dyk, dyv, dcq, dck_w, dcv, dsmall, dalog, ddtb, dbf = _gdn_prep_bwd_a(
        proj, offs, cws, pvecs, (dqn, dkn, dvc), dgsm, dlf_sm, GH, FH, GW)
    dxq, dxk, dxv = _gdn_prep_bwd_b((dyq, dyk, dyv), cws, GW)
    G["conv_w"] = jnp.concatenate([dcq[:CONV_K], dck_w[:CONV_K], dcv[:CONV_K]], axis=1)
    G["a_log"] = dalog[:, GH:2 * GH]
    G["dt_bias"] = ddtb[:, GH:2 * GH]
    G["fox_b_f"] = dbf[:, lane0:lane0 + FH]
    segs = {"ga": dga, "gb": dgb, "gm": dgm, "q": dxq, "k": dxk, "v": dxv, "z": dz, "fq": dfq, "fk": dfk, "fv": dfv,
            "mq": dmq, "small": dsmall}
    parts, cur = [], 0
    for key in ("ga", "gb", "gm", "q", "k", "v", "z", "fq", "fk", "fv", "mq", "small"):
        if offs[key] > cur:
            parts.append(jnp.zeros((T, offs[key] - cur), BF16))
        parts.append(segs[key])
        cur = offs[key] + segs[key].shape[1]
    if NP > cur:
        parts.append(jnp.zeros((T, NP - cur), BF16))
    dproj = jnp.concatenate(parts, axis=1)
    dh = _mm(dproj, w_in_p, "nt", "in_dx")
    G["w_in"] = _unpad_cols(_mm(h, dproj, "tn", "in_dw", out_dtypes=(BF16,)), pieces)
    grad_x, _, G["g_mix"] = _rms_bwd(x, W["g_mix"], dh, dx1, "rms_mix_bwd")
    return loss, grad_x, G


BIG = ["w_in", "w_mem_kv", "w_up_gdn", "w_up_fox", "w_up_mem", "w_out", "w_ff1", "w_ff2"]
SMALL = ["g_mix", "a_log", "dt_bias", "gdn_norm_g", "fox_b_f", "fox_q_norm", "fox_k_norm", "g_mem", "mem_q_norm",
         "mem_k_norm", "g_mlp"]
ORDER = ["g_mix", "w_in", "conv_w", "a_log", "dt_bias", "gdn_norm_g", "fox_b_f", "fox_q_norm", "fox_k_norm", "g_mem",
         "w_mem_kv", "mem_q_norm", "mem_k_norm", "w_up_gdn", "w_up_fox", "w_up_mem", "w_out", "g_mlp", "w_ff1", "w_ff2"]
SHARD_AXIS = {"w_in": 1, "w_mem_kv": 0, "w_up_gdn": 1, "w_up_fox": 1, "w_up_mem": 1, "w_out": 0, "w_ff1": 1, "w_ff2": 0}


def _step(x, mem, tgt, w, m, v, flash_blk=512):
    xi, yi, ci = _coords()
    chip = 2 * xi + yi

    shard_shapes = [w[n].shape for n in BIG]
    packed_w = _pack([w[n] for n in BIG], BF16, PACK_ROWS)
    gathered = _allgather_chips(packed_w)
    gathered = lax.dynamic_update_slice(gathered, packed_w[None], (chip, 0, 0))
    axes = [SHARD_AXIS[n] for n in BIG]
    W = dict(zip(BIG, _unpack_chips(gathered, shard_shapes, axes)))
    cw_rows = jnp.pad(w["conv_w"], ((0, SUB - CONV_K), (0, 0)))
    cw_all = _allgather_small(cw_rows)
    W["conv_w"] = jnp.concatenate([cw_all[16 * j:16 * j + CONV_K] for j in range(4)], axis=1)
    for n in SMALL:
        W[n] = w[n]

    loss, grad_x, G = _local_step(x, mem, tgt, W, flash_blk)
    loss = lax.psum(loss, ("x", "y", "c"))

    small_shapes = [G[n].shape for n in SMALL] + [G["conv_w"].shape]
    sm = _pack_flat([G[n] for n in SMALL] + [G["conv_w"]], F32, SUB)
    sm_sum = _sum8(_allgather_small(sm), sm.shape[0])
    sm_list = _unpack_flat(sm_sum, small_shapes)
    g = {n: sm_list[i] for i, n in enumerate(SMALL)}
    cw_full = sm_list[-1]
    gw4 = cw_full.shape[1] // 4
    g["conv_w"] = lax.dynamic_slice(cw_full, (0, chip * gw4), (CONV_K, gw4))

    gflat = _pack_chips([G[n] for n in BIG], shard_shapes, axes, BF16, PACK_ROWS)
    rb1 = _rs_pair_exchange(gflat)
    part = _rs_pair_add(gflat, rb1, jnp.reshape(ci, (1,)).astype(jnp.int32))
    rb2 = _rs_chip_exchange(part)
    half_sum = _sum4(part, rb2, jnp.reshape(chip, (1,)).astype(jnp.int32))
    mine = _pair_allgather(half_sum)
    mine = lax.dynamic_update_slice(mine, half_sum, (ci * half_sum.shape[0], 0))
    for i, gv in enumerate(_unpack(mine, shard_shapes)):
        g[BIG[i]] = gv

    delta, new_m, new_v = {}, {}, {}
    for n in BIG:
        delta[n], new_m[n], new_v[n] = _adamw(w[n], g[n], m[n], v[n], "adamw_" + n)
    rest = SMALL + ["conv_w"]
    rest_shapes = [w[n].shape for n in rest]
    packed = [_pack_flat([d[n] for n in rest], F32, SUB) for d in (w, g, m, v)]
    outs = _adamw(*packed, "adamw_small", tb=packed[0].shape[0])
    for d, buf in zip((delta, new_m, new_v), outs):
        for n, val in zip(rest, _unpack_flat(buf, rest_shapes)):
            d[n] = val
    return loss, grad_x, g, delta, new_m, new_v


def kernel(x, mem, g_mix, w_in, conv_w, a_log, dt_bias, gdn_norm_g, fox_b_f, fox_q_norm, fox_k_norm, g_mem, w_mem_kv, mem_q_norm, mem_k_norm, w_up_gdn, w_up_fox, w_up_mem, w_out, g_mlp, w_ff1, w_ff2, loss_target, m_g_mix, m_w_in, m_conv_w, m_a_log, m_dt_bias, m_gdn_norm_g, m_fox_b_f, m_fox_q_norm, m_fox_k_norm, m_g_mem, m_w_mem_kv, m_mem_q_norm, m_mem_k_norm, m_w_up_gdn, m_w_up_fox, m_w_up_mem, m_w_out, m_g_mlp, m_w_ff1, m_w_ff2, v_g_mix, v_w_in, v_conv_w, v_a_log, v_dt_bias, v_gdn_norm_g, v_fox_b_f, v_fox_q_norm, v_fox_k_norm, v_g_mem, v_w_mem_kv, v_mem_q_norm, v_mem_k_norm, v_w_up_gdn, v_w_up_fox, v_w_up_mem, v_w_out, v_g_mlp, v_w_ff1, v_w_ff2):
    ws = (g_mix, w_in, conv_w, a_log, dt_bias, gdn_norm_g, fox_b_f, fox_q_norm, fox_k_norm, g_mem, w_mem_kv, mem_q_norm,
          mem_k_norm, w_up_gdn, w_up_fox, w_up_mem, w_out, g_mlp, w_ff1, w_ff2)
    ms = (m_g_mix, m_w_in, m_conv_w, m_a_log, m_dt_bias, m_gdn_norm_g, m_fox_b_f, m_fox_q_norm, m_fox_k_norm, m_g_mem,
          m_w_mem_kv, m_mem_q_norm, m_mem_k_norm, m_w_up_gdn, m_w_up_fox, m_w_up_mem, m_w_out, m_g_mlp, m_w_ff1, m_w_ff2)
    vs = (v_g_mix, v_w_in, v_conv_w, v_a_log, v_dt_bias, v_gdn_norm_g, v_fox_b_f, v_fox_q_norm, v_fox_k_norm, v_g_mem,
          v_w_mem_kv, v_mem_q_norm, v_mem_k_norm, v_w_up_gdn, v_w_up_fox, v_w_up_mem, v_w_out, v_g_mlp, v_w_ff1, v_w_ff2)
    drop = lambda a: a[0] if a.ndim == 3 else a
    w = {n: drop(a) for n, a in zip(ORDER, ws)}
    m = {n: drop(a) for n, a in zip(ORDER, ms)}
    v = {n: drop(a) for n, a in zip(ORDER, vs)}
    loss, grad_x, g, delta, new_m, new_v = _step(x[0], mem[0], loss_target[0], w, m, v)
    out = [loss, grad_x[None]]
    for d in (g, delta, new_m, new_v):
        out += [d[n].reshape(a.shape) for n, a in zip(ORDER, ws)]
    return tuple(out)
```

```python
import numpy as np

import jax
import jax.numpy as jnp
from jax import lax
from jax.experimental import pallas as pl
from jax.experimental.pallas import tpu as pltpu

F32 = jnp.float32
BF16 = jnp.bfloat16
HI = lax.Precision.HIGHEST
MESH = pl.DeviceIdType.MESH

EPS = 1e-6
HEAD = 128
MEM_DH = 256
CONV_K = 4
CHUNK = 64
CHUNK_SHIFT = 6
LANES = 128
SUB = 8
PACK_W = 1024
PACK_ROWS = 512
VMEM_LIMIT = 56 * 1024 * 1024
NEG = -1e30
SOLVE_PREC = None
FLASH_HEADS_PER_STEP = 4

ADAM_LR, ADAM_B1, ADAM_B2, ADAM_EPS, ADAM_WD, ADAM_STEP = 0.001, 0.9, 0.999, 1e-08, 0.01, 10


def _pcall(body, **kw):
    return pl.pallas_call(body, **kw)


def _cp(*sem):
    return pltpu.CompilerParams(dimension_semantics=sem, vmem_limit_bytes=VMEM_LIMIT)


def _dot(a, b, prec=None):
    return lax.dot_general(a, b, (((1,), (0,)), ((), ())), precision=prec, preferred_element_type=F32)


def _dot_nt(a, b, prec=None):
    return lax.dot_general(a, b, (((1,), (1,)), ((), ())), precision=prec, preferred_element_type=F32)


def _dot_tn(a, b, prec=None):
    return lax.dot_general(a, b, (((0,), (0,)), ((), ())), precision=prec, preferred_element_type=F32)


def _sigmoid(x):
    return 1.0 / (1.0 + jnp.exp(-x))


def _softplus(x):
    return jnp.maximum(x, 0.0) + jnp.log(1.0 + jnp.exp(-jnp.abs(x)))


def _silu(x):
    return x * _sigmoid(x)


def _rms(x, g):
    return x * lax.rsqrt(jnp.mean(x * x, axis=-1, keepdims=True) + EPS) * g


def _ru(a, m):
    return (a + m - 1) // m * m


def _mm(a, b, mode, name, out_dtypes=(F32,), epilogue=None, extras=(), tm=1024, tn=1024, tk=2048):
    if mode == "nn":
        (M, K), (K2, N) = a.shape, b.shape
    elif mode == "nt":
        (M, K), (N, K2) = a.shape, b.shape
    else:
        (K, M), (K2, N) = a.shape, b.shape
    assert K == K2, (a.shape, b.shape, mode)
    tm, tn = min(tm, M), min(tn, N)
    tk = next((t for t in (tk, 1024, 512, 256, LANES) if t <= K and K % t == 0), K)
    assert M % tm == 0 and N % tn == 0 and K % tk == 0, (M, N, K, tm, tn, tk)
    nk = K // tk
    n_ex, n_out = len(extras), len(out_dtypes)
    dims = {"nn": ((1,), (0,)), "nt": ((1,), (1,)), "tn": ((0,), (0,))}[mode]

    def finish(res, ex_refs, out_refs):
        outs = epilogue(res, *[r[...] for r in ex_refs]) if epilogue is not None else (res,)
        for o_ref, o in zip(out_refs, outs):
            o_ref[...] = o.astype(o_ref.dtype)

    def body(a_ref, b_ref, *rest):
        ex_refs, out_refs = rest[:n_ex], rest[n_ex:n_ex + n_out]
        part = lax.dot_general(a_ref[...], b_ref[...], (dims, ((), ())), preferred_element_type=F32)
        if nk == 1:
            finish(part, ex_refs, out_refs)
            return
        acc = rest[-1]
        k = pl.program_id(2)

        @pl.when(k == 0)
        def _():
            acc[...] = part

        @pl.when(k > 0)
        def _():
            acc[...] += part

        @pl.when(k == nk - 1)
        def _():
            finish(acc[...], ex_refs, out_refs)

    a_spec = pl.BlockSpec((tk, tm), lambda i, j, k: (k, i)) if mode == "tn" else pl.BlockSpec((tm, tk), lambda i, j, k: (i, k))
    b_spec = pl.BlockSpec((tn, tk), lambda i, j, k: (j, k)) if mode == "nt" else pl.BlockSpec((tk, tn), lambda i, j, k: (k, j))
    mn_spec = pl.BlockSpec((tm, tn), lambda i, j, k: (i, j))
    outs = _pcall(
        body, name=name, grid=(M // tm, N // tn, nk),
        in_specs=[a_spec, b_spec] + [mn_spec] * n_ex,
        out_specs=[mn_spec] * n_out,
        out_shape=[jax.ShapeDtypeStruct((M, N), dt) for dt in out_dtypes],
        scratch_shapes=[pltpu.VMEM((tm, tn), F32)] if nk > 1 else [],
        compiler_params=_cp("parallel", "parallel", "arbitrary"),
    )(a, b, *extras)
    return outs[0] if n_out == 1 else outs


def _rowwise(fn, T, tb, ins, outs, name, scratch=()):
    tb = min(tb, T)
    assert T % tb == 0 and (tb % SUB == 0 or tb == T)
    nblk = T // tb
    r8 = tb // SUB
    in_specs, arrs = [], []
    for spec in ins:
        kind, arr = spec[0], spec[1]
        arrs.append(arr)
        if kind == "full":
            nd = arr.ndim
            in_specs.append(pl.BlockSpec(arr.shape, lambda i, nd=nd: (0,) * nd))
            continue
        off, w = spec[2], spec[3]
        assert off % w == 0 and arr.shape[0] == T, (name, off, w, arr.shape)
        cb = off // w
        if kind == "row":
            in_specs.append(pl.BlockSpec((tb, w), lambda i, cb=cb: (i, cb)))
        elif kind == "prev":
            in_specs.append(pl.BlockSpec((SUB, w), lambda i, cb=cb: (jnp.maximum(i * r8 - 1, 0), cb)))
        else:
            in_specs.append(pl.BlockSpec((SUB, w), lambda i, cb=cb: (jnp.minimum((i + 1) * r8, T // SUB - 1), cb)))
    out_specs, out_shapes, is_acc = [], [], []
    for spec in outs:
        if spec[0] == "row":
            out_specs.append(pl.BlockSpec((tb, spec[1]), lambda i: (i, 0)))
            out_shapes.append(jax.ShapeDtypeStruct((T, spec[1]), spec[2]))
            is_acc.append(False)
        else:
            nd = len(spec[1])
            out_specs.append(pl.BlockSpec(spec[1], lambda i, nd=nd: (0,) * nd))
            out_shapes.append(jax.ShapeDtypeStruct(spec[1], spec[2]))
            is_acc.append(True)
    n_in, n_out = len(ins), len(outs)
    seq = any(is_acc) or len(scratch) > 0

    def body(*refs):
        in_refs, out_refs, scr = refs[:n_in], refs[n_in:n_in + n_out], refs[n_in + n_out:]
        i = pl.program_id(0)
        vals = fn(i, nblk, *[r[...] for r in in_refs], *scr)
        for o_ref, v, acc in zip(out_refs, vals, is_acc):
            if acc:
                @pl.when(i == 0)
                def _(o_ref=o_ref):
                    o_ref[...] = jnp.zeros_like(o_ref)

                o_ref[...] += v.astype(o_ref.dtype)
            else:
                o_ref[...] = v.astype(o_ref.dtype)

    res = _pcall(
        body, name=name, grid=(nblk,), in_specs=in_specs, out_specs=out_specs, out_shape=out_shapes,
        scratch_shapes=list(scratch), compiler_params=_cp("arbitrary" if seq else "parallel"),
    )(*arrs)
    return res


def _heads(x, width):
    return [x[:, h * width:(h + 1) * width] for h in range(x.shape[1] // width)]


def _cat(xs):
    return xs[0] if len(xs) == 1 else jnp.concatenate(xs, axis=1)


def _rms_fwd(x, g, name, tb=512):
    T, D = x.shape
    return _rowwise(lambda i, n, xv, gv: (_rms(xv, gv),), T, tb,
                    [("row", x, 0, D), ("full", g)], [("row", D, BF16)], name)[0]


def _rms_bwd(x, g, dh, dres, name, tb=256):
    T, D = x.shape

    def fn(i, n, xv, gv, dhv, drv):
        _, vjp = jax.vjp(_rms, xv, gv)
        dx, dg = vjp(dhv)
        tot = drv + dx
        return tot, tot, dg

    return _rowwise(fn, T, tb, [("row", x, 0, D), ("full", g), ("row", dh, 0, D), ("row", dres, 0, D)],
                    [("row", D, F32), ("row", D, BF16), ("acc", (1, D), F32)], name)


def _rms_dg(x, g, dh, name, tb=256):
    T, D = x.shape

    def fn(i, n, xv, gv, dhv):
        _, vjp = jax.vjp(lambda gg: _rms(xv, gg), gv)
        return vjp(dhv)

    return _rowwise(fn, T, tb, [("row", x, 0, D), ("full", g), ("row", dh, 0, D)], [("acc", (1, D), F32)], name)[0]


def _shift_down(x, halo, s, first):
    if s == 0:
        return x
    tb, c = x.shape
    xr = pltpu.roll(x, s, 0)
    hr = jnp.where(first, 0.0, pltpu.roll(halo, s, 0))
    hfull = hr if tb == SUB else jnp.concatenate([hr, jnp.zeros((tb - SUB, c), x.dtype)], axis=0)
    row = lax.broadcasted_iota(jnp.int32, x.shape, 0)
    return jnp.where(row < s, hfull, xr)


def _shift_up(z, halo, s, last):
    if s == 0:
        return z
    tb, c = z.shape
    zr = pltpu.roll(z, tb - s, 0)
    hr = jnp.where(last, 0.0, pltpu.roll(halo, SUB - s, 0))
    hfull = hr if tb == SUB else jnp.concatenate([jnp.zeros((tb - SUB, c), z.dtype), hr], axis=0)
    row = lax.broadcasted_iota(jnp.int32, z.shape, 0)
    return jnp.where(row >= tb - s, hfull, zr)


def _conv_pre(x, halo, cw, first):
    xs = [_shift_down(x, halo, s, first) for s in range(CONV_K)]
    y = cw[0:1, :] * xs[3]
    for i in range(1, CONV_K):
        y = y + cw[i:i + 1, :] * xs[CONV_K - 1 - i]
    return y, xs


def _qk_post(y, scale):
    a = _silu(y)
    return a * lax.rsqrt(jnp.sum(a * a, axis=-1, keepdims=True) + EPS) * scale


def _small_fn(s, alog, dtb, bf, gh, fh):
    lane = lax.broadcasted_iota(jnp.int32, s.shape, 1)
    beta = _sigmoid(s)
    g = -jnp.exp(alog) * _softplus(s + dtb)
    lf = -_softplus(-(s + bf))
    return jnp.where(lane < gh, beta, jnp.where(lane < 2 * gh, g, jnp.where(lane < 2 * gh + fh, lf, 0.0)))


def _gdn_prep(proj, offs, cws, pvecs, GH, FH, GW, tb=256):
    T = proj.shape[0]
    tb = min(tb, T)
    qscale = HEAD ** -0.5

    def fn(i, n, xq, hq, xk, hk, xv, hv, cwq, cwk, cwv, s, alog, dtb, bf, carry):
        first = i == 0
        yq, _ = _conv_pre(xq, hq, cwq, first)
        yk, _ = _conv_pre(xk, hk, cwk, first)
        yv, _ = _conv_pre(xv, hv, cwv, first)
        qn = _cat([_qk_post(y, qscale) for y in _heads(yq, HEAD)])
        kn = _cat([_qk_post(y, 1.0) for y in _heads(yk, HEAD)])
        vc = _silu(yv)
        gsm = _small_fn(s, alog, dtb, bf, GH, FH)

        @pl.when(first)
        def _():
            carry[...] = jnp.zeros_like(carry)

        ri = lax.broadcasted_iota(jnp.int32, (tb, tb), 0)
        ci = lax.broadcasted_iota(jnp.int32, (tb, tb), 1)
        cum = _dot((ri >= ci).astype(F32), gsm, HI) + carry[0:1, :]
        carry[...] += _dot(jnp.ones((SUB, tb), F32), gsm, HI)
        in_chunk = (ri >= ci) & ((ri >> CHUNK_SHIFT) == (ci >> CHUNK_SHIFT))
        lane = lax.broadcasted_iota(jnp.int32, gsm.shape, 1)
        gbm = jnp.where(lane < GH, gsm, _dot(in_chunk.astype(F32), gsm, HI))
        return qn, kn, vc, cum, gbm

    ins = []
    for key in ("q", "k", "v"):
        ins += [("row", proj, offs[key], GW), ("prev", proj, offs[key], GW)]
    ins += [("full", c) for c in cws] + [("row", proj, offs["small"], LANES)] + [("full", p) for p in pvecs]
    outs = [("row", GW, F32)] * 3 + [("row", LANES, F32)] * 2
    return _rowwise(fn, T, tb, ins, outs, "gdn_prep", scratch=[pltpu.VMEM((SUB, LANES), F32)])


def _gdn_prep_bwd_a(proj, offs, cws, pvecs, cts, dgsm_scan, dlf_sm, GH, FH, GW, tb=256):
    T = proj.shape[0]
    qscale = HEAD ** -0.5

    def one(x, halo, cw, ct, first, post):
        y, xs = _conv_pre(x, halo, cw, first)
        if post is None:
            _, vjp = jax.vjp(_silu, y)
            dy = vjp(ct)[0]
        else:
            dys = []
            for yh, cth in zip(_heads(y, HEAD), _heads(ct, HEAD)):
                _, vjp = jax.vjp(lambda t: _qk_post(t, post), yh)
                dys.append(vjp(cth)[0])
            dy = _cat(dys)
        row = lax.broadcasted_iota(jnp.int32, (SUB, x.shape[1]), 0)
        dcw = jnp.zeros((SUB, x.shape[1]), F32)
        for i in range(CONV_K):
            dcw = dcw + jnp.where(row == i, jnp.sum(dy * xs[CONV_K - 1 - i], axis=0, keepdims=True), 0.0)
        return dy, dcw

    def fn(i, n, xq, hq, xk, hk, xv, hv, cwq, cwk, cwv, cq, ck, cv, s, alog, dtb, bf, d1, d2):
        first = i == 0
        dyq, dcq = one(xq, hq, cwq, cq, first, qscale)
        dyk, dck = one(xk, hk, cwk, ck, first, 1.0)
        dyv, dcv = one(xv, hv, cwv, cv, first, None)
        tb_ = d1.shape[0]
        ri = lax.broadcasted_iota(jnp.int32, (tb_, tb_), 0)
        ci = lax.broadcasted_iota(jnp.int32, (tb_, tb_), 1)
        later = (ci >= ri) & ((ri >> CHUNK_SHIFT) == (ci >> CHUNK_SHIFT))
        lane = lax.broadcasted_iota(jnp.int32, d1.shape, 1)
        d1 = jnp.where(lane < GH, d1, _dot(later.astype(F32), d1, HI))
        _, vjp = jax.vjp(lambda a, b, c, d: _small_fn(a, b, c, d, GH, FH), s, alog, dtb, bf)
        ds, dalog, ddtb, dbf = vjp(d1 + d2)
        return dyq, dyk, dyv, dcq, dck, dcv, ds, dalog, ddtb, dbf

    ins = []
    for key in ("q", "k", "v"):
        ins += [("row", proj, offs[key], GW), ("prev", proj, offs[key], GW)]
    ins += [("full", c) for c in cws] + [("row", c, 0, GW) for c in cts]
    ins += [("row", proj, offs["small"], LANES)] + [("full", p) for p in pvecs]
    ins += [("row", dgsm_scan, 0, LANES), ("row", dlf_sm, 0, LANES)]
    outs = [("row", GW, F32)] * 3 + [("acc", (SUB, GW), F32)] * 3 + [("row", LANES, BF16)] + [("acc", (1, LANES), F32)] * 3
    return _rowwise(fn, T, tb, ins, outs, "gdn_prep_bwd_a")


def _gdn_prep_bwd_b(dys, cws, GW, tb=256):
    T = dys[0].shape[0]

    def fn(i, n, dq, nq, dk, nk, dv, nv, cwq, cwk, cwv):
        last = i == n - 1
        res = []
        for dy, nh, cw in ((dq, nq, cwq), (dk, nk, cwk), (dv, nv, cwv)):
            dx = cw[CONV_K - 1:CONV_K, :] * dy
            for t in range(CONV_K - 1):
                dx = dx + cw[t:t + 1, :] * _shift_up(dy, nh, CONV_K - 1 - t, last)
            res.append(dx)
        return tuple(res)

    ins = []
    for dy in dys:
        ins += [("row", dy, 0, GW), ("next", dy, 0, GW)]
    ins += [("full", c) for c in cws]
    return _rowwise(fn, T, tb, ins, [("row", GW, BF16)] * 3, "gdn_prep_bwd_b")


def _gdn_chunks(qs, ks, vs, gams, bcols, s0s):
    c, d = qs[0].shape
    hs = range(len(qs))
    ri = lax.broadcasted_iota(jnp.int32, (c, c), 0)
    ci = lax.broadcasted_iota(jnp.int32, (c, c), 1)
    incl, strict = ri >= ci, ri > ci
    eye = (ri == ci).astype(F32)
    ones_cc = jnp.ones((c, c), F32)
    rows = lax.broadcasted_iota(jnp.int32, (c, 1), 0)
    b16 = (ri >> 4) == (ci >> 4)
    b32 = (ri >> 5) == (ci >> 5)
    gam_cc = [gams[h] * ones_cc for h in hs]
    gam_t = [_dot_nt(eye, gam_cc[h], HI) for h in hs]
    glast = [jnp.sum(jnp.where(rows == c - 1, gams[h], 0.0), axis=0, keepdims=True) for h in hs]
    dec_i = [jnp.where(incl, jnp.exp(jnp.where(incl, gam_cc[h] - gam_t[h], 0.0)), 0.0) for h in hs]
    kk = [_dot_nt(ks[h], ks[h]) for h in hs]
    m = [bcols[h] * kk[h] * jnp.where(strict, dec_i[h], 0.0) for h in hs]
    m32 = [jnp.where(b32 & ~b16, m[h], 0.0) for h in hs]
    m64 = [jnp.where(b32, 0.0, m[h]) for h in hs]
    mp = [jnp.where(b16, m[h], 0.0) for h in hs]
    p = [eye - mp[h] for h in hs]
    for _ in range(3):
        mp = [_dot(mp[h], mp[h], SOLVE_PREC) for h in hs]
        p = [p[h] + _dot(p[h], mp[h], SOLVE_PREC) for h in hs]
    t = [_dot(p[h], m32[h], SOLVE_PREC) for h in hs]
    p = [p[h] - _dot(t[h], p[h], SOLVE_PREC) for h in hs]
    t = [_dot(p[h], m64[h], SOLVE_PREC) for h in hs]
    ainv = [p[h] - _dot(t[h], p[h], SOLVE_PREC) for h in hs]
    eg = [jnp.exp(gams[h]) for h in hs]
    w = [_dot(ainv[h], (bcols[h] * eg[h]) * ks[h], SOLVE_PREC) for h in hs]
    u0 = [_dot(ainv[h], bcols[h] * vs[h], SOLVE_PREC) for h in hs]
    qk = [_dot_nt(qs[h], ks[h]) * dec_i[h] for h in hs]
    u = [u0[h] - _dot(w[h], s0s[h]) for h in hs]
    o = [_dot(qs[h] * eg[h], s0s[h]) + _dot(qk[h], u[h]) for h in hs]
    s1 = [jnp.exp(glast[h]) * s0s[h] + _dot_tn(ks[h] * jnp.exp(glast[h] - gams[h]), u[h]) for h in hs]
    return tuple(o), tuple(s1)


def _lane_col(x, lane_idx):
    lane = lax.broadcasted_iota(jnp.int32, x.shape, 1)
    return jnp.sum(jnp.where(lane == lane_idx, x, 0.0), axis=1, keepdims=True)


def _gdn_scan_fwd(qn, kn, vc, gsm, GH):
    T, GW = qn.shape
    nc = T // CHUNK

    def body(q_ref, k_ref, v_ref, g_ref, o_ref, sall_ref, s_scr):
        @pl.when(pl.program_id(0) == 0)
        def _():
            s_scr[...] = jnp.zeros_like(s_scr)

        gs = g_ref[...]
        sls = [slice(h * HEAD, (h + 1) * HEAD) for h in range(GH)]
        s0s = tuple(s_scr[h] for h in range(GH))
        os_, s1s = _gdn_chunks(tuple(q_ref[:, sl] for sl in sls), tuple(k_ref[:, sl] for sl in sls),
                               tuple(v_ref[:, sl] for sl in sls), tuple(_lane_col(gs, GH + h) for h in range(GH)),
                               tuple(_lane_col(gs, h) for h in range(GH)), s0s)
        for h in range(GH):
            sall_ref[0, h] = s0s[h]
            o_ref[:, sls[h]] = os_[h]
            s_scr[h] = s1s[h]

    row = pl.BlockSpec((CHUNK, GW), lambda i: (i, 0))
    return _pcall(
        body, name="gdn_scan_fwd", grid=(nc,),
        in_specs=[row, row, row, pl.BlockSpec((CHUNK, LANES), lambda i: (i, 0))],
        out_specs=[row, pl.BlockSpec((1, GH, HEAD, HEAD), lambda i: (i, 0, 0, 0))],
        out_shape=[jax.ShapeDtypeStruct((T, GW), F32), jax.ShapeDtypeStruct((nc, GH, HEAD, HEAD), F32)],
        scratch_shapes=[pltpu.VMEM((GH, HEAD, HEAD), F32)],
        compiler_params=_cp("arbitrary"),
    )(qn, kn, vc, gsm)


def _gdn_scan_bwd(qn, kn, vc, gsm, sall, do, GH):
    T, GW = qn.shape
    nc = T // CHUNK

    def body(q_ref, k_ref, v_ref, g_ref, sall_ref, do_ref, dq_ref, dk_ref, dv_ref, dg_ref, ds_scr):
        @pl.when(pl.program_id(0) == 0)
        def _():
            ds_scr[...] = jnp.zeros_like(ds_scr)

        gs = g_ref[...]
        lane = lax.broadcasted_iota(jnp.int32, gs.shape, 1)
        sls = [slice(h * HEAD, (h + 1) * HEAD) for h in range(GH)]
        _, vjp = jax.vjp(_gdn_chunks, tuple(q_ref[:, sl] for sl in sls), tuple(k_ref[:, sl] for sl in sls),
                         tuple(v_ref[:, sl] for sl in sls), tuple(_lane_col(gs, GH + h) for h in range(GH)),
                         tuple(_lane_col(gs, h) for h in range(GH)), tuple(sall_ref[0, h] for h in range(GH)))
        dq, dk, dv, dgc, dbc, ds0 = vjp((tuple(do_ref[:, sl] for sl in sls), tuple(ds_scr[h] for h in range(GH))))
        dgs = jnp.zeros_like(gs)
        for h in range(GH):
            dq_ref[:, sls[h]] = dq[h]
            dk_ref[:, sls[h]] = dk[h]
            dv_ref[:, sls[h]] = dv[h]
            dgs = dgs + jnp.where(lane == h, dbc[h], 0.0) + jnp.where(lane == GH + h, dgc[h], 0.0)
            ds_scr[h] = ds0[h]
        dg_ref[...] = dgs

    row = pl.BlockSpec((CHUNK, GW), lambda i: (nc - 1 - i, 0))
    sm = pl.BlockSpec((CHUNK, LANES), lambda i: (nc - 1 - i, 0))
    return _pcall(
        body, name="gdn_scan_bwd", grid=(nc,),
        in_specs=[row, row, row, sm, pl.BlockSpec((1, GH, HEAD, HEAD), lambda i: (nc - 1 - i, 0, 0, 0)), row],
        out_specs=[row, row, row, sm],
        out_shape=[jax.ShapeDtypeStruct((T, GW), F32)] * 3 + [jax.ShapeDtypeStruct((T, LANES), F32)],
        scratch_shapes=[pltpu.VMEM((GH, HEAD, HEAD), F32)],
        compiler_params=_cp("arbitrary"),
    )(qn, kn, vc, gsm, sall, do)


def _gdn_post_fn(o, z, g):
    return _rms(o, g) * _silu(z)


def _gdn_post(o, proj, off_z, g, GW, tb=512):
    T = o.shape[0]

    def fn(i, n, ov, zv, gv):
        return (_cat([_gdn_post_fn(a, b, gv) for a, b in zip(_heads(ov, HEAD), _heads(zv, HEAD))]),)

    return _rowwise(fn, T, tb, [("row", o, 0, GW), ("row", proj, off_z, GW), ("full", g)], [("row", GW, BF16)], "gdn_post")[0]


def _gdn_post_bwd(o, proj, off_z, g, dout, GW, tb=256):
    T = o.shape[0]

    def fn(i, n, ov, zv, gv, dv):
        dos, dzs, dg = [], [], jnp.zeros_like(gv)
        for a, b, c in zip(_heads(ov, HEAD), _heads(zv, HEAD), _heads(dv, HEAD)):
            _, vjp = jax.vjp(_gdn_post_fn, a, b, gv)
            da, db, dgh = vjp(c)
            dos.append(da)
            dzs.append(db)
            dg = dg + dgh
        return _cat(dos), _cat(dzs), dg

    return _rowwise(fn, T, tb, [("row", o, 0, GW), ("row", proj, off_z, GW), ("full", g), ("row", dout, 0, GW)],
                    [("row", GW, F32), ("row", GW, BF16), ("acc", (1, HEAD), F32)], "gdn_post_bwd")


def _fox_prep(proj, offs, gq, gk, FW, tb=512):
    T = proj.shape[0]

    def fn(i, n, q, k, v, gqv, gkv):
        return (_cat([_rms(a, gqv) * (HEAD ** -0.5) for a in _heads(q, HEAD)]),
                _cat([_rms(a, gkv) for a in _heads(k, HEAD)]), v)

    return _rowwise(fn, T, tb, [("row", proj, offs["fq"], FW), ("row", proj, offs["fk"], FW), ("row", proj, offs["fv"], FW),
                                ("full", gq), ("full", gk)], [("row", FW, BF16)] * 3, "fox_prep")


def _fox_prep_bwd(proj, offs, gq, gk, dq, dk, dv, FW, tb=256):
    T = proj.shape[0]

    def fn(i, n, q, k, gqv, gkv, dqv, dkv, dvv):
        res = []
        for x, g, d in ((q, gqv, dqv), (k, gkv, dkv)):
            dxs, dg = [], jnp.zeros_like(g)
            for a, c in zip(_heads(x, HEAD), _heads(d, HEAD)):
                _, vjp = jax.vjp(_rms, a, g)
                da, dgh = vjp(c)
                dxs.append(da)
                dg = dg + dgh
            res += [_cat(dxs), dg]
        return res[0], res[2], dvv, res[1], res[3]

    return _rowwise(fn, T, tb, [("row", proj, offs["fq"], FW), ("row", proj, offs["fk"], FW), ("full", gq), ("full", gk),
                                ("row", dq, 0, FW), ("row", dk, 0, FW), ("row", dv, 0, FW)],
                    [("row", FW, BF16)] * 3 + [("acc", (1, HEAD), F32)] * 2, "fox_prep_bwd")


def _sub_row(x, sub_idx):
    sub = lax.broadcasted_iota(jnp.int32, x.shape, 0)
    return jnp.sum(jnp.where(sub == sub_idx, x, 0.0), axis=0, keepdims=True)


def _causal_pairs(nb, key_major):
    if key_major:
        pairs = [(i, j) for j in range(nb) for i in range(j, nb)]
    else:
        pairs = [(i, j) for i in range(nb) for j in range(i + 1)]
    return (jnp.asarray(np.array([p[0] for p in pairs], np.int32)),
            jnp.asarray(np.array([p[1] for p in pairs], np.int32)))


def _fox_scores(q, k, ck, diagonal):
    s = _dot_nt(q, k) - ck
    if diagonal:
        row = lax.broadcasted_iota(jnp.int32, s.shape, 0)
        col = lax.broadcasted_iota(jnp.int32, s.shape, 1)
        s = jnp.where(row >= col, s, NEG)
    return s


def _flash_fwd(qb, kb, vb, cumr, FH, blk):
    T, FW = qb.shape
    blk = min(blk, T)
    nb = T // blk
    qi_arr, kj_arr = _causal_pairs(nb, False)

    hps = FLASH_HEADS_PER_STEP
    assert FH % hps == 0
    us = range(hps)
    sl = [slice(u * HEAD, (u + 1) * HEAD) for u in us]

    def body(qi_ref, kj_ref, q_ref, k_ref, v_ref, ck_ref, ob_ref, lse_ref, m_scr, l_scr, acc):
        t = pl.program_id(1)
        qi, kj = qi_ref[t], kj_ref[t]

        @pl.when(kj == 0)
        def _():
            m_scr[...] = jnp.full_like(m_scr, NEG)
            l_scr[...] = jnp.zeros_like(l_scr)
            acc[...] = jnp.zeros_like(acc)

        def update(diagonal):
            s = [_fox_scores(q_ref[:, sl[u]], k_ref[:, sl[u]], ck_ref[u], diagonal) for u in us]
            m_old = [m_scr[u] for u in us]
            m_new = [jnp.maximum(m_old[u], jnp.max(s[u], axis=1, keepdims=True)) for u in us]
            alpha = [jnp.exp(m_old[u] - m_new[u]) for u in us]
            p = [jnp.exp(s[u] - m_new[u]) for u in us]
            pv = [_dot(p[u].astype(BF16), v_ref[:, sl[u]]) for u in us]
            for u in us:
                l_scr[u] = alpha[u] * l_scr[u] + jnp.sum(p[u], axis=1, keepdims=True)
                acc[:, sl[u]] = alpha[u] * acc[:, sl[u]] + pv[u]
                m_scr[u] = m_new[u]

        @pl.when(kj < qi)
        def _():
            update(False)

        @pl.when(kj == qi)
        def _():
            update(True)
            for u in us:
                ob_ref[:, sl[u]] = (acc[:, sl[u]] / l_scr[u]).astype(BF16)
                lse_ref[u] = m_scr[u] + jnp.log(l_scr[u])

    qs = pl.BlockSpec((blk, hps * HEAD), lambda g, t, qr, kr: (qr[t], g))
    ks = pl.BlockSpec((blk, hps * HEAD), lambda g, t, qr, kr: (kr[t], g))
    col = pl.BlockSpec((hps, blk, 1), lambda g, t, qr, kr: (g, qr[t], 0))
    gs = pltpu.PrefetchScalarGridSpec(
        num_scalar_prefetch=2, grid=(FH // hps, qi_arr.shape[0]),
        in_specs=[qs, ks, ks, pl.BlockSpec((hps, 1, blk), lambda g, t, qr, kr: (g, 0, kr[t]))],
        out_specs=[qs, col],
        scratch_shapes=[pltpu.VMEM((hps, blk, 1), F32), pltpu.VMEM((hps, blk, 1), F32), pltpu.VMEM((blk, hps * HEAD), F32)])
    return _pcall(
        body, name="flash_fwd", grid_spec=gs,
        out_shape=[jax.ShapeDtypeStruct((T, FW), BF16), jax.ShapeDtypeStruct((FH, T, 1), F32)],
        compiler_params=_cp("parallel", "arbitrary"),
    )(qi_arr, kj_arr, qb, kb, vb, cumr)


def _flash_bwd_q(qb, kb, vb, cumr, lse, do, dl, FH, blk):
    T, FW = qb.shape
    blk = min(blk, T)
    nb = T // blk
    qi_arr, kj_arr = _causal_pairs(nb, False)
    want_dq = dl is not None
    hps = FLASH_HEADS_PER_STEP
    assert FH % hps == 0
    us = range(hps)
    sl = [slice(u * HEAD, (u + 1) * HEAD) for u in us]

    def body(qi_ref, kj_ref, q_ref, k_ref, v_ref, ck_ref, lse_ref, do_ref, *rest):
        if want_dq:
            dl_ref, dq_ref, rs_ref, acc, rs_acc = rest
        else:
            dl_ref, acc = rest
        t = pl.program_id(1)
        qi, kj = qi_ref[t], kj_ref[t]

        @pl.when(kj == 0)
        def _():
            acc[...] = jnp.zeros_like(acc)
            if want_dq:
                rs_acc[...] = jnp.zeros_like(rs_acc)

        def update(diagonal):
            s = [_fox_scores(q_ref[:, sl[u]], k_ref[:, sl[u]], ck_ref[u], diagonal) for u in us]
            p = [jnp.exp(s[u] - lse_ref[u]) for u in us]
            dp = [_dot_nt(do_ref[:, sl[u]].astype(BF16), v_ref[:, sl[u]]) for u in us]
            if want_dq:
                ds = [p[u] * (dp[u] - dl_ref[u]) for u in us]
                dqp = [_dot(ds[u].astype(BF16), k_ref[:, sl[u]]) for u in us]
                for u in us:
                    acc[:, sl[u]] += dqp[u]
                    rs_acc[u] += jnp.sum(ds[u], axis=1, keepdims=True)
            else:
                for u in us:
                    acc[u] += jnp.sum(p[u] * dp[u], axis=1, keepdims=True)

        @pl.when(kj < qi)
        def _():
            update(False)

        @pl.when(kj == qi)
        def _():
            update(True)
            if want_dq:
                dq_ref[...] = acc[...] * (HEAD ** -0.5)
                rs_ref[...] = rs_acc[...]
            else:
                dl_ref[...] = acc[...]

    qs = pl.BlockSpec((blk, hps * HEAD), lambda g, t, qr, kr: (qr[t], g))
    ks = pl.BlockSpec((blk, hps * HEAD), lambda g, t, qr, kr: (kr[t], g))
    col = pl.BlockSpec((hps, blk, 1), lambda g, t, qr, kr: (g, qr[t], 0))
    in_specs = [qs, ks, ks, pl.BlockSpec((hps, 1, blk), lambda g, t, qr, kr: (g, 0, kr[t])), col, qs]
    args = [qi_arr, kj_arr, qb, kb, vb, cumr, lse, do]
    colshape = jax.ShapeDtypeStruct((FH, T, 1), F32)
    if want_dq:
        gs = pltpu.PrefetchScalarGridSpec(
            num_scalar_prefetch=2, grid=(FH // hps, qi_arr.shape[0]), in_specs=in_specs + [col], out_specs=[qs, col],
            scratch_shapes=[pltpu.VMEM((blk, hps * HEAD), F32), pltpu.VMEM((hps, blk, 1), F32)])
        return _pcall(body, name="flash_bwd_dq", grid_spec=gs,
                      out_shape=[jax.ShapeDtypeStruct((T, FW), F32), colshape],
                      compiler_params=_cp("parallel", "arbitrary"))(*args, dl)
    gs = pltpu.PrefetchScalarGridSpec(
        num_scalar_prefetch=2, grid=(FH // hps, qi_arr.shape[0]), in_specs=in_specs, out_specs=col,
        scratch_shapes=[pltpu.VMEM((hps, blk, 1), F32)])
    return _pcall(body, name="flash_bwd_rowterm", grid_spec=gs, out_shape=colshape,
                  compiler_params=_cp("parallel", "arbitrary"))(*args)


def _flash_bwd_dkv(qb, kb, vb, cumc, lse_row, dl_row, do, FH, blk):
    T, FW = qb.shape
    blk = min(blk, T)
    nb = T // blk

    qi_arr, kj_arr = _causal_pairs(nb, True)

    hps = FLASH_HEADS_PER_STEP
    assert FH % hps == 0
    us = range(hps)
    sl = [slice(u * HEAD, (u + 1) * HEAD) for u in us]

    def body(qi_ref, kj_ref, q_ref, k_ref, v_ref, ck_ref, lse_ref, dl_ref, do_ref,
             dk_ref, dv_ref, dc_ref, dk_acc, dv_acc, dc_acc):
        t = pl.program_id(1)
        qi, kj = qi_ref[t], kj_ref[t]

        def update(diagonal, first):
            st = [_dot_nt(k_ref[:, sl[u]], q_ref[:, sl[u]]) - ck_ref[u] for u in us]
            if diagonal:
                krow = lax.broadcasted_iota(jnp.int32, st[0].shape, 0)
                qcol = lax.broadcasted_iota(jnp.int32, st[0].shape, 1)
                st = [jnp.where(qcol >= krow, st[u], NEG) for u in us]
            pt = [jnp.exp(st[u] - lse_ref[u]) for u in us]
            dob = [do_ref[:, sl[u]].astype(BF16) for u in us]
            dpt = [_dot_nt(v_ref[:, sl[u]], dob[u]) for u in us]
            dst = [pt[u] * (dpt[u] - dl_ref[u]) for u in us]
            dk = [_dot(dst[u].astype(BF16), q_ref[:, sl[u]]) for u in us]
            dv = [_dot(pt[u].astype(BF16), dob[u]) for u in us]
            for u in us:
                dc = -jnp.sum(dst[u], axis=1, keepdims=True)
                if first:
                    dk_acc[:, sl[u]] = dk[u]
                    dv_acc[:, sl[u]] = dv[u]
                    dc_acc[u] = dc
                else:
                    dk_acc[:, sl[u]] += dk[u]
                    dv_acc[:, sl[u]] += dv[u]
                    dc_acc[u] += dc

        @pl.when(qi == kj)
        def _():
            update(True, True)

        @pl.when(qi > kj)
        def _():
            update(False, False)

        @pl.when(qi == nb - 1)
        def _():
            dk_ref[...] = dk_acc[...]
            dv_ref[...] = dv_acc[...]
            dc_ref[...] = dc_acc[...]

    ks = pl.BlockSpec((blk, hps * HEAD), lambda g, t, qr, kr: (kr[t], g))
    qs = pl.BlockSpec((blk, hps * HEAD), lambda g, t, qr, kr: (qr[t], g))
    rowq = pl.BlockSpec((hps, 1, blk), lambda g, t, qr, kr: (g, 0, qr[t]))
    colk = pl.BlockSpec((hps, blk, 1), lambda g, t, qr, kr: (g, kr[t], 0))
    gs = pltpu.PrefetchScalarGridSpec(
        num_scalar_prefetch=2, grid=(FH // hps, qi_arr.shape[0]),
        in_specs=[qs, ks, ks, colk, rowq, rowq, qs],
        out_specs=[ks, ks, colk],
        scratch_shapes=[pltpu.VMEM((blk, hps * HEAD), F32), pltpu.VMEM((blk, hps * HEAD), F32),
                        pltpu.VMEM((hps, blk, 1), F32)])
    return _pcall(
        body, name="flash_bwd_dkv", grid_spec=gs,
        out_shape=[jax.ShapeDtypeStruct((T, FW), F32), jax.ShapeDtypeStruct((T, FW), F32),
                   jax.ShapeDtypeStruct((FH, T, 1), F32)],
        compiler_params=_cp("parallel", "arbitrary"),
    )(qi_arr, kj_arr, qb, kb, vb, cumc, lse_row, dl_row, do)


def _rev_cumsum_rows(r1, r2, tb=512):
    H, T = r1.shape
    tb = min(tb, T)
    nb = T // tb

    def body(r1_ref, r2_ref, o_ref, carry):
        @pl.when(pl.program_id(0) == 0)
        def _():
            carry[...] = jnp.zeros_like(carry)

        rv = r1_ref[...] + r2_ref[...]
        si = lax.broadcasted_iota(jnp.int32, (tb, tb), 0)
        ti = lax.broadcasted_iota(jnp.int32, (tb, tb), 1)
        o_ref[...] = _dot(rv, (si >= ti).astype(F32), HI) + carry[...]
        carry[...] += jnp.sum(rv, axis=1, keepdims=True)

    spec = pl.BlockSpec((H, tb), lambda i: (0, nb - 1 - i))
    return _pcall(body, name="rev_cumsum", grid=(nb,), in_specs=[spec, spec], out_specs=spec,
                  out_shape=jax.ShapeDtypeStruct((H, T), F32), scratch_shapes=[pltpu.VMEM((H, 1), F32)],
                  compiler_params=_cp("arbitrary"))(r1, r2)


def _mem_head(q, k, v, gq, gk):
    logits = _dot_nt(_rms(q, gq), _rms(k, gk)) * (MEM_DH ** -0.5)
    mx = jnp.max(logits, axis=1, keepdims=True)
    e = jnp.exp(logits - mx)
    p = e / jnp.sum(e, axis=1, keepdims=True)
    return _dot(p, v)


def _mem_attn(proj, off_q, kv, gq, gk, MW, tb=512):
    T = proj.shape[0]
    MH = MW // MEM_DH

    def fn(i, n, q, kvv, gqv, gkv):
        ks, vs = _heads(kvv[:, :MW], MEM_DH), _heads(kvv[:, MW:], MEM_DH)
        return (_cat([_mem_head(a, b, c, gqv, gkv) for a, b, c in zip(_heads(q, MEM_DH), ks, vs)]),)

    return _rowwise(fn, T, tb, [("row", proj, off_q, MW), ("full", kv), ("full", gq), ("full", gk)],
                    [("row", MW, BF16)], "mem_attn")[0]


def _mem_attn_bwd(proj, off_q, kv, gq, gk, dout, MW, tb=256):
    T = proj.shape[0]
    ML = kv.shape[0]

    def fn(i, n, q, kvv, gqv, gkv, dv):
        ks, vs = _heads(kvv[:, :MW], MEM_DH), _heads(kvv[:, MW:], MEM_DH)
        dqs, dks, dvs = [], [], []
        dgq, dgk = jnp.zeros_like(gqv), jnp.zeros_like(gkv)
        for a, b, c, d in zip(_heads(q, MEM_DH), ks, vs, _heads(dv, MEM_DH)):
            _, vjp = jax.vjp(_mem_head, a, b, c, gqv, gkv)
            da, db, dc, dg1, dg2 = vjp(d)
            dqs.append(da)
            dks.append(db)
            dvs.append(dc)
            dgq, dgk = dgq + dg1, dgk + dg2
        return _cat(dqs), _cat(dks + dvs), dgq, dgk

    return _rowwise(fn, T, tb, [("row", proj, off_q, MW), ("full", kv), ("full", gq), ("full", gk), ("row", dout, 0, MW)],
                    [("row", MW, BF16), ("acc", (ML, 2 * MW), F32), ("acc", (1, MEM_DH), F32), ("acc", (1, MEM_DH), F32)],
                    "mem_attn_bwd")


def _merge_fn(ga, gb, gm, ua, ub, um):
    return _sigmoid(ga) * ua + _sigmoid(gb) * ub + _sigmoid(gm) * um


def _merge(proj, offs, ua, ub, um, D, tb=256):
    T = proj.shape[0]
    ins = [("row", proj, offs[k], D) for k in ("ga", "gb", "gm")] + [("row", u, 0, D) for u in (ua, ub, um)]
    return _rowwise(lambda i, n, *v: (_merge_fn(*v),), T, tb, ins, [("row", D, BF16)], "merge")[0]


def _merge_bwd(proj, offs, ua, ub, um, dy, D, tb=256):
    T = proj.shape[0]

    def fn(i, n, *v):
        _, vjp = jax.vjp(_merge_fn, *v[:6])
        return vjp(v[6])

    ins = [("row", proj, offs[k], D) for k in ("ga", "gb", "gm")] + [("row", u, 0, D) for u in (ua, ub, um)] + [("row", dy, 0, D)]
    return _rowwise(fn, T, tb, ins, [("row", D, BF16)] * 6, "merge_bwd")


def _loss_grad(x2, tgt, tb=256):
    T, D = x2.shape

    def fn(i, n, a, b):
        e = a - b
        part = jnp.sum(jnp.sum(e * e, axis=1, keepdims=True), axis=0, keepdims=True) * (0.5 / D)
        g = e * (1.0 / D)
        return g, g, part + jnp.zeros((SUB, LANES), F32)

    return _rowwise(fn, T, tb, [("row", x2, 0, D), ("row", tgt, 0, D)],
                    [("row", D, F32), ("row", D, BF16), ("acc", (SUB, LANES), F32)], "loss_grad")


def _adamw(w, g, m, v, name, tb=128):
    R, C = w.shape
    c1 = 1.0 / (1.0 - ADAM_B1 ** ADAM_STEP)
    c2 = 1.0 / (1.0 - ADAM_B2 ** ADAM_STEP)

    def fn(i, n, wv, gv, mv, vv):
        mn = ADAM_B1 * mv + (1.0 - ADAM_B1) * gv
        vn = ADAM_B2 * vv + (1.0 - ADAM_B2) * (gv * gv)
        delta = -ADAM_LR * ((mn * c1) / (jnp.sqrt(vn * c2) + ADAM_EPS) + ADAM_WD * wv)
        return delta, mn, vn

    return _rowwise(fn, R, tb, [("row", a, 0, C) for a in (w, g, m, v)], [("row", C, F32)] * 3, name)


def _coords():
    return lax.axis_index("x"), lax.axis_index("y"), lax.axis_index("c")


def _allgather_small(blk):
    m_per, n = blk.shape

    def body(x_ref, out_ref, send_sems, recv_sems, local_sem):
        x, y, c = _coords()
        me, sibling = (x, y, c), (x, y, 1 - c)
        chips = [(1 - x, y), (x, 1 - y), (1 - x, 1 - y)]

        def rows(px, py, pc):
            return out_ref.at[pl.ds((4 * px + 2 * py + pc) * m_per, m_per), :]

        def copy(k, block, to, src=None):
            return pltpu.make_async_remote_copy(
                src_ref=rows(*block) if src is None else src, dst_ref=rows(*block),
                send_sem=send_sems.at[k], recv_sem=recv_sems.at[k], device_id=to, device_id_type=MESH)

        mine = pltpu.make_async_copy(x_ref, rows(*me), local_sem)
        mine.start()
        first = [copy(0, me, sibling, src=x_ref)]
        first += [copy(1 + j, me, (*chip, c), src=x_ref) for j, chip in enumerate(chips)]
        for cp in first:
            cp.start()
        passed = [copy(4 + j, (*chip, c), sibling) for j, chip in enumerate(chips)]
        for j, chip in enumerate(chips):
            copy(1 + j, (*chip, c), me).wait_recv()
            passed[j].start()
        copy(0, sibling, me).wait_recv()
        for j, chip in enumerate(chips):
            copy(4 + j, (*chip, 1 - c), me).wait_recv()
        for cp in first + passed:
            cp.wait_send()
        mine.wait()

    return _pcall(
        body, name="allgather_small", out_shape=jax.ShapeDtypeStruct((8 * m_per, n), blk.dtype),
        in_specs=[pl.BlockSpec(memory_space=pltpu.VMEM)], out_specs=pl.BlockSpec(memory_space=pltpu.VMEM),
        scratch_shapes=[pltpu.SemaphoreType.DMA((7,)), pltpu.SemaphoreType.DMA((7,)), pltpu.SemaphoreType.DMA],
        compiler_params=pltpu.CompilerParams(vmem_limit_bytes=VMEM_LIMIT),
    )(blk)


def _sum8(g, m_per):
    n = g.shape[1]

    def body(g_ref, o_ref):
        acc = g_ref[pl.ds(0, m_per), :]
        for d in range(1, 8):
            acc = acc + g_ref[pl.ds(d * m_per, m_per), :]
        o_ref[...] = acc

    return _pcall(body, name="sum8", out_shape=jax.ShapeDtypeStruct((m_per, n), g.dtype))(g)


_ANY = pl.BlockSpec(memory_space=pl.ANY)


def _allgather_chips(buf):
    nr, w = buf.shape
    half = nr // 2

    def body(in_ref, out_ref, send_sems, recv_sems):
        x, y, c = _coords()
        me = 2 * x + y
        chips = [(1 - x, y), (x, 1 - y), (1 - x, 1 - y)]
        mine_rows = pl.ds(pl.multiple_of(c * half, 16), half)
        other_rows = pl.ds(pl.multiple_of((1 - c) * half, 16), half)

        def copy(k, src, dst, to):
            return pltpu.make_async_remote_copy(src_ref=src, dst_ref=dst, send_sem=send_sems.at[k],
                                                recv_sem=recv_sems.at[k], device_id=to, device_id_type=MESH)

        first = [copy(j, in_ref.at[mine_rows], out_ref.at[me, mine_rows], (cx, cy, c)) for j, (cx, cy) in enumerate(chips)]
        for cp in first:
            cp.start()
        passed = []
        for j, (cx, cy) in enumerate(chips):
            slot = out_ref.at[2 * cx + cy, mine_rows]
            copy(j, slot, slot, (cx, cy, c)).wait_recv()
            fwd = copy(3 + j, slot, slot, (x, y, 1 - c))
            fwd.start()
            passed.append(fwd)
        for j, (cx, cy) in enumerate(chips):
            slot = out_ref.at[2 * cx + cy, other_rows]
            copy(3 + j, slot, slot, (x, y, 1 - c)).wait_recv()
        for cp in first + passed:
            cp.wait_send()

    return _pcall(
        body, name="allgather_chips", out_shape=jax.ShapeDtypeStruct((4, nr, w), buf.dtype),
        in_specs=[_ANY], out_specs=_ANY,
        scratch_shapes=[pltpu.SemaphoreType.DMA((6,)), pltpu.SemaphoreType.DMA((6,))],
    )(buf)


def _rs_pair_exchange(g):
    _, nr, w = g.shape
    half = nr // 2

    def body(g_ref, rb_ref, send_sem, recv_sem):
        x, y, c = _coords()
        other_rows = pl.ds(pl.multiple_of((1 - c) * half, SUB), half)
        cp = pltpu.make_async_remote_copy(src_ref=g_ref.at[:, other_rows], dst_ref=rb_ref, send_sem=send_sem,
                                          recv_sem=recv_sem, device_id=(x, y, 1 - c), device_id_type=MESH)
        cp.start()
        cp.wait()

    return _pcall(body, name="rs_pair_exchange", out_shape=jax.ShapeDtypeStruct((4, half, w), g.dtype),
                  in_specs=[_ANY], out_specs=_ANY,
                  scratch_shapes=[pltpu.SemaphoreType.DMA, pltpu.SemaphoreType.DMA])(g)


def _rs_pair_add(g, rb, cidx, tb=256):
    _, nr, w = g.shape
    half = nr // 2
    tb = min(tb, half)
    assert half % tb == 0
    hb = half // tb

    def body(c_ref, g_ref, r_ref, o_ref):
        o_ref[...] = (g_ref[...].astype(F32) + r_ref[...].astype(F32)).astype(o_ref.dtype)

    gs = pltpu.PrefetchScalarGridSpec(
        num_scalar_prefetch=1, grid=(4, hb),
        in_specs=[pl.BlockSpec((1, tb, w), lambda j, i, c_ref: (j, c_ref[0] * hb + i, 0)),
                  pl.BlockSpec((1, tb, w), lambda j, i, c_ref: (j, i, 0))],
        out_specs=pl.BlockSpec((1, tb, w), lambda j, i, c_ref: (j, i, 0)))
    return _pcall(body, name="rs_pair_add", grid_spec=gs, out_shape=jax.ShapeDtypeStruct((4, half, w), BF16),
                  compiler_params=_cp("parallel", "parallel"))(cidx, g, rb)


def _rs_chip_exchange(p):
    _, h, w = p.shape

    def body(p_ref, rb_ref, send_sems, recv_sems):
        x, y, c = _coords()
        chips = [(1 - x, y), (x, 1 - y), (1 - x, 1 - y)]
        cps = [pltpu.make_async_remote_copy(src_ref=p_ref.at[2 * cx + cy], dst_ref=rb_ref.at[j], send_sem=send_sems.at[j],
                                            recv_sem=recv_sems.at[j], device_id=(cx, cy, c), device_id_type=MESH)
               for j, (cx, cy) in enumerate(chips)]
        for cp in cps:
            cp.start()
        for cp in cps:
            cp.wait()

    return _pcall(body, name="rs_chip_exchange", out_shape=jax.ShapeDtypeStruct((3, h, w), p.dtype),
                  in_specs=[_ANY], out_specs=_ANY,
                  scratch_shapes=[pltpu.SemaphoreType.DMA((3,)), pltpu.SemaphoreType.DMA((3,))])(p)


def _sum4(p, rb, chip_idx, tb=256):
    _, h, w = rb.shape
    tb = min(tb, h)
    assert h % tb == 0

    def body(m_ref, p_ref, r_ref, o_ref):
        f = lambda t: t.astype(F32)
        o_ref[...] = ((f(p_ref[0]) + f(r_ref[0])) + f(r_ref[1])) + f(r_ref[2])

    gs = pltpu.PrefetchScalarGridSpec(
        num_scalar_prefetch=1, grid=(h // tb,),
        in_specs=[pl.BlockSpec((1, tb, w), lambda i, m_ref: (m_ref[0], i, 0)),
                  pl.BlockSpec((3, tb, w), lambda i, m_ref: (0, i, 0))],
        out_specs=pl.BlockSpec((tb, w), lambda i, m_ref: (i, 0)))
    return _pcall(body, name="sum4", grid_spec=gs, out_shape=jax.ShapeDtypeStruct((h, w), F32),
                  compiler_params=_cp("parallel"))(chip_idx, p, rb)


def _pair_allgather(f):
    h, w = f.shape

    def body(f_ref, out_ref, send_sem, recv_sem):
        x, y, c = _coords()
        mine_rows = pl.ds(pl.multiple_of(c * h, SUB), h)
        other_rows = pl.ds(pl.multiple_of((1 - c) * h, SUB), h)
        send = pltpu.make_async_remote_copy(src_ref=f_ref, dst_ref=out_ref.at[mine_rows], send_sem=send_sem,
                                            recv_sem=recv_sem, device_id=(x, y, 1 - c), device_id_type=MESH)
        send.start()
        send.wait_send()
        pltpu.make_async_remote_copy(src_ref=f_ref, dst_ref=out_ref.at[other_rows], send_sem=send_sem,
                                     recv_sem=recv_sem, device_id=(x, y, 1 - c), device_id_type=MESH).wait_recv()

    return _pcall(body, name="pair_allgather", out_shape=jax.ShapeDtypeStruct((2 * h, w), f.dtype),
                  in_specs=[_ANY], out_specs=_ANY,
                  scratch_shapes=[pltpu.SemaphoreType.DMA, pltpu.SemaphoreType.DMA])(f)


def _size(shape):
    n = 1
    for d in shape:
        n *= d
    return n


PACK_ALIGN = 16


def _pack(arrs, dtype, row_mult):
    parts = []
    for a in arrs:
        flat = a.astype(dtype).reshape(-1)
        n = flat.shape[0]
        full = _ru(n, PACK_W * PACK_ALIGN)
        if full > n:
            flat = jnp.pad(flat, (0, full - n))
        parts.append(flat.reshape(-1, PACK_W))
    rows = sum(p.shape[0] for p in parts)
    if rows % row_mult:
        parts.append(jnp.zeros((_ru(rows, row_mult) - rows, PACK_W), dtype))
    return jnp.concatenate(parts, axis=0)


def _unpack(buf, shapes):
    out, off = [], 0
    for s in shapes:
        n = _size(s)
        r = _ru(-(-n // PACK_W), PACK_ALIGN)
        part = buf[off:off + r]
        out.append(part.reshape(s) if n == r * PACK_W else part.reshape(-1)[:n].reshape(s))
        off += r
    return out


def _pack_flat(arrs, dtype, row_mult):
    flat = jnp.concatenate([a.astype(dtype).reshape(-1) for a in arrs])
    n = flat.shape[0]
    rows = _ru(-(-n // PACK_W), row_mult)
    return jnp.pad(flat, (0, rows * PACK_W - n)).reshape(rows, PACK_W)


def _unpack_flat(buf, shapes):
    flat = buf.reshape(-1)
    out, off = [], 0
    for s in shapes:
        n = _size(s)
        out.append(flat[off:off + n].reshape(s))
        off += n
    return out


def _in_layout(D, GW, GH, FW, FH, MW, tn):
    o_z = 3 * GW
    o_beta = 4 * GW
    o_fq = o_beta + 2 * GH
    o_ff = o_fq + 3 * FW
    o_mq = o_ff + FH
    o_g = o_mq + MW
    orig = {"q": (0, GW), "k": (GW, GW), "v": (2 * GW, GW), "z": (o_z, GW), "beta": (o_beta, GH), "dec": (o_beta + GH, GH),
            "fq": (o_fq, FW), "fk": (o_fq + FW, FW), "fv": (o_fq + 2 * FW, FW), "ff": (o_ff, FH), "mq": (o_mq, MW),
            "ga": (o_g, D), "gb": (o_g + D, D), "gm": (o_g + 2 * D, D)}
    offs, cur = {}, 0
    for key, width in (("ga", D), ("gb", D), ("gm", D), ("q", GW), ("k", GW), ("v", GW), ("z", GW),
                       ("fq", FW), ("fk", FW), ("fv", FW), ("mq", MW), ("small", LANES)):
        cur = _ru(cur, width)
        offs[key] = cur
        cur += width
    total = _ru(cur, tn)
    pieces = [(offs[k], orig[k][0], orig[k][1]) for k in ("ga", "gb", "gm", "q", "k", "v", "z", "fq", "fk", "fv", "mq")]
    pieces += [(offs["small"], orig["beta"][0], GH), (offs["small"] + GH, orig["dec"][0], GH),
               (offs["small"] + 2 * GH, orig["ff"][0], FH)]
    return offs, total, pieces, o_g + 3 * D


def _pad_cols(w, pieces, total):
    parts, cur = [], 0
    for pstart, ostart, n in pieces:
        if pstart > cur:
            parts.append(jnp.zeros((w.shape[0], pstart - cur), w.dtype))
        parts.append(w[:, ostart:ostart + n])
        cur = pstart + n
    if total > cur:
        parts.append(jnp.zeros((w.shape[0], total - cur), w.dtype))
    return jnp.concatenate(parts, axis=1)


def _unpad_cols(wp, pieces):
    return jnp.concatenate([wp[:, pstart:pstart + n] for pstart, ostart, n in sorted(pieces, key=lambda t: t[1])], axis=1)


def _local_step(x, mem, tgt, W, flash_blk=512):
    T, D = x.shape
    GW = W["w_up_gdn"].shape[0]
    FW = W["w_up_fox"].shape[0]
    MW = W["w_up_mem"].shape[0]
    GH, FH = GW // HEAD, FW // HEAD
    offs, NP, pieces, d_in = _in_layout(D, GW, GH, FW, FH, MW, 1024)
    assert W["w_in"].shape[1] == d_in
    w_in_p = _pad_cols(W["w_in"], pieces, NP)
    cw = W["conv_w"]
    cws = [cw[:, i * GW:(i + 1) * GW] for i in range(3)]
    zl = jnp.zeros((1, LANES), F32)
    pvecs = [lax.dynamic_update_slice(zl, W["a_log"], (0, GH)), lax.dynamic_update_slice(zl, W["dt_bias"], (0, GH)),
             lax.dynamic_update_slice(zl, W["fox_b_f"], (0, 2 * GH))]
    lane0 = 2 * GH

    h = _rms_fwd(x, W["g_mix"], "rms_mix")
    proj = _mm(h, w_in_p, "nn", "in_proj")
    qn, kn, vc, cum, gbm = _gdn_prep(proj, offs, cws, pvecs, GH, FH, GW)
    o_gdn, sall = _gdn_scan_fwd(qn, kn, vc, gbm, GH)
    o_a = _gdn_post(o_gdn, proj, offs["z"], W["gdn_norm_g"], GW)
    qb, kb, vb = _fox_prep(proj, offs, W["fox_q_norm"], W["fox_k_norm"], FW)
    cumh = cum[:, lane0:lane0 + FH].T
    cumc, cumr = cumh.reshape(FH, T, 1), cumh.reshape(FH, 1, T)
    o_b16, lse = _flash_fwd(qb, kb, vb, cumr, FH, flash_blk)
    memn = _rms_fwd(mem, W["g_mem"], "rms_mem")
    kv = _mm(memn, W["w_mem_kv"], "nn", "mem_kv")
    o_m = _mem_attn(proj, offs["mq"], kv, W["mem_q_norm"], W["mem_k_norm"], MW)
    ua = _mm(o_a, W["w_up_gdn"], "nn", "up_gdn")
    ub = _mm(o_b16, W["w_up_fox"], "nn", "up_fox")
    um = _mm(o_m, W["w_up_mem"], "nn", "up_mem")
    y = _merge(proj, offs, ua, ub, um, D)
    x1 = _mm(y, W["w_out"], "nn", "out_proj", epilogue=lambda acc, r: (acc + r,), extras=(x,))
    h2 = _rms_fwd(x1, W["g_mlp"], "rms_mlp")
    u, a = _mm(h2, W["w_ff1"], "nn", "ff1", out_dtypes=(F32, BF16),
               epilogue=lambda acc: (acc, jnp.square(jnp.maximum(acc, 0.0))))
    x2 = _mm(a, W["w_ff2"], "nn", "ff2", epilogue=lambda acc, r: (acc + r,), extras=(x1,))
    dx2, dx2b, lpart = _loss_grad(x2, tgt)
    loss = lpart[0, 0]

    G = {}
    du = _mm(dx2b, W["w_ff2"], "nt", "ff2_dx", out_dtypes=(BF16,),
             epilogue=lambda acc, uu: (acc * (2.0 * jnp.maximum(uu, 0.0)),), extras=(u,))
    G["w_ff2"] = _mm(a, dx2b, "tn", "ff2_dw", out_dtypes=(BF16,))
    dh2 = _mm(du, W["w_ff1"], "nt", "ff1_dx")
    G["w_ff1"] = _mm(h2, du, "tn", "ff1_dw", out_dtypes=(BF16,))
    dx1, dx1b, G["g_mlp"] = _rms_bwd(x1, W["g_mlp"], dh2, dx2, "rms_mlp_bwd")
    dy = _mm(dx1b, W["w_out"], "nt", "out_dx")
    G["w_out"] = _mm(y, dx1b, "tn", "out_dw", out_dtypes=(BF16,))
    dga, dgb, dgm, dua, dub, dum = _merge_bwd(proj, offs, ua, ub, um, dy, D)
    do_a = _mm(dua, W["w_up_gdn"], "nt", "up_gdn_dx")
    G["w_up_gdn"] = _mm(o_a, dua, "tn", "up_gdn_dw", out_dtypes=(BF16,))
    do_b = _mm(dub, W["w_up_fox"], "nt", "up_fox_dx")
    G["w_up_fox"] = _mm(o_b16, dub, "tn", "up_fox_dw", out_dtypes=(BF16,))
    do_m = _mm(dum, W["w_up_mem"], "nt", "up_mem_dx")
    G["w_up_mem"] = _mm(o_m, dum, "tn", "up_mem_dw", out_dtypes=(BF16,))
    dmq, dkv, G["mem_q_norm"], G["mem_k_norm"] = _mem_attn_bwd(proj, offs["mq"], kv, W["mem_q_norm"], W["mem_k_norm"], do_m, MW)
    dkvb = dkv.astype(BF16)
    dmemn = _mm(dkvb, W["w_mem_kv"], "nt", "mem_kv_dx")
    G["w_mem_kv"] = _mm(memn, dkvb, "tn", "mem_kv_dw", out_dtypes=(BF16,))
    G["g_mem"] = _rms_dg(mem, W["g_mem"], dmemn, "rms_mem_bwd")
    dl = _flash_bwd_q(qb, kb, vb, cumr, lse, do_b, None, FH, flash_blk)
    dqb, dcq = _flash_bwd_q(qb, kb, vb, cumr, lse, do_b, dl, FH, flash_blk)
    dkb, dvb, dck = _flash_bwd_dkv(qb, kb, vb, cumc, lse.reshape(FH, 1, T), dl.reshape(FH, 1, T), do_b, FH, flash_blk)
    dlf = _rev_cumsum_rows(dcq.reshape(FH, T), dck.reshape(FH, T))
    dlf_sm = jnp.pad(dlf.T, ((0, 0), (lane0, LANES - lane0 - FH)))
    dfq, dfk, dfv, G["fox_q_norm"], G["fox_k_norm"] = _fox_prep_bwd(proj, offs, W["fox_q_norm"], W["fox_k_norm"], dqb, dkb, dvb, FW)
    do_gdn, dz, G["gdn_norm_g"] = _gdn_post_bwd(o_gdn, proj, offs["z"], W["gdn_norm_g"], do_a, GW)
    dqn, dkn, dvc, dgsm = _gdn_scan_bwd(qn, kn, vc, gbm, sall, do_gdn, GH)
    dyq, dyk, dyv, dcq, dck_w, dcv, dsmall, dalog, ddtb, dbf = _gdn_prep_bwd_a(
        proj, offs, cws, pvecs, (dqn, dkn, dvc), dgsm, dlf_sm, GH, FH, GW)
    dxq, dxk, dxv = _gdn_prep_bwd_b((dyq, dyk, dyv), cws, GW)
    G["conv_w"] = jnp.concatenate([dcq[:CONV_K], dck_w[:CONV_K], dcv[:CONV_K]], axis=1)
    G["a_log"] = dalog[:, GH:2 * GH]
    G["dt_bias"] = ddtb[:, GH:2 * GH]
    G["fox_b_f"] = dbf[:, lane0:lane0 + FH]
    segs = {"ga": dga, "gb": dgb, "gm": dgm, "q": dxq, "k": dxk, "v": dxv, "z": dz, "fq": dfq, "fk": dfk, "fv": dfv,
            "mq": dmq, "small": dsmall}
    parts, cur = [], 0
    for key in ("ga", "gb", "gm", "q", "k", "v", "z", "fq", "fk", "fv", "mq", "small"):
        if offs[key] > cur:
            parts.append(jnp.zeros((T, offs[key] - cur), BF16))
        parts.append(segs[key])
        cur = offs[key] + segs[key].shape[1]
    if NP > cur:
        parts.append(jnp.zeros((T, NP - cur), BF16))
    dproj = jnp.concatenate(parts, axis=1)
    dh = _mm(dproj, w_in_p, "nt", "in_dx")
    G["w_in"] = _unpad_cols(_mm(h, dproj, "tn", "in_dw", out_dtypes=(BF16,)), pieces)
    grad_x, _, G["g_mix"] = _rms_bwd(x, W["g_mix"], dh, dx1, "rms_mix_bwd")
    return loss, grad_x, G


BIG = ["w_in", "w_mem_kv", "w_up_gdn", "w_up_fox", "w_up_mem", "w_out", "w_ff1", "w_ff2"]
SMALL = ["g_mix", "a_log", "dt_bias", "gdn_norm_g", "fox_b_f", "fox_q_norm", "fox_k_norm", "g_mem", "mem_q_norm",
         "mem_k_norm", "g_mlp"]
ORDER = ["g_mix", "w_in", "conv_w", "a_log", "dt_bias", "gdn_norm_g", "fox_b_f", "fox_q_norm", "fox_k_norm", "g_mem",
         "w_mem_kv", "mem_q_norm", "mem_k_norm", "w_up_gdn", "w_up_fox", "w_up_mem", "w_out", "g_mlp", "w_ff1", "w_ff2"]
SHARD_AXIS = {"w_in": 1, "w_mem_kv": 0, "w_up_gdn": 1, "w_up_fox": 1, "w_up_mem": 1, "w_out": 0, "w_ff1": 1, "w_ff2": 0}


def _step(x, mem, tgt, w, m, v, flash_blk=512):
    xi, yi, ci = _coords()
    chip = 2 * xi + yi

    shard_shapes = [w[n].shape for n in BIG]
    packed_w = _pack([w[n] for n in BIG], BF16, PACK_ROWS)
    gathered = _allgather_chips(packed_w)
    gathered = lax.dynamic_update_slice(gathered, packed_w[None], (chip, 0, 0))
    per_chip = [_unpack(gathered[j], shard_shapes) for j in range(4)]
    W = {n: jnp.concatenate([per_chip[j][i] for j in range(4)], axis=SHARD_AXIS[n]) for i, n in enumerate(BIG)}
    cw_rows = jnp.pad(w["conv_w"], ((0, SUB - CONV_K), (0, 0)))
    cw_all = _allgather_small(cw_rows)
    W["conv_w"] = jnp.concatenate([cw_all[16 * j:16 * j + CONV_K] for j in range(4)], axis=1)
    for n in SMALL:
        W[n] = w[n]

    loss, grad_x, G = _local_step(x, mem, tgt, W, flash_blk)
    loss = lax.psum(loss, ("x", "y", "c"))

    small_shapes = [G[n].shape for n in SMALL] + [G["conv_w"].shape]
    sm = _pack_flat([G[n] for n in SMALL] + [G["conv_w"]], F32, SUB)
    sm_sum = _sum8(_allgather_small(sm), sm.shape[0])
    sm_list = _unpack_flat(sm_sum, small_shapes)
    g = {n: sm_list[i] for i, n in enumerate(SMALL)}
    cw_full = sm_list[-1]
    gw4 = cw_full.shape[1] // 4
    g["conv_w"] = lax.dynamic_slice(cw_full, (0, chip * gw4), (CONV_K, gw4))

    by_dest = []
    for j in range(4):
        shards = []
        for n in BIG:
            size = w[n].shape[SHARD_AXIS[n]]
            shards.append(lax.slice_in_dim(G[n], j * size, (j + 1) * size, axis=SHARD_AXIS[n]))
        by_dest.append(_pack(shards, BF16, PACK_ROWS))
    gflat = jnp.stack(by_dest)
    rb1 = _rs_pair_exchange(gflat)
    part = _rs_pair_add(gflat, rb1, jnp.reshape(ci, (1,)).astype(jnp.int32))
    rb2 = _rs_chip_exchange(part)
    half_sum = _sum4(part, rb2, jnp.reshape(chip, (1,)).astype(jnp.int32))
    mine = _pair_allgather(half_sum)
    mine = lax.dynamic_update_slice(mine, half_sum, (ci * half_sum.shape[0], 0))
    for i, gv in enumerate(_unpack(mine, shard_shapes)):
        g[BIG[i]] = gv

    delta, new_m, new_v = {}, {}, {}
    for n in BIG:
        delta[n], new_m[n], new_v[n] = _adamw(w[n], g[n], m[n], v[n], "adamw_" + n)
    rest = SMALL + ["conv_w"]
    rest_shapes = [w[n].shape for n in rest]
    packed = [_pack_flat([d[n] for n in rest], F32, SUB) for d in (w, g, m, v)]
    outs = _adamw(*packed, "adamw_small", tb=packed[0].shape[0])
    for d, buf in zip((delta, new_m, new_v), outs):
        for n, val in zip(rest, _unpack_flat(buf, rest_shapes)):
            d[n] = val
    return loss, grad_x, g, delta, new_m, new_v


def kernel(x, mem, g_mix, w_in, conv_w, a_log, dt_bias, gdn_norm_g, fox_b_f, fox_q_norm, fox_k_norm, g_mem, w_mem_kv, mem_q_norm, mem_k_norm, w_up_gdn, w_up_fox, w_up_mem, w_out, g_mlp, w_ff1, w_ff2, loss_target, m_g_mix, m_w_in, m_conv_w, m_a_log, m_dt_bias, m_gdn_norm_g, m_fox_b_f, m_fox_q_norm, m_fox_k_norm, m_g_mem, m_w_mem_kv, m_mem_q_norm, m_mem_k_norm, m_w_up_gdn, m_w_up_fox, m_w_up_mem, m_w_out, m_g_mlp, m_w_ff1, m_w_ff2, v_g_mix, v_w_in, v_conv_w, v_a_log, v_dt_bias, v_gdn_norm_g, v_fox_b_f, v_fox_q_norm, v_fox_k_norm, v_g_mem, v_w_mem_kv, v_mem_q_norm, v_mem_k_norm, v_w_up_gdn, v_w_up_fox, v_w_up_mem, v_w_out, v_g_mlp, v_w_ff1, v_w_ff2):
    ws = (g_mix, w_in, conv_w, a_log, dt_bias, gdn_norm_g, fox_b_f, fox_q_norm, fox_k_norm, g_mem, w_mem_kv, mem_q_norm,
          mem_k_norm, w_up_gdn, w_up_fox, w_up_mem, w_out, g_mlp, w_ff1, w_ff2)
    ms = (m_g_mix, m_w_in, m_conv_w, m_a_log, m_dt_bias, m_gdn_norm_g, m_fox_b_f, m_fox_q_norm, m_fox_k_norm, m_g_mem,
          m_w_mem_kv, m_mem_q_norm, m_mem_k_norm, m_w_up_gdn, m_w_up_fox, m_w_up_mem, m_w_out, m_g_mlp, m_w_ff1, m_w_ff2)
    vs = (v_g_mix, v_w_in, v_conv_w, v_a_log, v_dt_bias, v_gdn_norm_g, v_fox_b_f, v_fox_q_norm, v_fox_k_norm, v_g_mem,
          v_w_mem_kv, v_mem_q_norm, v_mem_k_norm, v_w_up_gdn, v_w_up_fox, v_w_up_mem, v_w_out, v_g_mlp, v_w_ff1, v_w_ff2)
    drop = lambda a: a[0] if a.ndim == 3 else a
    w = {n: drop(a) for n, a in zip(ORDER, ws)}
    m = {n: drop(a) for n, a in zip(ORDER, ms)}
    v = {n: drop(a) for n, a in zip(ORDER, vs)}
    loss, grad_x, g, delta, new_m, new_v = _step(x[0], mem[0], loss_target[0], w, m, v)
    out = [loss, grad_x[None]]
    for d in (g, delta, new_m, new_v):
        out += [d[n].reshape(a.shape) for n, a in zip(ORDER, ws)]
    return tuple(out)
```

```python
import numpy as np

import jax
import jax.numpy as jnp
from jax import lax
from jax.experimental import pallas as pl
from jax.experimental.pallas import tpu as pltpu

F32 = jnp.float32
BF16 = jnp.bfloat16
HI = lax.Precision.HIGHEST
MESH = pl.DeviceIdType.MESH

EPS = 1e-6
HEAD = 128
MEM_DH = 256
CONV_K = 4
CHUNK = 64
CHUNK_SHIFT = 6
LANES = 128
SUB = 8
PACK_W = 1024
PACK_ROWS = 512
VMEM_LIMIT = 56 * 1024 * 1024
NEG = -1e30
SOLVE_PREC = None
FLASH_HEADS_PER_STEP = 8
FLASH_FWD_HEADS_PER_STEP = 4

ADAM_LR, ADAM_B1, ADAM_B2, ADAM_EPS, ADAM_WD, ADAM_STEP = 0.001, 0.9, 0.999, 1e-08, 0.01, 10


def _pcall(body, **kw):
    return pl.pallas_call(body, **kw)


def _cp(*sem):
    return pltpu.CompilerParams(dimension_semantics=sem, vmem_limit_bytes=VMEM_LIMIT)


def _dot(a, b, prec=None):
    return lax.dot_general(a, b, (((1,), (0,)), ((), ())), precision=prec, preferred_element_type=F32)


def _dot_nt(a, b, prec=None):
    return lax.dot_general(a, b, (((1,), (1,)), ((), ())), precision=prec, preferred_element_type=F32)


def _dot_tn(a, b, prec=None):
    return lax.dot_general(a, b, (((0,), (0,)), ((), ())), precision=prec, preferred_element_type=F32)


def _sigmoid(x):
    return 1.0 / (1.0 + jnp.exp(-x))


def _softplus(x):
    return jnp.maximum(x, 0.0) + jnp.log(1.0 + jnp.exp(-jnp.abs(x)))


def _silu(x):
    return x * _sigmoid(x)


def _rms(x, g):
    return x * lax.rsqrt(jnp.mean(x * x, axis=-1, keepdims=True) + EPS) * g


def _ru(a, m):
    return (a + m - 1) // m * m


def _mm(a, b, mode, name, out_dtypes=(F32,), epilogue=None, extras=(), tm=1024, tn=1024, tk=2048):
    if mode == "nn":
        (M, K), (K2, N) = a.shape, b.shape
    elif mode == "nt":
        (M, K), (N, K2) = a.shape, b.shape
    else:
        (K, M), (K2, N) = a.shape, b.shape
    assert K == K2, (a.shape, b.shape, mode)
    tm, tn = min(tm, M), min(tn, N)
    tk = next((t for t in (tk, 1024, 512, 256, LANES) if t <= K and K % t == 0), K)
    assert M % tm == 0 and N % tn == 0 and K % tk == 0, (M, N, K, tm, tn, tk)
    nk = K // tk
    n_ex, n_out = len(extras), len(out_dtypes)
    dims = {"nn": ((1,), (0,)), "nt": ((1,), (1,)), "tn": ((0,), (0,))}[mode]

    def finish(res, ex_refs, out_refs):
        outs = epilogue(res, *[r[...] for r in ex_refs]) if epilogue is not None else (res,)
        for o_ref, o in zip(out_refs, outs):
            o_ref[...] = o.astype(o_ref.dtype)

    def body(a_ref, b_ref, *rest):
        ex_refs, out_refs = rest[:n_ex], rest[n_ex:n_ex + n_out]
        part = lax.dot_general(a_ref[...], b_ref[...], (dims, ((), ())), preferred_element_type=F32)
        if nk == 1:
            finish(part, ex_refs, out_refs)
            return
        acc = rest[-1]
        k = pl.program_id(2)

        @pl.when(k == 0)
        def _():
            acc[...] = part

        @pl.when(k > 0)
        def _():
            acc[...] += part

        @pl.when(k == nk - 1)
        def _():
            finish(acc[...], ex_refs, out_refs)

    a_spec = pl.BlockSpec((tk, tm), lambda i, j, k: (k, i)) if mode == "tn" else pl.BlockSpec((tm, tk), lambda i, j, k: (i, k))
    b_spec = pl.BlockSpec((tn, tk), lambda i, j, k: (j, k)) if mode == "nt" else pl.BlockSpec((tk, tn), lambda i, j, k: (k, j))
    mn_spec = pl.BlockSpec((tm, tn), lambda i, j, k: (i, j))
    outs = _pcall(
        body, name=name, grid=(M // tm, N // tn, nk),
        in_specs=[a_spec, b_spec] + [mn_spec] * n_ex,
        out_specs=[mn_spec] * n_out,
        out_shape=[jax.ShapeDtypeStruct((M, N), dt) for dt in out_dtypes],
        scratch_shapes=[pltpu.VMEM((tm, tn), F32)] if nk > 1 else [],
        compiler_params=_cp("parallel", "parallel", "arbitrary"),
    )(a, b, *extras)
    return outs[0] if n_out == 1 else outs


def _rowwise(fn, T, tb, ins, outs, name, scratch=()):
    tb = min(tb, T)
    assert T % tb == 0 and (tb % SUB == 0 or tb == T)
    nblk = T // tb
    r8 = tb // SUB
    in_specs, arrs = [], []
    for spec in ins:
        kind, arr = spec[0], spec[1]
        arrs.append(arr)
        if kind == "full":
            nd = arr.ndim
            in_specs.append(pl.BlockSpec(arr.shape, lambda i, nd=nd: (0,) * nd))
            continue
        off, w = spec[2], spec[3]
        assert off % w == 0 and arr.shape[0] == T, (name, off, w, arr.shape)
        cb = off // w
        if kind == "row":
            in_specs.append(pl.BlockSpec((tb, w), lambda i, cb=cb: (i, cb)))
        elif kind == "prev":
            in_specs.append(pl.BlockSpec((SUB, w), lambda i, cb=cb: (jnp.maximum(i * r8 - 1, 0), cb)))
        else:
            in_specs.append(pl.BlockSpec((SUB, w), lambda i, cb=cb: (jnp.minimum((i + 1) * r8, T // SUB - 1), cb)))
    out_specs, out_shapes, is_acc = [], [], []
    for spec in outs:
        if spec[0] == "row":
            out_specs.append(pl.BlockSpec((tb, spec[1]), lambda i: (i, 0)))
            out_shapes.append(jax.ShapeDtypeStruct((T, spec[1]), spec[2]))
            is_acc.append(False)
        else:
            nd = len(spec[1])
            out_specs.append(pl.BlockSpec(spec[1], lambda i, nd=nd: (0,) * nd))
            out_shapes.append(jax.ShapeDtypeStruct(spec[1], spec[2]))
            is_acc.append(True)
    n_in, n_out = len(ins), len(outs)
    seq = any(is_acc) or len(scratch) > 0

    def body(*refs):
        in_refs, out_refs, scr = refs[:n_in], refs[n_in:n_in + n_out], refs[n_in + n_out:]
        i = pl.program_id(0)
        vals = fn(i, nblk, *[r[...] for r in in_refs], *scr)
        for o_ref, v, acc in zip(out_refs, vals, is_acc):
            if acc:
                @pl.when(i == 0)
                def _(o_ref=o_ref):
                    o_ref[...] = jnp.zeros_like(o_ref)

                o_ref[...] += v.astype(o_ref.dtype)
            else:
                o_ref[...] = v.astype(o_ref.dtype)

    res = _pcall(
        body, name=name, grid=(nblk,), in_specs=in_specs, out_specs=out_specs, out_shape=out_shapes,
        scratch_shapes=list(scratch), compiler_params=_cp("arbitrary" if seq else "parallel"),
    )(*arrs)
    return res


def _heads(x, width):
    return [x[:, h * width:(h + 1) * width] for h in range(x.shape[1] // width)]


def _cat(xs):
    return xs[0] if len(xs) == 1 else jnp.concatenate(xs, axis=1)


def _rms_fwd(x, g, name, tb=512):
    T, D = x.shape
    return _rowwise(lambda i, n, xv, gv: (_rms(xv, gv),), T, tb,
                    [("row", x, 0, D), ("full", g)], [("row", D, BF16)], name)[0]


def _rms_bwd(x, g, dh, dres, name, tb=256):
    T, D = x.shape

    def fn(i, n, xv, gv, dhv, drv):
        _, vjp = jax.vjp(_rms, xv, gv)
        dx, dg = vjp(dhv)
        tot = drv + dx
        return tot, tot, dg

    return _rowwise(fn, T, tb, [("row", x, 0, D), ("full", g), ("row", dh, 0, D), ("row", dres, 0, D)],
                    [("row", D, F32), ("row", D, BF16), ("acc", (1, D), F32)], name)


def _rms_dg(x, g, dh, name, tb=256):
    T, D = x.shape

    def fn(i, n, xv, gv, dhv):
        _, vjp = jax.vjp(lambda gg: _rms(xv, gg), gv)
        return vjp(dhv)

    return _rowwise(fn, T, tb, [("row", x, 0, D), ("full", g), ("row", dh, 0, D)], [("acc", (1, D), F32)], name)[0]


def _shift_down(x, halo, s, first):
    if s == 0:
        return x
    tb, c = x.shape
    xr = pltpu.roll(x, s, 0)
    hr = jnp.where(first, 0.0, pltpu.roll(halo, s, 0))
    hfull = hr if tb == SUB else jnp.concatenate([hr, jnp.zeros((tb - SUB, c), x.dtype)], axis=0)
    row = lax.broadcasted_iota(jnp.int32, x.shape, 0)
    return jnp.where(row < s, hfull, xr)


def _shift_up(z, halo, s, last):
    if s == 0:
        return z
    tb, c = z.shape
    zr = pltpu.roll(z, tb - s, 0)
    hr = jnp.where(last, 0.0, pltpu.roll(halo, SUB - s, 0))
    hfull = hr if tb == SUB else jnp.concatenate([jnp.zeros((tb - SUB, c), z.dtype), hr], axis=0)
    row = lax.broadcasted_iota(jnp.int32, z.shape, 0)
    return jnp.where(row >= tb - s, hfull, zr)


def _conv_pre(x, halo, cw, first):
    xs = [_shift_down(x, halo, s, first) for s in range(CONV_K)]
    y = cw[0:1, :] * xs[3]
    for i in range(1, CONV_K):
        y = y + cw[i:i + 1, :] * xs[CONV_K - 1 - i]
    return y, xs


def _qk_post(y, scale):
    a = _silu(y)
    return a * lax.rsqrt(jnp.sum(a * a, axis=-1, keepdims=True) + EPS) * scale


def _small_fn(s, alog, dtb, bf, gh, fh):
    lane = lax.broadcasted_iota(jnp.int32, s.shape, 1)
    beta = _sigmoid(s)
    g = -jnp.exp(alog) * _softplus(s + dtb)
    lf = -_softplus(-(s + bf))
    return jnp.where(lane < gh, beta, jnp.where(lane < 2 * gh, g, jnp.where(lane < 2 * gh + fh, lf, 0.0)))


def _gdn_prep(proj, offs, cws, pvecs, GH, FH, GW, tb=256):
    T = proj.shape[0]
    tb = min(tb, T)
    qscale = HEAD ** -0.5

    def fn(i, n, xq, hq, xk, hk, xv, hv, cwq, cwk, cwv, s, alog, dtb, bf, carry):
        first = i == 0
        yq, _ = _conv_pre(xq, hq, cwq, first)
        yk, _ = _conv_pre(xk, hk, cwk, first)
        yv, _ = _conv_pre(xv, hv, cwv, first)
        qn = _cat([_qk_post(y, qscale) for y in _heads(yq, HEAD)])
        kn = _cat([_qk_post(y, 1.0) for y in _heads(yk, HEAD)])
        vc = _silu(yv)
        gsm = _small_fn(s, alog, dtb, bf, GH, FH)

        @pl.when(first)
        def _():
            carry[...] = jnp.zeros_like(carry)

        ri = lax.broadcasted_iota(jnp.int32, (tb, tb), 0)
        ci = lax.broadcasted_iota(jnp.int32, (tb, tb), 1)
        cum = _dot((ri >= ci).astype(F32), gsm, HI) + carry[0:1, :]
        carry[...] += _dot(jnp.ones((SUB, tb), F32), gsm, HI)
        in_chunk = (ri >= ci) & ((ri >> CHUNK_SHIFT) == (ci >> CHUNK_SHIFT))
        lane = lax.broadcasted_iota(jnp.int32, gsm.shape, 1)
        gbm = jnp.where(lane < GH, gsm, _dot(in_chunk.astype(F32), gsm, HI))
        return qn, kn, vc, cum, gbm

    ins = []
    for key in ("q", "k", "v"):
        ins += [("row", proj, offs[key], GW), ("prev", proj, offs[key], GW)]
    ins += [("full", c) for c in cws] + [("row", proj, offs["small"], LANES)] + [("full", p) for p in pvecs]
    outs = [("row", GW, F32)] * 3 + [("row", LANES, F32)] * 2
    return _rowwise(fn, T, tb, ins, outs, "gdn_prep", scratch=[pltpu.VMEM((SUB, LANES), F32)])


def _gdn_prep_bwd_a(proj, offs, cws, pvecs, cts, dgsm_scan, dlf_sm, GH, FH, GW, tb=256):
    T = proj.shape[0]
    qscale = HEAD ** -0.5

    def one(x, halo, cw, ct, first, post):
        y, xs = _conv_pre(x, halo, cw, first)
        if post is None:
            _, vjp = jax.vjp(_silu, y)
            dy = vjp(ct)[0]
        else:
            dys = []
            for yh, cth in zip(_heads(y, HEAD), _heads(ct, HEAD)):
                _, vjp = jax.vjp(lambda t: _qk_post(t, post), yh)
                dys.append(vjp(cth)[0])
            dy = _cat(dys)
        row = lax.broadcasted_iota(jnp.int32, (SUB, x.shape[1]), 0)
        dcw = jnp.zeros((SUB, x.shape[1]), F32)
        for i in range(CONV_K):
            dcw = dcw + jnp.where(row == i, jnp.sum(dy * xs[CONV_K - 1 - i], axis=0, keepdims=True), 0.0)
        return dy, dcw

    def fn(i, n, xq, hq, xk, hk, xv, hv, cwq, cwk, cwv, cq, ck, cv, s, alog, dtb, bf, d1, d2):
        first = i == 0
        dyq, dcq = one(xq, hq, cwq, cq, first, qscale)
        dyk, dck = one(xk, hk, cwk, ck, first, 1.0)
        dyv, dcv = one(xv, hv, cwv, cv, first, None)
        tb_ = d1.shape[0]
        ri = lax.broadcasted_iota(jnp.int32, (tb_, tb_), 0)
        ci = lax.broadcasted_iota(jnp.int32, (tb_, tb_), 1)
        later = (ci >= ri) & ((ri >> CHUNK_SHIFT) == (ci >> CHUNK_SHIFT))
        lane = lax.broadcasted_iota(jnp.int32, d1.shape, 1)
        d1 = jnp.where(lane < GH, d1, _dot(later.astype(F32), d1, HI))
        _, vjp = jax.vjp(lambda a, b, c, d: _small_fn(a, b, c, d, GH, FH), s, alog, dtb, bf)
        ds, dalog, ddtb, dbf = vjp(d1 + d2)
        return dyq, dyk, dyv, dcq, dck, dcv, ds, dalog, ddtb, dbf

    ins = []
    for key in ("q", "k", "v"):
        ins += [("row", proj, offs[key], GW), ("prev", proj, offs[key], GW)]
    ins += [("full", c) for c in cws] + [("row", c, 0, GW) for c in cts]
    ins += [("row", proj, offs["small"], LANES)] + [("full", p) for p in pvecs]
    ins += [("row", dgsm_scan, 0, LANES), ("row", dlf_sm, 0, LANES)]
    outs = [("row", GW, F32)] * 3 + [("acc", (SUB, GW), F32)] * 3 + [("row", LANES, BF16)] + [("acc", (1, LANES), F32)] * 3
    return _rowwise(fn, T, tb, ins, outs, "gdn_prep_bwd_a")


def _gdn_prep_bwd_b(dys, cws, GW, tb=256):
    T = dys[0].shape[0]

    def fn(i, n, dq, nq, dk, nk, dv, nv, cwq, cwk, cwv):
        last = i == n - 1
        res = []
        for dy, nh, cw in ((dq, nq, cwq), (dk, nk, cwk), (dv, nv, cwv)):
            dx = cw[CONV_K - 1:CONV_K, :] * dy
            for t in range(CONV_K - 1):
                dx = dx + cw[t:t + 1, :] * _shift_up(dy, nh, CONV_K - 1 - t, last)
            res.append(dx)
        return tuple(res)

    ins = []
    for dy in dys:
        ins += [("row", dy, 0, GW), ("next", dy, 0, GW)]
    ins += [("full", c) for c in cws]
    return _rowwise(fn, T, tb, ins, [("row", GW, BF16)] * 3, "gdn_prep_bwd_b")


def _gdn_chunks(qs, ks, vs, gams, bcols, s0s):
    c, d = qs[0].shape
    hs = range(len(qs))
    ri = lax.broadcasted_iota(jnp.int32, (c, c), 0)
    ci = lax.broadcasted_iota(jnp.int32, (c, c), 1)
    incl, strict = ri >= ci, ri > ci
    eye = (ri == ci).astype(F32)
    ones_cc = jnp.ones((c, c), F32)
    rows = lax.broadcasted_iota(jnp.int32, (c, 1), 0)
    b16 = (ri >> 4) == (ci >> 4)
    b32 = (ri >> 5) == (ci >> 5)
    gam_cc = [gams[h] * ones_cc for h in hs]
    gam_t = [_dot_nt(eye, gam_cc[h], HI) for h in hs]
    glast = [jnp.sum(jnp.where(rows == c - 1, gams[h], 0.0), axis=0, keepdims=True) for h in hs]
    dec_i = [jnp.where(incl, jnp.exp(jnp.where(incl, gam_cc[h] - gam_t[h], 0.0)), 0.0) for h in hs]
    kk = [_dot_nt(ks[h], ks[h]) for h in hs]
    m = [bcols[h] * kk[h] * jnp.where(strict, dec_i[h], 0.0) for h in hs]
    m32 = [jnp.where(b32 & ~b16, m[h], 0.0) for h in hs]
    m64 = [jnp.where(b32, 0.0, m[h]) for h in hs]
    mp = [jnp.where(b16, m[h], 0.0) for h in hs]
    p = [eye - mp[h] for h in hs]
    for _ in range(3):
        mp = [_dot(mp[h], mp[h], SOLVE_PREC) for h in hs]
        p = [p[h] + _dot(p[h], mp[h], SOLVE_PREC) for h in hs]
    t = [_dot(p[h], m32[h], SOLVE_PREC) for h in hs]
    p = [p[h] - _dot(t[h], p[h], SOLVE_PREC) for h in hs]
    t = [_dot(p[h], m64[h], SOLVE_PREC) for h in hs]
    ainv = [p[h] - _dot(t[h], p[h], SOLVE_PREC) for h in hs]
    eg = [jnp.exp(gams[h]) for h in hs]
    w = [_dot(ainv[h], (bcols[h] * eg[h]) * ks[h], SOLVE_PREC) for h in hs]
    u0 = [_dot(ainv[h], bcols[h] * vs[h], SOLVE_PREC) for h in hs]
    qk = [_dot_nt(qs[h], ks[h]) * dec_i[h] for h in hs]
    u = [u0[h] - _dot(w[h], s0s[h]) for h in hs]
    o = [_dot(qs[h] * eg[h], s0s[h]) + _dot(qk[h], u[h]) for h in hs]
    s1 = [jnp.exp(glast[h]) * s0s[h] + _dot_tn(ks[h] * jnp.exp(glast[h] - gams[h]), u[h]) for h in hs]
    return tuple(o), tuple(s1)


def _lane_col(x, lane_idx):
    lane = lax.broadcasted_iota(jnp.int32, x.shape, 1)
    return jnp.sum(jnp.where(lane == lane_idx, x, 0.0), axis=1, keepdims=True)


def _gdn_scan_fwd(qn, kn, vc, gsm, GH):
    T, GW = qn.shape
    nc = T // CHUNK

    def body(q_ref, k_ref, v_ref, g_ref, o_ref, sall_ref, s_scr):
        @pl.when(pl.program_id(0) == 0)
        def _():
            s_scr[...] = jnp.zeros_like(s_scr)

        gs = g_ref[...]
        sls = [slice(h * HEAD, (h + 1) * HEAD) for h in range(GH)]
        s0s = tuple(s_scr[h] for h in range(GH))
        os_, s1s = _gdn_chunks(tuple(q_ref[:, sl] for sl in sls), tuple(k_ref[:, sl] for sl in sls),
                               tuple(v_ref[:, sl] for sl in sls), tuple(_lane_col(gs, GH + h) for h in range(GH)),
                               tuple(_lane_col(gs, h) for h in range(GH)), s0s)
        for h in range(GH):
            sall_ref[0, h] = s0s[h]
            o_ref[:, sls[h]] = os_[h]
            s_scr[h] = s1s[h]

    row = pl.BlockSpec((CHUNK, GW), lambda i: (i, 0))
    return _pcall(
        body, name="gdn_scan_fwd", grid=(nc,),
        in_specs=[row, row, row, pl.BlockSpec((CHUNK, LANES), lambda i: (i, 0))],
        out_specs=[row, pl.BlockSpec((1, GH, HEAD, HEAD), lambda i: (i, 0, 0, 0))],
        out_shape=[jax.ShapeDtypeStruct((T, GW), F32), jax.ShapeDtypeStruct((nc, GH, HEAD, HEAD), F32)],
        scratch_shapes=[pltpu.VMEM((GH, HEAD, HEAD), F32)],
        compiler_params=_cp("arbitrary"),
    )(qn, kn, vc, gsm)


def _gdn_scan_bwd(qn, kn, vc, gsm, sall, do, GH):
    T, GW = qn.shape
    nc = T // CHUNK

    def body(q_ref, k_ref, v_ref, g_ref, sall_ref, do_ref, dq_ref, dk_ref, dv_ref, dg_ref, ds_scr):
        @pl.when(pl.program_id(0) == 0)
        def _():
            ds_scr[...] = jnp.zeros_like(ds_scr)

        gs = g_ref[...]
        lane = lax.broadcasted_iota(jnp.int32, gs.shape, 1)
        sls = [slice(h * HEAD, (h + 1) * HEAD) for h in range(GH)]
        _, vjp = jax.vjp(_gdn_chunks, tuple(q_ref[:, sl] for sl in sls), tuple(k_ref[:, sl] for sl in sls),
                         tuple(v_ref[:, sl] for sl in sls), tuple(_lane_col(gs, GH + h) for h in range(GH)),
                         tuple(_lane_col(gs, h) for h in range(GH)), tuple(sall_ref[0, h] for h in range(GH)))
        dq, dk, dv, dgc, dbc, ds0 = vjp((tuple(do_ref[:, sl] for sl in sls), tuple(ds_scr[h] for h in range(GH))))
        dgs = jnp.zeros_like(gs)
        for h in range(GH):
            dq_ref[:, sls[h]] = dq[h]
            dk_ref[:, sls[h]] = dk[h]
            dv_ref[:, sls[h]] = dv[h]
            dgs = dgs + jnp.where(lane == h, dbc[h], 0.0) + jnp.where(lane == GH + h, dgc[h], 0.0)
            ds_scr[h] = ds0[h]
        dg_ref[...] = dgs

    row = pl.BlockSpec((CHUNK, GW), lambda i: (nc - 1 - i, 0))
    sm = pl.BlockSpec((CHUNK, LANES), lambda i: (nc - 1 - i, 0))
    return _pcall(
        body, name="gdn_scan_bwd", grid=(nc,),
        in_specs=[row, row, row, sm, pl.BlockSpec((1, GH, HEAD, HEAD), lambda i: (nc - 1 - i, 0, 0, 0)), row],
        out_specs=[row, row, row, sm],
        out_shape=[jax.ShapeDtypeStruct((T, GW), F32)] * 3 + [jax.ShapeDtypeStruct((T, LANES), F32)],
        scratch_shapes=[pltpu.VMEM((GH, HEAD, HEAD), F32)],
        compiler_params=_cp("arbitrary"),
    )(qn, kn, vc, gsm, sall, do)


def _gdn_post_fn(o, z, g):
    return _rms(o, g) * _silu(z)


def _gdn_post(o, proj, off_z, g, GW, tb=512):
    T = o.shape[0]

    def fn(i, n, ov, zv, gv):
        return (_cat([_gdn_post_fn(a, b, gv) for a, b in zip(_heads(ov, HEAD), _heads(zv, HEAD))]),)

    return _rowwise(fn, T, tb, [("row", o, 0, GW), ("row", proj, off_z, GW), ("full", g)], [("row", GW, BF16)], "gdn_post")[0]


def _gdn_post_bwd(o, proj, off_z, g, dout, GW, tb=256):
    T = o.shape[0]

    def fn(i, n, ov, zv, gv, dv):
        dos, dzs, dg = [], [], jnp.zeros_like(gv)
        for a, b, c in zip(_heads(ov, HEAD), _heads(zv, HEAD), _heads(dv, HEAD)):
            _, vjp = jax.vjp(_gdn_post_fn, a, b, gv)
            da, db, dgh = vjp(c)
            dos.append(da)
            dzs.append(db)
            dg = dg + dgh
        return _cat(dos), _cat(dzs), dg

    return _rowwise(fn, T, tb, [("row", o, 0, GW), ("row", proj, off_z, GW), ("full", g), ("row", dout, 0, GW)],
                    [("row", GW, F32), ("row", GW, BF16), ("acc", (1, HEAD), F32)], "gdn_post_bwd")


def _fox_prep(proj, offs, gq, gk, FW, tb=512):
    T = proj.shape[0]

    def fn(i, n, q, k, v, gqv, gkv):
        return (_cat([_rms(a, gqv) * (HEAD ** -0.5) for a in _heads(q, HEAD)]),
                _cat([_rms(a, gkv) for a in _heads(k, HEAD)]), v)

    return _rowwise(fn, T, tb, [("row", proj, offs["fq"], FW), ("row", proj, offs["fk"], FW), ("row", proj, offs["fv"], FW),
                                ("full", gq), ("full", gk)], [("row", FW, BF16)] * 3, "fox_prep")


def _fox_prep_bwd(proj, offs, gq, gk, dq, dk, dv, FW, tb=256):
    T = proj.shape[0]

    def fn(i, n, q, k, gqv, gkv, dqv, dkv, dvv):
        res = []
        for x, g, d in ((q, gqv, dqv), (k, gkv, dkv)):
            dxs, dg = [], jnp.zeros_like(g)
            for a, c in zip(_heads(x, HEAD), _heads(d, HEAD)):
                _, vjp = jax.vjp(_rms, a, g)
                da, dgh = vjp(c)
                dxs.append(da)
                dg = dg + dgh
            res += [_cat(dxs), dg]
        return res[0], res[2], dvv, res[1], res[3]

    return _rowwise(fn, T, tb, [("row", proj, offs["fq"], FW), ("row", proj, offs["fk"], FW), ("full", gq), ("full", gk),
                                ("row", dq, 0, FW), ("row", dk, 0, FW), ("row", dv, 0, FW)],
                    [("row", FW, BF16)] * 3 + [("acc", (1, HEAD), F32)] * 2, "fox_prep_bwd")


def _sub_row(x, sub_idx):
    sub = lax.broadcasted_iota(jnp.int32, x.shape, 0)
    return jnp.sum(jnp.where(sub == sub_idx, x, 0.0), axis=0, keepdims=True)


def _causal_pairs(nb, key_major):
    if key_major:
        pairs = [(i, j) for j in range(nb) for i in range(j, nb)]
    else:
        pairs = [(i, j) for i in range(nb) for j in range(i + 1)]
    return (jnp.asarray(np.array([p[0] for p in pairs], np.int32)),
            jnp.asarray(np.array([p[1] for p in pairs], np.int32)))


def _fox_scores(q, k, ck, diagonal):
    s = _dot_nt(q, k) - ck
    if diagonal:
        row = lax.broadcasted_iota(jnp.int32, s.shape, 0)
        col = lax.broadcasted_iota(jnp.int32, s.shape, 1)
        s = jnp.where(row >= col, s, NEG)
    return s


def _flash_fwd(qb, kb, vb, cumr, FH, blk):
    T, FW = qb.shape
    blk = min(blk, T)
    nb = T // blk
    qi_arr, kj_arr = _causal_pairs(nb, False)

    hps = min(FLASH_FWD_HEADS_PER_STEP, FH)
    assert FH % hps == 0
    us = range(hps)
    sl = [slice(u * HEAD, (u + 1) * HEAD) for u in us]

    def body(qi_ref, kj_ref, q_ref, k_ref, v_ref, ck_ref, ob_ref, lse_ref, m_scr, l_scr, acc):
        t = pl.program_id(1)
        qi, kj = qi_ref[t], kj_ref[t]

        @pl.when(kj == 0)
        def _():
            m_scr[...] = jnp.full_like(m_scr, NEG)
            l_scr[...] = jnp.zeros_like(l_scr)
            acc[...] = jnp.zeros_like(acc)

        def update(diagonal):
            s = [_fox_scores(q_ref[:, sl[u]], k_ref[:, sl[u]], ck_ref[u], diagonal) for u in us]
            m_old = [m_scr[u] for u in us]
            m_new = [jnp.maximum(m_old[u], jnp.max(s[u], axis=1, keepdims=True)) for u in us]
            alpha = [jnp.exp(m_old[u] - m_new[u]) for u in us]
            p = [jnp.exp(s[u] - m_new[u]) for u in us]
            pv = [_dot(p[u].astype(BF16), v_ref[:, sl[u]]) for u in us]
            for u in us:
                l_scr[u] = alpha[u] * l_scr[u] + jnp.sum(p[u], axis=1, keepdims=True)
                acc[:, sl[u]] = alpha[u] * acc[:, sl[u]] + pv[u]
                m_scr[u] = m_new[u]

        @pl.when(kj < qi)
        def _():
            update(False)

        @pl.when(kj == qi)
        def _():
            update(True)
            for u in us:
                ob_ref[:, sl[u]] = (acc[:, sl[u]] / l_scr[u]).astype(BF16)
                lse_ref[u] = m_scr[u] + jnp.log(l_scr[u])

    qs = pl.BlockSpec((blk, hps * HEAD), lambda g, t, qr, kr: (qr[t], g))
    ks = pl.BlockSpec((blk, hps * HEAD), lambda g, t, qr, kr: (kr[t], g))
    col = pl.BlockSpec((hps, blk, 1), lambda g, t, qr, kr: (g, qr[t], 0))
    gs = pltpu.PrefetchScalarGridSpec(
        num_scalar_prefetch=2, grid=(FH // hps, qi_arr.shape[0]),
        in_specs=[qs, ks, ks, pl.BlockSpec((hps, 1, blk), lambda g, t, qr, kr: (g, 0, kr[t]))],
        out_specs=[qs, col],
        scratch_shapes=[pltpu.VMEM((hps, blk, 1), F32), pltpu.VMEM((hps, blk, 1), F32), pltpu.VMEM((blk, hps * HEAD), F32)])
    return _pcall(
        body, name="flash_fwd", grid_spec=gs,
        out_shape=[jax.ShapeDtypeStruct((T, FW), BF16), jax.ShapeDtypeStruct((FH, T, 1), F32)],
        compiler_params=_cp("parallel", "arbitrary"),
    )(qi_arr, kj_arr, qb, kb, vb, cumr)


def _flash_bwd_q(qb, kb, vb, cumr, lse, do, dl, FH, blk):
    T, FW = qb.shape
    blk = min(blk, T)
    nb = T // blk
    qi_arr, kj_arr = _causal_pairs(nb, False)
    want_dq = dl is not None
    hps = min(FLASH_HEADS_PER_STEP, FH)
    assert FH % hps == 0
    us = range(hps)
    sl = [slice(u * HEAD, (u + 1) * HEAD) for u in us]

    def body(qi_ref, kj_ref, q_ref, k_ref, v_ref, ck_ref, lse_ref, do_ref, *rest):
        if want_dq:
            dl_ref, dq_ref, rs_ref, acc, rs_acc = rest
        else:
            dl_ref, acc = rest
        t = pl.program_id(1)
        qi, kj = qi_ref[t], kj_ref[t]

        @pl.when(kj == 0)
        def _():
            acc[...] = jnp.zeros_like(acc)
            if want_dq:
                rs_acc[...] = jnp.zeros_like(rs_acc)

        def update(diagonal):
            s = [_fox_scores(q_ref[:, sl[u]], k_ref[:, sl[u]], ck_ref[u], diagonal) for u in us]
            p = [jnp.exp(s[u] - lse_ref[u]) for u in us]
            dp = [_dot_nt(do_ref[:, sl[u]].astype(BF16), v_ref[:, sl[u]]) for u in us]
            if want_dq:
                ds = [p[u] * (dp[u] - dl_ref[u]) for u in us]
                dqp = [_dot(ds[u].astype(BF16), k_ref[:, sl[u]]) for u in us]
                for u in us:
                    acc[:, sl[u]] += dqp[u]
                    rs_acc[u] += jnp.sum(ds[u], axis=1, keepdims=True)
            else:
                for u in us:
                    acc[u] += jnp.sum(p[u] * dp[u], axis=1, keepdims=True)

        @pl.when(kj < qi)
        def _():
            update(False)

        @pl.when(kj == qi)
        def _():
            update(True)
            if want_dq:
                dq_ref[...] = acc[...] * (HEAD ** -0.5)
                rs_ref[...] = rs_acc[...]
            else:
                dl_ref[...] = acc[...]

    qs = pl.BlockSpec((blk, hps * HEAD), lambda g, t, qr, kr: (qr[t], g))
    ks = pl.BlockSpec((blk, hps * HEAD), lambda g, t, qr, kr: (kr[t], g))
    col = pl.BlockSpec((hps, blk, 1), lambda g, t, qr, kr: (g, qr[t], 0))
    in_specs = [qs, ks, ks, pl.BlockSpec((hps, 1, blk), lambda g, t, qr, kr: (g, 0, kr[t])), col, qs]
    args = [qi_arr, kj_arr, qb, kb, vb, cumr, lse, do]
    colshape = jax.ShapeDtypeStruct((FH, T, 1), F32)
    if want_dq:
        gs = pltpu.PrefetchScalarGridSpec(
            num_scalar_prefetch=2, grid=(FH // hps, qi_arr.shape[0]), in_specs=in_specs + [col], out_specs=[qs, col],
            scratch_shapes=[pltpu.VMEM((blk, hps * HEAD), F32), pltpu.VMEM((hps, blk, 1), F32)])
        return _pcall(body, name="flash_bwd_dq", grid_spec=gs,
                      out_shape=[jax.ShapeDtypeStruct((T, FW), F32), colshape],
                      compiler_params=_cp("parallel", "arbitrary"))(*args, dl)
    gs = pltpu.PrefetchScalarGridSpec(
        num_scalar_prefetch=2, grid=(FH // hps, qi_arr.shape[0]), in_specs=in_specs, out_specs=col,
        scratch_shapes=[pltpu.VMEM((hps, blk, 1), F32)])
    return _pcall(body, name="flash_bwd_rowterm", grid_spec=gs, out_shape=colshape,
                  compiler_params=_cp("parallel", "arbitrary"))(*args)


def _flash_bwd_dkv(qb, kb, vb, cumc, lse_row, dl_row, do, FH, blk):
    T, FW = qb.shape
    blk = min(blk, T)
    nb = T // blk

    qi_arr, kj_arr = _causal_pairs(nb, True)

    hps = min(FLASH_HEADS_PER_STEP, FH)
    assert FH % hps == 0
    us = range(hps)
    sl = [slice(u * HEAD, (u + 1) * HEAD) for u in us]

    def body(qi_ref, kj_ref, q_ref, k_ref, v_ref, ck_ref, lse_ref, dl_ref, do_ref,
             dk_ref, dv_ref, dc_ref, dk_acc, dv_acc, dc_acc):
        t = pl.program_id(1)
        qi, kj = qi_ref[t], kj_ref[t]

        def update(diagonal, first):
            st = [_dot_nt(k_ref[:, sl[u]], q_ref[:, sl[u]]) - ck_ref[u] for u in us]
            if diagonal:
                krow = lax.broadcasted_iota(jnp.int32, st[0].shape, 0)
                qcol = lax.broadcasted_iota(jnp.int32, st[0].shape, 1)
                st = [jnp.where(qcol >= krow, st[u], NEG) for u in us]
            pt = [jnp.exp(st[u] - lse_ref[u]) for u in us]
            dob = [do_ref[:, sl[u]].astype(BF16) for u in us]
            dpt = [_dot_nt(v_ref[:, sl[u]], dob[u]) for u in us]
            dst = [pt[u] * (dpt[u] - dl_ref[u]) for u in us]
            dk = [_dot(dst[u].astype(BF16), q_ref[:, sl[u]]) for u in us]
            dv = [_dot(pt[u].astype(BF16), dob[u]) for u in us]
            for u in us:
                dc = -jnp.sum(dst[u], axis=1, keepdims=True)
                if first:
                    dk_acc[:, sl[u]] = dk[u]
                    dv_acc[:, sl[u]] = dv[u]
                    dc_acc[u] = dc
                else:
                    dk_acc[:, sl[u]] += dk[u]
                    dv_acc[:, sl[u]] += dv[u]
                    dc_acc[u] += dc

        @pl.when(qi == kj)
        def _():
            update(True, True)

        @pl.when(qi > kj)
        def _():
            update(False, False)

        @pl.when(qi == nb - 1)
        def _():
            dk_ref[...] = dk_acc[...]
            dv_ref[...] = dv_acc[...]
            dc_ref[...] = dc_acc[...]

    ks = pl.BlockSpec((blk, hps * HEAD), lambda g, t, qr, kr: (kr[t], g))
    qs = pl.BlockSpec((blk, hps * HEAD), lambda g, t, qr, kr: (qr[t], g))
    rowq = pl.BlockSpec((hps, 1, blk), lambda g, t, qr, kr: (g, 0, qr[t]))
    colk = pl.BlockSpec((hps, blk, 1), lambda g, t, qr, kr: (g, kr[t], 0))
    gs = pltpu.PrefetchScalarGridSpec(
        num_scalar_prefetch=2, grid=(FH // hps, qi_arr.shape[0]),
        in_specs=[qs, ks, ks, colk, rowq, rowq, qs],
        out_specs=[ks, ks, colk],
        scratch_shapes=[pltpu.VMEM((blk, hps * HEAD), F32), pltpu.VMEM((blk, hps * HEAD), F32),
                        pltpu.VMEM((hps, blk, 1), F32)])
    return _pcall(
        body, name="flash_bwd_dkv", grid_spec=gs,
        out_shape=[jax.ShapeDtypeStruct((T, FW), F32), jax.ShapeDtypeStruct((T, FW), F32),
                   jax.ShapeDtypeStruct((FH, T, 1), F32)],
        compiler_params=_cp("parallel", "arbitrary"),
    )(qi_arr, kj_arr, qb, kb, vb, cumc, lse_row, dl_row, do)


def _rev_cumsum_rows(r1, r2, tb=512):
    H, T = r1.shape
    tb = min(tb, T)
    nb = T // tb

    def body(r1_ref, r2_ref, o_ref, carry):
        @pl.when(pl.program_id(0) == 0)
        def _():
            carry[...] = jnp.zeros_like(carry)

        rv = r1_ref[...] + r2_ref[...]
        si = lax.broadcasted_iota(jnp.int32, (tb, tb), 0)
        ti = lax.broadcasted_iota(jnp.int32, (tb, tb), 1)
        o_ref[...] = _dot(rv, (si >= ti).astype(F32), HI) + carry[...]
        carry[...] += jnp.sum(rv, axis=1, keepdims=True)

    spec = pl.BlockSpec((H, tb), lambda i: (0, nb - 1 - i))
    return _pcall(body, name="rev_cumsum", grid=(nb,), in_specs=[spec, spec], out_specs=spec,
                  out_shape=jax.ShapeDtypeStruct((H, T), F32), scratch_shapes=[pltpu.VMEM((H, 1), F32)],
                  compiler_params=_cp("arbitrary"))(r1, r2)


def _mem_head(q, k, v, gq, gk):
    logits = _dot_nt(_rms(q, gq), _rms(k, gk)) * (MEM_DH ** -0.5)
    mx = jnp.max(logits, axis=1, keepdims=True)
    e = jnp.exp(logits - mx)
    p = e / jnp.sum(e, axis=1, keepdims=True)
    return _dot(p, v)


def _mem_attn(proj, off_q, kv, gq, gk, MW, tb=512):
    T = proj.shape[0]
    MH = MW // MEM_DH

    def fn(i, n, q, kvv, gqv, gkv):
        ks, vs = _heads(kvv[:, :MW], MEM_DH), _heads(kvv[:, MW:], MEM_DH)
        return (_cat([_mem_head(a, b, c, gqv, gkv) for a, b, c in zip(_heads(q, MEM_DH), ks, vs)]),)

    return _rowwise(fn, T, tb, [("row", proj, off_q, MW), ("full", kv), ("full", gq), ("full", gk)],
                    [("row", MW, BF16)], "mem_attn")[0]


def _mem_attn_bwd(proj, off_q, kv, gq, gk, dout, MW, tb=256):
    T = proj.shape[0]
    ML = kv.shape[0]

    def fn(i, n, q, kvv, gqv, gkv, dv):
        ks, vs = _heads(kvv[:, :MW], MEM_DH), _heads(kvv[:, MW:], MEM_DH)
        dqs, dks, dvs = [], [], []
        dgq, dgk = jnp.zeros_like(gqv), jnp.zeros_like(gkv)
        for a, b, c, d in zip(_heads(q, MEM_DH), ks, vs, _heads(dv, MEM_DH)):
            _, vjp = jax.vjp(_mem_head, a, b, c, gqv, gkv)
            da, db, dc, dg1, dg2 = vjp(d)
            dqs.append(da)
            dks.append(db)
            dvs.append(dc)
            dgq, dgk = dgq + dg1, dgk + dg2
        return _cat(dqs), _cat(dks + dvs), dgq, dgk

    return _rowwise(fn, T, tb, [("row", proj, off_q, MW), ("full", kv), ("full", gq), ("full", gk), ("row", dout, 0, MW)],
                    [("row", MW, BF16), ("acc", (ML, 2 * MW), F32), ("acc", (1, MEM_DH), F32), ("acc", (1, MEM_DH), F32)],
                    "mem_attn_bwd")


def _merge_fn(ga, gb, gm, ua, ub, um):
    return _sigmoid(ga) * ua + _sigmoid(gb) * ub + _sigmoid(gm) * um


def _merge(proj, offs, ua, ub, um, D, tb=256):
    T = proj.shape[0]
    ins = [("row", proj, offs[k], D) for k in ("ga", "gb", "gm")] + [("row", u, 0, D) for u in (ua, ub, um)]
    return _rowwise(lambda i, n, *v: (_merge_fn(*v),), T, tb, ins, [("row", D, BF16)], "merge")[0]


def _merge_bwd(proj, offs, ua, ub, um, dy, D, tb=256):
    T = proj.shape[0]

    def fn(i, n, *v):
        _, vjp = jax.vjp(_merge_fn, *v[:6])
        return vjp(v[6])

    ins = [("row", proj, offs[k], D) for k in ("ga", "gb", "gm")] + [("row", u, 0, D) for u in (ua, ub, um)] + [("row", dy, 0, D)]
    return _rowwise(fn, T, tb, ins, [("row", D, BF16)] * 6, "merge_bwd")


def _loss_grad(x2, tgt, tb=256):
    T, D = x2.shape

    def fn(i, n, a, b):
        e = a - b
        part = jnp.sum(jnp.sum(e * e, axis=1, keepdims=True), axis=0, keepdims=True) * (0.5 / D)
        g = e * (1.0 / D)
        return g, g, part + jnp.zeros((SUB, LANES), F32)

    return _rowwise(fn, T, tb, [("row", x2, 0, D), ("row", tgt, 0, D)],
                    [("row", D, F32), ("row", D, BF16), ("acc", (SUB, LANES), F32)], "loss_grad")


def _adamw(w, g, m, v, name, tb=128):
    R, C = w.shape
    c1 = 1.0 / (1.0 - ADAM_B1 ** ADAM_STEP)
    c2 = 1.0 / (1.0 - ADAM_B2 ** ADAM_STEP)

    def fn(i, n, wv, gv, mv, vv):
        mn = ADAM_B1 * mv + (1.0 - ADAM_B1) * gv
        vn = ADAM_B2 * vv + (1.0 - ADAM_B2) * (gv * gv)
        delta = -ADAM_LR * ((mn * c1) / (jnp.sqrt(vn * c2) + ADAM_EPS) + ADAM_WD * wv)
        return delta, mn, vn

    return _rowwise(fn, R, tb, [("row", a, 0, C) for a in (w, g, m, v)], [("row", C, F32)] * 3, name)


def _coords():
    return lax.axis_index("x"), lax.axis_index("y"), lax.axis_index("c")


def _allgather_small(blk):
    m_per, n = blk.shape

    def body(x_ref, out_ref, send_sems, recv_sems, local_sem):
        x, y, c = _coords()
        me, sibling = (x, y, c), (x, y, 1 - c)
        chips = [(1 - x, y), (x, 1 - y), (1 - x, 1 - y)]

        def rows(px, py, pc):
            return out_ref.at[pl.ds((4 * px + 2 * py + pc) * m_per, m_per), :]

        def copy(k, block, to, src=None):
            return pltpu.make_async_remote_copy(
                src_ref=rows(*block) if src is None else src, dst_ref=rows(*block),
                send_sem=send_sems.at[k], recv_sem=recv_sems.at[k], device_id=to, device_id_type=MESH)

        mine = pltpu.make_async_copy(x_ref, rows(*me), local_sem)
        mine.start()
        first = [copy(0, me, sibling, src=x_ref)]
        first += [copy(1 + j, me, (*chip, c), src=x_ref) for j, chip in enumerate(chips)]
        for cp in first:
            cp.start()
        passed = [copy(4 + j, (*chip, c), sibling) for j, chip in enumerate(chips)]
        for j, chip in enumerate(chips):
            copy(1 + j, (*chip, c), me).wait_recv()
            passed[j].start()
        copy(0, sibling, me).wait_recv()
        for j, chip in enumerate(chips):
            copy(4 + j, (*chip, 1 - c), me).wait_recv()
        for cp in first + passed:
            cp.wait_send()
        mine.wait()

    return _pcall(
        body, name="allgather_small", out_shape=jax.ShapeDtypeStruct((8 * m_per, n), blk.dtype),
        in_specs=[pl.BlockSpec(memory_space=pltpu.VMEM)], out_specs=pl.BlockSpec(memory_space=pltpu.VMEM),
        scratch_shapes=[pltpu.SemaphoreType.DMA((7,)), pltpu.SemaphoreType.DMA((7,)), pltpu.SemaphoreType.DMA],
        compiler_params=pltpu.CompilerParams(vmem_limit_bytes=VMEM_LIMIT),
    )(blk)


def _sum8(g, m_per):
    n = g.shape[1]

    def body(g_ref, o_ref):
        acc = g_ref[pl.ds(0, m_per), :]
        for d in range(1, 8):
            acc = acc + g_ref[pl.ds(d * m_per, m_per), :]
        o_ref[...] = acc

    return _pcall(body, name="sum8", out_shape=jax.ShapeDtypeStruct((m_per, n), g.dtype))(g)


_ANY = pl.BlockSpec(memory_space=pl.ANY)


def _allgather_chips(buf):
    nr, w = buf.shape
    half = nr // 2

    def body(in_ref, out_ref, send_sems, recv_sems):
        x, y, c = _coords()
        me = 2 * x + y
        chips = [(1 - x, y), (x, 1 - y), (1 - x, 1 - y)]
        mine_rows = pl.ds(pl.multiple_of(c * half, 16), half)
        other_rows = pl.ds(pl.multiple_of((1 - c) * half, 16), half)

        def copy(k, src, dst, to):
            return pltpu.make_async_remote_copy(src_ref=src, dst_ref=dst, send_sem=send_sems.at[k],
                                                recv_sem=recv_sems.at[k], device_id=to, device_id_type=MESH)

        first = [copy(j, in_ref.at[mine_rows], out_ref.at[me, mine_rows], (cx, cy, c)) for j, (cx, cy) in enumerate(chips)]
        for cp in first:
            cp.start()
        passed = []
        for j, (cx, cy) in enumerate(chips):
            slot = out_ref.at[2 * cx + cy, mine_rows]
            copy(j, slot, slot, (cx, cy, c)).wait_recv()
            fwd = copy(3 + j, slot, slot, (x, y, 1 - c))
            fwd.start()
            passed.append(fwd)
        for j, (cx, cy) in enumerate(chips):
            slot = out_ref.at[2 * cx + cy, other_rows]
            copy(3 + j, slot, slot, (x, y, 1 - c)).wait_recv()
        for cp in first + passed:
            cp.wait_send()

    return _pcall(
        body, name="allgather_chips", out_shape=jax.ShapeDtypeStruct((4, nr, w), buf.dtype),
        in_specs=[_ANY], out_specs=_ANY,
        scratch_shapes=[pltpu.SemaphoreType.DMA((6,)), pltpu.SemaphoreType.DMA((6,))],
    )(buf)


def _rs_pair_exchange(g):
    _, nr, w = g.shape
    half = nr // 2

    def body(g_ref, rb_ref, send_sem, recv_sem):
        x, y, c = _coords()
        other_rows = pl.ds(pl.multiple_of((1 - c) * half, SUB), half)
        cp = pltpu.make_async_remote_copy(src_ref=g_ref.at[:, other_rows], dst_ref=rb_ref, send_sem=send_sem,
                                          recv_sem=recv_sem, device_id=(x, y, 1 - c), device_id_type=MESH)
        cp.start()
        cp.wait()

    return _pcall(body, name="rs_pair_exchange", out_shape=jax.ShapeDtypeStruct((4, half, w), g.dtype),
                  in_specs=[_ANY], out_specs=_ANY,
                  scratch_shapes=[pltpu.SemaphoreType.DMA, pltpu.SemaphoreType.DMA])(g)


def _rs_pair_add(g, rb, cidx, tb=256):
    _, nr, w = g.shape
    half = nr // 2
    tb = min(tb, half)
    assert half % tb == 0
    hb = half // tb

    def body(c_ref, g_ref, r_ref, o_ref):
        o_ref[...] = (g_ref[...].astype(F32) + r_ref[...].astype(F32)).astype(o_ref.dtype)

    gs = pltpu.PrefetchScalarGridSpec(
        num_scalar_prefetch=1, grid=(4, hb),
        in_specs=[pl.BlockSpec((1, tb, w), lambda j, i, c_ref: (j, c_ref[0] * hb + i, 0)),
                  pl.BlockSpec((1, tb, w), lambda j, i, c_ref: (j, i, 0))],
        out_specs=pl.BlockSpec((1, tb, w), lambda j, i, c_ref: (j, i, 0)))
    return _pcall(body, name="rs_pair_add", grid_spec=gs, out_shape=jax.ShapeDtypeStruct((4, half, w), BF16),
                  compiler_params=_cp("parallel", "parallel"))(cidx, g, rb)


def _rs_chip_exchange(p):
    _, h, w = p.shape

    def body(p_ref, rb_ref, send_sems, recv_sems):
        x, y, c = _coords()
        chips = [(1 - x, y), (x, 1 - y), (1 - x, 1 - y)]
        cps = [pltpu.make_async_remote_copy(src_ref=p_ref.at[2 * cx + cy], dst_ref=rb_ref.at[j], send_sem=send_sems.at[j],
                                            recv_sem=recv_sems.at[j], device_id=(cx, cy, c), device_id_type=MESH)
               for j, (cx, cy) in enumerate(chips)]
        for cp in cps:
            cp.start()
        for cp in cps:
            cp.wait()

    return _pcall(body, name="rs_chip_exchange", out_shape=jax.ShapeDtypeStruct((3, h, w), p.dtype),
                  in_specs=[_ANY], out_specs=_ANY,
                  scratch_shapes=[pltpu.SemaphoreType.DMA((3,)), pltpu.SemaphoreType.DMA((3,))])(p)


def _sum4(p, rb, chip_idx, tb=256):
    _, h, w = rb.shape
    tb = min(tb, h)
    assert h % tb == 0

    def body(m_ref, p_ref, r_ref, o_ref):
        f = lambda t: t.astype(F32)
        o_ref[...] = ((f(p_ref[0]) + f(r_ref[0])) + f(r_ref[1])) + f(r_ref[2])

    gs = pltpu.PrefetchScalarGridSpec(
        num_scalar_prefetch=1, grid=(h // tb,),
        in_specs=[pl.BlockSpec((1, tb, w), lambda i, m_ref: (m_ref[0], i, 0)),
                  pl.BlockSpec((3, tb, w), lambda i, m_ref: (0, i, 0))],
        out_specs=pl.BlockSpec((tb, w), lambda i, m_ref: (i, 0)))
    return _pcall(body, name="sum4", grid_spec=gs, out_shape=jax.ShapeDtypeStruct((h, w), F32),
                  compiler_params=_cp("parallel"))(chip_idx, p, rb)


def _pair_allgather(f):
    h, w = f.shape

    def body(f_ref, out_ref, send_sem, recv_sem):
        x, y, c = _coords()
        mine_rows = pl.ds(pl.multiple_of(c * h, SUB), h)
        other_rows = pl.ds(pl.multiple_of((1 - c) * h, SUB), h)
        send = pltpu.make_async_remote_copy(src_ref=f_ref, dst_ref=out_ref.at[mine_rows], send_sem=send_sem,
                                            recv_sem=recv_sem, device_id=(x, y, 1 - c), device_id_type=MESH)
        send.start()
        send.wait_send()
        pltpu.make_async_remote_copy(src_ref=f_ref, dst_ref=out_ref.at[other_rows], send_sem=send_sem,
                                     recv_sem=recv_sem, device_id=(x, y, 1 - c), device_id_type=MESH).wait_recv()

    return _pcall(body, name="pair_allgather", out_shape=jax.ShapeDtypeStruct((2 * h, w), f.dtype),
                  in_specs=[_ANY], out_specs=_ANY,
                  scratch_shapes=[pltpu.SemaphoreType.DMA, pltpu.SemaphoreType.DMA])(f)


def _size(shape):
    n = 1
    for d in shape:
        n *= d
    return n


PACK_ALIGN = 16


def _pack(arrs, dtype, row_mult):
    parts = []
    for a in arrs:
        flat = a.astype(dtype).reshape(-1)
        n = flat.shape[0]
        full = _ru(n, PACK_W * PACK_ALIGN)
        if full > n:
            flat = jnp.pad(flat, (0, full - n))
        parts.append(flat.reshape(-1, PACK_W))
    rows = sum(p.shape[0] for p in parts)
    if rows % row_mult:
        parts.append(jnp.zeros((_ru(rows, row_mult) - rows, PACK_W), dtype))
    return jnp.concatenate(parts, axis=0)


def _unpack(buf, shapes):
    out, off = [], 0
    for s in shapes:
        n = _size(s)
        r = _ru(-(-n // PACK_W), PACK_ALIGN)
        part = buf[off:off + r]
        out.append(part.reshape(s) if n == r * PACK_W else part.reshape(-1)[:n].reshape(s))
        off += r
    return out


def _pack_flat(arrs, dtype, row_mult):
    flat = jnp.concatenate([a.astype(dtype).reshape(-1) for a in arrs])
    n = flat.shape[0]
    rows = _ru(-(-n // PACK_W), row_mult)
    return jnp.pad(flat, (0, rows * PACK_W - n)).reshape(rows, PACK_W)


def _unpack_flat(buf, shapes):
    flat = buf.reshape(-1)
    out, off = [], 0
    for s in shapes:
        n = _size(s)
        out.append(flat[off:off + n].reshape(s))
        off += n
    return out


def _in_layout(D, GW, GH, FW, FH, MW, tn):
    o_z = 3 * GW
    o_beta = 4 * GW
    o_fq = o_beta + 2 * GH
    o_ff = o_fq + 3 * FW
    o_mq = o_ff + FH
    o_g = o_mq + MW
    orig = {"q": (0, GW), "k": (GW, GW), "v": (2 * GW, GW), "z": (o_z, GW), "beta": (o_beta, GH), "dec": (o_beta + GH, GH),
            "fq": (o_fq, FW), "fk": (o_fq + FW, FW), "fv": (o_fq + 2 * FW, FW), "ff": (o_ff, FH), "mq": (o_mq, MW),
            "ga": (o_g, D), "gb": (o_g + D, D), "gm": (o_g + 2 * D, D)}
    offs, cur = {}, 0
    for key, width in (("ga", D), ("gb", D), ("gm", D), ("q", GW), ("k", GW), ("v", GW), ("z", GW),
                       ("fq", FW), ("fk", FW), ("fv", FW), ("mq", MW), ("small", LANES)):
        cur = _ru(cur, width)
        offs[key] = cur
        cur += width
    total = _ru(cur, tn)
    pieces = [(offs[k], orig[k][0], orig[k][1]) for k in ("ga", "gb", "gm", "q", "k", "v", "z", "fq", "fk", "fv", "mq")]
    pieces += [(offs["small"], orig["beta"][0], GH), (offs["small"] + GH, orig["dec"][0], GH),
               (offs["small"] + 2 * GH, orig["ff"][0], FH)]
    return offs, total, pieces, o_g + 3 * D


def _pad_cols(w, pieces, total):
    parts, cur = [], 0
    for pstart, ostart, n in pieces:
        if pstart > cur:
            parts.append(jnp.zeros((w.shape[0], pstart - cur), w.dtype))
        parts.append(w[:, ostart:ostart + n])
        cur = pstart + n
    if total > cur:
        parts.append(jnp.zeros((w.shape[0], total - cur), w.dtype))
    return jnp.concatenate(parts, axis=1)


def _unpad_cols(wp, pieces):
    return jnp.concatenate([wp[:, pstart:pstart + n] for pstart, ostart, n in sorted(pieces, key=lambda t: t[1])], axis=1)


def _local_step(x, mem, tgt, W, flash_blk=512):
    T, D = x.shape
    GW = W["w_up_gdn"].shape[0]
    FW = W["w_up_fox"].shape[0]
    MW = W["w_up_mem"].shape[0]
    GH, FH = GW // HEAD, FW // HEAD
    offs, NP, pieces, d_in = _in_layout(D, GW, GH, FW, FH, MW, 1024)
    assert W["w_in"].shape[1] == d_in
    w_in_p = _pad_cols(W["w_in"], pieces, NP)
    cw = W["conv_w"]
    cws = [cw[:, i * GW:(i + 1) * GW] for i in range(3)]
    zl = jnp.zeros((1, LANES), F32)
    pvecs = [lax.dynamic_update_slice(zl, W["a_log"], (0, GH)), lax.dynamic_update_slice(zl, W["dt_bias"], (0, GH)),
             lax.dynamic_update_slice(zl, W["fox_b_f"], (0, 2 * GH))]
    lane0 = 2 * GH

    h = _rms_fwd(x, W["g_mix"], "rms_mix")
    proj = _mm(h, w_in_p, "nn", "in_proj")
    qn, kn, vc, cum, gbm = _gdn_prep(proj, offs, cws, pvecs, GH, FH, GW)
    o_gdn, sall = _gdn_scan_fwd(qn, kn, vc, gbm, GH)
    o_a = _gdn_post(o_gdn, proj, offs["z"], W["gdn_norm_g"], GW)
    qb, kb, vb = _fox_prep(proj, offs, W["fox_q_norm"], W["fox_k_norm"], FW)
    cumh = cum[:, lane0:lane0 + FH].T
    cumc, cumr = cumh.reshape(FH, T, 1), cumh.reshape(FH, 1, T)
    o_b16, lse = _flash_fwd(qb, kb, vb, cumr, FH, flash_blk)
    memn = _rms_fwd(mem, W["g_mem"], "rms_mem")
    kv = _mm(memn, W["w_mem_kv"], "nn", "mem_kv")
    o_m = _mem_attn(proj, offs["mq"], kv, W["mem_q_norm"], W["mem_k_norm"], MW)
    ua = _mm(o_a, W["w_up_gdn"], "nn", "up_gdn")
    ub = _mm(o_b16, W["w_up_fox"], "nn", "up_fox")
    um = _mm(o_m, W["w_up_mem"], "nn", "up_mem")
    y = _merge(proj, offs, ua, ub, um, D)
    x1 = _mm(y, W["w_out"], "nn", "out_proj", epilogue=lambda acc, r: (acc + r,), extras=(x,))
    h2 = _rms_fwd(x1, W["g_mlp"], "rms_mlp")
    u, a = _mm(h2, W["w_ff1"], "nn", "ff1", out_dtypes=(F32, BF16),
               epilogue=lambda acc: (acc, jnp.square(jnp.maximum(acc, 0.0))))
    x2 = _mm(a, W["w_ff2"], "nn", "ff2", epilogue=lambda acc, r: (acc + r,), extras=(x1,))
    dx2, dx2b, lpart = _loss_grad(x2, tgt)
    loss = lpart[0, 0]

    G = {}
    du = _mm(dx2b, W["w_ff2"], "nt", "ff2_dx", out_dtypes=(BF16,),
             epilogue=lambda acc, uu: (acc * (2.0 * jnp.maximum(uu, 0.0)),), extras=(u,))
    G["w_ff2"] = _mm(a, dx2b, "tn", "ff2_dw", out_dtypes=(BF16,))
    dh2 = _mm(du, W["w_ff1"], "nt", "ff1_dx")
    G["w_ff1"] = _mm(h2, du, "tn", "ff1_dw", out_dtypes=(BF16,))
    dx1, dx1b, G["g_mlp"] = _rms_bwd(x1, W["g_mlp"], dh2, dx2, "rms_mlp_bwd")
    dy = _mm(dx1b, W["w_out"], "nt", "out_dx")
    G["w_out"] = _mm(y, dx1b, "tn", "out_dw", out_dtypes=(BF16,))
    dga, dgb, dgm, dua, dub, dum = _merge_bwd(proj, offs, ua, ub, um, dy, D)
    do_a = _mm(dua, W["w_up_gdn"], "nt", "up_gdn_dx")
    G["w_up_gdn"] = _mm(o_a, dua, "tn", "up_gdn_dw", out_dtypes=(BF16,))
    do_b = _mm(dub, W["w_up_fox"], "nt", "up_fox_dx")
    G["w_up_fox"] = _mm(o_b16, dub, "tn", "up_fox_dw", out_dtypes=(BF16,))
    do_m = _mm(dum, W["w_up_mem"], "nt", "up_mem_dx")
    G["w_up_mem"] = _mm(o_m, dum, "tn", "up_mem_dw", out_dtypes=(BF16,))
    dmq, dkv, G["mem_q_norm"], G["mem_k_norm"] = _mem_attn_bwd(proj, offs["mq"], kv, W["mem_q_norm"], W["mem_k_norm"], do_m, MW)
    dkvb = dkv.astype(BF16)
    dmemn = _mm(dkvb, W["w_mem_kv"], "nt", "mem_kv_dx")
    G["w_mem_kv"] = _mm(memn, dkvb, "tn", "mem_kv_dw", out_dtypes=(BF16,))
    G["g_mem"] = _rms_dg(mem, W["g_mem"], dmemn, "rms_mem_bwd")
    dl = _flash_bwd_q(qb, kb, vb, cumr, lse, do_b, None, FH, flash_blk)
    dqb, dcq = _flash_bwd_q(qb, kb, vb, cumr, lse, do_b, dl, FH, flash_blk)
    dkb, dvb, dck = _flash_bwd_dkv(qb, kb, vb, cumc, lse.reshape(FH, 1, T), dl.reshape(FH, 1, T), do_b, FH, flash_blk)
    dlf = _rev_cumsum_rows(dcq.reshape(FH, T), dck.reshape(FH, T))
    dlf_sm = jnp.pad(dlf.T, ((0, 0), (lane0, LANES - lane0 - FH)))
    dfq, dfk, dfv, G["fox_q_norm"], G["fox_k_norm"] = _fox_prep_bwd(proj, offs, W["fox_q_norm"], W["fox_k_norm"], dqb, dkb, dvb, FW)
    do_gdn, dz, G["gdn_norm_g"] = _gdn_post_bwd(o_gdn, proj, offs["z"], W["gdn_norm_g"], do_a, GW)
    dqn, dkn, dvc, dgsm = _gdn_scan_bwd(qn, kn, vc, gbm, sall, do_gdn, GH)
    dyq, dyk, dyv, dcq, dck_w, dcv, dsmall, dalog, ddtb, dbf = _gdn_prep_bwd_a(
        proj, offs, cws, pvecs, (dqn, dkn, dvc), dgsm, dlf_sm, GH, FH, GW)
    dxq, dxk, dxv = _gdn_prep_bwd_b((dyq, dyk, dyv), cws, GW)
    G["conv_w"] = jnp.concatenate([dcq[:CONV_K], dck_w[:CONV_K], dcv[:CONV_K]], axis=1)
    G["a_log"] = dalog[:, GH:2 * GH]
    G["dt_bias"] = ddtb[:, GH:2 * GH]
    G["fox_b_f"] = dbf[:, lane0:lane0 + FH]
    segs = {"ga": dga, "gb": dgb, "gm": dgm, "q": dxq, "k": dxk, "v": dxv, "z": dz, "fq": dfq, "fk": dfk, "fv": dfv,
            "mq": dmq, "small": dsmall}
    parts, cur = [], 0
    for key in ("ga", "gb", "gm", "q", "k", "v", "z", "fq", "fk", "fv", "mq", "small"):
        if offs[key] > cur:
            parts.append(jnp.zeros((T, offs[key] - cur), BF16))
        parts.append(segs[key])
        cur = offs[key] + segs[key].shape[1]
    if NP > cur:
        parts.append(jnp.zeros((T, NP - cur), BF16))
    dproj = jnp.concatenate(parts, axis=1)
    dh = _mm(dproj, w_in_p, "nt", "in_dx")
    G["w_in"] = _unpad_cols(_mm(h, dproj, "tn", "in_dw", out_dtypes=(BF16,)), pieces)
    grad_x, _, G["g_mix"] = _rms_bwd(x, W["g_mix"], dh, dx1, "rms_mix_bwd")
    return loss, grad_x, G


BIG = ["w_in", "w_mem_kv", "w_up_gdn", "w_up_fox", "w_up_mem", "w_out", "w_ff1", "w_ff2"]
SMALL = ["g_mix", "a_log", "dt_bias", "gdn_norm_g", "fox_b_f", "fox_q_norm", "fox_k_norm", "g_mem", "mem_q_norm",
         "mem_k_norm", "g_mlp"]
ORDER = ["g_mix", "w_in", "conv_w", "a_log", "dt_bias", "gdn_norm_g", "fox_b_f", "fox_q_norm", "fox_k_norm", "g_mem",
         "w_mem_kv", "mem_q_norm", "mem_k_norm", "w_up_gdn", "w_up_fox", "w_up_mem", "w_out", "g_mlp", "w_ff1", "w_ff2"]
SHARD_AXIS = {"w_in": 1, "w_mem_kv": 0, "w_up_gdn": 1, "w_up_fox": 1, "w_up_mem": 1, "w_out": 0, "w_ff1": 1, "w_ff2": 0}


def _step(x, mem, tgt, w, m, v, flash_blk=512):
    xi, yi, ci = _coords()
    chip = 2 * xi + yi

    shard_shapes = [w[n].shape for n in BIG]
    packed_w = _pack([w[n] for n in BIG], BF16, PACK_ROWS)
    gathered = _allgather_chips(packed_w)
    gathered = lax.dynamic_update_slice(gathered, packed_w[None], (chip, 0, 0))
    per_chip = [_unpack(gathered[j], shard_shapes) for j in range(4)]
    W = {n: jnp.concatenate([per_chip[j][i] for j in range(4)], axis=SHARD_AXIS[n]) for i, n in enumerate(BIG)}
    cw_rows = jnp.pad(w["conv_w"], ((0, SUB - CONV_K), (0, 0)))
    cw_all = _allgather_small(cw_rows)
    W["conv_w"] = jnp.concatenate([cw_all[16 * j:16 * j + CONV_K] for j in range(4)], axis=1)
    for n in SMALL:
        W[n] = w[n]

    loss, grad_x, G = _local_step(x, mem, tgt, W, flash_blk)
    loss = lax.psum(loss, ("x", "y", "c"))

    small_shapes = [G[n].shape for n in SMALL] + [G["conv_w"].shape]
    sm = _pack_flat([G[n] for n in SMALL] + [G["conv_w"]], F32, SUB)
    sm_sum = _sum8(_allgather_small(sm), sm.shape[0])
    sm_list = _unpack_flat(sm_sum, small_shapes)
    g = {n: sm_list[i] for i, n in enumerate(SMALL)}
    cw_full = sm_list[-1]
    gw4 = cw_full.shape[1] // 4
    g["conv_w"] = lax.dynamic_slice(cw_full, (0, chip * gw4), (CONV_K, gw4))

    by_dest = []
    for j in range(4):
        shards = []
        for n in BIG:
            size = w[n].shape[SHARD_AXIS[n]]
            shards.append(lax.slice_in_dim(G[n], j * size, (j + 1) * size, axis=SHARD_AXIS[n]))
        by_dest.append(_pack(shards, BF16, PACK_ROWS))
    gflat = jnp.stack(by_dest)
    rb1 = _rs_pair_exchange(gflat)
    part = _rs_pair_add(gflat, rb1, jnp.reshape(ci, (1,)).astype(jnp.int32))
    rb2 = _rs_chip_exchange(part)
    half_sum = _sum4(part, rb2, jnp.reshape(chip, (1,)).astype(jnp.int32))
    mine = _pair_allgather(half_sum)
    mine = lax.dynamic_update_slice(mine, half_sum, (ci * half_sum.shape[0], 0))
    for i, gv in enumerate(_unpack(mine, shard_shapes)):
        g[BIG[i]] = gv

    delta, new_m, new_v = {}, {}, {}
    for n in BIG:
        delta[n], new_m[n], new_v[n] = _adamw(w[n], g[n], m[n], v[n], "adamw_" + n)
    rest = SMALL + ["conv_w"]
    rest_shapes = [w[n].shape for n in rest]
    packed = [_pack_flat([d[n] for n in rest], F32, SUB) for d in (w, g, m, v)]
    outs = _adamw(*packed, "adamw_small", tb=packed[0].shape[0])
    for d, buf in zip((delta, new_m, new_v), outs):
        for n, val in zip(rest, _unpack_flat(buf, rest_shapes)):
            d[n] = val
    return loss, grad_x, g, delta, new_m, new_v


def kernel(x, mem, g_mix, w_in, conv_w, a_log, dt_bias, gdn_norm_g, fox_b_f, fox_q_norm, fox_k_norm, g_mem, w_mem_kv, mem_q_norm, mem_k_norm, w_up_gdn, w_up_fox, w_up_mem, w_out, g_mlp, w_ff1, w_ff2, loss_target, m_g_mix, m_w_in, m_conv_w, m_a_log, m_dt_bias, m_gdn_norm_g, m_fox_b_f, m_fox_q_norm, m_fox_k_norm, m_g_mem, m_w_mem_kv, m_mem_q_norm, m_mem_k_norm, m_w_up_gdn, m_w_up_fox, m_w_up_mem, m_w_out, m_g_mlp, m_w_ff1, m_w_ff2, v_g_mix, v_w_in, v_conv_w, v_a_log, v_dt_bias, v_gdn_norm_g, v_fox_b_f, v_fox_q_norm, v_fox_k_norm, v_g_mem, v_w_mem_kv, v_mem_q_norm, v_mem_k_norm, v_w_up_gdn, v_w_up_fox, v_w_up_mem, v_w_out, v_g_mlp, v_w_ff1, v_w_ff2):
    ws = (g_mix, w_in, conv_w, a_log, dt_bias, gdn_norm_g, fox_b_f, fox_q_norm, fox_k_norm, g_mem, w_mem_kv, mem_q_norm,
          mem_k_norm, w_up_gdn, w_up_fox, w_up_mem, w_out, g_mlp, w_ff1, w_ff2)
    ms = (m_g_mix, m_w_in, m_conv_w, m_a_log, m_dt_bias, m_gdn_norm_g, m_fox_b_f, m_fox_q_norm, m_fox_k_norm, m_g_mem,
          m_w_mem_kv, m_mem_q_norm, m_mem_k_norm, m_w_up_gdn, m_w_up_fox, m_w_up_mem, m_w_out, m_g_mlp, m_w_ff1, m_w_ff2)
    vs = (v_g_mix, v_w_in, v_conv_w, v_a_log, v_dt_bias, v_gdn_norm_g, v_fox_b_f, v_fox_q_norm, v_fox_k_norm, v_g_mem,
          v_w_mem_kv, v_mem_q_norm, v_mem_k_norm, v_w_up_gdn, v_w_up_fox, v_w_up_mem, v_w_out, v_g_mlp, v_w_ff1, v_w_ff2)
    drop = lambda a: a[0] if a.ndim == 3 else a
    w = {n: drop(a) for n, a in zip(ORDER, ws)}
    m = {n: drop(a) for n, a in zip(ORDER, ms)}
    v = {n: drop(a) for n, a in zip(ORDER, vs)}
    loss, grad_x, g, delta, new_m, new_v = _step(x[0], mem[0], loss_target[0], w, m, v)
    out = [loss, grad_x[None]]
    for d in (g, delta, new_m, new_v):
        out += [d[n].reshape(a.shape) for n, a in zip(ORDER, ws)]
    return tuple(out)
```

```python
import numpy as np

import jax
import jax.numpy as jnp
from jax import lax
from jax.experimental import pallas as pl
from jax.experimental.pallas import tpu as pltpu

F32 = jnp.float32
BF16 = jnp.bfloat16
HI = lax.Precision.HIGHEST
MESH = pl.DeviceIdType.MESH

EPS = 1e-6
HEAD = 128
MEM_DH = 256
CONV_K = 4
CHUNK = 64
CHUNK_SHIFT = 6
LANES = 128
SUB = 8
PACK_W = 1024
PACK_ROWS = 512
VMEM_LIMIT = 56 * 1024 * 1024
NEG = -1e30
SOLVE_PREC = None
FLASH_HEADS_PER_STEP = 8
FLASH_FWD_HEADS_PER_STEP = 4

ADAM_LR, ADAM_B1, ADAM_B2, ADAM_EPS, ADAM_WD, ADAM_STEP = 0.001, 0.9, 0.999, 1e-08, 0.01, 10


def _pcall(body, **kw):
    return pl.pallas_call(body, **kw)


def _cp(*sem):
    return pltpu.CompilerParams(dimension_semantics=sem, vmem_limit_bytes=VMEM_LIMIT)


def _dot(a, b, prec=None):
    return lax.dot_general(a, b, (((1,), (0,)), ((), ())), precision=prec, preferred_element_type=F32)


def _dot_nt(a, b, prec=None):
    return lax.dot_general(a, b, (((1,), (1,)), ((), ())), precision=prec, preferred_element_type=F32)


def _dot_tn(a, b, prec=None):
    return lax.dot_general(a, b, (((0,), (0,)), ((), ())), precision=prec, preferred_element_type=F32)


def _sigmoid(x):
    return 1.0 / (1.0 + jnp.exp(-x))


def _softplus(x):
    return jnp.maximum(x, 0.0) + jnp.log(1.0 + jnp.exp(-jnp.abs(x)))


def _silu(x):
    return x * _sigmoid(x)


def _rms(x, g):
    return x * lax.rsqrt(jnp.mean(x * x, axis=-1, keepdims=True) + EPS) * g


def _ru(a, m):
    return (a + m - 1) // m * m


def _mm(a, b, mode, name, out_dtypes=(F32,), epilogue=None, extras=(), tm=1024, tn=1024, tk=2048):
    if mode == "nn":
        (M, K), (K2, N) = a.shape, b.shape
    elif mode == "nt":
        (M, K), (N, K2) = a.shape, b.shape
    else:
        (K, M), (K2, N) = a.shape, b.shape
    assert K == K2, (a.shape, b.shape, mode)
    tm, tn = min(tm, M), min(tn, N)
    tk = next((t for t in (tk, 3072, 1024, 512, 256, LANES) if t <= K and K % t == 0), K)
    assert M % tm == 0 and N % tn == 0 and K % tk == 0, (M, N, K, tm, tn, tk)
    nk = K // tk
    n_ex, n_out = len(extras), len(out_dtypes)
    dims = {"nn": ((1,), (0,)), "nt": ((1,), (1,)), "tn": ((0,), (0,))}[mode]

    def finish(res, ex_refs, out_refs):
        outs = epilogue(res, *[r[...] for r in ex_refs]) if epilogue is not None else (res,)
        for o_ref, o in zip(out_refs, outs):
            o_ref[...] = o.astype(o_ref.dtype)

    def body(a_ref, b_ref, *rest):
        ex_refs, out_refs = rest[:n_ex], rest[n_ex:n_ex + n_out]
        part = lax.dot_general(a_ref[...], b_ref[...], (dims, ((), ())), preferred_element_type=F32)
        if nk == 1:
            finish(part, ex_refs, out_refs)
            return
        acc = rest[-1]
        k = pl.program_id(2)

        @pl.when(k == 0)
        def _():
            acc[...] = part

        @pl.when(k > 0)
        def _():
            acc[...] += part

        @pl.when(k == nk - 1)
        def _():
            finish(acc[...], ex_refs, out_refs)

    a_spec = pl.BlockSpec((tk, tm), lambda i, j, k: (k, i)) if mode == "tn" else pl.BlockSpec((tm, tk), lambda i, j, k: (i, k))
    b_spec = pl.BlockSpec((tn, tk), lambda i, j, k: (j, k)) if mode == "nt" else pl.BlockSpec((tk, tn), lambda i, j, k: (k, j))
    mn_spec = pl.BlockSpec((tm, tn), lambda i, j, k: (i, j))
    outs = _pcall(
        body, name=name, grid=(M // tm, N // tn, nk),
        in_specs=[a_spec, b_spec] + [mn_spec] * n_ex,
        out_specs=[mn_spec] * n_out,
        out_shape=[jax.ShapeDtypeStruct((M, N), dt) for dt in out_dtypes],
        scratch_shapes=[pltpu.VMEM((tm, tn), F32)] if nk > 1 else [],
        compiler_params=_cp("parallel", "parallel", "arbitrary"),
    )(a, b, *extras)
    return outs[0] if n_out == 1 else outs


def _rowwise(fn, T, tb, ins, outs, name, scratch=()):
    tb = min(tb, T)
    assert T % tb == 0 and (tb % SUB == 0 or tb == T)
    nblk = T // tb
    r8 = tb // SUB
    in_specs, arrs = [], []
    for spec in ins:
        kind, arr = spec[0], spec[1]
        arrs.append(arr)
        if kind == "full":
            nd = arr.ndim
            in_specs.append(pl.BlockSpec(arr.shape, lambda i, nd=nd: (0,) * nd))
            continue
        off, w = spec[2], spec[3]
        assert off % w == 0 and arr.shape[0] == T, (name, off, w, arr.shape)
        cb = off // w
        if kind == "row":
            in_specs.append(pl.BlockSpec((tb, w), lambda i, cb=cb: (i, cb)))
        elif kind == "prev":
            in_specs.append(pl.BlockSpec((SUB, w), lambda i, cb=cb: (jnp.maximum(i * r8 - 1, 0), cb)))
        else:
            in_specs.append(pl.BlockSpec((SUB, w), lambda i, cb=cb: (jnp.minimum((i + 1) * r8, T // SUB - 1), cb)))
    out_specs, out_shapes, is_acc = [], [], []
    for spec in outs:
        if spec[0] == "row":
            out_specs.append(pl.BlockSpec((tb, spec[1]), lambda i: (i, 0)))
            out_shapes.append(jax.ShapeDtypeStruct((T, spec[1]), spec[2]))
            is_acc.append(False)
        else:
            nd = len(spec[1])
            out_specs.append(pl.BlockSpec(spec[1], lambda i, nd=nd: (0,) * nd))
            out_shapes.append(jax.ShapeDtypeStruct(spec[1], spec[2]))
            is_acc.append(True)
    n_in, n_out = len(ins), len(outs)
    seq = any(is_acc) or len(scratch) > 0

    def body(*refs):
        in_refs, out_refs, scr = refs[:n_in], refs[n_in:n_in + n_out], refs[n_in + n_out:]
        i = pl.program_id(0)
        vals = fn(i, nblk, *[r[...] for r in in_refs], *scr)
        for o_ref, v, acc in zip(out_refs, vals, is_acc):
            if acc:
                @pl.when(i == 0)
                def _(o_ref=o_ref):
                    o_ref[...] = jnp.zeros_like(o_ref)

                o_ref[...] += v.astype(o_ref.dtype)
            else:
                o_ref[...] = v.astype(o_ref.dtype)

    res = _pcall(
        body, name=name, grid=(nblk,), in_specs=in_specs, out_specs=out_specs, out_shape=out_shapes,
        scratch_shapes=list(scratch), compiler_params=_cp("arbitrary" if seq else "parallel"),
    )(*arrs)
    return res


def _heads(x, width):
    return [x[:, h * width:(h + 1) * width] for h in range(x.shape[1] // width)]


def _cat(xs):
    return xs[0] if len(xs) == 1 else jnp.concatenate(xs, axis=1)


def _rms_fwd(x, g, name, tb=512):
    T, D = x.shape
    return _rowwise(lambda i, n, xv, gv: (_rms(xv, gv),), T, tb,
                    [("row", x, 0, D), ("full", g)], [("row", D, BF16)], name)[0]


def _rms_bwd(x, g, dh, dres, name, tb=256):
    T, D = x.shape

    def fn(i, n, xv, gv, dhv, drv):
        _, vjp = jax.vjp(_rms, xv, gv)
        dx, dg = vjp(dhv)
        tot = drv + dx
        return tot, tot, dg

    return _rowwise(fn, T, tb, [("row", x, 0, D), ("full", g), ("row", dh, 0, D), ("row", dres, 0, D)],
                    [("row", D, F32), ("row", D, BF16), ("acc", (1, D), F32)], name)


def _rms_dg(x, g, dh, name, tb=256):
    T, D = x.shape

    def fn(i, n, xv, gv, dhv):
        _, vjp = jax.vjp(lambda gg: _rms(xv, gg), gv)
        return vjp(dhv)

    return _rowwise(fn, T, tb, [("row", x, 0, D), ("full", g), ("row", dh, 0, D)], [("acc", (1, D), F32)], name)[0]


def _shift_down(x, halo, s, first):
    if s == 0:
        return x
    tb, c = x.shape
    xr = pltpu.roll(x, s, 0)
    hr = jnp.where(first, 0.0, pltpu.roll(halo, s, 0))
    hfull = hr if tb == SUB else jnp.concatenate([hr, jnp.zeros((tb - SUB, c), x.dtype)], axis=0)
    row = lax.broadcasted_iota(jnp.int32, x.shape, 0)
    return jnp.where(row < s, hfull, xr)


def _shift_up(z, halo, s, last):
    if s == 0:
        return z
    tb, c = z.shape
    zr = pltpu.roll(z, tb - s, 0)
    hr = jnp.where(last, 0.0, pltpu.roll(halo, SUB - s, 0))
    hfull = hr if tb == SUB else jnp.concatenate([jnp.zeros((tb - SUB, c), z.dtype), hr], axis=0)
    row = lax.broadcasted_iota(jnp.int32, z.shape, 0)
    return jnp.where(row >= tb - s, hfull, zr)


def _conv_pre(x, halo, cw, first):
    xs = [_shift_down(x, halo, s, first) for s in range(CONV_K)]
    y = cw[0:1, :] * xs[3]
    for i in range(1, CONV_K):
        y = y + cw[i:i + 1, :] * xs[CONV_K - 1 - i]
    return y, xs


def _qk_post(y, scale):
    a = _silu(y)
    return a * lax.rsqrt(jnp.sum(a * a, axis=-1, keepdims=True) + EPS) * scale


def _small_fn(s, alog, dtb, bf, gh, fh):
    lane = lax.broadcasted_iota(jnp.int32, s.shape, 1)
    beta = _sigmoid(s)
    g = -jnp.exp(alog) * _softplus(s + dtb)
    lf = -_softplus(-(s + bf))
    return jnp.where(lane < gh, beta, jnp.where(lane < 2 * gh, g, jnp.where(lane < 2 * gh + fh, lf, 0.0)))


def _gdn_prep(proj, offs, cws, pvecs, GH, FH, GW, tb=256):
    T = proj.shape[0]
    tb = min(tb, T)
    qscale = HEAD ** -0.5

    def fn(i, n, xq, hq, xk, hk, xv, hv, cwq, cwk, cwv, s, alog, dtb, bf, carry):
        first = i == 0
        yq, _ = _conv_pre(xq, hq, cwq, first)
        yk, _ = _conv_pre(xk, hk, cwk, first)
        yv, _ = _conv_pre(xv, hv, cwv, first)
        qn = _cat([_qk_post(y, qscale) for y in _heads(yq, HEAD)])
        kn = _cat([_qk_post(y, 1.0) for y in _heads(yk, HEAD)])
        vc = _silu(yv)
        gsm = _small_fn(s, alog, dtb, bf, GH, FH)

        @pl.when(first)
        def _():
            carry[...] = jnp.zeros_like(carry)

        ri = lax.broadcasted_iota(jnp.int32, (tb, tb), 0)
        ci = lax.broadcasted_iota(jnp.int32, (tb, tb), 1)
        cum = _dot((ri >= ci).astype(F32), gsm, HI) + carry[0:1, :]
        carry[...] += _dot(jnp.ones((SUB, tb), F32), gsm, HI)
        in_chunk = (ri >= ci) & ((ri >> CHUNK_SHIFT) == (ci >> CHUNK_SHIFT))
        lane = lax.broadcasted_iota(jnp.int32, gsm.shape, 1)
        gbm = jnp.where(lane < GH, gsm, _dot(in_chunk.astype(F32), gsm, HI))
        return qn, kn, vc, cum, gbm

    ins = []
    for key in ("q", "k", "v"):
        ins += [("row", proj, offs[key], GW), ("prev", proj, offs[key], GW)]
    ins += [("full", c) for c in cws] + [("row", proj, offs["small"], LANES)] + [("full", p) for p in pvecs]
    outs = [("row", GW, F32)] * 3 + [("row", LANES, F32)] * 2
    return _rowwise(fn, T, tb, ins, outs, "gdn_prep", scratch=[pltpu.VMEM((SUB, LANES), F32)])


def _gdn_prep_bwd_a(proj, offs, cws, pvecs, cts, dgsm_scan, dlf_sm, GH, FH, GW, tb=256):
    T = proj.shape[0]
    qscale = HEAD ** -0.5

    def one(x, halo, cw, ct, first, post):
        y, xs = _conv_pre(x, halo, cw, first)
        if post is None:
            _, vjp = jax.vjp(_silu, y)
            dy = vjp(ct)[0]
        else:
            dys = []
            for yh, cth in zip(_heads(y, HEAD), _heads(ct, HEAD)):
                _, vjp = jax.vjp(lambda t: _qk_post(t, post), yh)
                dys.append(vjp(cth)[0])
            dy = _cat(dys)
        row = lax.broadcasted_iota(jnp.int32, (SUB, x.shape[1]), 0)
        dcw = jnp.zeros((SUB, x.shape[1]), F32)
        for i in range(CONV_K):
            dcw = dcw + jnp.where(row == i, jnp.sum(dy * xs[CONV_K - 1 - i], axis=0, keepdims=True), 0.0)
        return dy, dcw

    def fn(i, n, xq, hq, xk, hk, xv, hv, cwq, cwk, cwv, cq, ck, cv, s, alog, dtb, bf, d1, d2):
        first = i == 0
        dyq, dcq = one(xq, hq, cwq, cq, first, qscale)
        dyk, dck = one(xk, hk, cwk, ck, first, 1.0)
        dyv, dcv = one(xv, hv, cwv, cv, first, None)
        tb_ = d1.shape[0]
        ri = lax.broadcasted_iota(jnp.int32, (tb_, tb_), 0)
        ci = lax.broadcasted_iota(jnp.int32, (tb_, tb_), 1)
        later = (ci >= ri) & ((ri >> CHUNK_SHIFT) == (ci >> CHUNK_SHIFT))
        lane = lax.broadcasted_iota(jnp.int32, d1.shape, 1)
        d1 = jnp.where(lane < GH, d1, _dot(later.astype(F32), d1, HI))
        _, vjp = jax.vjp(lambda a, b, c, d: _small_fn(a, b, c, d, GH, FH), s, alog, dtb, bf)
        ds, dalog, ddtb, dbf = vjp(d1 + d2)
        return dyq, dyk, dyv, dcq, dck, dcv, ds, dalog, ddtb, dbf

    ins = []
    for key in ("q", "k", "v"):
        ins += [("row", proj, offs[key], GW), ("prev", proj, offs[key], GW)]
    ins += [("full", c) for c in cws] + [("row", c, 0, GW) for c in cts]
    ins += [("row", proj, offs["small"], LANES)] + [("full", p) for p in pvecs]
    ins += [("row", dgsm_scan, 0, LANES), ("row", dlf_sm, 0, LANES)]
    outs = [("row", GW, F32)] * 3 + [("acc", (SUB, GW), F32)] * 3 + [("row", LANES, BF16)] + [("acc", (1, LANES), F32)] * 3
    return _rowwise(fn, T, tb, ins, outs, "gdn_prep_bwd_a")


def _gdn_prep_bwd_b(dys, cws, GW, tb=256):
    T = dys[0].shape[0]

    def fn(i, n, dq, nq, dk, nk, dv, nv, cwq, cwk, cwv):
        last = i == n - 1
        res = []
        for dy, nh, cw in ((dq, nq, cwq), (dk, nk, cwk), (dv, nv, cwv)):
            dx = cw[CONV_K - 1:CONV_K, :] * dy
            for t in range(CONV_K - 1):
                dx = dx + cw[t:t + 1, :] * _shift_up(dy, nh, CONV_K - 1 - t, last)
            res.append(dx)
        return tuple(res)

    ins = []
    for dy in dys:
        ins += [("row", dy, 0, GW), ("next", dy, 0, GW)]
    ins += [("full", c) for c in cws]
    return _rowwise(fn, T, tb, ins, [("row", GW, BF16)] * 3, "gdn_prep_bwd_b")


def _gdn_chunks(qs, ks, vs, gams, bcols, s0s):
    c, d = qs[0].shape
    hs = range(len(qs))
    ri = lax.broadcasted_iota(jnp.int32, (c, c), 0)
    ci = lax.broadcasted_iota(jnp.int32, (c, c), 1)
    incl, strict = ri >= ci, ri > ci
    eye = (ri == ci).astype(F32)
    ones_cc = jnp.ones((c, c), F32)
    rows = lax.broadcasted_iota(jnp.int32, (c, 1), 0)
    b16 = (ri >> 4) == (ci >> 4)
    b32 = (ri >> 5) == (ci >> 5)
    gam_cc = [gams[h] * ones_cc for h in hs]
    gam_t = [_dot_nt(eye, gam_cc[h], HI) for h in hs]
    glast = [jnp.sum(jnp.where(rows == c - 1, gams[h], 0.0), axis=0, keepdims=True) for h in hs]
    dec_i = [jnp.where(incl, jnp.exp(jnp.where(incl, gam_cc[h] - gam_t[h], 0.0)), 0.0) for h in hs]
    kk = [_dot_nt(ks[h], ks[h]) for h in hs]
    m = [bcols[h] * kk[h] * jnp.where(strict, dec_i[h], 0.0) for h in hs]
    m32 = [jnp.where(b32 & ~b16, m[h], 0.0) for h in hs]
    m64 = [jnp.where(b32, 0.0, m[h]) for h in hs]
    mp = [jnp.where(b16, m[h], 0.0) for h in hs]
    p = [eye - mp[h] for h in hs]
    for _ in range(3):
        mp = [_dot(mp[h], mp[h], SOLVE_PREC) for h in hs]
        p = [p[h] + _dot(p[h], mp[h], SOLVE_PREC) for h in hs]
    t = [_dot(p[h], m32[h], SOLVE_PREC) for h in hs]
    p = [p[h] - _dot(t[h], p[h], SOLVE_PREC) for h in hs]
    t = [_dot(p[h], m64[h], SOLVE_PREC) for h in hs]
    ainv = [p[h] - _dot(t[h], p[h], SOLVE_PREC) for h in hs]
    eg = [jnp.exp(gams[h]) for h in hs]
    w = [_dot(ainv[h], (bcols[h] * eg[h]) * ks[h], SOLVE_PREC) for h in hs]
    u0 = [_dot(ainv[h], bcols[h] * vs[h], SOLVE_PREC) for h in hs]
    qk = [_dot_nt(qs[h], ks[h]) * dec_i[h] for h in hs]
    u = [u0[h] - _dot(w[h], s0s[h]) for h in hs]
    o = [_dot(qs[h] * eg[h], s0s[h]) + _dot(qk[h], u[h]) for h in hs]
    s1 = [jnp.exp(glast[h]) * s0s[h] + _dot_tn(ks[h] * jnp.exp(glast[h] - gams[h]), u[h]) for h in hs]
    return tuple(o), tuple(s1)


def _lane_col(x, lane_idx):
    lane = lax.broadcasted_iota(jnp.int32, x.shape, 1)
    return jnp.sum(jnp.where(lane == lane_idx, x, 0.0), axis=1, keepdims=True)


def _gdn_scan_fwd(qn, kn, vc, gsm, GH):
    T, GW = qn.shape
    nc = T // CHUNK

    def body(q_ref, k_ref, v_ref, g_ref, o_ref, sall_ref, s_scr):
        @pl.when(pl.program_id(0) == 0)
        def _():
            s_scr[...] = jnp.zeros_like(s_scr)

        gs = g_ref[...]
        sls = [slice(h * HEAD, (h + 1) * HEAD) for h in range(GH)]
        s0s = tuple(s_scr[h] for h in range(GH))
        os_, s1s = _gdn_chunks(tuple(q_ref[:, sl] for sl in sls), tuple(k_ref[:, sl] for sl in sls),
                               tuple(v_ref[:, sl] for sl in sls), tuple(_lane_col(gs, GH + h) for h in range(GH)),
                               tuple(_lane_col(gs, h) for h in range(GH)), s0s)
        for h in range(GH):
            sall_ref[0, h] = s0s[h]
            o_ref[:, sls[h]] = os_[h]
            s_scr[h] = s1s[h]

    row = pl.BlockSpec((CHUNK, GW), lambda i: (i, 0))
    return _pcall(
        body, name="gdn_scan_fwd", grid=(nc,),
        in_specs=[row, row, row, pl.BlockSpec((CHUNK, LANES), lambda i: (i, 0))],
        out_specs=[row, pl.BlockSpec((1, GH, HEAD, HEAD), lambda i: (i, 0, 0, 0))],
        out_shape=[jax.ShapeDtypeStruct((T, GW), F32), jax.ShapeDtypeStruct((nc, GH, HEAD, HEAD), F32)],
        scratch_shapes=[pltpu.VMEM((GH, HEAD, HEAD), F32)],
        compiler_params=_cp("arbitrary"),
    )(qn, kn, vc, gsm)


def _gdn_scan_bwd(qn, kn, vc, gsm, sall, do, GH):
    T, GW = qn.shape
    nc = T // CHUNK

    def body(q_ref, k_ref, v_ref, g_ref, sall_ref, do_ref, dq_ref, dk_ref, dv_ref, dg_ref, ds_scr):
        @pl.when(pl.program_id(0) == 0)
        def _():
            ds_scr[...] = jnp.zeros_like(ds_scr)

        gs = g_ref[...]
        lane = lax.broadcasted_iota(jnp.int32, gs.shape, 1)
        sls = [slice(h * HEAD, (h + 1) * HEAD) for h in range(GH)]
        _, vjp = jax.vjp(_gdn_chunks, tuple(q_ref[:, sl] for sl in sls), tuple(k_ref[:, sl] for sl in sls),
                         tuple(v_ref[:, sl] for sl in sls), tuple(_lane_col(gs, GH + h) for h in range(GH)),
                         tuple(_lane_col(gs, h) for h in range(GH)), tuple(sall_ref[0, h] for h in range(GH)))
        dq, dk, dv, dgc, dbc, ds0 = vjp((tuple(do_ref[:, sl] for sl in sls), tuple(ds_scr[h] for h in range(GH))))
        dgs = jnp.zeros_like(gs)
        for h in range(GH):
            dq_ref[:, sls[h]] = dq[h]
            dk_ref[:, sls[h]] = dk[h]
            dv_ref[:, sls[h]] = dv[h]
            dgs = dgs + jnp.where(lane == h, dbc[h], 0.0) + jnp.where(lane == GH + h, dgc[h], 0.0)
            ds_scr[h] = ds0[h]
        dg_ref[...] = dgs

    row = pl.BlockSpec((CHUNK, GW), lambda i: (nc - 1 - i, 0))
    sm = pl.BlockSpec((CHUNK, LANES), lambda i: (nc - 1 - i, 0))
    return _pcall(
        body, name="gdn_scan_bwd", grid=(nc,),
        in_specs=[row, row, row, sm, pl.BlockSpec((1, GH, HEAD, HEAD), lambda i: (nc - 1 - i, 0, 0, 0)), row],
        out_specs=[row, row, row, sm],
        out_shape=[jax.ShapeDtypeStruct((T, GW), F32)] * 3 + [jax.ShapeDtypeStruct((T, LANES), F32)],
        scratch_shapes=[pltpu.VMEM((GH, HEAD, HEAD), F32)],
        compiler_params=_cp("arbitrary"),
    )(qn, kn, vc, gsm, sall, do)


def _gdn_post_fn(o, z, g):
    return _rms(o, g) * _silu(z)


def _gdn_post(o, proj, off_z, g, GW, tb=512):
    T = o.shape[0]

    def fn(i, n, ov, zv, gv):
        return (_cat([_gdn_post_fn(a, b, gv) for a, b in zip(_heads(ov, HEAD), _heads(zv, HEAD))]),)

    return _rowwise(fn, T, tb, [("row", o, 0, GW), ("row", proj, off_z, GW), ("full", g)], [("row", GW, BF16)], "gdn_post")[0]


def _gdn_post_bwd(o, proj, off_z, g, dout, GW, tb=256):
    T = o.shape[0]

    def fn(i, n, ov, zv, gv, dv):
        dos, dzs, dg = [], [], jnp.zeros_like(gv)
        for a, b, c in zip(_heads(ov, HEAD), _heads(zv, HEAD), _heads(dv, HEAD)):
            _, vjp = jax.vjp(_gdn_post_fn, a, b, gv)
            da, db, dgh = vjp(c)
            dos.append(da)
            dzs.append(db)
            dg = dg + dgh
        return _cat(dos), _cat(dzs), dg

    return _rowwise(fn, T, tb, [("row", o, 0, GW), ("row", proj, off_z, GW), ("full", g), ("row", dout, 0, GW)],
                    [("row", GW, F32), ("row", GW, BF16), ("acc", (1, HEAD), F32)], "gdn_post_bwd")


def _fox_prep(proj, offs, gq, gk, FW, tb=512):
    T = proj.shape[0]

    def fn(i, n, q, k, v, gqv, gkv):
        return (_cat([_rms(a, gqv) * (HEAD ** -0.5) for a in _heads(q, HEAD)]),
                _cat([_rms(a, gkv) for a in _heads(k, HEAD)]), v)

    return _rowwise(fn, T, tb, [("row", proj, offs["fq"], FW), ("row", proj, offs["fk"], FW), ("row", proj, offs["fv"], FW),
                                ("full", gq), ("full", gk)], [("row", FW, BF16)] * 3, "fox_prep")


def _fox_prep_bwd(proj, offs, gq, gk, dq, dk, dv, FW, tb=256):
    T = proj.shape[0]

    def fn(i, n, q, k, gqv, gkv, dqv, dkv, dvv):
        res = []
        for x, g, d in ((q, gqv, dqv), (k, gkv, dkv)):
            dxs, dg = [], jnp.zeros_like(g)
            for a, c in zip(_heads(x, HEAD), _heads(d, HEAD)):
                _, vjp = jax.vjp(_rms, a, g)
                da, dgh = vjp(c)
                dxs.append(da)
                dg = dg + dgh
            res += [_cat(dxs), dg]
        return res[0], res[2], dvv, res[1], res[3]

    return _rowwise(fn, T, tb, [("row", proj, offs["fq"], FW), ("row", proj, offs["fk"], FW), ("full", gq), ("full", gk),
                                ("row", dq, 0, FW), ("row", dk, 0, FW), ("row", dv, 0, FW)],
                    [("row", FW, BF16)] * 3 + [("acc", (1, HEAD), F32)] * 2, "fox_prep_bwd")


def _sub_row(x, sub_idx):
    sub = lax.broadcasted_iota(jnp.int32, x.shape, 0)
    return jnp.sum(jnp.where(sub == sub_idx, x, 0.0), axis=0, keepdims=True)


def _causal_pairs(nb, key_major):
    if key_major:
        pairs = [(i, j) for j in range(nb) for i in range(j, nb)]
    else:
        pairs = [(i, j) for i in range(nb) for j in range(i + 1)]
    return (jnp.asarray(np.array([p[0] for p in pairs], np.int32)),
            jnp.asarray(np.array([p[1] for p in pairs], np.int32)))


def _fox_scores(q, k, ck, diagonal):
    s = _dot_nt(q, k) - ck
    if diagonal:
        row = lax.broadcasted_iota(jnp.int32, s.shape, 0)
        col = lax.broadcasted_iota(jnp.int32, s.shape, 1)
        s = jnp.where(row >= col, s, NEG)
    return s


def _flash_fwd(qb, kb, vb, cumr, FH, blk):
    T, FW = qb.shape
    blk = min(blk, T)
    nb = T // blk
    qi_arr, kj_arr = _causal_pairs(nb, False)

    hps = min(FLASH_FWD_HEADS_PER_STEP, FH)
    assert FH % hps == 0
    us = range(hps)
    sl = [slice(u * HEAD, (u + 1) * HEAD) for u in us]

    def body(qi_ref, kj_ref, q_ref, k_ref, v_ref, ck_ref, ob_ref, lse_ref, m_scr, l_scr, acc):
        t = pl.program_id(1)
        qi, kj = qi_ref[t], kj_ref[t]

        @pl.when(kj == 0)
        def _():
            m_scr[...] = jnp.full_like(m_scr, NEG)
            l_scr[...] = jnp.zeros_like(l_scr)
            acc[...] = jnp.zeros_like(acc)

        def update(diagonal):
            s = [_fox_scores(q_ref[:, sl[u]], k_ref[:, sl[u]], ck_ref[u], diagonal) for u in us]
            m_old = [m_scr[u] for u in us]
            m_new = [jnp.maximum(m_old[u], jnp.max(s[u], axis=1, keepdims=True)) for u in us]
            alpha = [jnp.exp(m_old[u] - m_new[u]) for u in us]
            p = [jnp.exp(s[u] - m_new[u]) for u in us]
            pv = [_dot(p[u].astype(BF16), v_ref[:, sl[u]]) for u in us]
            for u in us:
                l_scr[u] = alpha[u] * l_scr[u] + jnp.sum(p[u], axis=1, keepdims=True)
                acc[:, sl[u]] = alpha[u] * acc[:, sl[u]] + pv[u]
                m_scr[u] = m_new[u]

        @pl.when(kj < qi)
        def _():
            update(False)

        @pl.when(kj == qi)
        def _():
            update(True)
            for u in us:
                ob_ref[:, sl[u]] = (acc[:, sl[u]] / l_scr[u]).astype(BF16)
                lse_ref[u] = m_scr[u] + jnp.log(l_scr[u])

    qs = pl.BlockSpec((blk, hps * HEAD), lambda g, t, qr, kr: (qr[t], g))
    ks = pl.BlockSpec((blk, hps * HEAD), lambda g, t, qr, kr: (kr[t], g))
    col = pl.BlockSpec((hps, blk, 1), lambda g, t, qr, kr: (g, qr[t], 0))
    gs = pltpu.PrefetchScalarGridSpec(
        num_scalar_prefetch=2, grid=(FH // hps, qi_arr.shape[0]),
        in_specs=[qs, ks, ks, pl.BlockSpec((hps, 1, blk), lambda g, t, qr, kr: (g, 0, kr[t]))],
        out_specs=[qs, col],
        scratch_shapes=[pltpu.VMEM((hps, blk, 1), F32), pltpu.VMEM((hps, blk, 1), F32), pltpu.VMEM((blk, hps * HEAD), F32)])
    return _pcall(
        body, name="flash_fwd", grid_spec=gs,
        out_shape=[jax.ShapeDtypeStruct((T, FW), BF16), jax.ShapeDtypeStruct((FH, T, 1), F32)],
        compiler_params=_cp("parallel", "arbitrary"),
    )(qi_arr, kj_arr, qb, kb, vb, cumr)


def _flash_bwd_q(qb, kb, vb, cumr, lse, do, dl, FH, blk):
    T, FW = qb.shape
    blk = min(blk, T)
    nb = T // blk
    qi_arr, kj_arr = _causal_pairs(nb, False)
    want_dq = dl is not None
    hps = min(FLASH_HEADS_PER_STEP, FH)
    assert FH % hps == 0
    us = range(hps)
    sl = [slice(u * HEAD, (u + 1) * HEAD) for u in us]

    def body(qi_ref, kj_ref, q_ref, k_ref, v_ref, ck_ref, lse_ref, do_ref, *rest):
        if want_dq:
            dl_ref, dq_ref, rs_ref, acc, rs_acc = rest
        else:
            dl_ref, acc = rest
        t = pl.program_id(1)
        qi, kj = qi_ref[t], kj_ref[t]

        @pl.when(kj == 0)
        def _():
            acc[...] = jnp.zeros_like(acc)
            if want_dq:
                rs_acc[...] = jnp.zeros_like(rs_acc)

        def update(diagonal):
            s = [_fox_scores(q_ref[:, sl[u]], k_ref[:, sl[u]], ck_ref[u], diagonal) for u in us]
            p = [jnp.exp(s[u] - lse_ref[u]) for u in us]
            dp = [_dot_nt(do_ref[:, sl[u]].astype(BF16), v_ref[:, sl[u]]) for u in us]
            if want_dq:
                ds = [p[u] * (dp[u] - dl_ref[u]) for u in us]
                dqp = [_dot(ds[u].astype(BF16), k_ref[:, sl[u]]) for u in us]
                for u in us:
                    acc[:, sl[u]] += dqp[u]
                    rs_acc[u] += jnp.sum(ds[u], axis=1, keepdims=True)
            else:
                for u in us:
                    acc[u] += jnp.sum(p[u] * dp[u], axis=1, keepdims=True)

        @pl.when(kj < qi)
        def _():
            update(False)

        @pl.when(kj == qi)
        def _():
            update(True)
            if want_dq:
                dq_ref[...] = acc[...] * (HEAD ** -0.5)
                rs_ref[...] = rs_acc[...]
            else:
                dl_ref[...] = acc[...]

    qs = pl.BlockSpec((blk, hps * HEAD), lambda g, t, qr, kr: (qr[t], g))
    ks = pl.BlockSpec((blk, hps * HEAD), lambda g, t, qr, kr: (kr[t], g))
    col = pl.BlockSpec((hps, blk, 1), lambda g, t, qr, kr: (g, qr[t], 0))
    in_specs = [qs, ks, ks, pl.BlockSpec((hps, 1, blk), lambda g, t, qr, kr: (g, 0, kr[t])), col, qs]
    args = [qi_arr, kj_arr, qb, kb, vb, cumr, lse, do]
    colshape = jax.ShapeDtypeStruct((FH, T, 1), F32)
    if want_dq:
        gs = pltpu.PrefetchScalarGridSpec(
            num_scalar_prefetch=2, grid=(FH // hps, qi_arr.shape[0]), in_specs=in_specs + [col], out_specs=[qs, col],
            scratch_shapes=[pltpu.VMEM((blk, hps * HEAD), F32), pltpu.VMEM((hps, blk, 1), F32)])
        return _pcall(body, name="flash_bwd_dq", grid_spec=gs,
                      out_shape=[jax.ShapeDtypeStruct((T, FW), F32), colshape],
                      compiler_params=_cp("parallel", "arbitrary"))(*args, dl)
    gs = pltpu.PrefetchScalarGridSpec(
        num_scalar_prefetch=2, grid=(FH // hps, qi_arr.shape[0]), in_specs=in_specs, out_specs=col,
        scratch_shapes=[pltpu.VMEM((hps, blk, 1), F32)])
    return _pcall(body, name="flash_bwd_rowterm", grid_spec=gs, out_shape=colshape,
                  compiler_params=_cp("parallel", "arbitrary"))(*args)


def _flash_bwd_dkv(qb, kb, vb, cumc, lse_row, dl_row, do, FH, blk):
    T, FW = qb.shape
    blk = min(blk, T)
    nb = T // blk

    qi_arr, kj_arr = _causal_pairs(nb, True)

    hps = min(FLASH_HEADS_PER_STEP, FH)
    assert FH % hps == 0
    us = range(hps)
    sl = [slice(u * HEAD, (u + 1) * HEAD) for u in us]

    def body(qi_ref, kj_ref, q_ref, k_ref, v_ref, ck_ref, lse_ref, dl_ref, do_ref,
             dk_ref, dv_ref, dc_ref, dk_acc, dv_acc, dc_acc):
        t = pl.program_id(1)
        qi, kj = qi_ref[t], kj_ref[t]

        def update(diagonal, first):
            st = [_dot_nt(k_ref[:, sl[u]], q_ref[:, sl[u]]) - ck_ref[u] for u in us]
            if diagonal:
                krow = lax.broadcasted_iota(jnp.int32, st[0].shape, 0)
                qcol = lax.broadcasted_iota(jnp.int32, st[0].shape, 1)
                st = [jnp.where(qcol >= krow, st[u], NEG) for u in us]
            pt = [jnp.exp(st[u] - lse_ref[u]) for u in us]
            dob = [do_ref[:, sl[u]].astype(BF16) for u in us]
            dpt = [_dot_nt(v_ref[:, sl[u]], dob[u]) for u in us]
            dst = [pt[u] * (dpt[u] - dl_ref[u]) for u in us]
            dk = [_dot(dst[u].astype(BF16), q_ref[:, sl[u]]) for u in us]
            dv = [_dot(pt[u].astype(BF16), dob[u]) for u in us]
            for u in us:
                dc = -jnp.sum(dst[u], axis=1, keepdims=True)
                if first:
                    dk_acc[:, sl[u]] = dk[u]
                    dv_acc[:, sl[u]] = dv[u]
                    dc_acc[u] = dc
                else:
                    dk_acc[:, sl[u]] += dk[u]
                    dv_acc[:, sl[u]] += dv[u]
                    dc_acc[u] += dc

        @pl.when(qi == kj)
        def _():
            update(True, True)

        @pl.when(qi > kj)
        def _():
            update(False, False)

        @pl.when(qi == nb - 1)
        def _():
            dk_ref[...] = dk_acc[...]
            dv_ref[...] = dv_acc[...]
            dc_ref[...] = dc_acc[...]

    ks = pl.BlockSpec((blk, hps * HEAD), lambda g, t, qr, kr: (kr[t], g))
    qs = pl.BlockSpec((blk, hps * HEAD), lambda g, t, qr, kr: (qr[t], g))
    rowq = pl.BlockSpec((hps, 1, blk), lambda g, t, qr, kr: (g, 0, qr[t]))
    colk = pl.BlockSpec((hps, blk, 1), lambda g, t, qr, kr: (g, kr[t], 0))
    gs = pltpu.PrefetchScalarGridSpec(
        num_scalar_prefetch=2, grid=(FH // hps, qi_arr.shape[0]),
        in_specs=[qs, ks, ks, colk, rowq, rowq, qs],
        out_specs=[ks, ks, colk],
        scratch_shapes=[pltpu.VMEM((blk, hps * HEAD), F32), pltpu.VMEM((blk, hps * HEAD), F32),
                        pltpu.VMEM((hps, blk, 1), F32)])
    return _pcall(
        body, name="flash_bwd_dkv", grid_spec=gs,
        out_shape=[jax.ShapeDtypeStruct((T, FW), F32), jax.ShapeDtypeStruct((T, FW), F32),
                   jax.ShapeDtypeStruct((FH, T, 1), F32)],
        compiler_params=_cp("parallel", "arbitrary"),
    )(qi_arr, kj_arr, qb, kb, vb, cumc, lse_row, dl_row, do)


def _rev_cumsum_rows(r1, r2, tb=512):
    H, T = r1.shape
    tb = min(tb, T)
    nb = T // tb

    def body(r1_ref, r2_ref, o_ref, carry):
        @pl.when(pl.program_id(0) == 0)
        def _():
            carry[...] = jnp.zeros_like(carry)

        rv = r1_ref[...] + r2_ref[...]
        si = lax.broadcasted_iota(jnp.int32, (tb, tb), 0)
        ti = lax.broadcasted_iota(jnp.int32, (tb, tb), 1)
        o_ref[...] = _dot(rv, (si >= ti).astype(F32), HI) + carry[...]
        carry[...] += jnp.sum(rv, axis=1, keepdims=True)

    spec = pl.BlockSpec((H, tb), lambda i: (0, nb - 1 - i))
    return _pcall(body, name="rev_cumsum", grid=(nb,), in_specs=[spec, spec], out_specs=spec,
                  out_shape=jax.ShapeDtypeStruct((H, T), F32), scratch_shapes=[pltpu.VMEM((H, 1), F32)],
                  compiler_params=_cp("arbitrary"))(r1, r2)


def _mem_head(q, k, v, gq, gk):
    logits = _dot_nt(_rms(q, gq), _rms(k, gk)) * (MEM_DH ** -0.5)
    mx = jnp.max(logits, axis=1, keepdims=True)
    e = jnp.exp(logits - mx)
    p = e / jnp.sum(e, axis=1, keepdims=True)
    return _dot(p, v)


def _mem_attn(proj, off_q, kv, gq, gk, MW, tb=512):
    T = proj.shape[0]
    MH = MW // MEM_DH

    def fn(i, n, q, kvv, gqv, gkv):
        ks, vs = _heads(kvv[:, :MW], MEM_DH), _heads(kvv[:, MW:], MEM_DH)
        return (_cat([_mem_head(a, b, c, gqv, gkv) for a, b, c in zip(_heads(q, MEM_DH), ks, vs)]),)

    return _rowwise(fn, T, tb, [("row", proj, off_q, MW), ("full", kv), ("full", gq), ("full", gk)],
                    [("row", MW, BF16)], "mem_attn")[0]


def _mem_attn_bwd(proj, off_q, kv, gq, gk, dout, MW, tb=256):
    T = proj.shape[0]
    ML = kv.shape[0]

    def fn(i, n, q, kvv, gqv, gkv, dv):
        ks, vs = _heads(kvv[:, :MW], MEM_DH), _heads(kvv[:, MW:], MEM_DH)
        dqs, dks, dvs = [], [], []
        dgq, dgk = jnp.zeros_like(gqv), jnp.zeros_like(gkv)
        for a, b, c, d in zip(_heads(q, MEM_DH), ks, vs, _heads(dv, MEM_DH)):
            _, vjp = jax.vjp(_mem_head, a, b, c, gqv, gkv)
            da, db, dc, dg1, dg2 = vjp(d)
            dqs.append(da)
            dks.append(db)
            dvs.append(dc)
            dgq, dgk = dgq + dg1, dgk + dg2
        return _cat(dqs), _cat(dks + dvs), dgq, dgk

    return _rowwise(fn, T, tb, [("row", proj, off_q, MW), ("full", kv), ("full", gq), ("full", gk), ("row", dout, 0, MW)],
                    [("row", MW, BF16), ("acc", (ML, 2 * MW), F32), ("acc", (1, MEM_DH), F32), ("acc", (1, MEM_DH), F32)],
                    "mem_attn_bwd")


def _merge_fn(ga, gb, gm, ua, ub, um):
    return _sigmoid(ga) * ua + _sigmoid(gb) * ub + _sigmoid(gm) * um


def _merge(proj, offs, ua, ub, um, D, tb=256):
    T = proj.shape[0]
    ins = [("row", proj, offs[k], D) for k in ("ga", "gb", "gm")] + [("row", u, 0, D) for u in (ua, ub, um)]
    return _rowwise(lambda i, n, *v: (_merge_fn(*v),), T, tb, ins, [("row", D, BF16)], "merge")[0]


def _merge_bwd(proj, offs, ua, ub, um, dy, D, tb=256):
    T = proj.shape[0]

    def fn(i, n, *v):
        _, vjp = jax.vjp(_merge_fn, *v[:6])
        return vjp(v[6])

    ins = [("row", proj, offs[k], D) for k in ("ga", "gb", "gm")] + [("row", u, 0, D) for u in (ua, ub, um)] + [("row", dy, 0, D)]
    return _rowwise(fn, T, tb, ins, [("row", D, BF16)] * 6, "merge_bwd")


def _loss_grad(x2, tgt, tb=256):
    T, D = x2.shape

    def fn(i, n, a, b):
        e = a - b
        part = jnp.sum(jnp.sum(e * e, axis=1, keepdims=True), axis=0, keepdims=True) * (0.5 / D)
        g = e * (1.0 / D)
        return g, g, part + jnp.zeros((SUB, LANES), F32)

    return _rowwise(fn, T, tb, [("row", x2, 0, D), ("row", tgt, 0, D)],
                    [("row", D, F32), ("row", D, BF16), ("acc", (SUB, LANES), F32)], "loss_grad")


def _adamw(w, g, m, v, name, tb=128):
    R, C = w.shape
    c1 = 1.0 / (1.0 - ADAM_B1 ** ADAM_STEP)
    c2 = 1.0 / (1.0 - ADAM_B2 ** ADAM_STEP)

    def fn(i, n, wv, gv, mv, vv):
        mn = ADAM_B1 * mv + (1.0 - ADAM_B1) * gv
        vn = ADAM_B2 * vv + (1.0 - ADAM_B2) * (gv * gv)
        delta = -ADAM_LR * ((mn * c1) / (jnp.sqrt(vn * c2) + ADAM_EPS) + ADAM_WD * wv)
        return delta, mn, vn

    return _rowwise(fn, R, tb, [("row", a, 0, C) for a in (w, g, m, v)], [("row", C, F32)] * 3, name)


def _coords():
    return lax.axis_index("x"), lax.axis_index("y"), lax.axis_index("c")


def _allgather_small(blk):
    m_per, n = blk.shape

    def body(x_ref, out_ref, send_sems, recv_sems, local_sem):
        x, y, c = _coords()
        me, sibling = (x, y, c), (x, y, 1 - c)
        chips = [(1 - x, y), (x, 1 - y), (1 - x, 1 - y)]

        def rows(px, py, pc):
            return out_ref.at[pl.ds((4 * px + 2 * py + pc) * m_per, m_per), :]

        def copy(k, block, to, src=None):
            return pltpu.make_async_remote_copy(
                src_ref=rows(*block) if src is None else src, dst_ref=rows(*block),
                send_sem=send_sems.at[k], recv_sem=recv_sems.at[k], device_id=to, device_id_type=MESH)

        mine = pltpu.make_async_copy(x_ref, rows(*me), local_sem)
        mine.start()
        first = [copy(0, me, sibling, src=x_ref)]
        first += [copy(1 + j, me, (*chip, c), src=x_ref) for j, chip in enumerate(chips)]
        for cp in first:
            cp.start()
        passed = [copy(4 + j, (*chip, c), sibling) for j, chip in enumerate(chips)]
        for j, chip in enumerate(chips):
            copy(1 + j, (*chip, c), me).wait_recv()
            passed[j].start()
        copy(0, sibling, me).wait_recv()
        for j, chip in enumerate(chips):
            copy(4 + j, (*chip, 1 - c), me).wait_recv()
        for cp in first + passed:
            cp.wait_send()
        mine.wait()

    return _pcall(
        body, name="allgather_small", out_shape=jax.ShapeDtypeStruct((8 * m_per, n), blk.dtype),
        in_specs=[pl.BlockSpec(memory_space=pltpu.VMEM)], out_specs=pl.BlockSpec(memory_space=pltpu.VMEM),
        scratch_shapes=[pltpu.SemaphoreType.DMA((7,)), pltpu.SemaphoreType.DMA((7,)), pltpu.SemaphoreType.DMA],
        compiler_params=pltpu.CompilerParams(vmem_limit_bytes=VMEM_LIMIT),
    )(blk)


def _sum8(g, m_per):
    n = g.shape[1]

    def body(g_ref, o_ref):
        acc = g_ref[pl.ds(0, m_per), :]
        for d in range(1, 8):
            acc = acc + g_ref[pl.ds(d * m_per, m_per), :]
        o_ref[...] = acc

    return _pcall(body, name="sum8", out_shape=jax.ShapeDtypeStruct((m_per, n), g.dtype))(g)


_ANY = pl.BlockSpec(memory_space=pl.ANY)


def _allgather_chips(buf):
    nr, w = buf.shape
    half = nr // 2

    def body(in_ref, out_ref, send_sems, recv_sems):
        x, y, c = _coords()
        me = 2 * x + y
        chips = [(1 - x, y), (x, 1 - y), (1 - x, 1 - y)]
        mine_rows = pl.ds(pl.multiple_of(c * half, 16), half)
        other_rows = pl.ds(pl.multiple_of((1 - c) * half, 16), half)

        def copy(k, src, dst, to):
            return pltpu.make_async_remote_copy(src_ref=src, dst_ref=dst, send_sem=send_sems.at[k],
                                                recv_sem=recv_sems.at[k], device_id=to, device_id_type=MESH)

        first = [copy(j, in_ref.at[mine_rows], out_ref.at[me, mine_rows], (cx, cy, c)) for j, (cx, cy) in enumerate(chips)]
        for cp in first:
            cp.start()
        passed = []
        for j, (cx, cy) in enumerate(chips):
            slot = out_ref.at[2 * cx + cy, mine_rows]
            copy(j, slot, slot, (cx, cy, c)).wait_recv()
            fwd = copy(3 + j, slot, slot, (x, y, 1 - c))
            fwd.start()
            passed.append(fwd)
        for j, (cx, cy) in enumerate(chips):
            slot = out_ref.at[2 * cx + cy, other_rows]
            copy(3 + j, slot, slot, (x, y, 1 - c)).wait_recv()
        for cp in first + passed:
            cp.wait_send()

    return _pcall(
        body, name="allgather_chips", out_shape=jax.ShapeDtypeStruct((4, nr, w), buf.dtype),
        in_specs=[_ANY], out_specs=_ANY,
        scratch_shapes=[pltpu.SemaphoreType.DMA((6,)), pltpu.SemaphoreType.DMA((6,))],
    )(buf)


def _rs_pair_exchange(g):
    _, nr, w = g.shape
    half = nr // 2

    def body(g_ref, rb_ref, send_sem, recv_sem):
        x, y, c = _coords()
        other_rows = pl.ds(pl.multiple_of((1 - c) * half, SUB), half)
        cp = pltpu.make_async_remote_copy(src_ref=g_ref.at[:, other_rows], dst_ref=rb_ref, send_sem=send_sem,
                                          recv_sem=recv_sem, device_id=(x, y, 1 - c), device_id_type=MESH)
        cp.start()
        cp.wait()

    return _pcall(body, name="rs_pair_exchange", out_shape=jax.ShapeDtypeStruct((4, half, w), g.dtype),
                  in_specs=[_ANY], out_specs=_ANY,
                  scratch_shapes=[pltpu.SemaphoreType.DMA, pltpu.SemaphoreType.DMA])(g)


def _rs_pair_add(g, rb, cidx, tb=256):
    _, nr, w = g.shape
    half = nr // 2
    tb = min(tb, half)
    assert half % tb == 0
    hb = half // tb

    def body(c_ref, g_ref, r_ref, o_ref):
        o_ref[...] = (g_ref[...].astype(F32) + r_ref[...].astype(F32)).astype(o_ref.dtype)

    gs = pltpu.PrefetchScalarGridSpec(
        num_scalar_prefetch=1, grid=(4, hb),
        in_specs=[pl.BlockSpec((1, tb, w), lambda j, i, c_ref: (j, c_ref[0] * hb + i, 0)),
                  pl.BlockSpec((1, tb, w), lambda j, i, c_ref: (j, i, 0))],
        out_specs=pl.BlockSpec((1, tb, w), lambda j, i, c_ref: (j, i, 0)))
    return _pcall(body, name="rs_pair_add", grid_spec=gs, out_shape=jax.ShapeDtypeStruct((4, half, w), BF16),
                  compiler_params=_cp("parallel", "parallel"))(cidx, g, rb)


def _rs_chip_exchange(p):
    _, h, w = p.shape

    def body(p_ref, rb_ref, send_sems, recv_sems):
        x, y, c = _coords()
        chips = [(1 - x, y), (x, 1 - y), (1 - x, 1 - y)]
        cps = [pltpu.make_async_remote_copy(src_ref=p_ref.at[2 * cx + cy], dst_ref=rb_ref.at[j], send_sem=send_sems.at[j],
                                            recv_sem=recv_sems.at[j], device_id=(cx, cy, c), device_id_type=MESH)
               for j, (cx, cy) in enumerate(chips)]
        for cp in cps:
            cp.start()
        for cp in cps:
            cp.wait()

    return _pcall(body, name="rs_chip_exchange", out_shape=jax.ShapeDtypeStruct((3, h, w), p.dtype),
                  in_specs=[_ANY], out_specs=_ANY,
                  scratch_shapes=[pltpu.SemaphoreType.DMA((3,)), pltpu.SemaphoreType.DMA((3,))])(p)


def _sum4(p, rb, chip_idx, tb=256):
    _, h, w = rb.shape
    tb = min(tb, h)
    assert h % tb == 0

    def body(m_ref, p_ref, r_ref, o_ref):
        f = lambda t: t.astype(F32)
        o_ref[...] = ((f(p_ref[0]) + f(r_ref[0])) + f(r_ref[1])) + f(r_ref[2])

    gs = pltpu.PrefetchScalarGridSpec(
        num_scalar_prefetch=1, grid=(h // tb,),
        in_specs=[pl.BlockSpec((1, tb, w), lambda i, m_ref: (m_ref[0], i, 0)),
                  pl.BlockSpec((3, tb, w), lambda i, m_ref: (0, i, 0))],
        out_specs=pl.BlockSpec((tb, w), lambda i, m_ref: (i, 0)))
    return _pcall(body, name="sum4", grid_spec=gs, out_shape=jax.ShapeDtypeStruct((h, w), F32),
                  compiler_params=_cp("parallel"))(chip_idx, p, rb)


def _pair_allgather(f):
    h, w = f.shape

    def body(f_ref, out_ref, send_sem, recv_sem):
        x, y, c = _coords()
        mine_rows = pl.ds(pl.multiple_of(c * h, SUB), h)
        other_rows = pl.ds(pl.multiple_of((1 - c) * h, SUB), h)
        send = pltpu.make_async_remote_copy(src_ref=f_ref, dst_ref=out_ref.at[mine_rows], send_sem=send_sem,
                                            recv_sem=recv_sem, device_id=(x, y, 1 - c), device_id_type=MESH)
        send.start()
        send.wait_send()
        pltpu.make_async_remote_copy(src_ref=f_ref, dst_ref=out_ref.at[other_rows], send_sem=send_sem,
                                     recv_sem=recv_sem, device_id=(x, y, 1 - c), device_id_type=MESH).wait_recv()

    return _pcall(body, name="pair_allgather", out_shape=jax.ShapeDtypeStruct((2 * h, w), f.dtype),
                  in_specs=[_ANY], out_specs=_ANY,
                  scratch_shapes=[pltpu.SemaphoreType.DMA, pltpu.SemaphoreType.DMA])(f)


def _size(shape):
    n = 1
    for d in shape:
        n *= d
    return n


PACK_ALIGN = 16


def _pack(arrs, dtype, row_mult):
    parts = []
    for a in arrs:
        flat = a.astype(dtype).reshape(-1)
        n = flat.shape[0]
        full = _ru(n, PACK_W * PACK_ALIGN)
        if full > n:
            flat = jnp.pad(flat, (0, full - n))
        parts.append(flat.reshape(-1, PACK_W))
    rows = sum(p.shape[0] for p in parts)
    if rows % row_mult:
        parts.append(jnp.zeros((_ru(rows, row_mult) - rows, PACK_W), dtype))
    return jnp.concatenate(parts, axis=0)


def _unpack(buf, shapes):
    out, off = [], 0
    for s in shapes:
        n = _size(s)
        r = _ru(-(-n // PACK_W), PACK_ALIGN)
        part = buf[off:off + r]
        out.append(part.reshape(s) if n == r * PACK_W else part.reshape(-1)[:n].reshape(s))
        off += r
    return out


def _pack_flat(arrs, dtype, row_mult):
    flat = jnp.concatenate([a.astype(dtype).reshape(-1) for a in arrs])
    n = flat.shape[0]
    rows = _ru(-(-n // PACK_W), row_mult)
    return jnp.pad(flat, (0, rows * PACK_W - n)).reshape(rows, PACK_W)


def _unpack_flat(buf, shapes):
    flat = buf.reshape(-1)
    out, off = [], 0
    for s in shapes:
        n = _size(s)
        out.append(flat[off:off + n].reshape(s))
        off += n
    return out


def _in_layout(D, GW, GH, FW, FH, MW, tn):
    o_z = 3 * GW
    o_beta = 4 * GW
    o_fq = o_beta + 2 * GH
    o_ff = o_fq + 3 * FW
    o_mq = o_ff + FH
    o_g = o_mq + MW
    orig = {"q": (0, GW), "k": (GW, GW), "v": (2 * GW, GW), "z": (o_z, GW), "beta": (o_beta, GH), "dec": (o_beta + GH, GH),
            "fq": (o_fq, FW), "fk": (o_fq + FW, FW), "fv": (o_fq + 2 * FW, FW), "ff": (o_ff, FH), "mq": (o_mq, MW),
            "ga": (o_g, D), "gb": (o_g + D, D), "gm": (o_g + 2 * D, D)}
    offs, cur = {}, 0
    for key, width in (("ga", D), ("gb", D), ("gm", D), ("q", GW), ("k", GW), ("v", GW), ("z", GW),
                       ("fq", FW), ("fk", FW), ("fv", FW), ("mq", MW), ("small", LANES)):
        cur = _ru(cur, width)
        offs[key] = cur
        cur += width
    total = _ru(cur, tn)
    pieces = [(offs[k], orig[k][0], orig[k][1]) for k in ("ga", "gb", "gm", "q", "k", "v", "z", "fq", "fk", "fv", "mq")]
    pieces += [(offs["small"], orig["beta"][0], GH), (offs["small"] + GH, orig["dec"][0], GH),
               (offs["small"] + 2 * GH, orig["ff"][0], FH)]
    return offs, total, pieces, o_g + 3 * D


def _pad_cols(w, pieces, total):
    parts, cur = [], 0
    for pstart, ostart, n in pieces:
        if pstart > cur:
            parts.append(jnp.zeros((w.shape[0], pstart - cur), w.dtype))
        parts.append(w[:, ostart:ostart + n])
        cur = pstart + n
    if total > cur:
        parts.append(jnp.zeros((w.shape[0], total - cur), w.dtype))
    return jnp.concatenate(parts, axis=1)


def _unpad_cols(wp, pieces):
    return jnp.concatenate([wp[:, pstart:pstart + n] for pstart, ostart, n in sorted(pieces, key=lambda t: t[1])], axis=1)


def _local_step(x, mem, tgt, W, flash_blk=512):
    T, D = x.shape
    GW = W["w_up_gdn"].shape[0]
    FW = W["w_up_fox"].shape[0]
    MW = W["w_up_mem"].shape[0]
    GH, FH = GW // HEAD, FW // HEAD
    offs, NP, pieces, d_in = _in_layout(D, GW, GH, FW, FH, MW, 1024)
    assert W["w_in"].shape[1] == d_in
    w_in_p = _pad_cols(W["w_in"], pieces, NP)
    cw = W["conv_w"]
    cws = [cw[:, i * GW:(i + 1) * GW] for i in range(3)]
    zl = jnp.zeros((1, LANES), F32)
    pvecs = [lax.dynamic_update_slice(zl, W["a_log"], (0, GH)), lax.dynamic_update_slice(zl, W["dt_bias"], (0, GH)),
             lax.dynamic_update_slice(zl, W["fox_b_f"], (0, 2 * GH))]
    lane0 = 2 * GH

    h = _rms_fwd(x, W["g_mix"], "rms_mix")
    proj = _mm(h, w_in_p, "nn", "in_proj")
    qn, kn, vc, cum, gbm = _gdn_prep(proj, offs, cws, pvecs, GH, FH, GW)
    o_gdn, sall = _gdn_scan_fwd(qn, kn, vc, gbm, GH)
    o_a = _gdn_post(o_gdn, proj, offs["z"], W["gdn_norm_g"], GW)
    qb, kb, vb = _fox_prep(proj, offs, W["fox_q_norm"], W["fox_k_norm"], FW)
    cumh = cum[:, lane0:lane0 + FH].T
    cumc, cumr = cumh.reshape(FH, T, 1), cumh.reshape(FH, 1, T)
    o_b16, lse = _flash_fwd(qb, kb, vb, cumr, FH, flash_blk)
    memn = _rms_fwd(mem, W["g_mem"], "rms_mem")
    kv = _mm(memn, W["w_mem_kv"], "nn", "mem_kv")
    o_m = _mem_attn(proj, offs["mq"], kv, W["mem_q_norm"], W["mem_k_norm"], MW)
    ua = _mm(o_a, W["w_up_gdn"], "nn", "up_gdn")
    ub = _mm(o_b16, W["w_up_fox"], "nn", "up_fox")
    um = _mm(o_m, W["w_up_mem"], "nn", "up_mem")
    y = _merge(proj, offs, ua, ub, um, D)
    x1 = _mm(y, W["w_out"], "nn", "out_proj", epilogue=lambda acc, r: (acc + r,), extras=(x,))
    h2 = _rms_fwd(x1, W["g_mlp"], "rms_mlp")
    u, a = _mm(h2, W["w_ff1"], "nn", "ff1", out_dtypes=(F32, BF16),
               epilogue=lambda acc: (acc, jnp.square(jnp.maximum(acc, 0.0))))
    x2 = _mm(a, W["w_ff2"], "nn", "ff2", epilogue=lambda acc, r: (acc + r,), extras=(x1,))
    dx2, dx2b, lpart = _loss_grad(x2, tgt)
    loss = lpart[0, 0]

    G = {}
    du = _mm(dx2b, W["w_ff2"], "nt", "ff2_dx", out_dtypes=(BF16,),
             epilogue=lambda acc, uu: (acc * (2.0 * jnp.maximum(uu, 0.0)),), extras=(u,))
    G["w_ff2"] = _mm(a, dx2b, "tn", "ff2_dw", out_dtypes=(BF16,))
    dh2 = _mm(du, W["w_ff1"], "nt", "ff1_dx")
    G["w_ff1"] = _mm(h2, du, "tn", "ff1_dw", out_dtypes=(BF16,))
    dx1, dx1b, G["g_mlp"] = _rms_bwd(x1, W["g_mlp"], dh2, dx2, "rms_mlp_bwd")
    dy = _mm(dx1b, W["w_out"], "nt", "out_dx")
    G["w_out"] = _mm(y, dx1b, "tn", "out_dw", out_dtypes=(BF16,))
    dga, dgb, dgm, dua, dub, dum = _merge_bwd(proj, offs, ua, ub, um, dy, D)
    do_a = _mm(dua, W["w_up_gdn"], "nt", "up_gdn_dx")
    G["w_up_gdn"] = _mm(o_a, dua, "tn", "up_gdn_dw", out_dtypes=(BF16,))
    do_b = _mm(dub, W["w_up_fox"], "nt", "up_fox_dx")
    G["w_up_fox"] = _mm(o_b16, dub, "tn", "up_fox_dw", out_dtypes=(BF16,))
    do_m = _mm(dum, W["w_up_mem"], "nt", "up_mem_dx")
    G["w_up_mem"] = _mm(o_m, dum, "tn", "up_mem_dw", out_dtypes=(BF16,))
    dmq, dkv, G["mem_q_norm"], G["mem_k_norm"] = _mem_attn_bwd(proj, offs["mq"], kv, W["mem_q_norm"], W["mem_k_norm"], do_m, MW)
    dkvb = dkv.astype(BF16)
    dmemn = _mm(dkvb, W["w_mem_kv"], "nt", "mem_kv_dx")
    G["w_mem_kv"] = _mm(memn, dkvb, "tn", "mem_kv_dw", out_dtypes=(BF16,))
    G["g_mem"] = _rms_dg(mem, W["g_mem"], dmemn, "rms_mem_bwd")
    dl = _flash_bwd_q(qb, kb, vb, cumr, lse, do_b, None, FH, flash_blk)
    dqb, dcq = _flash_bwd_q(qb, kb, vb, cumr, lse, do_b, dl, FH, flash_blk)
    dkb, dvb, dck = _flash_bwd_dkv(qb, kb, vb, cumc, lse.reshape(FH, 1, T), dl.reshape(FH, 1, T), do_b, FH, flash_blk)
    dlf = _rev_cumsum_rows(dcq.reshape(FH, T), dck.reshape(FH, T))
    dlf_sm = jnp.pad(dlf.T, ((0, 0), (lane0, LANES - lane0 - FH)))
    dfq, dfk, dfv, G["fox_q_norm"], G["fox_k_norm"] = _fox_prep_bwd(proj, offs, W["fox_q_norm"], W["fox_k_norm"], dqb, dkb, dvb, FW)
    do_gdn, dz, G["gdn_norm_g"] = _gdn_post_bwd(o_gdn, proj, offs["z"], W["gdn_norm_g"], do_a, GW)
    dqn, dkn, dvc, dgsm = _gdn_scan_bwd(qn, kn, vc, gbm, sall, do_gdn, GH)
    dyq, dyk, dyv, dcq, dck_w, dcv, dsmall, dalog, ddtb, dbf = _gdn_prep_bwd_a(
        proj, offs, cws, pvecs, (dqn, dkn, dvc), dgsm, dlf_sm, GH, FH, GW)
    dxq, dxk, dxv = _gdn_prep_bwd_b((dyq, dyk, dyv), cws, GW)
    G["conv_w"] = jnp.concatenate([dcq[:CONV_K], dck_w[:CONV_K], dcv[:CONV_K]], axis=1)
    G["a_log"] = dalog[:, GH:2 * GH]
    G["dt_bias"] = ddtb[:, GH:2 * GH]
    G["fox_b_f"] = dbf[:, lane0:lane0 + FH]
    segs = {"ga": dga, "gb": dgb, "gm": dgm, "q": dxq, "k": dxk, "v": dxv, "z": dz, "fq": dfq, "fk": dfk, "fv": dfv,
            "mq": dmq, "small": dsmall}
    parts, cur = [], 0
    for key in ("ga", "gb", "gm", "q", "k", "v", "z", "fq", "fk", "fv", "mq", "small"):
        if offs[key] > cur:
            parts.append(jnp.zeros((T, offs[key] - cur), BF16))
        parts.append(segs[key])
        cur = offs[key] + segs[key].shape[1]
    if NP > cur:
        parts.append(jnp.zeros((T, NP - cur), BF16))
    dproj = jnp.concatenate(parts, axis=1)
    dh = _mm(dproj, w_in_p, "nt", "in_dx")
    G["w_in"] = _unpad_cols(_mm(h, dproj, "tn", "in_dw", out_dtypes=(BF16,)), pieces)
    grad_x, _, G["g_mix"] = _rms_bwd(x, W["g_mix"], dh, dx1, "rms_mix_bwd")
    return loss, grad_x, G


BIG = ["w_in", "w_mem_kv", "w_up_gdn", "w_up_fox", "w_up_mem", "w_out", "w_ff1", "w_ff2"]
SMALL = ["g_mix", "a_log", "dt_bias", "gdn_norm_g", "fox_b_f", "fox_q_norm", "fox_k_norm", "g_mem", "mem_q_norm",
         "mem_k_norm", "g_mlp"]
ORDER = ["g_mix", "w_in", "conv_w", "a_log", "dt_bias", "gdn_norm_g", "fox_b_f", "fox_q_norm", "fox_k_norm", "g_mem",
         "w_mem_kv", "mem_q_norm", "mem_k_norm", "w_up_gdn", "w_up_fox", "w_up_mem", "w_out", "g_mlp", "w_ff1", "w_ff2"]
SHARD_AXIS = {"w_in": 1, "w_mem_kv": 0, "w_up_gdn": 1, "w_up_fox": 1, "w_up_mem": 1, "w_out": 0, "w_ff1": 1, "w_ff2": 0}


def _step(x, mem, tgt, w, m, v, flash_blk=512):
    xi, yi, ci = _coords()
    chip = 2 * xi + yi

    shard_shapes = [w[n].shape for n in BIG]
    packed_w = _pack([w[n] for n in BIG], BF16, PACK_ROWS)
    gathered = _allgather_chips(packed_w)
    gathered = lax.dynamic_update_slice(gathered, packed_w[None], (chip, 0, 0))
    per_chip = [_unpack(gathered[j], shard_shapes) for j in range(4)]
    W = {n: jnp.concatenate([per_chip[j][i] for j in range(4)], axis=SHARD_AXIS[n]) for i, n in enumerate(BIG)}
    cw_rows = jnp.pad(w["conv_w"], ((0, SUB - CONV_K), (0, 0)))
    cw_all = _allgather_small(cw_rows)
    W["conv_w"] = jnp.concatenate([cw_all[16 * j:16 * j + CONV_K] for j in range(4)], axis=1)
    for n in SMALL:
        W[n] = w[n]

    loss, grad_x, G = _local_step(x, mem, tgt, W, flash_blk)
    loss = lax.psum(loss, ("x", "y", "c"))

    small_shapes = [G[n].shape for n in SMALL] + [G["conv_w"].shape]
    sm = _pack_flat([G[n] for n in SMALL] + [G["conv_w"]], F32, SUB)
    sm_sum = _sum8(_allgather_small(sm), sm.shape[0])
    sm_list = _unpack_flat(sm_sum, small_shapes)
    g = {n: sm_list[i] for i, n in enumerate(SMALL)}
    cw_full = sm_list[-1]
    gw4 = cw_full.shape[1] // 4
    g["conv_w"] = lax.dynamic_slice(cw_full, (0, chip * gw4), (CONV_K, gw4))

    by_dest = []
    for j in range(4):
        shards = []
        for n in BIG:
            size = w[n].shape[SHARD_AXIS[n]]
            shards.append(lax.slice_in_dim(G[n], j * size, (j + 1) * size, axis=SHARD_AXIS[n]))
        by_dest.append(_pack(shards, BF16, PACK_ROWS))
    gflat = jnp.stack(by_dest)
    rb1 = _rs_pair_exchange(gflat)
    part = _rs_pair_add(gflat, rb1, jnp.reshape(ci, (1,)).astype(jnp.int32))
    rb2 = _rs_chip_exchange(part)
    half_sum = _sum4(part, rb2, jnp.reshape(chip, (1,)).astype(jnp.int32))
    mine = _pair_allgather(half_sum)
    mine = lax.dynamic_update_slice(mine, half_sum, (ci * half_sum.shape[0], 0))
    for i, gv in enumerate(_unpack(mine, shard_shapes)):
        g[BIG[i]] = gv

    delta, new_m, new_v = {}, {}, {}
    for n in BIG:
        delta[n], new_m[n], new_v[n] = _adamw(w[n], g[n], m[n], v[n], "adamw_" + n)
    rest = SMALL + ["conv_w"]
    rest_shapes = [w[n].shape for n in rest]
    packed = [_pack_flat([d[n] for n in rest], F32, SUB) for d in (w, g, m, v)]
    outs = _adamw(*packed, "adamw_small", tb=packed[0].shape[0])
    for d, buf in zip((delta, new_m, new_v), outs):
        for n, val in zip(rest, _unpack_flat(buf, rest_shapes)):
            d[n] = val
    return loss, grad_x, g, delta, new_m, new_v


def kernel(x, mem, g_mix, w_in, conv_w, a_log, dt_bias, gdn_norm_g, fox_b_f, fox_q_norm, fox_k_norm, g_mem, w_mem_kv, mem_q_norm, mem_k_norm, w_up_gdn, w_up_fox, w_up_mem, w_out, g_mlp, w_ff1, w_ff2, loss_target, m_g_mix, m_w_in, m_conv_w, m_a_log, m_dt_bias, m_gdn_norm_g, m_fox_b_f, m_fox_q_norm, m_fox_k_norm, m_g_mem, m_w_mem_kv, m_mem_q_norm, m_mem_k_norm, m_w_up_gdn, m_w_up_fox, m_w_up_mem, m_w_out, m_g_mlp, m_w_ff1, m_w_ff2, v_g_mix, v_w_in, v_conv_w, v_a_log, v_dt_bias, v_gdn_norm_g, v_fox_b_f, v_fox_q_norm, v_fox_k_norm, v_g_mem, v_w_mem_kv, v_mem_q_norm, v_mem_k_norm, v_w_up_gdn, v_w_up_fox, v_w_up_mem, v_w_out, v_g_mlp, v_w_ff1, v_w_ff2):
    ws = (g_mix, w_in, conv_w, a_log, dt_bias, gdn_norm_g, fox_b_f, fox_q_norm, fox_k_norm, g_mem, w_mem_kv, mem_q_norm,
          mem_k_norm, w_up_gdn, w_up_fox, w_up_mem, w_out, g_mlp, w_ff1, w_ff2)
    ms = (m_g_mix, m_w_in, m_conv_w, m_a_log, m_dt_bias, m_gdn_norm_g, m_fox_b_f, m_fox_q_norm, m_fox_k_norm, m_g_mem,
          m_w_mem_kv, m_mem_q_norm, m_mem_k_norm, m_w_up_gdn, m_w_up_fox, m_w_up_mem, m_w_out, m_g_mlp, m_w_ff1, m_w_ff2)
    vs = (v_g_mix, v_w_in, v_conv_w, v_a_log, v_dt_bias, v_gdn_norm_g, v_fox_b_f, v_fox_q_norm, v_fox_k_norm, v_g_mem,
          v_w_mem_kv, v_mem_q_norm, v_mem_k_norm, v_w_up_gdn, v_w_up_fox, v_w_up_mem, v_w_out, v_g_mlp, v_w_ff1, v_w_ff2)
    drop = lambda a: a[0] if a.ndim == 3 else a
    w = {n: drop(a) for n, a in zip(ORDER, ws)}
    m = {n: drop(a) for n, a in zip(ORDER, ms)}
    v = {n: drop(a) for n, a in zip(ORDER, vs)}
    loss, grad_x, g, delta, new_m, new_v = _step(x[0], mem[0], loss_target[0], w, m, v)
    out = [loss, grad_x[None]]
    for d in (g, delta, new_m, new_v):
        out += [d[n].reshape(a.shape) for n, a in zip(ORDER, ws)]
    return tuple(out)
```

```python
import numpy as np

import jax
import jax.numpy as jnp
from jax import lax
from jax.experimental import pallas as pl
from jax.experimental.pallas import tpu as pltpu

F32 = jnp.float32
BF16 = jnp.bfloat16
HI = lax.Precision.HIGHEST
MESH = pl.DeviceIdType.MESH

EPS = 1e-6
HEAD = 128
MEM_DH = 256
CONV_K = 4
CHUNK = 64
CHUNK_SHIFT = 6
LANES = 128
SUB = 8
PACK_W = 1024
PACK_ROWS = 512
VMEM_LIMIT = 56 * 1024 * 1024
NEG = -1e30
SOLVE_PREC = None
FLASH_HEADS_PER_STEP = 8
FLASH_FWD_HEADS_PER_STEP = 4

ADAM_LR, ADAM_B1, ADAM_B2, ADAM_EPS, ADAM_WD, ADAM_STEP = 0.001, 0.9, 0.999, 1e-08, 0.01, 10


def _pcall(body, **kw):
    return pl.pallas_call(body, **kw)


def _cp(*sem):
    return pltpu.CompilerParams(dimension_semantics=sem, vmem_limit_bytes=VMEM_LIMIT)


def _dot(a, b, prec=None):
    return lax.dot_general(a, b, (((1,), (0,)), ((), ())), precision=prec, preferred_element_type=F32)


def _dot_nt(a, b, prec=None):
    return lax.dot_general(a, b, (((1,), (1,)), ((), ())), precision=prec, preferred_element_type=F32)


def _dot_tn(a, b, prec=None):
    return lax.dot_general(a, b, (((0,), (0,)), ((), ())), precision=prec, preferred_element_type=F32)


def _sigmoid(x):
    return 1.0 / (1.0 + jnp.exp(-x))


def _softplus(x):
    return jnp.maximum(x, 0.0) + jnp.log(1.0 + jnp.exp(-jnp.abs(x)))


def _silu(x):
    return x * _sigmoid(x)


def _rms(x, g):
    return x * lax.rsqrt(jnp.mean(x * x, axis=-1, keepdims=True) + EPS) * g


def _ru(a, m):
    return (a + m - 1) // m * m


def _mm(a, b, mode, name, out_dtypes=(F32,), epilogue=None, extras=(), tm=1024, tn=1024, tk=2048):
    if mode == "nn":
        (M, K), (K2, N) = a.shape, b.shape
    elif mode == "nt":
        (M, K), (N, K2) = a.shape, b.shape
    else:
        (K, M), (K2, N) = a.shape, b.shape
    assert K == K2, (a.shape, b.shape, mode)
    tm, tn = min(tm, M), min(tn, N)
    tk = next((t for t in (tk, 3072, 1024, 512, 256, LANES) if t <= K and K % t == 0), K)
    assert M % tm == 0 and N % tn == 0 and K % tk == 0, (M, N, K, tm, tn, tk)
    nk = K // tk
    n_ex, n_out = len(extras), len(out_dtypes)
    dims = {"nn": ((1,), (0,)), "nt": ((1,), (1,)), "tn": ((0,), (0,))}[mode]

    def finish(res, ex_refs, out_refs):
        outs = epilogue(res, *[r[...] for r in ex_refs]) if epilogue is not None else (res,)
        for o_ref, o in zip(out_refs, outs):
            o_ref[...] = o.astype(o_ref.dtype)

    def body(a_ref, b_ref, *rest):
        ex_refs, out_refs = rest[:n_ex], rest[n_ex:n_ex + n_out]
        part = lax.dot_general(a_ref[...], b_ref[...], (dims, ((), ())), preferred_element_type=F32)
        if nk == 1:
            finish(part, ex_refs, out_refs)
            return
        acc = rest[-1]
        k = pl.program_id(2)

        @pl.when(k == 0)
        def _():
            acc[...] = part

        @pl.when(k > 0)
        def _():
            acc[...] += part

        @pl.when(k == nk - 1)
        def _():
            finish(acc[...], ex_refs, out_refs)

    a_spec = pl.BlockSpec((tk, tm), lambda i, j, k: (k, i)) if mode == "tn" else pl.BlockSpec((tm, tk), lambda i, j, k: (i, k))
    b_spec = pl.BlockSpec((tn, tk), lambda i, j, k: (j, k)) if mode == "nt" else pl.BlockSpec((tk, tn), lambda i, j, k: (k, j))
    mn_spec = pl.BlockSpec((tm, tn), lambda i, j, k: (i, j))
    outs = _pcall(
        body, name=name, grid=(M // tm, N // tn, nk),
        in_specs=[a_spec, b_spec] + [mn_spec] * n_ex,
        out_specs=[mn_spec] * n_out,
        out_shape=[jax.ShapeDtypeStruct((M, N), dt) for dt in out_dtypes],
        scratch_shapes=[pltpu.VMEM((tm, tn), F32)] if nk > 1 else [],
        compiler_params=_cp("parallel", "parallel", "arbitrary"),
    )(a, b, *extras)
    return outs[0] if n_out == 1 else outs


def _rowwise(fn, T, tb, ins, outs, name, scratch=()):
    tb = min(tb, T)
    assert T % tb == 0 and (tb % SUB == 0 or tb == T)
    nblk = T // tb
    r8 = tb // SUB
    in_specs, arrs = [], []
    for spec in ins:
        kind, arr = spec[0], spec[1]
        arrs.append(arr)
        if kind == "full":
            nd = arr.ndim
            in_specs.append(pl.BlockSpec(arr.shape, lambda i, nd=nd: (0,) * nd))
            continue
        off, w = spec[2], spec[3]
        assert off % w == 0 and arr.shape[0] == T, (name, off, w, arr.shape)
        cb = off // w
        if kind == "row":
            in_specs.append(pl.BlockSpec((tb, w), lambda i, cb=cb: (i, cb)))
        elif kind == "prev":
            in_specs.append(pl.BlockSpec((SUB, w), lambda i, cb=cb: (jnp.maximum(i * r8 - 1, 0), cb)))
        else:
            in_specs.append(pl.BlockSpec((SUB, w), lambda i, cb=cb: (jnp.minimum((i + 1) * r8, T // SUB - 1), cb)))
    out_specs, out_shapes, is_acc = [], [], []
    for spec in outs:
        if spec[0] == "row":
            out_specs.append(pl.BlockSpec((tb, spec[1]), lambda i: (i, 0)))
            out_shapes.append(jax.ShapeDtypeStruct((T, spec[1]), spec[2]))
            is_acc.append(False)
        else:
            nd = len(spec[1])
            out_specs.append(pl.BlockSpec(spec[1], lambda i, nd=nd: (0,) * nd))
            out_shapes.append(jax.ShapeDtypeStruct(spec[1], spec[2]))
            is_acc.append(True)
    n_in, n_out = len(ins), len(outs)
    seq = any(is_acc) or len(scratch) > 0

    def body(*refs):
        in_refs, out_refs, scr = refs[:n_in], refs[n_in:n_in + n_out], refs[n_in + n_out:]
        i = pl.program_id(0)
        vals = fn(i, nblk, *[r[...] for r in in_refs], *scr)
        for o_ref, v, acc in zip(out_refs, vals, is_acc):
            if acc:
                @pl.when(i == 0)
                def _(o_ref=o_ref):
                    o_ref[...] = jnp.zeros_like(o_ref)

                o_ref[...] += v.astype(o_ref.dtype)
            else:
                o_ref[...] = v.astype(o_ref.dtype)

    res = _pcall(
        body, name=name, grid=(nblk,), in_specs=in_specs, out_specs=out_specs, out_shape=out_shapes,
        scratch_shapes=list(scratch), compiler_params=_cp("arbitrary" if seq else "parallel"),
    )(*arrs)
    return res


def _heads(x, width):
    return [x[:, h * width:(h + 1) * width] for h in range(x.shape[1] // width)]


def _cat(xs):
    return xs[0] if len(xs) == 1 else jnp.concatenate(xs, axis=1)


def _rms_fwd(x, g, name, tb=512):
    T, D = x.shape
    return _rowwise(lambda i, n, xv, gv: (_rms(xv, gv),), T, tb,
                    [("row", x, 0, D), ("full", g)], [("row", D, BF16)], name)[0]


def _rms_bwd(x, g, dh, dres, name, tb=256):
    T, D = x.shape

    def fn(i, n, xv, gv, dhv, drv):
        _, vjp = jax.vjp(_rms, xv, gv)
        dx, dg = vjp(dhv)
        tot = drv + dx
        return tot, tot, dg

    return _rowwise(fn, T, tb, [("row", x, 0, D), ("full", g), ("row", dh, 0, D), ("row", dres, 0, D)],
                    [("row", D, F32), ("row", D, BF16), ("acc", (1, D), F32)], name)


def _rms_dg(x, g, dh, name, tb=256):
    T, D = x.shape

    def fn(i, n, xv, gv, dhv):
        _, vjp = jax.vjp(lambda gg: _rms(xv, gg), gv)
        return vjp(dhv)

    return _rowwise(fn, T, tb, [("row", x, 0, D), ("full", g), ("row", dh, 0, D)], [("acc", (1, D), F32)], name)[0]


def _shift_down(x, halo, s, first):
    if s == 0:
        return x
    tb, c = x.shape
    xr = pltpu.roll(x, s, 0)
    hr = jnp.where(first, 0.0, pltpu.roll(halo, s, 0))
    hfull = hr if tb == SUB else jnp.concatenate([hr, jnp.zeros((tb - SUB, c), x.dtype)], axis=0)
    row = lax.broadcasted_iota(jnp.int32, x.shape, 0)
    return jnp.where(row < s, hfull, xr)


def _shift_up(z, halo, s, last):
    if s == 0:
        return z
    tb, c = z.shape
    zr = pltpu.roll(z, tb - s, 0)
    hr = jnp.where(last, 0.0, pltpu.roll(halo, SUB - s, 0))
    hfull = hr if tb == SUB else jnp.concatenate([jnp.zeros((tb - SUB, c), z.dtype), hr], axis=0)
    row = lax.broadcasted_iota(jnp.int32, z.shape, 0)
    return jnp.where(row >= tb - s, hfull, zr)


def _conv_pre(x, halo, cw, first):
    xs = [_shift_down(x, halo, s, first) for s in range(CONV_K)]
    y = cw[0:1, :] * xs[3]
    for i in range(1, CONV_K):
        y = y + cw[i:i + 1, :] * xs[CONV_K - 1 - i]
    return y, xs


def _qk_post(y, scale):
    a = _silu(y)
    return a * lax.rsqrt(jnp.sum(a * a, axis=-1, keepdims=True) + EPS) * scale


def _small_fn(s, alog, dtb, bf, gh, fh):
    lane = lax.broadcasted_iota(jnp.int32, s.shape, 1)
    beta = _sigmoid(s)
    g = -jnp.exp(alog) * _softplus(s + dtb)
    lf = -_softplus(-(s + bf))
    return jnp.where(lane < gh, beta, jnp.where(lane < 2 * gh, g, jnp.where(lane < 2 * gh + fh, lf, 0.0)))


def _gdn_prep(proj, offs, cws, pvecs, GH, FH, GW, tb=256):
    T = proj.shape[0]
    tb = min(tb, T)
    qscale = HEAD ** -0.5

    def fn(i, n, xq, hq, xk, hk, xv, hv, cwq, cwk, cwv, s, alog, dtb, bf, carry):
        first = i == 0
        yq, _ = _conv_pre(xq, hq, cwq, first)
        yk, _ = _conv_pre(xk, hk, cwk, first)
        yv, _ = _conv_pre(xv, hv, cwv, first)
        qn = _cat([_qk_post(y, qscale) for y in _heads(yq, HEAD)])
        kn = _cat([_qk_post(y, 1.0) for y in _heads(yk, HEAD)])
        vc = _silu(yv)
        gsm = _small_fn(s, alog, dtb, bf, GH, FH)

        @pl.when(first)
        def _():
            carry[...] = jnp.zeros_like(carry)

        ri = lax.broadcasted_iota(jnp.int32, (tb, tb), 0)
        ci = lax.broadcasted_iota(jnp.int32, (tb, tb), 1)
        cum = _dot((ri >= ci).astype(F32), gsm, HI) + carry[0:1, :]
        carry[...] += _dot(jnp.ones((SUB, tb), F32), gsm, HI)
        in_chunk = (ri >= ci) & ((ri >> CHUNK_SHIFT) == (ci >> CHUNK_SHIFT))
        lane = lax.broadcasted_iota(jnp.int32, gsm.shape, 1)
        gbm = jnp.where(lane < GH, gsm, _dot(in_chunk.astype(F32), gsm, HI))
        return qn, kn, vc, cum, gbm

    ins = []
    for key in ("q", "k", "v"):
        ins += [("row", proj, offs[key], GW), ("prev", proj, offs[key], GW)]
    ins += [("full", c) for c in cws] + [("row", proj, offs["small"], LANES)] + [("full", p) for p in pvecs]
    outs = [("row", GW, F32)] * 3 + [("row", LANES, F32)] * 2
    return _rowwise(fn, T, tb, ins, outs, "gdn_prep", scratch=[pltpu.VMEM((SUB, LANES), F32)])


def _gdn_prep_bwd_a(proj, offs, cws, pvecs, cts, dgsm_scan, dlf_sm, GH, FH, GW, tb=256):
    T = proj.shape[0]
    qscale = HEAD ** -0.5

    def one(x, halo, cw, ct, first, post):
        y, xs = _conv_pre(x, halo, cw, first)
        if post is None:
            _, vjp = jax.vjp(_silu, y)
            dy = vjp(ct)[0]
        else:
            dys = []
            for yh, cth in zip(_heads(y, HEAD), _heads(ct, HEAD)):
                _, vjp = jax.vjp(lambda t: _qk_post(t, post), yh)
                dys.append(vjp(cth)[0])
            dy = _cat(dys)
        row = lax.broadcasted_iota(jnp.int32, (SUB, x.shape[1]), 0)
        dcw = jnp.zeros((SUB, x.shape[1]), F32)
        for i in range(CONV_K):
            dcw = dcw + jnp.where(row == i, jnp.sum(dy * xs[CONV_K - 1 - i], axis=0, keepdims=True), 0.0)
        return dy, dcw

    def fn(i, n, xq, hq, xk, hk, xv, hv, cwq, cwk, cwv, cq, ck, cv, s, alog, dtb, bf, d1, d2):
        first = i == 0
        dyq, dcq = one(xq, hq, cwq, cq, first, qscale)
        dyk, dck = one(xk, hk, cwk, ck, first, 1.0)
        dyv, dcv = one(xv, hv, cwv, cv, first, None)
        tb_ = d1.shape[0]
        ri = lax.broadcasted_iota(jnp.int32, (tb_, tb_), 0)
        ci = lax.broadcasted_iota(jnp.int32, (tb_, tb_), 1)
        later = (ci >= ri) & ((ri >> CHUNK_SHIFT) == (ci >> CHUNK_SHIFT))
        lane = lax.broadcasted_iota(jnp.int32, d1.shape, 1)
        d1 = jnp.where(lane < GH, d1, _dot(later.astype(F32), d1, HI))
        _, vjp = jax.vjp(lambda a, b, c, d: _small_fn(a, b, c, d, GH, FH), s, alog, dtb, bf)
        ds, dalog, ddtb, dbf = vjp(d1 + d2)
        return dyq, dyk, dyv, dcq, dck, dcv, ds, dalog, ddtb, dbf

    ins = []
    for key in ("q", "k", "v"):
        ins += [("row", proj, offs[key], GW), ("prev", proj, offs[key], GW)]
    ins += [("full", c) for c in cws] + [("row", c, 0, GW) for c in cts]
    ins += [("row", proj, offs["small"], LANES)] + [("full", p) for p in pvecs]
    ins += [("row", dgsm_scan, 0, LANES), ("row", dlf_sm, 0, LANES)]
    outs = [("row", GW, F32)] * 3 + [("acc", (SUB, GW), F32)] * 3 + [("row", LANES, BF16)] + [("acc", (1, LANES), F32)] * 3
    return _rowwise(fn, T, tb, ins, outs, "gdn_prep_bwd_a")


def _gdn_prep_bwd_b(dys, cws, GW, tb=256):
    T = dys[0].shape[0]

    def fn(i, n, dq, nq, dk, nk, dv, nv, cwq, cwk, cwv):
        last = i == n - 1
        res = []
        for dy, nh, cw in ((dq, nq, cwq), (dk, nk, cwk), (dv, nv, cwv)):
            dx = cw[CONV_K - 1:CONV_K, :] * dy
            for t in range(CONV_K - 1):
                dx = dx + cw[t:t + 1, :] * _shift_up(dy, nh, CONV_K - 1 - t, last)
            res.append(dx)
        return tuple(res)

    ins = []
    for dy in dys:
        ins += [("row", dy, 0, GW), ("next", dy, 0, GW)]
    ins += [("full", c) for c in cws]
    return _rowwise(fn, T, tb, ins, [("row", GW, BF16)] * 3, "gdn_prep_bwd_b")


def _gdn_chunks(qs, ks, vs, gams, bcols, s0s):
    c, d = qs[0].shape
    hs = range(len(qs))
    ri = lax.broadcasted_iota(jnp.int32, (c, c), 0)
    ci = lax.broadcasted_iota(jnp.int32, (c, c), 1)
    incl, strict = ri >= ci, ri > ci
    eye = (ri == ci).astype(F32)
    ones_cc = jnp.ones((c, c), F32)
    rows = lax.broadcasted_iota(jnp.int32, (c, 1), 0)
    b16 = (ri >> 4) == (ci >> 4)
    b32 = (ri >> 5) == (ci >> 5)
    gam_cc = [gams[h] * ones_cc for h in hs]
    gam_t = [_dot_nt(eye, gam_cc[h], HI) for h in hs]
    glast = [jnp.sum(jnp.where(rows == c - 1, gams[h], 0.0), axis=0, keepdims=True) for h in hs]
    dec_i = [jnp.where(incl, jnp.exp(jnp.where(incl, gam_cc[h] - gam_t[h], 0.0)), 0.0) for h in hs]
    kk = [_dot_nt(ks[h], ks[h]) for h in hs]
    m = [bcols[h] * kk[h] * jnp.where(strict, dec_i[h], 0.0) for h in hs]
    m32 = [jnp.where(b32 & ~b16, m[h], 0.0) for h in hs]
    m64 = [jnp.where(b32, 0.0, m[h]) for h in hs]
    mp = [jnp.where(b16, m[h], 0.0) for h in hs]
    p = [eye - mp[h] for h in hs]
    for _ in range(3):
        mp = [_dot(mp[h], mp[h], SOLVE_PREC) for h in hs]
        p = [p[h] + _dot(p[h], mp[h], SOLVE_PREC) for h in hs]
    t = [_dot(p[h], m32[h], SOLVE_PREC) for h in hs]
    p = [p[h] - _dot(t[h], p[h], SOLVE_PREC) for h in hs]
    t = [_dot(p[h], m64[h], SOLVE_PREC) for h in hs]
    ainv = [p[h] - _dot(t[h], p[h], SOLVE_PREC) for h in hs]
    eg = [jnp.exp(gams[h]) for h in hs]
    w = [_dot(ainv[h], (bcols[h] * eg[h]) * ks[h], SOLVE_PREC) for h in hs]
    u0 = [_dot(ainv[h], bcols[h] * vs[h], SOLVE_PREC) for h in hs]
    qk = [_dot_nt(qs[h], ks[h]) * dec_i[h] for h in hs]
    u = [u0[h] - _dot(w[h], s0s[h]) for h in hs]
    o = [_dot(qs[h] * eg[h], s0s[h]) + _dot(qk[h], u[h]) for h in hs]
    s1 = [jnp.exp(glast[h]) * s0s[h] + _dot_tn(ks[h] * jnp.exp(glast[h] - gams[h]), u[h]) for h in hs]
    return tuple(o), tuple(s1)


def _lane_col(x, lane_idx):
    lane = lax.broadcasted_iota(jnp.int32, x.shape, 1)
    return jnp.sum(jnp.where(lane == lane_idx, x, 0.0), axis=1, keepdims=True)


def _gdn_scan_fwd(qn, kn, vc, gsm, GH):
    T, GW = qn.shape
    nc = T // CHUNK

    def body(q_ref, k_ref, v_ref, g_ref, o_ref, sall_ref, s_scr):
        @pl.when(pl.program_id(0) == 0)
        def _():
            s_scr[...] = jnp.zeros_like(s_scr)

        gs = g_ref[...]
        sls = [slice(h * HEAD, (h + 1) * HEAD) for h in range(GH)]
        s0s = tuple(s_scr[h] for h in range(GH))
        os_, s1s = _gdn_chunks(tuple(q_ref[:, sl] for sl in sls), tuple(k_ref[:, sl] for sl in sls),
                               tuple(v_ref[:, sl] for sl in sls), tuple(_lane_col(gs, GH + h) for h in range(GH)),
                               tuple(_lane_col(gs, h) for h in range(GH)), s0s)
        for h in range(GH):
            sall_ref[0, h] = s0s[h]
            o_ref[:, sls[h]] = os_[h]
            s_scr[h] = s1s[h]

    row = pl.BlockSpec((CHUNK, GW), lambda i: (i, 0))
    return _pcall(
        body, name="gdn_scan_fwd", grid=(nc,),
        in_specs=[row, row, row, pl.BlockSpec((CHUNK, LANES), lambda i: (i, 0))],
        out_specs=[row, pl.BlockSpec((1, GH, HEAD, HEAD), lambda i: (i, 0, 0, 0))],
        out_shape=[jax.ShapeDtypeStruct((T, GW), F32), jax.ShapeDtypeStruct((nc, GH, HEAD, HEAD), F32)],
        scratch_shapes=[pltpu.VMEM((GH, HEAD, HEAD), F32)],
        compiler_params=_cp("arbitrary"),
    )(qn, kn, vc, gsm)


def _gdn_scan_bwd(qn, kn, vc, gsm, sall, do, GH):
    T, GW = qn.shape
    nc = T // CHUNK

    def body(q_ref, k_ref, v_ref, g_ref, sall_ref, do_ref, dq_ref, dk_ref, dv_ref, dg_ref, ds_scr):
        @pl.when(pl.program_id(0) == 0)
        def _():
            ds_scr[...] = jnp.zeros_like(ds_scr)

        gs = g_ref[...]
        lane = lax.broadcasted_iota(jnp.int32, gs.shape, 1)
        sls = [slice(h * HEAD, (h + 1) * HEAD) for h in range(GH)]
        _, vjp = jax.vjp(_gdn_chunks, tuple(q_ref[:, sl] for sl in sls), tuple(k_ref[:, sl] for sl in sls),
                         tuple(v_ref[:, sl] for sl in sls), tuple(_lane_col(gs, GH + h) for h in range(GH)),
                         tuple(_lane_col(gs, h) for h in range(GH)), tuple(sall_ref[0, h] for h in range(GH)))
        dq, dk, dv, dgc, dbc, ds0 = vjp((tuple(do_ref[:, sl] for sl in sls), tuple(ds_scr[h] for h in range(GH))))
        dgs = jnp.zeros_like(gs)
        for h in range(GH):
            dq_ref[:, sls[h]] = dq[h]
            dk_ref[:, sls[h]] = dk[h]
            dv_ref[:, sls[h]] = dv[h]
            dgs = dgs + jnp.where(lane == h, dbc[h], 0.0) + jnp.where(lane == GH + h, dgc[h], 0.0)
            ds_scr[h] = ds0[h]
        dg_ref[...] = dgs

    row = pl.BlockSpec((CHUNK, GW), lambda i: (nc - 1 - i, 0))
    sm = pl.BlockSpec((CHUNK, LANES), lambda i: (nc - 1 - i, 0))
    return _pcall(
        body, name="gdn_scan_bwd", grid=(nc,),
        in_specs=[row, row, row, sm, pl.BlockSpec((1, GH, HEAD, HEAD), lambda i: (nc - 1 - i, 0, 0, 0)), row],
        out_specs=[row, row, row, sm],
        out_shape=[jax.ShapeDtypeStruct((T, GW), F32)] * 3 + [jax.ShapeDtypeStruct((T, LANES), F32)],
        scratch_shapes=[pltpu.VMEM((GH, HEAD, HEAD), F32)],
        compiler_params=_cp("arbitrary"),
    )(qn, kn, vc, gsm, sall, do)


def _gdn_post_fn(o, z, g):
    return _rms(o, g) * _silu(z)


def _gdn_post(o, proj, off_z, g, GW, tb=512):
    T = o.shape[0]

    def fn(i, n, ov, zv, gv):
        return (_cat([_gdn_post_fn(a, b, gv) for a, b in zip(_heads(ov, HEAD), _heads(zv, HEAD))]),)

    return _rowwise(fn, T, tb, [("row", o, 0, GW), ("row", proj, off_z, GW), ("full", g)], [("row", GW, BF16)], "gdn_post")[0]


def _gdn_post_bwd(o, proj, off_z, g, dout, GW, tb=256):
    T = o.shape[0]

    def fn(i, n, ov, zv, gv, dv):
        dos, dzs, dg = [], [], jnp.zeros_like(gv)
        for a, b, c in zip(_heads(ov, HEAD), _heads(zv, HEAD), _heads(dv, HEAD)):
            _, vjp = jax.vjp(_gdn_post_fn, a, b, gv)
            da, db, dgh = vjp(c)
            dos.append(da)
            dzs.append(db)
            dg = dg + dgh
        return _cat(dos), _cat(dzs), dg

    return _rowwise(fn, T, tb, [("row", o, 0, GW), ("row", proj, off_z, GW), ("full", g), ("row", dout, 0, GW)],
                    [("row", GW, F32), ("row", GW, BF16), ("acc", (1, HEAD), F32)], "gdn_post_bwd")


def _fox_prep(proj, offs, gq, gk, FW, tb=512):
    T = proj.shape[0]

    def fn(i, n, q, k, v, gqv, gkv):
        return (_cat([_rms(a, gqv) * (HEAD ** -0.5) for a in _heads(q, HEAD)]),
                _cat([_rms(a, gkv) for a in _heads(k, HEAD)]), v)

    return _rowwise(fn, T, tb, [("row", proj, offs["fq"], FW), ("row", proj, offs["fk"], FW), ("row", proj, offs["fv"], FW),
                                ("full", gq), ("full", gk)], [("row", FW, BF16)] * 3, "fox_prep")


def _fox_prep_bwd(proj, offs, gq, gk, dq, dk, dv, FW, tb=256):
    T = proj.shape[0]

    def fn(i, n, q, k, gqv, gkv, dqv, dkv, dvv):
        res = []
        for x, g, d in ((q, gqv, dqv), (k, gkv, dkv)):
            dxs, dg = [], jnp.zeros_like(g)
            for a, c in zip(_heads(x, HEAD), _heads(d, HEAD)):
                _, vjp = jax.vjp(_rms, a, g)
                da, dgh = vjp(c)
                dxs.append(da)
                dg = dg + dgh
            res += [_cat(dxs), dg]
        return res[0], res[2], dvv, res[1], res[3]

    return _rowwise(fn, T, tb, [("row", proj, offs["fq"], FW), ("row", proj, offs["fk"], FW), ("full", gq), ("full", gk),
                                ("row", dq, 0, FW), ("row", dk, 0, FW), ("row", dv, 0, FW)],
                    [("row", FW, BF16)] * 3 + [("acc", (1, HEAD), F32)] * 2, "fox_prep_bwd")


def _sub_row(x, sub_idx):
    sub = lax.broadcasted_iota(jnp.int32, x.shape, 0)
    return jnp.sum(jnp.where(sub == sub_idx, x, 0.0), axis=0, keepdims=True)


def _causal_pairs(nb, key_major):
    if key_major:
        pairs = [(i, j) for j in range(nb) for i in range(j, nb)]
    else:
        pairs = [(i, j) for i in range(nb) for j in range(i + 1)]
    return (jnp.asarray(np.array([p[0] for p in pairs], np.int32)),
            jnp.asarray(np.array([p[1] for p in pairs], np.int32)))


def _fox_scores(q, k, ck, diagonal):
    s = _dot_nt(q, k) - ck
    if diagonal:
        row = lax.broadcasted_iota(jnp.int32, s.shape, 0)
        col = lax.broadcasted_iota(jnp.int32, s.shape, 1)
        s = jnp.where(row >= col, s, NEG)
    return s


def _flash_fwd(qb, kb, vb, cumr, FH, blk):
    T, FW = qb.shape
    blk = min(blk, T)
    nb = T // blk
    qi_arr, kj_arr = _causal_pairs(nb, False)

    hps = min(FLASH_FWD_HEADS_PER_STEP, FH)
    assert FH % hps == 0
    us = range(hps)
    sl = [slice(u * HEAD, (u + 1) * HEAD) for u in us]

    def body(qi_ref, kj_ref, q_ref, k_ref, v_ref, ck_ref, ob_ref, lse_ref, m_scr, l_scr, acc):
        t = pl.program_id(1)
        qi, kj = qi_ref[t], kj_ref[t]

        @pl.when(kj == 0)
        def _():
            m_scr[...] = jnp.full_like(m_scr, NEG)
            l_scr[...] = jnp.zeros_like(l_scr)
            acc[...] = jnp.zeros_like(acc)

        def update(diagonal):
            s = [_fox_scores(q_ref[:, sl[u]], k_ref[:, sl[u]], ck_ref[u], diagonal) for u in us]
            m_old = [m_scr[u] for u in us]
            m_new = [jnp.maximum(m_old[u], jnp.max(s[u], axis=1, keepdims=True)) for u in us]
            alpha = [jnp.exp(m_old[u] - m_new[u]) for u in us]
            p = [jnp.exp(s[u] - m_new[u]) for u in us]
            pv = [_dot(p[u].astype(BF16), v_ref[:, sl[u]]) for u in us]
            for u in us:
                l_scr[u] = alpha[u] * l_scr[u] + jnp.sum(p[u], axis=1, keepdims=True)
                acc[:, sl[u]] = alpha[u] * acc[:, sl[u]] + pv[u]
                m_scr[u] = m_new[u]

        @pl.when(kj < qi)
        def _():
            update(False)

        @pl.when(kj == qi)
        def _():
            update(True)
            for u in us:
                ob_ref[:, sl[u]] = (acc[:, sl[u]] / l_scr[u]).astype(BF16)
                lse_ref[u] = m_scr[u] + jnp.log(l_scr[u])

    qs = pl.BlockSpec((blk, hps * HEAD), lambda g, t, qr, kr: (qr[t], g))
    ks = pl.BlockSpec((blk, hps * HEAD), lambda g, t, qr, kr: (kr[t], g))
    col = pl.BlockSpec((hps, blk, 1), lambda g, t, qr, kr: (g, qr[t], 0))
    gs = pltpu.PrefetchScalarGridSpec(
        num_scalar_prefetch=2, grid=(FH // hps, qi_arr.shape[0]),
        in_specs=[qs, ks, ks, pl.BlockSpec((hps, 1, blk), lambda g, t, qr, kr: (g, 0, kr[t]))],
        out_specs=[qs, col],
        scratch_shapes=[pltpu.VMEM((hps, blk, 1), F32), pltpu.VMEM((hps, blk, 1), F32), pltpu.VMEM((blk, hps * HEAD), F32)])
    return _pcall(
        body, name="flash_fwd", grid_spec=gs,
        out_shape=[jax.ShapeDtypeStruct((T, FW), BF16), jax.ShapeDtypeStruct((FH, T, 1), F32)],
        compiler_params=_cp("parallel", "arbitrary"),
    )(qi_arr, kj_arr, qb, kb, vb, cumr)


def _flash_bwd_q(qb, kb, vb, cumr, lse, do, dl, FH, blk):
    T, FW = qb.shape
    blk = min(blk, T)
    nb = T // blk
    qi_arr, kj_arr = _causal_pairs(nb, False)
    want_dq = dl is not None
    hps = min(FLASH_HEADS_PER_STEP, FH)
    assert FH % hps == 0
    us = range(hps)
    sl = [slice(u * HEAD, (u + 1) * HEAD) for u in us]

    def body(qi_ref, kj_ref, q_ref, k_ref, v_ref, ck_ref, lse_ref, do_ref, *rest):
        if want_dq:
            dl_ref, dq_ref, rs_ref, acc, rs_acc = rest
        else:
            dl_ref, acc = rest
        t = pl.program_id(1)
        qi, kj = qi_ref[t], kj_ref[t]

        @pl.when(kj == 0)
        def _():
            acc[...] = jnp.zeros_like(acc)
            if want_dq:
                rs_acc[...] = jnp.zeros_like(rs_acc)

        def update(diagonal):
            s = [_fox_scores(q_ref[:, sl[u]], k_ref[:, sl[u]], ck_ref[u], diagonal) for u in us]
            p = [jnp.exp(s[u] - lse_ref[u]) for u in us]
            dp = [_dot_nt(do_ref[:, sl[u]].astype(BF16), v_ref[:, sl[u]]) for u in us]
            if want_dq:
                ds = [p[u] * (dp[u] - dl_ref[u]) for u in us]
                dqp = [_dot(ds[u].astype(BF16), k_ref[:, sl[u]]) for u in us]
                for u in us:
                    acc[:, sl[u]] += dqp[u]
                    rs_acc[u] += jnp.sum(ds[u], axis=1, keepdims=True)
            else:
                for u in us:
                    acc[u] += jnp.sum(p[u] * dp[u], axis=1, keepdims=True)

        @pl.when(kj < qi)
        def _():
            update(False)

        @pl.when(kj == qi)
        def _():
            update(True)
            if want_dq:
                dq_ref[...] = acc[...] * (HEAD ** -0.5)
                rs_ref[...] = rs_acc[...]
            else:
                dl_ref[...] = acc[...]

    qs = pl.BlockSpec((blk, hps * HEAD), lambda g, t, qr, kr: (qr[t], g))
    ks = pl.BlockSpec((blk, hps * HEAD), lambda g, t, qr, kr: (kr[t], g))
    col = pl.BlockSpec((hps, blk, 1), lambda g, t, qr, kr: (g, qr[t], 0))
    in_specs = [qs, ks, ks, pl.BlockSpec((hps, 1, blk), lambda g, t, qr, kr: (g, 0, kr[t])), col, qs]
    args = [qi_arr, kj_arr, qb, kb, vb, cumr, lse, do]
    colshape = jax.ShapeDtypeStruct((FH, T, 1), F32)
    if want_dq:
        gs = pltpu.PrefetchScalarGridSpec(
            num_scalar_prefetch=2, grid=(FH // hps, qi_arr.shape[0]), in_specs=in_specs + [col], out_specs=[qs, col],
            scratch_shapes=[pltpu.VMEM((blk, hps * HEAD), F32), pltpu.VMEM((hps, blk, 1), F32)])
        return _pcall(body, name="flash_bwd_dq", grid_spec=gs,
                      out_shape=[jax.ShapeDtypeStruct((T, FW), F32), colshape],
                      compiler_params=_cp("parallel", "arbitrary"))(*args, dl)
    gs = pltpu.PrefetchScalarGridSpec(
        num_scalar_prefetch=2, grid=(FH // hps, qi_arr.shape[0]), in_specs=in_specs, out_specs=col,
        scratch_shapes=[pltpu.VMEM((hps, blk, 1), F32)])
    return _pcall(body, name="flash_bwd_rowterm", grid_spec=gs, out_shape=colshape,
                  compiler_params=_cp("parallel", "arbitrary"))(*args)


def _flash_bwd_dkv(qb, kb, vb, cumc, lse_row, dl_row, do, FH, blk):
    T, FW = qb.shape
    blk = min(blk, T)
    nb = T // blk

    qi_arr, kj_arr = _causal_pairs(nb, True)

    hps = min(FLASH_HEADS_PER_STEP, FH)
    assert FH % hps == 0
    us = range(hps)
    sl = [slice(u * HEAD, (u + 1) * HEAD) for u in us]

    def body(qi_ref, kj_ref, q_ref, k_ref, v_ref, ck_ref, lse_ref, dl_ref, do_ref,
             dk_ref, dv_ref, dc_ref, dk_acc, dv_acc, dc_acc):
        t = pl.program_id(1)
        qi, kj = qi_ref[t], kj_ref[t]

        def update(diagonal, first):
            st = [_dot_nt(k_ref[:, sl[u]], q_ref[:, sl[u]]) - ck_ref[u] for u in us]
            if diagonal:
                krow = lax.broadcasted_iota(jnp.int32, st[0].shape, 0)
                qcol = lax.broadcasted_iota(jnp.int32, st[0].shape, 1)
                st = [jnp.where(qcol >= krow, st[u], NEG) for u in us]
            pt = [jnp.exp(st[u] - lse_ref[u]) for u in us]
            dob = [do_ref[:, sl[u]].astype(BF16) for u in us]
            dpt = [_dot_nt(v_ref[:, sl[u]], dob[u]) for u in us]
            dst = [pt[u] * (dpt[u] - dl_ref[u]) for u in us]
            dk = [_dot(dst[u].astype(BF16), q_ref[:, sl[u]]) for u in us]
            dv = [_dot(pt[u].astype(BF16), dob[u]) for u in us]
            for u in us:
                dc = -jnp.sum(dst[u], axis=1, keepdims=True)
                if first:
                    dk_acc[:, sl[u]] = dk[u]
                    dv_acc[:, sl[u]] = dv[u]
                    dc_acc[u] = dc
                else:
                    dk_acc[:, sl[u]] += dk[u]
                    dv_acc[:, sl[u]] += dv[u]
                    dc_acc[u] += dc

        @pl.when(qi == kj)
        def _():
            update(True, True)

        @pl.when(qi > kj)
        def _():
            update(False, False)

        @pl.when(qi == nb - 1)
        def _():
            dk_ref[...] = dk_acc[...]
            dv_ref[...] = dv_acc[...]
            dc_ref[...] = dc_acc[...]

    ks = pl.BlockSpec((blk, hps * HEAD), lambda g, t, qr, kr: (kr[t], g))
    qs = pl.BlockSpec((blk, hps * HEAD), lambda g, t, qr, kr: (qr[t], g))
    rowq = pl.BlockSpec((hps, 1, blk), lambda g, t, qr, kr: (g, 0, qr[t]))
    colk = pl.BlockSpec((hps, blk, 1), lambda g, t, qr, kr: (g, kr[t], 0))
    gs = pltpu.PrefetchScalarGridSpec(
        num_scalar_prefetch=2, grid=(FH // hps, qi_arr.shape[0]),
        in_specs=[qs, ks, ks, colk, rowq, rowq, qs],
        out_specs=[ks, ks, colk],
        scratch_shapes=[pltpu.VMEM((blk, hps * HEAD), F32), pltpu.VMEM((blk, hps * HEAD), F32),
                        pltpu.VMEM((hps, blk, 1), F32)])
    return _pcall(
        body, name="flash_bwd_dkv", grid_spec=gs,
        out_shape=[jax.ShapeDtypeStruct((T, FW), F32), jax.ShapeDtypeStruct((T, FW), F32),
                   jax.ShapeDtypeStruct((FH, T, 1), F32)],
        compiler_params=_cp("parallel", "arbitrary"),
    )(qi_arr, kj_arr, qb, kb, vb, cumc, lse_row, dl_row, do)


def _rev_cumsum_rows(r1, r2, tb=512):
    H, T = r1.shape
    tb = min(tb, T)
    nb = T // tb

    def body(r1_ref, r2_ref, o_ref, carry):
        @pl.when(pl.program_id(0) == 0)
        def _():
            carry[...] = jnp.zeros_like(carry)

        rv = r1_ref[...] + r2_ref[...]
        si = lax.broadcasted_iota(jnp.int32, (tb, tb), 0)
        ti = lax.broadcasted_iota(jnp.int32, (tb, tb), 1)
        o_ref[...] = _dot(rv, (si >= ti).astype(F32), HI) + carry[...]
        carry[...] += jnp.sum(rv, axis=1, keepdims=True)

    spec = pl.BlockSpec((H, tb), lambda i: (0, nb - 1 - i))
    return _pcall(body, name="rev_cumsum", grid=(nb,), in_specs=[spec, spec], out_specs=spec,
                  out_shape=jax.ShapeDtypeStruct((H, T), F32), scratch_shapes=[pltpu.VMEM((H, 1), F32)],
                  compiler_params=_cp("arbitrary"))(r1, r2)


def _mem_head(q, k, v, gq, gk):
    logits = _dot_nt(_rms(q, gq), _rms(k, gk)) * (MEM_DH ** -0.5)
    mx = jnp.max(logits, axis=1, keepdims=True)
    e = jnp.exp(logits - mx)
    p = e / jnp.sum(e, axis=1, keepdims=True)
    return _dot(p, v)


def _mem_attn(proj, off_q, kv, gq, gk, MW, tb=512):
    T = proj.shape[0]
    MH = MW // MEM_DH

    def fn(i, n, q, kvv, gqv, gkv):
        ks, vs = _heads(kvv[:, :MW], MEM_DH), _heads(kvv[:, MW:], MEM_DH)
        return (_cat([_mem_head(a, b, c, gqv, gkv) for a, b, c in zip(_heads(q, MEM_DH), ks, vs)]),)

    return _rowwise(fn, T, tb, [("row", proj, off_q, MW), ("full", kv), ("full", gq), ("full", gk)],
                    [("row", MW, BF16)], "mem_attn")[0]


def _mem_attn_bwd(proj, off_q, kv, gq, gk, dout, MW, tb=256):
    T = proj.shape[0]
    ML = kv.shape[0]

    def fn(i, n, q, kvv, gqv, gkv, dv):
        ks, vs = _heads(kvv[:, :MW], MEM_DH), _heads(kvv[:, MW:], MEM_DH)
        dqs, dks, dvs = [], [], []
        dgq, dgk = jnp.zeros_like(gqv), jnp.zeros_like(gkv)
        for a, b, c, d in zip(_heads(q, MEM_DH), ks, vs, _heads(dv, MEM_DH)):
            _, vjp = jax.vjp(_mem_head, a, b, c, gqv, gkv)
            da, db, dc, dg1, dg2 = vjp(d)
            dqs.append(da)
            dks.append(db)
            dvs.append(dc)
            dgq, dgk = dgq + dg1, dgk + dg2
        return _cat(dqs), _cat(dks + dvs), dgq, dgk

    return _rowwise(fn, T, tb, [("row", proj, off_q, MW), ("full", kv), ("full", gq), ("full", gk), ("row", dout, 0, MW)],
                    [("row", MW, BF16), ("acc", (ML, 2 * MW), F32), ("acc", (1, MEM_DH), F32), ("acc", (1, MEM_DH), F32)],
                    "mem_attn_bwd")


def _merge_fn(ga, gb, gm, ua, ub, um):
    return _sigmoid(ga) * ua + _sigmoid(gb) * ub + _sigmoid(gm) * um


def _merge(proj, offs, ua, ub, um, D, tb=256):
    T = proj.shape[0]
    ins = [("row", proj, offs[k], D) for k in ("ga", "gb", "gm")] + [("row", u, 0, D) for u in (ua, ub, um)]
    return _rowwise(lambda i, n, *v: (_merge_fn(*v),), T, tb, ins, [("row", D, BF16)], "merge")[0]


def _merge_bwd(proj, offs, ua, ub, um, dy, D, tb=256):
    T = proj.shape[0]

    def fn(i, n, *v):
        _, vjp = jax.vjp(_merge_fn, *v[:6])
        return vjp(v[6])

    ins = [("row", proj, offs[k], D) for k in ("ga", "gb", "gm")] + [("row", u, 0, D) for u in (ua, ub, um)] + [("row", dy, 0, D)]
    return _rowwise(fn, T, tb, ins, [("row", D, BF16)] * 6, "merge_bwd")


def _loss_sum(dx2, tb=512):
    T, D = dx2.shape

    def fn(i, n, g):
        part = jnp.sum(jnp.sum(g * g, axis=1, keepdims=True), axis=0, keepdims=True) * (0.5 * D)
        return (part + jnp.zeros((SUB, LANES), F32),)

    return _rowwise(fn, T, tb, [("row", dx2, 0, D)], [("acc", (SUB, LANES), F32)], "loss_sum")[0]


def _adamw(w, g, m, v, name, tb=128):
    R, C = w.shape
    c1 = 1.0 / (1.0 - ADAM_B1 ** ADAM_STEP)
    c2 = 1.0 / (1.0 - ADAM_B2 ** ADAM_STEP)

    def fn(i, n, wv, gv, mv, vv):
        mn = ADAM_B1 * mv + (1.0 - ADAM_B1) * gv
        vn = ADAM_B2 * vv + (1.0 - ADAM_B2) * (gv * gv)
        delta = -ADAM_LR * ((mn * c1) / (jnp.sqrt(vn * c2) + ADAM_EPS) + ADAM_WD * wv)
        return delta, mn, vn

    return _rowwise(fn, R, tb, [("row", a, 0, C) for a in (w, g, m, v)], [("row", C, F32)] * 3, name)


def _coords():
    return lax.axis_index("x"), lax.axis_index("y"), lax.axis_index("c")


def _allgather_small(blk):
    m_per, n = blk.shape

    def body(x_ref, out_ref, send_sems, recv_sems, local_sem):
        x, y, c = _coords()
        me, sibling = (x, y, c), (x, y, 1 - c)
        chips = [(1 - x, y), (x, 1 - y), (1 - x, 1 - y)]

        def rows(px, py, pc):
            return out_ref.at[pl.ds((4 * px + 2 * py + pc) * m_per, m_per), :]

        def copy(k, block, to, src=None):
            return pltpu.make_async_remote_copy(
                src_ref=rows(*block) if src is None else src, dst_ref=rows(*block),
                send_sem=send_sems.at[k], recv_sem=recv_sems.at[k], device_id=to, device_id_type=MESH)

        mine = pltpu.make_async_copy(x_ref, rows(*me), local_sem)
        mine.start()
        first = [copy(0, me, sibling, src=x_ref)]
        first += [copy(1 + j, me, (*chip, c), src=x_ref) for j, chip in enumerate(chips)]
        for cp in first:
            cp.start()
        passed = [copy(4 + j, (*chip, c), sibling) for j, chip in enumerate(chips)]
        for j, chip in enumerate(chips):
            copy(1 + j, (*chip, c), me).wait_recv()
            passed[j].start()
        copy(0, sibling, me).wait_recv()
        for j, chip in enumerate(chips):
            copy(4 + j, (*chip, 1 - c), me).wait_recv()
        for cp in first + passed:
            cp.wait_send()
        mine.wait()

    return _pcall(
        body, name="allgather_small", out_shape=jax.ShapeDtypeStruct((8 * m_per, n), blk.dtype),
        in_specs=[pl.BlockSpec(memory_space=pltpu.VMEM)], out_specs=pl.BlockSpec(memory_space=pltpu.VMEM),
        scratch_shapes=[pltpu.SemaphoreType.DMA((7,)), pltpu.SemaphoreType.DMA((7,)), pltpu.SemaphoreType.DMA],
        compiler_params=pltpu.CompilerParams(vmem_limit_bytes=VMEM_LIMIT),
    )(blk)


def _sum8(g, m_per):
    n = g.shape[1]

    def body(g_ref, o_ref):
        acc = g_ref[pl.ds(0, m_per), :]
        for d in range(1, 8):
            acc = acc + g_ref[pl.ds(d * m_per, m_per), :]
        o_ref[...] = acc

    return _pcall(body, name="sum8", out_shape=jax.ShapeDtypeStruct((m_per, n), g.dtype))(g)


_ANY = pl.BlockSpec(memory_space=pl.ANY)


def _allgather_chips(buf):
    nr, w = buf.shape
    half = nr // 2

    def body(in_ref, out_ref, send_sems, recv_sems):
        x, y, c = _coords()
        me = 2 * x + y
        chips = [(1 - x, y), (x, 1 - y), (1 - x, 1 - y)]
        mine_rows = pl.ds(pl.multiple_of(c * half, 16), half)
        other_rows = pl.ds(pl.multiple_of((1 - c) * half, 16), half)

        def copy(k, src, dst, to):
            return pltpu.make_async_remote_copy(src_ref=src, dst_ref=dst, send_sem=send_sems.at[k],
                                                recv_sem=recv_sems.at[k], device_id=to, device_id_type=MESH)

        first = [copy(j, in_ref.at[mine_rows], out_ref.at[me, mine_rows], (cx, cy, c)) for j, (cx, cy) in enumerate(chips)]
        for cp in first:
            cp.start()
        passed = []
        for j, (cx, cy) in enumerate(chips):
            slot = out_ref.at[2 * cx + cy, mine_rows]
            copy(j, slot, slot, (cx, cy, c)).wait_recv()
            fwd = copy(3 + j, slot, slot, (x, y, 1 - c))
            fwd.start()
            passed.append(fwd)
        for j, (cx, cy) in enumerate(chips):
            slot = out_ref.at[2 * cx + cy, other_rows]
            copy(3 + j, slot, slot, (x, y, 1 - c)).wait_recv()
        for cp in first + passed:
            cp.wait_send()

    return _pcall(
        body, name="allgather_chips", out_shape=jax.ShapeDtypeStruct((4, nr, w), buf.dtype),
        in_specs=[_ANY], out_specs=_ANY,
        scratch_shapes=[pltpu.SemaphoreType.DMA((6,)), pltpu.SemaphoreType.DMA((6,))],
    )(buf)


def _rs_pair_exchange(g):
    _, nr, w = g.shape
    half = nr // 2

    def body(g_ref, rb_ref, send_sem, recv_sem):
        x, y, c = _coords()
        other_rows = pl.ds(pl.multiple_of((1 - c) * half, SUB), half)
        cp = pltpu.make_async_remote_copy(src_ref=g_ref.at[:, other_rows], dst_ref=rb_ref, send_sem=send_sem,
                                          recv_sem=recv_sem, device_id=(x, y, 1 - c), device_id_type=MESH)
        cp.start()
        cp.wait()

    return _pcall(body, name="rs_pair_exchange", out_shape=jax.ShapeDtypeStruct((4, half, w), g.dtype),
                  in_specs=[_ANY], out_specs=_ANY,
                  scratch_shapes=[pltpu.SemaphoreType.DMA, pltpu.SemaphoreType.DMA])(g)


def _rs_pair_add(g, rb, cidx, tb=256):
    _, nr, w = g.shape
    half = nr // 2
    tb = min(tb, half)
    assert half % tb == 0
    hb = half // tb

    def body(c_ref, g_ref, r_ref, o_ref):
        o_ref[...] = (g_ref[...].astype(F32) + r_ref[...].astype(F32)).astype(o_ref.dtype)

    gs = pltpu.PrefetchScalarGridSpec(
        num_scalar_prefetch=1, grid=(4, hb),
        in_specs=[pl.BlockSpec((1, tb, w), lambda j, i, c_ref: (j, c_ref[0] * hb + i, 0)),
                  pl.BlockSpec((1, tb, w), lambda j, i, c_ref: (j, i, 0))],
        out_specs=pl.BlockSpec((1, tb, w), lambda j, i, c_ref: (j, i, 0)))
    return _pcall(body, name="rs_pair_add", grid_spec=gs, out_shape=jax.ShapeDtypeStruct((4, half, w), BF16),
                  compiler_params=_cp("parallel", "parallel"))(cidx, g, rb)


def _rs_chip_exchange(p):
    _, h, w = p.shape

    def body(p_ref, rb_ref, send_sems, recv_sems):
        x, y, c = _coords()
        chips = [(1 - x, y), (x, 1 - y), (1 - x, 1 - y)]
        cps = [pltpu.make_async_remote_copy(src_ref=p_ref.at[2 * cx + cy], dst_ref=rb_ref.at[j], send_sem=send_sems.at[j],
                                            recv_sem=recv_sems.at[j], device_id=(cx, cy, c), device_id_type=MESH)
               for j, (cx, cy) in enumerate(chips)]
        for cp in cps:
            cp.start()
        for cp in cps:
            cp.wait()

    return _pcall(body, name="rs_chip_exchange", out_shape=jax.ShapeDtypeStruct((3, h, w), p.dtype),
                  in_specs=[_ANY], out_specs=_ANY,
                  scratch_shapes=[pltpu.SemaphoreType.DMA((3,)), pltpu.SemaphoreType.DMA((3,))])(p)


def _sum4(p, rb, chip_idx, tb=256):
    _, h, w = rb.shape
    tb = min(tb, h)
    assert h % tb == 0

    def body(m_ref, p_ref, r_ref, o_ref):
        f = lambda t: t.astype(F32)
        o_ref[...] = ((f(p_ref[0]) + f(r_ref[0])) + f(r_ref[1])) + f(r_ref[2])

    gs = pltpu.PrefetchScalarGridSpec(
        num_scalar_prefetch=1, grid=(h // tb,),
        in_specs=[pl.BlockSpec((1, tb, w), lambda i, m_ref: (m_ref[0], i, 0)),
                  pl.BlockSpec((3, tb, w), lambda i, m_ref: (0, i, 0))],
        out_specs=pl.BlockSpec((tb, w), lambda i, m_ref: (i, 0)))
    return _pcall(body, name="sum4", grid_spec=gs, out_shape=jax.ShapeDtypeStruct((h, w), F32),
                  compiler_params=_cp("parallel"))(chip_idx, p, rb)


def _pair_allgather(f):
    h, w = f.shape

    def body(f_ref, out_ref, send_sem, recv_sem):
        x, y, c = _coords()
        mine_rows = pl.ds(pl.multiple_of(c * h, SUB), h)
        other_rows = pl.ds(pl.multiple_of((1 - c) * h, SUB), h)
        send = pltpu.make_async_remote_copy(src_ref=f_ref, dst_ref=out_ref.at[mine_rows], send_sem=send_sem,
                                            recv_sem=recv_sem, device_id=(x, y, 1 - c), device_id_type=MESH)
        send.start()
        send.wait_send()
        pltpu.make_async_remote_copy(src_ref=f_ref, dst_ref=out_ref.at[other_rows], send_sem=send_sem,
                                     recv_sem=recv_sem, device_id=(x, y, 1 - c), device_id_type=MESH).wait_recv()

    return _pcall(body, name="pair_allgather", out_shape=jax.ShapeDtypeStruct((2 * h, w), f.dtype),
                  in_specs=[_ANY], out_specs=_ANY,
                  scratch_shapes=[pltpu.SemaphoreType.DMA, pltpu.SemaphoreType.DMA])(f)


def _size(shape):
    n = 1
    for d in shape:
        n *= d
    return n


PACK_ALIGN = 16


def _pack(arrs, dtype, row_mult):
    parts = []
    for a in arrs:
        flat = a.astype(dtype).reshape(-1)
        n = flat.shape[0]
        full = _ru(n, PACK_W * PACK_ALIGN)
        if full > n:
            flat = jnp.pad(flat, (0, full - n))
        parts.append(flat.reshape(-1, PACK_W))
    rows = sum(p.shape[0] for p in parts)
    if rows % row_mult:
        parts.append(jnp.zeros((_ru(rows, row_mult) - rows, PACK_W), dtype))
    return jnp.concatenate(parts, axis=0)


def _unpack(buf, shapes):
    out, off = [], 0
    for s in shapes:
        n = _size(s)
        r = _ru(-(-n // PACK_W), PACK_ALIGN)
        part = buf[off:off + r]
        out.append(part.reshape(s) if n == r * PACK_W else part.reshape(-1)[:n].reshape(s))
        off += r
    return out


def _pack_flat(arrs, dtype, row_mult):
    flat = jnp.concatenate([a.astype(dtype).reshape(-1) for a in arrs])
    n = flat.shape[0]
    rows = _ru(-(-n // PACK_W), row_mult)
    return jnp.pad(flat, (0, rows * PACK_W - n)).reshape(rows, PACK_W)


def _unpack_flat(buf, shapes):
    flat = buf.reshape(-1)
    out, off = [], 0
    for s in shapes:
        n = _size(s)
        out.append(flat[off:off + n].reshape(s))
        off += n
    return out


def _in_layout(D, GW, GH, FW, FH, MW, tn):
    o_z = 3 * GW
    o_beta = 4 * GW
    o_fq = o_beta + 2 * GH
    o_ff = o_fq + 3 * FW
    o_mq = o_ff + FH
    o_g = o_mq + MW
    orig = {"q": (0, GW), "k": (GW, GW), "v": (2 * GW, GW), "z": (o_z, GW), "beta": (o_beta, GH), "dec": (o_beta + GH, GH),
            "fq": (o_fq, FW), "fk": (o_fq + FW, FW), "fv": (o_fq + 2 * FW, FW), "ff": (o_ff, FH), "mq": (o_mq, MW),
            "ga": (o_g, D), "gb": (o_g + D, D), "gm": (o_g + 2 * D, D)}
    offs, cur = {}, 0
    for key, width in (("ga", D), ("gb", D), ("gm", D), ("q", GW), ("k", GW), ("v", GW), ("z", GW),
                       ("fq", FW), ("fk", FW), ("fv", FW), ("mq", MW), ("small", LANES)):
        cur = _ru(cur, width)
        offs[key] = cur
        cur += width
    total = _ru(cur, tn)
    pieces = [(offs[k], orig[k][0], orig[k][1]) for k in ("ga", "gb", "gm", "q", "k", "v", "z", "fq", "fk", "fv", "mq")]
    pieces += [(offs["small"], orig["beta"][0], GH), (offs["small"] + GH, orig["dec"][0], GH),
               (offs["small"] + 2 * GH, orig["ff"][0], FH)]
    return offs, total, pieces, o_g + 3 * D


def _pad_cols(w, pieces, total):
    parts, cur = [], 0
    for pstart, ostart, n in pieces:
        if pstart > cur:
            parts.append(jnp.zeros((w.shape[0], pstart - cur), w.dtype))
        parts.append(w[:, ostart:ostart + n])
        cur = pstart + n
    if total > cur:
        parts.append(jnp.zeros((w.shape[0], total - cur), w.dtype))
    return jnp.concatenate(parts, axis=1)


def _unpad_cols(wp, pieces):
    return jnp.concatenate([wp[:, pstart:pstart + n] for pstart, ostart, n in sorted(pieces, key=lambda t: t[1])], axis=1)


def _local_step(x, mem, tgt, W, flash_blk=512):
    T, D = x.shape
    GW = W["w_up_gdn"].shape[0]
    FW = W["w_up_fox"].shape[0]
    MW = W["w_up_mem"].shape[0]
    GH, FH = GW // HEAD, FW // HEAD
    offs, NP, pieces, d_in = _in_layout(D, GW, GH, FW, FH, MW, 1024)
    assert W["w_in"].shape[1] == d_in
    w_in_p = _pad_cols(W["w_in"], pieces, NP)
    cw = W["conv_w"]
    cws = [cw[:, i * GW:(i + 1) * GW] for i in range(3)]
    zl = jnp.zeros((1, LANES), F32)
    pvecs = [lax.dynamic_update_slice(zl, W["a_log"], (0, GH)), lax.dynamic_update_slice(zl, W["dt_bias"], (0, GH)),
             lax.dynamic_update_slice(zl, W["fox_b_f"], (0, 2 * GH))]
    lane0 = 2 * GH

    h = _rms_fwd(x, W["g_mix"], "rms_mix")
    proj = _mm(h, w_in_p, "nn", "in_proj")
    qn, kn, vc, cum, gbm = _gdn_prep(proj, offs, cws, pvecs, GH, FH, GW)
    o_gdn, sall = _gdn_scan_fwd(qn, kn, vc, gbm, GH)
    o_a = _gdn_post(o_gdn, proj, offs["z"], W["gdn_norm_g"], GW)
    qb, kb, vb = _fox_prep(proj, offs, W["fox_q_norm"], W["fox_k_norm"], FW)
    cumh = cum[:, lane0:lane0 + FH].T
    cumc, cumr = cumh.reshape(FH, T, 1), cumh.reshape(FH, 1, T)
    o_b16, lse = _flash_fwd(qb, kb, vb, cumr, FH, flash_blk)
    memn = _rms_fwd(mem, W["g_mem"], "rms_mem")
    kv = _mm(memn, W["w_mem_kv"], "nn", "mem_kv")
    o_m = _mem_attn(proj, offs["mq"], kv, W["mem_q_norm"], W["mem_k_norm"], MW)
    ua = _mm(o_a, W["w_up_gdn"], "nn", "up_gdn")
    ub = _mm(o_b16, W["w_up_fox"], "nn", "up_fox")
    um = _mm(o_m, W["w_up_mem"], "nn", "up_mem")
    y = _merge(proj, offs, ua, ub, um, D)
    x1 = _mm(y, W["w_out"], "nn", "out_proj", epilogue=lambda acc, r: (acc + r,), extras=(x,))
    h2 = _rms_fwd(x1, W["g_mlp"], "rms_mlp")
    u, a = _mm(h2, W["w_ff1"], "nn", "ff1", out_dtypes=(F32, BF16),
               epilogue=lambda acc: (acc, jnp.square(jnp.maximum(acc, 0.0))))
    def loss_epilogue(acc, r, t):
        g = ((acc + r) - t) * (1.0 / D)
        return g, g

    dx2, dx2b = _mm(a, W["w_ff2"], "nn", "ff2", out_dtypes=(F32, BF16), epilogue=loss_epilogue, extras=(x1, tgt))
    lpart = _loss_sum(dx2)
    loss = lpart[0, 0]

    G = {}
    du = _mm(dx2b, W["w_ff2"], "nt", "ff2_dx", out_dtypes=(BF16,),
             epilogue=lambda acc, uu: (acc * (2.0 * jnp.maximum(uu, 0.0)),), extras=(u,))
    G["w_ff2"] = _mm(a, dx2b, "tn", "ff2_dw", out_dtypes=(BF16,))
    dh2 = _mm(du, W["w_ff1"], "nt", "ff1_dx")
    G["w_ff1"] = _mm(h2, du, "tn", "ff1_dw", out_dtypes=(BF16,))
    dx1, dx1b, G["g_mlp"] = _rms_bwd(x1, W["g_mlp"], dh2, dx2, "rms_mlp_bwd")
    dy = _mm(dx1b, W["w_out"], "nt", "out_dx")
    G["w_out"] = _mm(y, dx1b, "tn", "out_dw", out_dtypes=(BF16,))
    dga, dgb, dgm, dua, dub, dum = _merge_bwd(proj, offs, ua, ub, um, dy, D)
    do_a = _mm(dua, W["w_up_gdn"], "nt", "up_gdn_dx")
    G["w_up_gdn"] = _mm(o_a, dua, "tn", "up_gdn_dw", out_dtypes=(BF16,))
    do_b = _mm(dub, W["w_up_fox"], "nt", "up_fox_dx")
    G["w_up_fox"] = _mm(o_b16, dub, "tn", "up_fox_dw", out_dtypes=(BF16,))
    do_m = _mm(dum, W["w_up_mem"], "nt", "up_mem_dx")
    G["w_up_mem"] = _mm(o_m, dum, "tn", "up_mem_dw", out_dtypes=(BF16,))
    dmq, dkv, G["mem_q_norm"], G["mem_k_norm"] = _mem_attn_bwd(proj, offs["mq"], kv, W["mem_q_norm"], W["mem_k_norm"], do_m, MW)
    dkvb = dkv.astype(BF16)
    dmemn = _mm(dkvb, W["w_mem_kv"], "nt", "mem_kv_dx")
    G["w_mem_kv"] = _mm(memn, dkvb, "tn", "mem_kv_dw", out_dtypes=(BF16,))
    G["g_mem"] = _rms_dg(mem, W["g_mem"], dmemn, "rms_mem_bwd")
    dl = _flash_bwd_q(qb, kb, vb, cumr, lse, do_b, None, FH, flash_blk)
    dqb, dcq = _flash_bwd_q(qb, kb, vb, cumr, lse, do_b, dl, FH, flash_blk)
    dkb, dvb, dck = _flash_bwd_dkv(qb, kb, vb, cumc, lse.reshape(FH, 1, T), dl.reshape(FH, 1, T), do_b, FH, flash_blk)
    dlf = _rev_cumsum_rows(dcq.reshape(FH, T), dck.reshape(FH, T))
    dlf_sm = jnp.pad(dlf.T, ((0, 0), (lane0, LANES - lane0 - FH)))
    dfq, dfk, dfv, G["fox_q_norm"], G["fox_k_norm"] = _fox_prep_bwd(proj, offs, W["fox_q_norm"], W["fox_k_norm"], dqb, dkb, dvb, FW)
    do_gdn, dz, G["gdn_norm_g"] = _gdn_post_bwd(o_gdn, proj, offs["z"], W["gdn_norm_g"], do_a, GW)
    dqn, dkn, dvc, dgsm = _gdn_scan_bwd(qn, kn, vc, gbm, sall, do_gdn, GH)
    dyq, dyk, dyv, dcq, dck_w, dcv, dsmall, dalog, ddtb, dbf = _gdn_prep_bwd_a(
        proj, offs, cws, pvecs, (dqn, dkn, dvc), dgsm, dlf_sm, GH, FH, GW)
    dxq, dxk, dxv = _gdn_prep_bwd_b((dyq, dyk, dyv), cws, GW)
    G["conv_w"] = jnp.concatenate([dcq[:CONV_K], dck_w[:CONV_K], dcv[:CONV_K]], axis=1)
    G["a_log"] = dalog[:, GH:2 * GH]
    G["dt_bias"] = ddtb[:, GH:2 * GH]
    G["fox_b_f"] = dbf[:, lane0:lane0 + FH]
    segs = {"ga": dga, "gb": dgb, "gm": dgm, "q": dxq, "k": dxk, "v": dxv, "z": dz, "fq": dfq, "fk": dfk, "fv": dfv,
            "mq": dmq, "small": dsmall}
    parts, cur = [], 0
    for key in ("ga", "gb", "gm", "q", "k", "v", "z", "fq", "fk", "fv", "mq", "small"):
        if offs[key] > cur:
            parts.append(jnp.zeros((T, offs[key] - cur), BF16))
        parts.append(segs[key])
        cur = offs[key] + segs[key].shape[1]
    if NP > cur:
        parts.append(jnp.zeros((T, NP - cur), BF16))
    dproj = jnp.concatenate(parts, axis=1)
    dh = _mm(dproj, w_in_p, "nt", "in_dx")
    G["w_in"] = _unpad_cols(_mm(h, dproj, "tn", "in_dw", out_dtypes=(BF16,)), pieces)
    grad_x, _, G["g_mix"] = _rms_bwd(x, W["g_mix"], dh, dx1, "rms_mix_bwd")
    return loss, grad_x, G


BIG = ["w_in", "w_mem_kv", "w_up_gdn", "w_up_fox", "w_up_mem", "w_out", "w_ff1", "w_ff2"]
SMALL = ["g_mix", "a_log", "dt_bias", "gdn_norm_g", "fox_b_f", "fox_q_norm", "fox_k_norm", "g_mem", "mem_q_norm",
         "mem_k_norm", "g_mlp"]
ORDER = ["g_mix", "w_in", "conv_w", "a_log", "dt_bias", "gdn_norm_g", "fox_b_f", "fox_q_norm", "fox_k_norm", "g_mem",
         "w_mem_kv", "mem_q_norm", "mem_k_norm", "w_up_gdn", "w_up_fox", "w_up_mem", "w_out", "g_mlp", "w_ff1", "w_ff2"]
SHARD_AXIS = {"w_in": 1, "w_mem_kv": 0, "w_up_gdn": 1, "w_up_fox": 1, "w_up_mem": 1, "w_out": 0, "w_ff1": 1, "w_ff2": 0}


def _step(x, mem, tgt, w, m, v, flash_blk=512):
    xi, yi, ci = _coords()
    chip = 2 * xi + yi

    shard_shapes = [w[n].shape for n in BIG]
    packed_w = _pack([w[n] for n in BIG], BF16, PACK_ROWS)
    gathered = _allgather_chips(packed_w)
    gathered = lax.dynamic_update_slice(gathered, packed_w[None], (chip, 0, 0))
    per_chip = [_unpack(gathered[j], shard_shapes) for j in range(4)]
    W = {n: jnp.concatenate([per_chip[j][i] for j in range(4)], axis=SHARD_AXIS[n]) for i, n in enumerate(BIG)}
    cw_rows = jnp.pad(w["conv_w"], ((0, SUB - CONV_K), (0, 0)))
    cw_all = _allgather_small(cw_rows)
    W["conv_w"] = jnp.concatenate([cw_all[16 * j:16 * j + CONV_K] for j in range(4)], axis=1)
    for n in SMALL:
        W[n] = w[n]

    loss, grad_x, G = _local_step(x, mem, tgt, W, flash_blk)
    loss = lax.psum(loss, ("x", "y", "c"))

    small_shapes = [G[n].shape for n in SMALL] + [G["conv_w"].shape]
    sm = _pack_flat([G[n] for n in SMALL] + [G["conv_w"]], F32, SUB)
    sm_sum = _sum8(_allgather_small(sm), sm.shape[0])
    sm_list = _unpack_flat(sm_sum, small_shapes)
    g = {n: sm_list[i] for i, n in enumerate(SMALL)}
    cw_full = sm_list[-1]
    gw4 = cw_full.shape[1] // 4
    g["conv_w"] = lax.dynamic_slice(cw_full, (0, chip * gw4), (CONV_K, gw4))

    by_dest = []
    for j in range(4):
        shards = []
        for n in BIG:
            size = w[n].shape[SHARD_AXIS[n]]
            shards.append(lax.slice_in_dim(G[n], j * size, (j + 1) * size, axis=SHARD_AXIS[n]))
        by_dest.append(_pack(shards, BF16, PACK_ROWS))
    gflat = jnp.stack(by_dest)
    rb1 = _rs_pair_exchange(gflat)
    part = _rs_pair_add(gflat, rb1, jnp.reshape(ci, (1,)).astype(jnp.int32))
    rb2 = _rs_chip_exchange(part)
    half_sum = _sum4(part, rb2, jnp.reshape(chip, (1,)).astype(jnp.int32))
    mine = _pair_allgather(half_sum)
    mine = lax.dynamic_update_slice(mine, half_sum, (ci * half_sum.shape[0], 0))
    for i, gv in enumerate(_unpack(mine, shard_shapes)):
        g[BIG[i]] = gv

    delta, new_m, new_v = {}, {}, {}
    for n in BIG:
        delta[n], new_m[n], new_v[n] = _adamw(w[n], g[n], m[n], v[n], "adamw_" + n)
    rest = SMALL + ["conv_w"]
    rest_shapes = [w[n].shape for n in rest]
    packed = [_pack_flat([d[n] for n in rest], F32, SUB) for d in (w, g, m, v)]
    outs = _adamw(*packed, "adamw_small", tb=packed[0].shape[0])
    for d, buf in zip((delta, new_m, new_v), outs):
        for n, val in zip(rest, _unpack_flat(buf, rest_shapes)):
            d[n] = val
    return loss, grad_x, g, delta, new_m, new_v


def kernel(x, mem, g_mix, w_in, conv_w, a_log, dt_bias, gdn_norm_g, fox_b_f, fox_q_norm, fox_k_norm, g_mem, w_mem_kv, mem_q_norm, mem_k_norm, w_up_gdn, w_up_fox, w_up_mem, w_out, g_mlp, w_ff1, w_ff2, loss_target, m_g_mix, m_w_in, m_conv_w, m_a_log, m_dt_bias, m_gdn_norm_g, m_fox_b_f, m_fox_q_norm, m_fox_k_norm, m_g_mem, m_w_mem_kv, m_mem_q_norm, m_mem_k_norm, m_w_up_gdn, m_w_up_fox, m_w_up_mem, m_w_out, m_g_mlp, m_w_ff1, m_w_ff2, v_g_mix, v_w_in, v_conv_w, v_a_log, v_dt_bias, v_gdn_norm_g, v_fox_b_f, v_fox_q_norm, v_fox_k_norm, v_g_mem, v_w_mem_kv, v_mem_q_norm, v_mem_k_norm, v_w_up_gdn, v_w_up_fox, v_w_up_mem, v_w_out, v_g_mlp, v_w_ff1, v_w_ff2):
    ws = (g_mix, w_in, conv_w, a_log, dt_bias, gdn_norm_g, fox_b_f, fox_q_norm, fox_k_norm, g_mem, w_mem_kv, mem_q_norm,
          mem_k_norm, w_up_gdn, w_up_fox, w_up_mem, w_out, g_mlp, w_ff1, w_ff2)
    ms = (m_g_mix, m_w_in, m_conv_w, m_a_log, m_dt_bias, m_gdn_norm_g, m_fox_b_f, m_fox_q_norm, m_fox_k_norm, m_g_mem,
          m_w_mem_kv, m_mem_q_norm, m_mem_k_norm, m_w_up_gdn, m_w_up_fox, m_w_up_mem, m_w_out, m_g_mlp, m_w_ff1, m_w_ff2)
    vs = (v_g_mix, v_w_in, v_conv_w, v_a_log, v_dt_bias, v_gdn_norm_g, v_fox_b_f, v_fox_q_norm, v_fox_k_norm, v_g_mem,
          v_w_mem_kv, v_mem_q_norm, v_mem_k_norm, v_w_up_gdn, v_w_up_fox, v_w_up_mem, v_w_out, v_g_mlp, v_w_ff1, v_w_ff2)
    drop = lambda a: a[0] if a.ndim == 3 else a
    w = {n: drop(a) for n, a in zip(ORDER, ws)}
    m = {n: drop(a) for n, a in zip(ORDER, ms)}
    v = {n: drop(a) for n, a in zip(ORDER, vs)}
    loss, grad_x, g, delta, new_m, new_v = _step(x[0], mem[0], loss_target[0], w, m, v)
    out = [loss, grad_x[None]]
    for d in (g, delta, new_m, new_v):
        out += [d[n].reshape(a.shape) for n, a in zip(ORDER, ws)]
    return tuple(out)
```
